```python
import math
import jax, jax.numpy as jnp
from jax import lax
import numpy as np

D_MODEL = 1024
BATCH = 1
SEQ = 16384
DEPTH = 2

HEAD_DIM = 64
FOX_HEADS = 8
FOX_WIDTH = FOX_HEADS * HEAD_DIM
DIFF_HEADS = 4
DIFF_QK_WIDTH = DIFF_HEADS * 2 * HEAD_DIM
DIFF_V_DIM = 2 * HEAD_DIM
DIFF_V_WIDTH = DIFF_HEADS * DIFF_V_DIM
IN_SPLIT_SIZES = (FOX_WIDTH, FOX_WIDTH, FOX_WIDTH, FOX_HEADS, DIFF_QK_WIDTH, DIFF_QK_WIDTH, DIFF_V_WIDTH, D_MODEL, D_MODEL)
IN_SPLIT_POINTS = tuple(int(p) for p in np.cumsum(IN_SPLIT_SIZES)[:-1])
N_IN = int(sum(IN_SPLIT_SIZES))
FOX_V_OFFSET = 2 * FOX_WIDTH
DIFF_V_OFFSET = 3 * FOX_WIDTH + FOX_HEADS + 2 * DIFF_QK_WIDTH
BLOCK_Q = 128
REL_BUCKETS = 32
REL_MAX_DISTANCE = 128
FGATE_BIAS_INIT = 3.0
D_FF = 2816
N_EXPERTS = 8
TOP_K = 2
D_FF_EXPERT = 3584
LN_EPS = 1e-5
SUBLN_EPS = 1e-5
DEEPNORM_ALPHA = (2.0 * DEPTH) ** 0.25
DEEPNORM_BETA = (8.0 * DEPTH) ** -0.25
N_DENSE = (DEPTH + 1) // 2
N_MOE = DEPTH // 2

kernel_name = 'hybrid_fox_diffattn_deepnorm_moe'


def layer_norm(x, g, b):
    x32 = x.astype(jnp.float32)
    mu = jnp.mean(x32, axis=-1, keepdims=True)
    xc = x32 - mu
    var = jnp.mean(xc * xc, axis=-1, keepdims=True)
    y = xc * lax.rsqrt(var + LN_EPS) * g.astype(jnp.float32) + b.astype(jnp.float32)
    return y.astype(x.dtype)


def t5_causal_bucket(dist):
    n = jnp.maximum(dist, 0)
    max_exact = REL_BUCKETS // 2
    nf = jnp.maximum(n, 1).astype(jnp.float32)
    log_part = jnp.log(nf / max_exact) / math.log(REL_MAX_DISTANCE / max_exact) * (REL_BUCKETS - max_exact)
    large = jnp.minimum(max_exact + log_part.astype(jnp.int32), REL_BUCKETS - 1)
    return jnp.where(n < max_exact, n, large)


def fox_attention(q, k, v, log_f):
    B, S, H, Dh = q.shape
    nb = S // BLOCK_Q
    scale = Dh ** -0.5
    cum = jnp.cumsum(log_f.astype(jnp.float32), axis=1)
    cum_k = jnp.transpose(cum, (0, 2, 1))
    kpos = jnp.arange(S)
    qb = jnp.moveaxis(q.reshape(B, nb, BLOCK_Q, H, Dh), 1, 0)
    cb = jnp.moveaxis(cum_k.reshape(B, H, nb, BLOCK_Q), 2, 0)

    def block(args):
        i, qi, ci = args
        qpos = i * BLOCK_Q + jnp.arange(BLOCK_Q)
        s = jnp.einsum('bqhd,bkhd->bhqk', qi, k, preferred_element_type=jnp.float32) * scale
        s = s + ci[..., :, None] - cum_k[:, :, None, :]
        s = jnp.where(kpos[None, :] <= qpos[:, None], s, -jnp.inf)
        p = jax.nn.softmax(s, axis=-1)
        return jnp.einsum('bhqk,bkhd->bqhd', p.astype(v.dtype), v, preferred_element_type=jnp.float32)

    out = lax.map(block, (jnp.arange(nb), qb, cb))
    return jnp.moveaxis(out, 0, 1).reshape(B, S, H, Dh)


def diff_attention(q, k, v, lam, rel_bias):
    B, S, H, _, Dh = q.shape
    nb = S // BLOCK_Q
    scale = Dh ** -0.5
    kpos = jnp.arange(S)
    table = rel_bias.astype(jnp.float32)
    qb = jnp.moveaxis(q.reshape(B, nb, BLOCK_Q, H, 2, Dh), 1, 0)

    def block(args):
        i, qi = args
        qpos = i * BLOCK_Q + jnp.arange(BLOCK_Q)
        dist = qpos[:, None] - kpos[None, :]
        bias = jnp.moveaxis(table[t5_causal_bucket(dist)], -1, 0)
        s = jnp.einsum('bqhcd,bkhcd->bhcqk', qi, k, preferred_element_type=jnp.float32) * scale
        s = s + bias[None, :, None]
        s = jnp.where(dist >= 0, s, -jnp.inf)
        p = jax.nn.softmax(s, axis=-1)
        a = p[:, :, 0] - lam * p[:, :, 1]
        return jnp.einsum('bhqk,bkhe->bqhe', a.astype(v.dtype), v, preferred_element_type=jnp.float32)

    out = lax.map(block, (jnp.arange(nb), qb))
    return jnp.moveaxis(out, 0, 1).reshape(B, S, H, v.shape[-1])


def token_mixer(u, w_in, b_f, lq1, lk1, lq2, lk2, subln_g, w_br_fox, w_br_diff, w_out, rel_bias, lam_init):
    B, S, _ = u.shape
    f32 = jnp.float32
    proj = jnp.einsum('bsd,dn->bsn', u, w_in)
    fq, fk, fv, fg, dq, dk, dv, ga, gb = jnp.split(proj, IN_SPLIT_POINTS, axis=-1)
    log_f = jax.nn.log_sigmoid(fg.astype(f32) + b_f.astype(f32))
    y_fox = fox_attention(fq.reshape(B, S, FOX_HEADS, HEAD_DIM), fk.reshape(B, S, FOX_HEADS, HEAD_DIM),
                          fv.reshape(B, S, FOX_HEADS, HEAD_DIM), log_f)
    y_fox = y_fox.reshape(B, S, FOX_WIDTH).astype(u.dtype)
    lam = (jnp.exp(jnp.sum(lq1.astype(f32) * lk1.astype(f32)))
           - jnp.exp(jnp.sum(lq2.astype(f32) * lk2.astype(f32))) + lam_init)
    o = diff_attention(dq.reshape(B, S, DIFF_HEADS, 2, HEAD_DIM), dk.reshape(B, S, DIFF_HEADS, 2, HEAD_DIM),
                       dv.reshape(B, S, DIFF_HEADS, DIFF_V_DIM), lam, rel_bias)
    o = o * lax.rsqrt(jnp.mean(o * o, axis=-1, keepdims=True) + SUBLN_EPS) * subln_g.astype(f32) * (1.0 - lam_init)
    y_diff = o.reshape(B, S, DIFF_V_WIDTH).astype(u.dtype)
    branch_fox = y_fox @ w_br_fox
    branch_diff = y_diff @ w_br_diff
    merged = jax.nn.sigmoid(ga) * branch_fox + jax.nn.sigmoid(gb) * branch_diff
    return merged @ w_out


def swiglu(x, w_gate_up, w_down):
    g, up = jnp.split(x @ w_gate_up, 2, axis=-1)
    return (jax.nn.silu(g) * up) @ w_down


def moe_swiglu(x, w_router, w_gate_up, w_down):
    logits = (x @ w_router).astype(jnp.float32)
    top_v, top_i = lax.top_k(logits, TOP_K)
    gates = jax.nn.softmax(top_v, axis=-1)
    combine = jnp.sum(jax.nn.one_hot(top_i, N_EXPERTS, dtype=jnp.float32) * gates[..., None], axis=-2)
    y = jnp.zeros_like(x)
    for e in range(N_EXPERTS):
        y = y + combine[..., e:e + 1].astype(x.dtype) * swiglu(x, w_gate_up[e], w_down[e])
    return y


def setup_inputs(seed: int = 0) -> dict:
    key = jax.random.key(seed)
    ks = jax.random.split(key, 32)
    f32 = jnp.float32

    def nrm(k, shape, scale):
        return jax.random.normal(k, shape, f32) * scale

    col_scale = np.ones((N_IN,), np.float32)
    col_scale[FOX_V_OFFSET:FOX_V_OFFSET + FOX_WIDTH] = DEEPNORM_BETA
    col_scale[DIFF_V_OFFSET:DIFF_V_OFFSET + DIFF_V_WIDTH] = DEEPNORM_BETA
    w_in = nrm(ks[3], (DEPTH, D_MODEL, N_IN), D_MODEL ** -0.5) * jnp.asarray(col_scale)
    return {
        'x': nrm(ks[0], (BATCH, SEQ, D_MODEL), 1.0),
        'ln_in_g': 1.0 + nrm(ks[1], (D_MODEL,), 0.02),
        'ln_in_b': nrm(ks[2], (D_MODEL,), 0.02),
        'w_in': w_in,
        'b_fgate': FGATE_BIAS_INIT + nrm(ks[4], (DEPTH, FOX_HEADS), 0.5),
        'lam_q1': nrm(ks[5], (DEPTH, HEAD_DIM), 0.1),
        'lam_k1': nrm(ks[6], (DEPTH, HEAD_DIM), 0.1),
        'lam_q2': nrm(ks[7], (DEPTH, HEAD_DIM), 0.1),
        'lam_k2': nrm(ks[8], (DEPTH, HEAD_DIM), 0.1),
        'subln_g': 1.0 + nrm(ks[9], (DEPTH, DIFF_V_DIM), 0.02),
        'w_branch_fox': nrm(ks[10], (DEPTH, FOX_WIDTH, D_MODEL), FOX_WIDTH ** -0.5 * DEEPNORM_BETA),
        'w_branch_diff': nrm(ks[11], (DEPTH, DIFF_V_WIDTH, D_MODEL), DIFF_V_WIDTH ** -0.5 * DEEPNORM_BETA),
        'w_out': nrm(ks[12], (DEPTH, D_MODEL, D_MODEL), D_MODEL ** -0.5 * DEEPNORM_BETA),
        'ln_mix_g': 1.0 + nrm(ks[13], (DEPTH, D_MODEL), 0.02),
        'ln_mix_b': nrm(ks[14], (DEPTH, D_MODEL), 0.02),
        'rel_bias': nrm(ks[15], (REL_BUCKETS, DIFF_HEADS), 0.5),
        'w_ffn_gate_up': nrm(ks[16], (N_DENSE, D_MODEL, 2 * D_FF), D_MODEL ** -0.5 * DEEPNORM_BETA),
        'w_ffn_down': nrm(ks[17], (N_DENSE, D_FF, D_MODEL), D_FF ** -0.5 * DEEPNORM_BETA),
        'w_router': nrm(ks[18], (N_MOE, D_MODEL, N_EXPERTS), D_MODEL ** -0.5),
        'w_expert_gate_up': nrm(ks[19], (N_MOE, N_EXPERTS, D_MODEL, 2 * D_FF_EXPERT), D_MODEL ** -0.5 * DEEPNORM_BETA),
        'w_expert_down': nrm(ks[20], (N_MOE, N_EXPERTS, D_FF_EXPERT, D_MODEL), D_FF_EXPERT ** -0.5 * DEEPNORM_BETA),
        'ln_ffn_g': 1.0 + nrm(ks[21], (DEPTH, D_MODEL), 0.02),
        'ln_ffn_b': nrm(ks[22], (DEPTH, D_MODEL), 0.02),
    }


def reference(x, ln_in_g, ln_in_b, w_in, b_fgate, lam_q1, lam_k1, lam_q2, lam_k2, subln_g,
              w_branch_fox, w_branch_diff, w_out, ln_mix_g, ln_mix_b, rel_bias,
              w_ffn_gate_up, w_ffn_down, w_router, w_expert_gate_up, w_expert_down,
              ln_ffn_g, ln_ffn_b):
    h = layer_norm(x, ln_in_g, ln_in_b)
    for l in range(DEPTH):
        lam_init = 0.8 - 0.6 * math.exp(-0.3 * l)
        mix = token_mixer(h, w_in[l], b_fgate[l], lam_q1[l], lam_k1[l], lam_q2[l], lam_k2[l], subln_g[l],
                          w_branch_fox[l], w_branch_diff[l], w_out[l], rel_bias, lam_init)
        h = layer_norm(DEEPNORM_ALPHA * h + mix, ln_mix_g[l], ln_mix_b[l])
        if l % 2 == 0:
            f = swiglu(h, w_ffn_gate_up[l // 2], w_ffn_down[l // 2])
        else:
            f = moe_swiglu(h, w_router[l // 2], w_expert_gate_up[l // 2], w_expert_down[l // 2])
        h = layer_norm(DEEPNORM_ALPHA * h + f, ln_ffn_g[l], ln_ffn_b[l])
    return h
```

```python
import functools
import math

import jax
import jax.numpy as jnp
from jax import lax
from jax.experimental import pallas as pl
from jax.experimental.pallas import tpu as pltpu

F32 = jnp.float32
BF16 = jnp.bfloat16

D_MODEL = 1024
HEAD_DIM = 64
FOX_HEADS = 8
FOX_WIDTH = FOX_HEADS * HEAD_DIM
DIFF_HEADS = 4
DIFF_QK_WIDTH = DIFF_HEADS * 2 * HEAD_DIM
DIFF_V_DIM = 2 * HEAD_DIM
DIFF_V_WIDTH = DIFF_HEADS * DIFF_V_DIM
IN_SPLIT_SIZES = (FOX_WIDTH, FOX_WIDTH, FOX_WIDTH, FOX_HEADS, DIFF_QK_WIDTH, DIFF_QK_WIDTH, DIFF_V_WIDTH,
                  D_MODEL, D_MODEL)
REL_BUCKETS = 32
REL_MAX_DISTANCE = 128
N_EXPERTS = 8
LN_EPS = 1e-5
SUBLN_EPS = 1e-5

LANES = 128
LOG2E = 1.4426950408889634
Q_SCALE = HEAD_DIM ** -0.5 * LOG2E
NEG = -1e30
VMEM_LIMIT = 56 * 1024 * 1024


def _cparams(sem):
    return pltpu.CompilerParams(dimension_semantics=sem, vmem_limit_bytes=VMEM_LIMIT)


def _layer_norm(x, g, b):
    mu = jnp.mean(x, axis=-1, keepdims=True)
    xc = x - mu
    var = jnp.mean(xc * xc, axis=-1, keepdims=True)
    return xc * lax.rsqrt(var + LN_EPS) * g + b


def _sigmoid(x):
    return 1.0 / (1.0 + jnp.exp(-x))


def _split3(x):
    hi = x.astype(BF16)
    r1 = x - hi.astype(F32)
    mid = r1.astype(BF16)
    lo = (r1 - mid.astype(F32)).astype(BF16)
    return hi, mid, lo


_C_FQ, _C_FK, _C_FV, _C_DQ, _C_DK, _C_DV, _C_GA, _C_GB, _C_FG, _C_END = (
    0, 512, 1024, 1536, 2048, 2560, 3072, 4096, 5120, 5248)


def _inproj_kernel(x_ref, g_ref, b_ref, w_ref, fq, fk, fv, dq, dk, dv, sa, sb, fg, *h_out, do_ln):
    x = x_ref[...]
    if do_ln:
        x = _layer_norm(x, g_ref[...], b_ref[...])
        h_out[0][...] = x
    xb = x.astype(BF16)

    def mm(a, b):
        return jnp.dot(xb, w_ref[:, a:b], preferred_element_type=F32)

    fq[...] = (mm(_C_FQ, _C_FK) * Q_SCALE).astype(BF16)
    fk[...] = mm(_C_FK, _C_FV).astype(BF16)
    fv[...] = mm(_C_FV, _C_DQ).astype(BF16)
    dq[...] = (mm(_C_DQ, _C_DK) * Q_SCALE).astype(BF16)
    dk[...] = mm(_C_DK, _C_DV).astype(BF16)
    dv[...] = mm(_C_DV, _C_GA).astype(BF16)
    sa[...] = _sigmoid(mm(_C_GA, _C_GB)).astype(BF16)
    sb[...] = _sigmoid(mm(_C_GB, _C_FG)).astype(BF16)
    fg[...] = mm(_C_FG, _C_END)


def _inproj(x, g, b, w, *, do_ln, tm):
    S = x.shape[0]
    row = lambda n: pl.BlockSpec((tm, n), lambda i: (i, 0))
    full = lambda a: pl.BlockSpec(a.shape, lambda i: (0,) * a.ndim)
    out_shape = [jax.ShapeDtypeStruct((S, 512), BF16)] * 6 + [jax.ShapeDtypeStruct((S, D_MODEL), BF16)] * 2 + [
        jax.ShapeDtypeStruct((S, LANES), F32)]
    out_specs = [row(512)] * 6 + [row(D_MODEL)] * 2 + [row(LANES)]
    if do_ln:
        out_shape.append(jax.ShapeDtypeStruct((S, D_MODEL), F32))
        out_specs.append(row(D_MODEL))
    return pl.pallas_call(
        functools.partial(_inproj_kernel, do_ln=do_ln),
        grid=(S // tm,),
        in_specs=[row(D_MODEL), full(g), full(b), full(w)],
        out_specs=out_specs,
        out_shape=out_shape,
        compiler_params=_cparams(("arbitrary",)),
        name="inproj",
    )(x, g, b, w)


def _prep_kernel(fq_ref, fk_ref, fv_ref, fg_ref, bf_ref, qa_ref, ka_ref, va_ref, carry_ref, *, tb):
    @pl.when(pl.program_id(0) == 0)
    def _():
        carry_ref[...] = jnp.zeros_like(carry_ref)

    z = fg_ref[...] + bf_ref[...]
    logf = jnp.minimum(z, 0.0) - jnp.log1p(jnp.exp(-jnp.abs(z)))
    r = lax.broadcasted_iota(jnp.int32, (tb, tb), 0)
    c = lax.broadcasted_iota(jnp.int32, (tb, tb), 1)
    tri = (c <= r).astype(BF16)
    hi, mid, lo = _split3(logf)
    dot = lambda a: jnp.dot(tri, a, preferred_element_type=F32)
    cum = (dot(hi) + dot(mid)) + dot(lo) + carry_ref[0:1, :]
    carry_ref[0:1, :] = cum[tb - 1:tb, :]

    c_hi, c_mid, c_lo = (t.astype(F32) for t in _split3(cum * LOG2E))
    lane = lax.broadcasted_iota(jnp.int32, (tb, LANES), 1)
    one = jnp.ones((tb, LANES), F32)
    zero = jnp.zeros((tb, LANES), F32)
    for h in range(FOX_HEADS):
        p, odd = divmod(h, 2)
        col = lambda t: jnp.broadcast_to(t[:, h:h + 1], (tb, LANES))
        chi, cmid, clo = col(c_hi), col(c_mid), col(c_lo)

        def head_slab(ref):
            slab = ref[:, p * LANES:(p + 1) * LANES].astype(F32)
            return pltpu.roll(slab, HEAD_DIM, 1) if odd else slab

        qa = jnp.where(lane < 64, head_slab(fq_ref),
                       jnp.where(lane == 64, chi, jnp.where(lane == 65, cmid, jnp.where(lane == 66, clo,
                                 jnp.where(lane < 70, one, zero)))))
        ka = jnp.where(lane < 64, head_slab(fk_ref),
                       jnp.where(lane < 67, one, jnp.where(lane == 67, -chi, jnp.where(lane == 68, -cmid,
                                 jnp.where(lane == 69, -clo, zero)))))
        qa_ref[h] = qa.astype(BF16)
        ka_ref[h] = ka.astype(BF16)
        vslab = fv_ref[:, p * LANES:(p + 1) * LANES].astype(F32)
        if odd:
            va = jnp.where(lane >= 64, vslab, jnp.where(lane == 0, one, zero))
        else:
            va = jnp.where(lane < 64, vslab, jnp.where(lane == 64, one, zero))
        va_ref[h] = va.astype(BF16)


def _prep(fq, fk, fv, fg, bf, *, tb):
    S = fq.shape[0]
    row = lambda n: pl.BlockSpec((tb, n), lambda i: (i, 0))
    hspec = pl.BlockSpec((FOX_HEADS, tb, LANES), lambda i: (0, i, 0))
    hshape = jax.ShapeDtypeStruct((FOX_HEADS, S, LANES), BF16)
    return pl.pallas_call(
        functools.partial(_prep_kernel, tb=tb),
        grid=(S // tb,),
        in_specs=[row(512), row(512), row(512), row(LANES), pl.BlockSpec((1, LANES), lambda i: (0, 0))],
        out_specs=[hspec] * 3,
        out_shape=[hshape] * 3,
        scratch_shapes=[pltpu.VMEM((8, LANES), F32)],
        compiler_params=_cparams(("arbitrary",)),
        name="fox_prep",
    )(fq, fk, fv, fg, bf)


def _nt_dot(a, b):
    return lax.dot_general(a, b, (((1,), (1,)), ((), ())), preferred_element_type=F32)


def _fox_kernel(q_ref, k_ref, v_ref, o_ref, m_sc, acc_sc, *, t):
    i = pl.program_id(1)
    m_sc[...] = jnp.full_like(m_sc, NEG)
    acc_sc[...] = jnp.zeros_like(acc_sc)

    def tile(j, masked):
        off = pl.multiple_of(j * t, t)
        for hh in range(2):
            s = _nt_dot(q_ref[hh], k_ref[hh, pl.ds(off, t), :])
            if masked:
                r = lax.broadcasted_iota(jnp.int32, (t, t), 0)
                c = lax.broadcasted_iota(jnp.int32, (t, t), 1)
                s = jnp.where(c <= r, s, NEG)
            m_prev = m_sc[hh]
            m_next = jnp.maximum(m_prev, jnp.max(s, axis=1, keepdims=True))
            alpha = jnp.exp2(m_prev - m_next)
            p = jnp.exp2(s - m_next).astype(BF16)
            pv = jnp.dot(p, v_ref[hh, pl.ds(off, t), :], preferred_element_type=F32)
            acc_sc[hh] = alpha * acc_sc[hh] + pv
            m_sc[hh] = m_next

    def body(j, carry):
        tile(j, False)
        return carry

    lax.fori_loop(0, i, body, 0)
    tile(i, True)

    a0 = acc_sc[0]
    a1 = acc_sc[1]
    lane = lax.broadcasted_iota(jnp.int32, (t, LANES), 1)
    o_ref[...] = jnp.where(lane < 64, a0 / a0[:, 64:65], a1 / a1[:, 0:1]).astype(o_ref.dtype)


def _fox_attention(qa, ka, va, *, t):
    S = qa.shape[1]
    return pl.pallas_call(
        functools.partial(_fox_kernel, t=t),
        grid=(FOX_HEADS // 2, S // t),
        in_specs=[pl.BlockSpec((2, t, LANES), lambda p, i: (p, i, 0)),
                  pl.BlockSpec((2, S, LANES), lambda p, i: (p, 0, 0)),
                  pl.BlockSpec((2, S, LANES), lambda p, i: (p, 0, 0))],
        out_specs=pl.BlockSpec((t, LANES), lambda p, i: (i, p)),
        out_shape=jax.ShapeDtypeStruct((S, FOX_WIDTH), BF16),
        scratch_shapes=[pltpu.VMEM((2, t, 1), F32), pltpu.VMEM((2, t, LANES), F32)],
        compiler_params=_cparams(("arbitrary", "arbitrary")),
        name="fox_attn",
    )(qa, ka, va)


def _bias_tile_kernel(table_ref, o_ref, *, t):
    h = pl.program_id(0)
    kind = pl.program_id(1)
    r = lax.broadcasted_iota(jnp.int32, (t, t), 0)
    c = lax.broadcasted_iota(jnp.int32, (t, t), 1)
    dist = r - c + kind * t
    n = jnp.maximum(dist, 0)
    max_exact = REL_BUCKETS // 2
    nf = jnp.maximum(n, 1).astype(F32)
    log_part = jnp.log(nf / max_exact) / math.log(REL_MAX_DISTANCE / max_exact) * (REL_BUCKETS - max_exact)
    large = jnp.minimum(max_exact + log_part.astype(jnp.int32), REL_BUCKETS - 1)
    bucket = jnp.where(n < max_exact, n, large)
    val = jnp.zeros((t, t), F32)
    for b in range(REL_BUCKETS):
        val = jnp.where(bucket == b, table_ref[h * REL_BUCKETS + b], val)
    o_ref[0, 0] = jnp.where(dist >= 0, val * LOG2E, NEG)


def _bias_tiles(table_flat, *, t):
    return pl.pallas_call(
        functools.partial(_bias_tile_kernel, t=t),
        grid=(DIFF_HEADS, 2),
        in_specs=[pl.BlockSpec(memory_space=pltpu.SMEM)],
        out_specs=pl.BlockSpec((1, 1, t, t), lambda h, k: (h, k, 0, 0)),
        out_shape=jax.ShapeDtypeStruct((DIFF_HEADS, 2, t, t), F32),
        compiler_params=_cparams(("arbitrary", "arbitrary")),
        name="bias_tiles",
    )(table_flat)


def _diff_kernel(q_ref, k_ref, v_ref, bias_ref, far_ref, lam_ref, g_ref, o_ref, m_sc, l_sc, acc_sc,
                 *, t, lam_init):
    i = pl.program_id(1)
    m_sc[...] = jnp.full_like(m_sc, NEG)
    l_sc[...] = jnp.zeros_like(l_sc)
    acc_sc[...] = jnp.zeros_like(acc_sc)

    lane = lax.broadcasted_iota(jnp.int32, (t, LANES), 1)
    q = q_ref[...]
    zero = jnp.zeros_like(q)
    qs = (jnp.where(lane < 64, q, zero), jnp.where(lane >= 64, q, zero))
    far = far_ref[0][:, 0:1]

    def tile(j, kind):
        off = pl.multiple_of(j * t, t)
        k = k_ref[pl.ds(off, t), :]
        v = v_ref[pl.ds(off, t), :]
        for c in range(2):
            s = _nt_dot(qs[c], k)
            m_prev = m_sc[c]
            if kind is None:
                m_next = jnp.maximum(m_prev, jnp.max(s, axis=1, keepdims=True) + far)
                p = jnp.exp2(s - (m_next - far))
            else:
                s = s + bias_ref[0, kind]
                m_next = jnp.maximum(m_prev, jnp.max(s, axis=1, keepdims=True))
                p = jnp.exp2(s - m_next)
            alpha = jnp.exp2(m_prev - m_next)
            l_sc[c] = alpha * l_sc[c] + jnp.sum(p, axis=1, keepdims=True)
            acc_sc[c] = alpha * acc_sc[c] + jnp.dot(p.astype(BF16), v, preferred_element_type=F32)
            m_sc[c] = m_next

    def body(j, carry):
        tile(j, None)
        return carry

    lax.fori_loop(0, i - 1, body, 0)

    @pl.when(i > 0)
    def _():
        tile(i - 1, 1)

    tile(i, 0)

    lv = lam_ref[...]
    lam = (jnp.exp(jnp.sum(lv[0:1] * lv[1:2], axis=1, keepdims=True))
           - jnp.exp(jnp.sum(lv[2:3] * lv[3:4], axis=1, keepdims=True)) + lam_init)
    o = acc_sc[0] / l_sc[0] - lam * (acc_sc[1] / l_sc[1])
    o = o * lax.rsqrt(jnp.mean(o * o, axis=1, keepdims=True) + SUBLN_EPS) * g_ref[...] * (1.0 - lam_init)
    o_ref[...] = o.astype(o_ref.dtype)


def _diff_attention(dq, dk, dv, bias, far, lamvec, g, *, t, lam_init):
    S = dq.shape[0]
    return pl.pallas_call(
        functools.partial(_diff_kernel, t=t, lam_init=lam_init),
        grid=(DIFF_HEADS, S // t),
        in_specs=[pl.BlockSpec((t, LANES), lambda h, i: (i, h)),
                  pl.BlockSpec((S, LANES), lambda h, i: (0, h)),
                  pl.BlockSpec((S, LANES), lambda h, i: (0, h)),
                  pl.BlockSpec((1, 2, t, t), lambda h, i: (h, 0, 0, 0)),
                  pl.BlockSpec((1, 1, LANES), lambda h, i: (h, 0, 0)),
                  pl.BlockSpec((8, LANES), lambda h, i: (0, 0)),
                  pl.BlockSpec((1, LANES), lambda h, i: (0, 0))],
        out_specs=pl.BlockSpec((t, LANES), lambda h, i: (i, h)),
        out_shape=jax.ShapeDtypeStruct((S, DIFF_V_WIDTH), BF16),
        scratch_shapes=[pltpu.VMEM((2, t, 1), F32), pltpu.VMEM((2, t, 1), F32), pltpu.VMEM((2, t, LANES), F32)],
        compiler_params=_cparams(("arbitrary", "arbitrary")),
        name="diff_attn",
    )(dq, dk, dv, bias, far, lamvec, g)


def _merge_kernel(yf_ref, yd_ref, sa_ref, sb_ref, h_ref, wf_ref, wd_ref, wo_ref, g_ref, b_ref, o_ref, *, alpha):
    bf = jnp.dot(yf_ref[...], wf_ref[...], preferred_element_type=F32)
    bd = jnp.dot(yd_ref[...], wd_ref[...], preferred_element_type=F32)
    merged = sa_ref[...].astype(F32) * bf + sb_ref[...].astype(F32) * bd
    mix = jnp.dot(merged.astype(BF16), wo_ref[...], preferred_element_type=F32)
    o_ref[...] = _layer_norm(alpha * h_ref[...] + mix, g_ref[...], b_ref[...])


def _merge(yf, yd, sa, sb, h, wf, wd, wo, g, b, *, alpha, tm):
    S = h.shape[0]
    row = lambda n: pl.BlockSpec((tm, n), lambda i: (i, 0))
    full = lambda a: pl.BlockSpec(a.shape, lambda i: (0,) * a.ndim)
    return pl.pallas_call(
        functools.partial(_merge_kernel, alpha=alpha),
        grid=(S // tm,),
        in_specs=[row(512), row(512), row(D_MODEL), row(D_MODEL), row(D_MODEL),
                  full(wf), full(wd), full(wo), full(g), full(b)],
        out_specs=row(D_MODEL),
        out_shape=jax.ShapeDtypeStruct((S, D_MODEL), F32),
        compiler_params=_cparams(("arbitrary",)),
        name="merge_out",
    )(yf, yd, sa, sb, h, wf, wd, wo, g, b)


def _router_kernel(h_ref, whi_ref, wlo_ref, cw_ref):
    h = h_ref[...]
    hhi = h.astype(BF16)
    hlo = (h - hhi.astype(F32)).astype(BF16)
    whi = whi_ref[...]
    dot = lambda a, b: jnp.dot(a, b, preferred_element_type=F32)
    logits = dot(hhi, whi) + dot(hhi, wlo_ref[...]) + dot(hlo, whi)
    lane = lax.broadcasted_iota(jnp.int32, logits.shape, 1).astype(F32)
    lg = jnp.where(lane < N_EXPERTS, logits, NEG)
    m1 = jnp.max(lg, axis=1, keepdims=True)
    i1 = jnp.min(jnp.where(lg == m1, lane, float(LANES)), axis=1, keepdims=True)
    lg2 = jnp.where(lane == i1, NEG, lg)
    m2 = jnp.max(lg2, axis=1, keepdims=True)
    i2 = jnp.min(jnp.where(lg2 == m2, lane, float(LANES)), axis=1, keepdims=True)
    e = jnp.exp(m2 - m1)
    g1 = 1.0 / (1.0 + e)
    cw_ref[...] = jnp.where(lane == i1, g1, 0.0) + jnp.where(lane == i2, e * g1, 0.0)


def _router(h, whi, wlo, *, tm):
    S = h.shape[0]
    return pl.pallas_call(
        _router_kernel,
        grid=(S // tm,),
        in_specs=[pl.BlockSpec((tm, D_MODEL), lambda i: (i, 0)),
                  pl.BlockSpec(whi.shape, lambda i: (0, 0)), pl.BlockSpec(wlo.shape, lambda i: (0, 0))],
        out_specs=pl.BlockSpec((tm, LANES), lambda i: (i, 0)),
        out_shape=jax.ShapeDtypeStruct((S, LANES), F32),
        compiler_params=_cparams(("arbitrary",)),
        name="router",
    )(h, whi, wlo)


def _ffn_kernel(*refs, alpha, routed):
    if routed:
        h_ref, cw_ref, wg_ref, wu_ref, wd_ref, g_ref, b_ref, o_ref, hb_sc, acc_sc = refs
    else:
        h_ref, wg_ref, wu_ref, wd_ref, g_ref, b_ref, o_ref, hb_sc, acc_sc = refs
    e = pl.program_id(1)
    c = pl.program_id(2)
    first = jnp.logical_and(e == 0, c == 0)
    last = jnp.logical_and(e == pl.num_programs(1) - 1, c == pl.num_programs(2) - 1)

    @pl.when(first)
    def _():
        hb_sc[...] = h_ref[...].astype(BF16)
        acc_sc[...] = jnp.zeros_like(acc_sc)

    hb = hb_sc[...]
    gate = jnp.dot(hb, wg_ref[0], preferred_element_type=F32)
    up = jnp.dot(hb, wu_ref[0], preferred_element_type=F32)
    a = gate * _sigmoid(gate) * up
    if routed:
        cw = cw_ref[...]
        lane = lax.broadcasted_iota(jnp.int32, cw.shape, 1)
        a = a * jnp.sum(jnp.where(lane == e, cw, 0.0), axis=1, keepdims=True)
    acc_sc[...] += jnp.dot(a.astype(BF16), wd_ref[0], preferred_element_type=F32)

    @pl.when(last)
    def _():
        o_ref[...] = _layer_norm(alpha * h_ref[...] + acc_sc[...], g_ref[...], b_ref[...])


def _ffn(h, cw, w_gu, w_dn, g, b, *, alpha, tm, fc):
    S = h.shape[0]
    E, _, F2 = w_gu.shape
    nc = F2 // 2 // fc
    routed = cw is not None
    in_specs = [pl.BlockSpec((tm, D_MODEL), lambda i, e, c: (i, 0))]
    args = [h]
    if routed:
        in_specs.append(pl.BlockSpec((tm, LANES), lambda i, e, c: (i, 0)))
        args.append(cw)
    in_specs += [pl.BlockSpec((1, D_MODEL, fc), lambda i, e, c: (e, 0, c)),
                 pl.BlockSpec((1, D_MODEL, fc), lambda i, e, c: (e, 0, nc + c)),
                 pl.BlockSpec((1, fc, D_MODEL), lambda i, e, c: (e, c, 0)),
                 pl.BlockSpec(g.shape, lambda i, e, c: (0, 0)), pl.BlockSpec(b.shape, lambda i, e, c: (0, 0))]
    args += [w_gu, w_gu, w_dn, g, b]
    return pl.pallas_call(
        functools.partial(_ffn_kernel, alpha=alpha, routed=routed),
        grid=(S // tm, E, nc),
        in_specs=in_specs,
        out_specs=pl.BlockSpec((tm, D_MODEL), lambda i, e, c: (i, 0)),
        out_shape=jax.ShapeDtypeStruct((S, D_MODEL), F32),
        scratch_shapes=[pltpu.VMEM((tm, D_MODEL), BF16), pltpu.VMEM((tm, D_MODEL), F32)],
        compiler_params=_cparams(("arbitrary", "arbitrary", "arbitrary")),
        name="moe_ffn" if routed else "dense_ffn",
    )(*args)


def _pad_lanes(a, n=LANES):
    return jnp.pad(a, [(0, 0)] * (a.ndim - 1) + [(0, n - a.shape[-1])])


def kernel(x, ln_in_g, ln_in_b, w_in, b_fgate, lam_q1, lam_k1, lam_q2, lam_k2, subln_g, w_branch_fox, w_branch_diff, w_out, ln_mix_g, ln_mix_b, rel_bias, w_ffn_gate_up, w_ffn_down, w_router, w_expert_gate_up, w_expert_down, ln_ffn_g, ln_ffn_b):
    B, S, _ = x.shape
    assert B == 1
    depth = w_in.shape[0]
    alpha = (2.0 * depth) ** 0.25
    t = min(512, S)
    tm = min(512, S)
    tf = min(1024, S)
    assert t >= LANES and S % t == 0 and S % tf == 0
    row = lambda v: v.reshape(1, -1).astype(F32)

    table = rel_bias.astype(F32)
    bias = _bias_tiles(table.T.reshape(-1), t=t)
    far = jnp.broadcast_to((table[REL_BUCKETS - 1] * LOG2E)[:, None, None], (DIFF_HEADS, 1, LANES))

    h = x[0]
    for l in range(depth):
        lam_init = 0.8 - 0.6 * math.exp(-0.3 * l)
        fq, fk, fv, fg, dq, dk, dv, ga, gb = jnp.split(w_in[l], list(_cumsum(IN_SPLIT_SIZES))[:-1], axis=-1)
        w_cat = jnp.concatenate([fq, fk, fv, dq, dk, dv, ga, gb, _pad_lanes(fg)], axis=-1).astype(BF16)
        outs = _inproj(h, row(ln_in_g), row(ln_in_b), w_cat, do_ln=(l == 0), tm=tm)
        fq, fk, fv, dq, dk, dv, sa, sb, fg = outs[:9]
        if l == 0:
            h = outs[9]
        qa, ka, va = _prep(fq, fk, fv, fg, _pad_lanes(row(b_fgate[l])), tb=tm)
        y_fox = _fox_attention(qa, ka, va, t=t)
        lamvec = jnp.zeros((8, LANES), F32).at[0:4, 0:HEAD_DIM].set(
            jnp.stack([lam_q1[l], lam_k1[l], lam_q2[l], lam_k2[l]]).astype(F32))
        y_diff = _diff_attention(dq, dk, dv, bias, far, lamvec, row(subln_g[l]), t=t, lam_init=lam_init)
        h = _merge(y_fox, y_diff, sa, sb, h, w_branch_fox[l].astype(BF16), w_branch_diff[l].astype(BF16),
                   w_out[l].astype(BF16), row(ln_mix_g[l]), row(ln_mix_b[l]), alpha=alpha, tm=tm)
        g, b = row(ln_ffn_g[l]), row(ln_ffn_b[l])
        if l % 2 == 0:
            h = _ffn(h, None, w_ffn_gate_up[l // 2][None].astype(BF16), w_ffn_down[l // 2][None].astype(BF16),
                     g, b, alpha=alpha, tm=tf, fc=1408)
        else:
            wr = _pad_lanes(w_router[l // 2].astype(F32))
            whi = wr.astype(BF16)
            wlo = (wr - whi.astype(F32)).astype(BF16)
            cw = _router(h, whi, wlo, tm=tf)
            h = _ffn(h, cw, w_expert_gate_up[l // 2].astype(BF16), w_expert_down[l // 2].astype(BF16),
                     g, b, alpha=alpha, tm=tf, fc=896)
    return h[None]


def _cumsum(sizes):
    tot = 0
    for s in sizes:
        tot += s
        yield tot
```

```python
import functools
import math

import jax
import jax.numpy as jnp
from jax import lax
from jax.experimental import pallas as pl
from jax.experimental.pallas import tpu as pltpu

F32 = jnp.float32
BF16 = jnp.bfloat16

D_MODEL = 1024
HEAD_DIM = 64
FOX_HEADS = 8
FOX_WIDTH = FOX_HEADS * HEAD_DIM
DIFF_HEADS = 4
DIFF_QK_WIDTH = DIFF_HEADS * 2 * HEAD_DIM
DIFF_V_DIM = 2 * HEAD_DIM
DIFF_V_WIDTH = DIFF_HEADS * DIFF_V_DIM
IN_SPLIT_SIZES = (FOX_WIDTH, FOX_WIDTH, FOX_WIDTH, FOX_HEADS, DIFF_QK_WIDTH, DIFF_QK_WIDTH, DIFF_V_WIDTH,
                  D_MODEL, D_MODEL)
REL_BUCKETS = 32
REL_MAX_DISTANCE = 128
N_EXPERTS = 8
LN_EPS = 1e-5
SUBLN_EPS = 1e-5

LANES = 128
LOG2E = 1.4426950408889634
Q_SCALE = HEAD_DIM ** -0.5 * LOG2E
NEG = -1e30
VMEM_LIMIT = 56 * 1024 * 1024
ATTN_TQ = 2048
ATTN_TK = 512


def _cparams(sem):
    return pltpu.CompilerParams(dimension_semantics=sem, vmem_limit_bytes=VMEM_LIMIT)


def _layer_norm(x, g, b):
    mu = jnp.mean(x, axis=-1, keepdims=True)
    xc = x - mu
    var = jnp.mean(xc * xc, axis=-1, keepdims=True)
    return xc * lax.rsqrt(var + LN_EPS) * g + b


def _sigmoid(x):
    return 1.0 / (1.0 + jnp.exp(-x))


def _split3(x):
    hi = x.astype(BF16)
    r1 = x - hi.astype(F32)
    mid = r1.astype(BF16)
    lo = (r1 - mid.astype(F32)).astype(BF16)
    return hi, mid, lo


_C_FQ, _C_FK, _C_FV, _C_DQ, _C_DK, _C_DV, _C_GA, _C_GB, _C_FG, _C_END = (
    0, 512, 1024, 1536, 2048, 2560, 3072, 4096, 5120, 5248)


def _inproj_kernel(x_ref, g_ref, b_ref, w_ref, fq, fk, fv, dq, dk, dv, sa, sb, fg, *h_out, do_ln):
    x = x_ref[...]
    if do_ln:
        x = _layer_norm(x, g_ref[...], b_ref[...])
        h_out[0][...] = x
    xb = x.astype(BF16)

    def mm(a, b):
        return jnp.dot(xb, w_ref[:, a:b], preferred_element_type=F32)

    fq[...] = (mm(_C_FQ, _C_FK) * Q_SCALE).astype(BF16)
    fk[...] = mm(_C_FK, _C_FV).astype(BF16)
    fv[...] = mm(_C_FV, _C_DQ).astype(BF16)
    dq[...] = (mm(_C_DQ, _C_DK) * Q_SCALE).astype(BF16)
    dk[...] = mm(_C_DK, _C_DV).astype(BF16)
    dvv = mm(_C_DV, _C_GA).astype(BF16)
    lane = lax.broadcasted_iota(jnp.int32, (dvv.shape[0], LANES), 1)
    ones_col = jnp.where(lane == 0, 1.0, 0.0).astype(BF16)
    dv[...] = jnp.concatenate(
        [blk for h in range(DIFF_HEADS) for blk in (dvv[:, h * LANES:(h + 1) * LANES], ones_col)], axis=1)
    sa[...] = _sigmoid(mm(_C_GA, _C_GB)).astype(BF16)
    sb[...] = _sigmoid(mm(_C_GB, _C_FG)).astype(BF16)
    fg[...] = mm(_C_FG, _C_END)


def _inproj(x, g, b, w, *, do_ln, tm):
    S = x.shape[0]
    row = lambda n: pl.BlockSpec((tm, n), lambda i: (i, 0))
    full = lambda a: pl.BlockSpec(a.shape, lambda i: (0,) * a.ndim)
    out_shape = [jax.ShapeDtypeStruct((S, 512), BF16)] * 5 + [jax.ShapeDtypeStruct((S, D_MODEL), BF16)] * 3 + [
        jax.ShapeDtypeStruct((S, LANES), F32)]
    out_specs = [row(512)] * 5 + [row(D_MODEL)] * 3 + [row(LANES)]
    if do_ln:
        out_shape.append(jax.ShapeDtypeStruct((S, D_MODEL), F32))
        out_specs.append(row(D_MODEL))
    return pl.pallas_call(
        functools.partial(_inproj_kernel, do_ln=do_ln),
        grid=(S // tm,),
        in_specs=[row(D_MODEL), full(g), full(b), full(w)],
        out_specs=out_specs,
        out_shape=out_shape,
        compiler_params=_cparams(("arbitrary",)),
        name="inproj",
    )(x, g, b, w)


def _prep_kernel(fq_ref, fk_ref, fv_ref, fg_ref, bf_ref, qa_ref, ka_ref, va_ref, carry_ref, *, tb):
    @pl.when(pl.program_id(0) == 0)
    def _():
        carry_ref[...] = jnp.zeros_like(carry_ref)

    z = fg_ref[...] + bf_ref[...]
    logf = jnp.minimum(z, 0.0) - jnp.log1p(jnp.exp(-jnp.abs(z)))
    r = lax.broadcasted_iota(jnp.int32, (tb, tb), 0)
    c = lax.broadcasted_iota(jnp.int32, (tb, tb), 1)
    tri = (c <= r).astype(BF16)
    hi, mid, lo = _split3(logf)
    dot = lambda a: jnp.dot(tri, a, preferred_element_type=F32)
    cum = (dot(hi) + dot(mid)) + dot(lo) + carry_ref[0:1, :]
    carry_ref[0:1, :] = cum[tb - 1:tb, :]

    c_hi, c_mid, c_lo = (t.astype(F32) for t in _split3(cum * LOG2E))
    lane = lax.broadcasted_iota(jnp.int32, (tb, LANES), 1)
    one = jnp.ones((tb, LANES), F32)
    zero = jnp.zeros((tb, LANES), F32)
    for h in range(FOX_HEADS):
        p, odd = divmod(h, 2)
        col = lambda t: jnp.broadcast_to(t[:, h:h + 1], (tb, LANES))
        chi, cmid, clo = col(c_hi), col(c_mid), col(c_lo)

        def head_slab(ref):
            slab = ref[:, p * LANES:(p + 1) * LANES].astype(F32)
            return pltpu.roll(slab, HEAD_DIM, 1) if odd else slab

        qa = jnp.where(lane < 64, head_slab(fq_ref),
                       jnp.where(lane == 64, chi, jnp.where(lane == 65, cmid, jnp.where(lane == 66, clo,
                                 jnp.where(lane < 70, one, zero)))))
        ka = jnp.where(lane < 64, head_slab(fk_ref),
                       jnp.where(lane < 67, one, jnp.where(lane == 67, -chi, jnp.where(lane == 68, -cmid,
                                 jnp.where(lane == 69, -clo, zero)))))
        qa_ref[h] = qa.astype(BF16)
        ka_ref[h] = ka.astype(BF16)
        vslab = fv_ref[:, p * LANES:(p + 1) * LANES].astype(F32)
        if odd:
            va = jnp.where(lane >= 64, vslab, jnp.where(lane == 0, one, zero))
        else:
            va = jnp.where(lane < 64, vslab, jnp.where(lane == 64, one, zero))
        va_ref[h] = va.astype(BF16)


def _prep(fq, fk, fv, fg, bf, *, tb):
    S = fq.shape[0]
    row = lambda n: pl.BlockSpec((tb, n), lambda i: (i, 0))
    hspec = pl.BlockSpec((FOX_HEADS, tb, LANES), lambda i: (0, i, 0))
    hshape = jax.ShapeDtypeStruct((FOX_HEADS, S, LANES), BF16)
    return pl.pallas_call(
        functools.partial(_prep_kernel, tb=tb),
        grid=(S // tb,),
        in_specs=[row(512), row(512), row(512), row(LANES), pl.BlockSpec((1, LANES), lambda i: (0, 0))],
        out_specs=[hspec] * 3,
        out_shape=[hshape] * 3,
        scratch_shapes=[pltpu.VMEM((8, LANES), F32)],
        compiler_params=_cparams(("arbitrary",)),
        name="fox_prep",
    )(fq, fk, fv, fg, bf)


def _nt_dot(a, b):
    return lax.dot_general(a, b, (((1,), (1,)), ((), ())), preferred_element_type=F32)


def _fox_kernel(q_ref, k_ref, v_ref, o_ref, m_sc, acc_sc, *, tq, tk):
    i = pl.program_id(1)
    m_sc[...] = jnp.full_like(m_sc, NEG)
    acc_sc[...] = jnp.zeros_like(acc_sc)
    nsub = tq // tk

    def tile(j, r0, masked):
        off = pl.multiple_of(j * tk, tk)
        rows = tq - r0
        for hh in range(2):
            s = _nt_dot(q_ref[hh, r0:, :], k_ref[hh, pl.ds(off, tk), :])
            if masked:
                r = lax.broadcasted_iota(jnp.int32, (rows, tk), 0)
                c = lax.broadcasted_iota(jnp.int32, (rows, tk), 1)
                s = jnp.where(c <= r, s, NEG)
            m_prev = m_sc[hh, r0:, :]
            m_next = jnp.maximum(m_prev, jnp.max(s, axis=1, keepdims=True))
            alpha = jnp.exp2(m_prev - m_next)
            p = jnp.exp2(s - jnp.concatenate([m_next] * (tk // LANES), axis=1)).astype(BF16)
            pv = jnp.dot(p, v_ref[hh, pl.ds(off, tk), :], preferred_element_type=F32)
            acc_sc[hh, r0:, :] = alpha * acc_sc[hh, r0:, :] + pv
            m_sc[hh, r0:, :] = m_next

    def body(j, carry):
        tile(j, 0, False)
        return carry

    lax.fori_loop(0, i * nsub, body, 0)
    for u in range(nsub):
        tile(i * nsub + u, u * tk, True)

    a0 = acc_sc[0]
    a1 = acc_sc[1]
    lane = lax.broadcasted_iota(jnp.int32, (tq, LANES), 1)
    o_ref[...] = jnp.where(lane < 64, a0 / a0[:, 64:65], a1 / a1[:, 0:1]).astype(o_ref.dtype)


def _fox_attention(qa, ka, va, *, tq, tk):
    S = qa.shape[1]
    return pl.pallas_call(
        functools.partial(_fox_kernel, tq=tq, tk=tk),
        grid=(FOX_HEADS // 2, S // tq),
        in_specs=[pl.BlockSpec((2, tq, LANES), lambda p, i: (p, i, 0)),
                  pl.BlockSpec((2, S, LANES), lambda p, i: (p, 0, 0)),
                  pl.BlockSpec((2, S, LANES), lambda p, i: (p, 0, 0))],
        out_specs=pl.BlockSpec((tq, LANES), lambda p, i: (i, p)),
        out_shape=jax.ShapeDtypeStruct((S, FOX_WIDTH), BF16),
        scratch_shapes=[pltpu.VMEM((2, tq, LANES), F32), pltpu.VMEM((2, tq, LANES), F32)],
        compiler_params=_cparams(("arbitrary", "arbitrary")),
        name="fox_attn",
    )(qa, ka, va)


BIAS_KINDS = 3


def _bias_tile_kernel(table_ref, o_ref, *, t):
    h = pl.program_id(0)
    kind = pl.program_id(1)
    r = lax.broadcasted_iota(jnp.int32, (t, t), 0)
    c = lax.broadcasted_iota(jnp.int32, (t, t), 1)
    dist = r - c + kind * t
    n = jnp.maximum(dist, 0)
    max_exact = REL_BUCKETS // 2
    nf = jnp.maximum(n, 1).astype(F32)
    log_part = jnp.log(nf / max_exact) / math.log(REL_MAX_DISTANCE / max_exact) * (REL_BUCKETS - max_exact)
    large = jnp.minimum(max_exact + log_part.astype(jnp.int32), REL_BUCKETS - 1)
    bucket = jnp.where(n < max_exact, n, large)
    val = jnp.zeros((t, t), F32)
    for b in range(REL_BUCKETS):
        val = jnp.where(bucket == b, table_ref[h * REL_BUCKETS + b], val)
    o_ref[0, 0] = jnp.where(dist >= 0, val * LOG2E, NEG)


def _bias_tiles(table_flat, *, t):
    return pl.pallas_call(
        functools.partial(_bias_tile_kernel, t=t),
        grid=(DIFF_HEADS, BIAS_KINDS),
        in_specs=[pl.BlockSpec(memory_space=pltpu.SMEM)],
        out_specs=pl.BlockSpec((1, 1, t, t), lambda h, k: (h, k, 0, 0)),
        out_shape=jax.ShapeDtypeStruct((DIFF_HEADS, BIAS_KINDS, t, t), F32),
        compiler_params=_cparams(("arbitrary", "arbitrary")),
        name="bias_tiles",
    )(table_flat)


def _diff_kernel(q_ref, k_ref, v_ref, bias_ref, far_ref, lam_ref, g_ref, o_ref, m_sc, acc_sc,
                 *, tq, tk, lam_init):
    i = pl.program_id(1)
    m_sc[...] = jnp.full_like(m_sc, NEG)
    acc_sc[...] = jnp.zeros_like(acc_sc)
    nsub = tq // tk

    lane = lax.broadcasted_iota(jnp.int32, (tq, LANES), 1)
    q = q_ref[...]
    zero = jnp.zeros_like(q)
    qs = (jnp.where(lane < 64, q, zero), jnp.where(lane >= 64, q, zero))
    far = far_ref[0]

    def tile(j, r0, kinds):
        off = pl.multiple_of(j * tk, tk)
        k = k_ref[pl.ds(off, tk), :]
        v = v_ref[pl.ds(off, tk), :]
        for c in range(2):
            s = _nt_dot(qs[c][r0:], k)
            m_prev = m_sc[c, r0:, :]
            if kinds is None:
                m_next = jnp.maximum(m_prev, jnp.max(s, axis=1, keepdims=True) + far)
                shift = m_next - far
            else:
                s = jnp.concatenate([s[a * tk:(a + 1) * tk] + bias_ref[0, kd] for a, kd in enumerate(kinds)],
                                    axis=0)
                m_next = jnp.maximum(m_prev, jnp.max(s, axis=1, keepdims=True))
                shift = m_next
            p = jnp.exp2(s - jnp.concatenate([shift] * (tk // LANES), axis=1)).astype(BF16)
            alpha = jnp.exp2(m_prev - m_next)
            pv = jnp.dot(p, v, preferred_element_type=F32)
            acc_sc[c, r0:, :] = jnp.concatenate([alpha, alpha], axis=1) * acc_sc[c, r0:, :] + pv
            m_sc[c, r0:, :] = m_next

    def body(j, carry):
        tile(j, 0, None)
        return carry

    lax.fori_loop(0, i * nsub - 1, body, 0)

    @pl.when(i > 0)
    def _():
        tile(i * nsub - 1, 0, (1,) + (2,) * (nsub - 1))

    for u in range(nsub):
        tile(i * nsub + u, u * tk, ((0, 1) + (2,) * nsub)[:nsub - u])

    lv = lam_ref[...]
    lam = (jnp.exp(jnp.sum(lv[0:1] * lv[1:2], axis=1, keepdims=True))
           - jnp.exp(jnp.sum(lv[2:3] * lv[3:4], axis=1, keepdims=True)) + lam_init)
    a0 = acc_sc[0]
    a1 = acc_sc[1]
    o = a0[:, :LANES] / a0[:, LANES:LANES + 1] - lam * (a1[:, :LANES] / a1[:, LANES:LANES + 1])
    o = o * lax.rsqrt(jnp.mean(o * o, axis=1, keepdims=True) + SUBLN_EPS) * g_ref[...] * (1.0 - lam_init)
    o_ref[...] = o.astype(o_ref.dtype)


def _diff_attention(dq, dk, dv, bias, far, lamvec, g, *, tq, tk, lam_init):
    S = dq.shape[0]
    return pl.pallas_call(
        functools.partial(_diff_kernel, tq=tq, tk=tk, lam_init=lam_init),
        grid=(DIFF_HEADS, S // tq),
        in_specs=[pl.BlockSpec((tq, LANES), lambda h, i: (i, h)),
                  pl.BlockSpec((S, LANES), lambda h, i: (0, h)),
                  pl.BlockSpec((S, 2 * LANES), lambda h, i: (0, h)),
                  pl.BlockSpec((1, BIAS_KINDS, tk, tk), lambda h, i: (h, 0, 0, 0)),
                  pl.BlockSpec((1, 1, LANES), lambda h, i: (h, 0, 0)),
                  pl.BlockSpec((8, LANES), lambda h, i: (0, 0)),
                  pl.BlockSpec((1, LANES), lambda h, i: (0, 0))],
        out_specs=pl.BlockSpec((tq, LANES), lambda h, i: (i, h)),
        out_shape=jax.ShapeDtypeStruct((S, DIFF_V_WIDTH), BF16),
        scratch_shapes=[pltpu.VMEM((2, tq, LANES), F32), pltpu.VMEM((2, tq, 2 * LANES), F32)],
        compiler_params=_cparams(("arbitrary", "arbitrary")),
        name="diff_attn",
    )(dq, dk, dv, bias, far, lamvec, g)


def _merge_kernel(yf_ref, yd_ref, sa_ref, sb_ref, h_ref, wf_ref, wd_ref, wo_ref, g_ref, b_ref, o_ref, *, alpha):
    bf = jnp.dot(yf_ref[...], wf_ref[...], preferred_element_type=F32)
    bd = jnp.dot(yd_ref[...], wd_ref[...], preferred_element_type=F32)
    merged = sa_ref[...].astype(F32) * bf + sb_ref[...].astype(F32) * bd
    mix = jnp.dot(merged.astype(BF16), wo_ref[...], preferred_element_type=F32)
    o_ref[...] = _layer_norm(alpha * h_ref[...] + mix, g_ref[...], b_ref[...])


def _merge(yf, yd, sa, sb, h, wf, wd, wo, g, b, *, alpha, tm):
    S = h.shape[0]
    row = lambda n: pl.BlockSpec((tm, n), lambda i: (i, 0))
    full = lambda a: pl.BlockSpec(a.shape, lambda i: (0,) * a.ndim)
    return pl.pallas_call(
        functools.partial(_merge_kernel, alpha=alpha),
        grid=(S // tm,),
        in_specs=[row(512), row(512), row(D_MODEL), row(D_MODEL), row(D_MODEL),
                  full(wf), full(wd), full(wo), full(g), full(b)],
        out_specs=row(D_MODEL),
        out_shape=jax.ShapeDtypeStruct((S, D_MODEL), F32),
        compiler_params=_cparams(("arbitrary",)),
        name="merge_out",
    )(yf, yd, sa, sb, h, wf, wd, wo, g, b)


def _router_kernel(h_ref, whi_ref, wlo_ref, cw_ref):
    h = h_ref[...]
    hhi = h.astype(BF16)
    hlo = (h - hhi.astype(F32)).astype(BF16)
    whi = whi_ref[...]
    dot = lambda a, b: jnp.dot(a, b, preferred_element_type=F32)
    logits = dot(hhi, whi) + dot(hhi, wlo_ref[...]) + dot(hlo, whi)
    lane = lax.broadcasted_iota(jnp.int32, logits.shape, 1).astype(F32)
    lg = jnp.where(lane < N_EXPERTS, logits, NEG)
    m1 = jnp.max(lg, axis=1, keepdims=True)
    i1 = jnp.min(jnp.where(lg == m1, lane, float(LANES)), axis=1, keepdims=True)
    lg2 = jnp.where(lane == i1, NEG, lg)
    m2 = jnp.max(lg2, axis=1, keepdims=True)
    i2 = jnp.min(jnp.where(lg2 == m2, lane, float(LANES)), axis=1, keepdims=True)
    e = jnp.exp(m2 - m1)
    g1 = 1.0 / (1.0 + e)
    cw_ref[...] = jnp.where(lane == i1, g1, 0.0) + jnp.where(lane == i2, e * g1, 0.0)


def _router(h, whi, wlo, *, tm):
    S = h.shape[0]
    return pl.pallas_call(
        _router_kernel,
        grid=(S // tm,),
        in_specs=[pl.BlockSpec((tm, D_MODEL), lambda i: (i, 0)),
                  pl.BlockSpec(whi.shape, lambda i: (0, 0)), pl.BlockSpec(wlo.shape, lambda i: (0, 0))],
        out_specs=pl.BlockSpec((tm, LANES), lambda i: (i, 0)),
        out_shape=jax.ShapeDtypeStruct((S, LANES), F32),
        compiler_params=_cparams(("arbitrary",)),
        name="router",
    )(h, whi, wlo)


def _ffn_kernel(*refs, alpha, routed):
    if routed:
        h_ref, cw_ref, wg_ref, wu_ref, wd_ref, g_ref, b_ref, o_ref, hb_sc, acc_sc = refs
    else:
        h_ref, wg_ref, wu_ref, wd_ref, g_ref, b_ref, o_ref, hb_sc, acc_sc = refs
    e = pl.program_id(1)
    c = pl.program_id(2)
    first = jnp.logical_and(e == 0, c == 0)
    last = jnp.logical_and(e == pl.num_programs(1) - 1, c == pl.num_programs(2) - 1)

    @pl.when(first)
    def _():
        hb_sc[...] = h_ref[...].astype(BF16)
        acc_sc[...] = jnp.zeros_like(acc_sc)

    hb = hb_sc[...]
    gate = jnp.dot(hb, wg_ref[0], preferred_element_type=F32)
    up = jnp.dot(hb, wu_ref[0], preferred_element_type=F32)
    a = gate * _sigmoid(gate) * up
    if routed:
        cw = cw_ref[...]
        lane = lax.broadcasted_iota(jnp.int32, cw.shape, 1)
        a = a * jnp.sum(jnp.where(lane == e, cw, 0.0), axis=1, keepdims=True)
    acc_sc[...] += jnp.dot(a.astype(BF16), wd_ref[0], preferred_element_type=F32)

    @pl.when(last)
    def _():
        o_ref[...] = _layer_norm(alpha * h_ref[...] + acc_sc[...], g_ref[...], b_ref[...])


def _ffn(h, cw, w_gu, w_dn, g, b, *, alpha, tm, fc):
    S = h.shape[0]
    E, _, F2 = w_gu.shape
    nc = F2 // 2 // fc
    routed = cw is not None
    in_specs = [pl.BlockSpec((tm, D_MODEL), lambda i, e, c: (i, 0))]
    args = [h]
    if routed:
        in_specs.append(pl.BlockSpec((tm, LANES), lambda i, e, c: (i, 0)))
        args.append(cw)
    in_specs += [pl.BlockSpec((1, D_MODEL, fc), lambda i, e, c: (e, 0, c)),
                 pl.BlockSpec((1, D_MODEL, fc), lambda i, e, c: (e, 0, nc + c)),
                 pl.BlockSpec((1, fc, D_MODEL), lambda i, e, c: (e, c, 0)),
                 pl.BlockSpec(g.shape, lambda i, e, c: (0, 0)), pl.BlockSpec(b.shape, lambda i, e, c: (0, 0))]
    args += [w_gu, w_gu, w_dn, g, b]
    return pl.pallas_call(
        functools.partial(_ffn_kernel, alpha=alpha, routed=routed),
        grid=(S // tm, E, nc),
        in_specs=in_specs,
        out_specs=pl.BlockSpec((tm, D_MODEL), lambda i, e, c: (i, 0)),
        out_shape=jax.ShapeDtypeStruct((S, D_MODEL), F32),
        scratch_shapes=[pltpu.VMEM((tm, D_MODEL), BF16), pltpu.VMEM((tm, D_MODEL), F32)],
        compiler_params=_cparams(("arbitrary", "arbitrary", "arbitrary")),
        name="moe_ffn" if routed else "dense_ffn",
    )(*args)


def _pad_lanes(a, n=LANES):
    return jnp.pad(a, [(0, 0)] * (a.ndim - 1) + [(0, n - a.shape[-1])])


def kernel(x, ln_in_g, ln_in_b, w_in, b_fgate, lam_q1, lam_k1, lam_q2, lam_k2, subln_g, w_branch_fox, w_branch_diff, w_out, ln_mix_g, ln_mix_b, rel_bias, w_ffn_gate_up, w_ffn_down, w_router, w_expert_gate_up, w_expert_down, ln_ffn_g, ln_ffn_b):
    B, S, _ = x.shape
    assert B == 1
    depth = w_in.shape[0]
    alpha = (2.0 * depth) ** 0.25
    tq = min(ATTN_TQ, S)
    tk = min(ATTN_TK, tq)
    tm = min(512, S)
    tf = min(1024, S)
    assert tk >= REL_MAX_DISTANCE and tq % tk == 0 and S % tq == 0 and S % tf == 0 and S % tm == 0
    row = lambda v: v.reshape(1, -1).astype(F32)

    table = rel_bias.astype(F32)
    bias = _bias_tiles(table.T.reshape(-1), t=tk)
    far = jnp.broadcast_to((table[REL_BUCKETS - 1] * LOG2E)[:, None, None], (DIFF_HEADS, 1, LANES))

    h = x[0]
    for l in range(depth):
        lam_init = 0.8 - 0.6 * math.exp(-0.3 * l)
        fq, fk, fv, fg, dq, dk, dv, ga, gb = jnp.split(w_in[l], list(_cumsum(IN_SPLIT_SIZES))[:-1], axis=-1)
        w_cat = jnp.concatenate([fq, fk, fv, dq, dk, dv, ga, gb, _pad_lanes(fg)], axis=-1).astype(BF16)
        outs = _inproj(h, row(ln_in_g), row(ln_in_b), w_cat, do_ln=(l == 0), tm=tm)
        fq, fk, fv, dq, dk, dv, sa, sb, fg = outs[:9]
        if l == 0:
            h = outs[9]
        qa, ka, va = _prep(fq, fk, fv, fg, _pad_lanes(row(b_fgate[l])), tb=tm)
        y_fox = _fox_attention(qa, ka, va, tq=tq, tk=tk)
        lamvec = jnp.zeros((8, LANES), F32).at[0:4, 0:HEAD_DIM].set(
            jnp.stack([lam_q1[l], lam_k1[l], lam_q2[l], lam_k2[l]]).astype(F32))
        y_diff = _diff_attention(dq, dk, dv, bias, far, lamvec, row(subln_g[l]), tq=tq, tk=tk,
                                 lam_init=lam_init)
        h = _merge(y_fox, y_diff, sa, sb, h, w_branch_fox[l].astype(BF16), w_branch_diff[l].astype(BF16),
                   w_out[l].astype(BF16), row(ln_mix_g[l]), row(ln_mix_b[l]), alpha=alpha, tm=tm)
        g, b = row(ln_ffn_g[l]), row(ln_ffn_b[l])
        if l % 2 == 0:
            h = _ffn(h, None, w_ffn_gate_up[l // 2][None].astype(BF16), w_ffn_down[l // 2][None].astype(BF16),
                     g, b, alpha=alpha, tm=tf, fc=1408)
        else:
            wr = _pad_lanes(w_router[l // 2].astype(F32))
            whi = wr.astype(BF16)
            wlo = (wr - whi.astype(F32)).astype(BF16)
            cw = _router(h, whi, wlo, tm=tf)
            h = _ffn(h, cw, w_expert_gate_up[l // 2].astype(BF16), w_expert_down[l // 2].astype(BF16),
                     g, b, alpha=alpha, tm=tf, fc=896)
    return h[None]


def _cumsum(sizes):
    tot = 0
    for s in sizes:
        tot += s
        yield tot
```

```python
import functools
import math

import jax
import jax.numpy as jnp
from jax import lax
from jax.experimental import pallas as pl
from jax.experimental.pallas import tpu as pltpu

F32 = jnp.float32
BF16 = jnp.bfloat16

D_MODEL = 1024
HEAD_DIM = 64
FOX_HEADS = 8
FOX_WIDTH = FOX_HEADS * HEAD_DIM
DIFF_HEADS = 4
DIFF_QK_WIDTH = DIFF_HEADS * 2 * HEAD_DIM
DIFF_V_DIM = 2 * HEAD_DIM
DIFF_V_WIDTH = DIFF_HEADS * DIFF_V_DIM
IN_SPLIT_SIZES = (FOX_WIDTH, FOX_WIDTH, FOX_WIDTH, FOX_HEADS, DIFF_QK_WIDTH, DIFF_QK_WIDTH, DIFF_V_WIDTH,
                  D_MODEL, D_MODEL)
REL_BUCKETS = 32
REL_MAX_DISTANCE = 128
N_EXPERTS = 8
LN_EPS = 1e-5
SUBLN_EPS = 1e-5

LANES = 128
LOG2E = 1.4426950408889634
Q_SCALE = HEAD_DIM ** -0.5 * LOG2E
NEG = -1e30
VMEM_LIMIT = 56 * 1024 * 1024
ATTN_TQ = 2048
ATTN_TK = 512
MOE_TM = 2048


def _cparams(sem):
    return pltpu.CompilerParams(dimension_semantics=sem, vmem_limit_bytes=VMEM_LIMIT)


def _layer_norm(x, g, b):
    mu = jnp.mean(x, axis=-1, keepdims=True)
    xc = x - mu
    var = jnp.mean(xc * xc, axis=-1, keepdims=True)
    return xc * lax.rsqrt(var + LN_EPS) * g + b


def _sigmoid(x):
    return 1.0 / (1.0 + jnp.exp(-x))


def _split3(x):
    hi = x.astype(BF16)
    r1 = x - hi.astype(F32)
    mid = r1.astype(BF16)
    lo = (r1 - mid.astype(F32)).astype(BF16)
    return hi, mid, lo


_C_FQ, _C_FK, _C_FV, _C_DQ, _C_DK, _C_DV, _C_GA, _C_GB, _C_FG, _C_END = (
    0, 512, 1024, 1536, 2048, 2560, 3072, 4096, 5120, 5248)


def _inproj_kernel(x_ref, g_ref, b_ref, w_ref, fq, fk, fv, dq, dk, dv, sa, sb, fg, *h_out, do_ln):
    x = x_ref[...]
    if do_ln:
        x = _layer_norm(x, g_ref[...], b_ref[...])
        h_out[0][...] = x
    xb = x.astype(BF16)

    def mm(a, b):
        return jnp.dot(xb, w_ref[:, a:b], preferred_element_type=F32)

    fq[...] = (mm(_C_FQ, _C_FK) * Q_SCALE).astype(BF16)
    fk[...] = mm(_C_FK, _C_FV).astype(BF16)
    fv[...] = mm(_C_FV, _C_DQ).astype(BF16)
    dq[...] = (mm(_C_DQ, _C_DK) * Q_SCALE).astype(BF16)
    dk[...] = mm(_C_DK, _C_DV).astype(BF16)
    dvv = mm(_C_DV, _C_GA).astype(BF16)
    lane = lax.broadcasted_iota(jnp.int32, (dvv.shape[0], LANES), 1)
    ones_col = jnp.where(lane == 0, 1.0, 0.0).astype(BF16)
    dv[...] = jnp.concatenate(
        [blk for h in range(DIFF_HEADS) for blk in (dvv[:, h * LANES:(h + 1) * LANES], ones_col)], axis=1)
    sa[...] = _sigmoid(mm(_C_GA, _C_GB)).astype(BF16)
    sb[...] = _sigmoid(mm(_C_GB, _C_FG)).astype(BF16)
    fg[...] = mm(_C_FG, _C_END)


def _inproj(x, g, b, w, *, do_ln, tm):
    S = x.shape[0]
    row = lambda n: pl.BlockSpec((tm, n), lambda i: (i, 0))
    full = lambda a: pl.BlockSpec(a.shape, lambda i: (0,) * a.ndim)
    out_shape = [jax.ShapeDtypeStruct((S, 512), BF16)] * 5 + [jax.ShapeDtypeStruct((S, D_MODEL), BF16)] * 3 + [
        jax.ShapeDtypeStruct((S, LANES), F32)]
    out_specs = [row(512)] * 5 + [row(D_MODEL)] * 3 + [row(LANES)]
    if do_ln:
        out_shape.append(jax.ShapeDtypeStruct((S, D_MODEL), F32))
        out_specs.append(row(D_MODEL))
    return pl.pallas_call(
        functools.partial(_inproj_kernel, do_ln=do_ln),
        grid=(S // tm,),
        in_specs=[row(D_MODEL), full(g), full(b), full(w)],
        out_specs=out_specs,
        out_shape=out_shape,
        compiler_params=_cparams(("arbitrary",)),
        name="inproj",
    )(x, g, b, w)


def _prep_kernel(fq_ref, fk_ref, fv_ref, fg_ref, bf_ref, qa_ref, ka_ref, va_ref, carry_ref, *, tb):
    @pl.when(pl.program_id(0) == 0)
    def _():
        carry_ref[...] = jnp.zeros_like(carry_ref)

    z = fg_ref[...] + bf_ref[...]
    logf = jnp.minimum(z, 0.0) - jnp.log1p(jnp.exp(-jnp.abs(z)))
    r = lax.broadcasted_iota(jnp.int32, (tb, tb), 0)
    c = lax.broadcasted_iota(jnp.int32, (tb, tb), 1)
    tri = (c <= r).astype(BF16)
    hi, mid, lo = _split3(logf)
    dot = lambda a: jnp.dot(tri, a, preferred_element_type=F32)
    cum = (dot(hi) + dot(mid)) + dot(lo) + carry_ref[0:1, :]
    carry_ref[0:1, :] = cum[tb - 1:tb, :]

    c_hi, c_mid, c_lo = (t.astype(F32) for t in _split3(cum * LOG2E))
    lane = lax.broadcasted_iota(jnp.int32, (tb, LANES), 1)
    one = jnp.ones((tb, LANES), F32)
    zero = jnp.zeros((tb, LANES), F32)
    for h in range(FOX_HEADS):
        p, odd = divmod(h, 2)
        col = lambda t: jnp.broadcast_to(t[:, h:h + 1], (tb, LANES))
        chi, cmid, clo = col(c_hi), col(c_mid), col(c_lo)

        def head_slab(ref):
            slab = ref[:, p * LANES:(p + 1) * LANES].astype(F32)
            return pltpu.roll(slab, HEAD_DIM, 1) if odd else slab

        qa = jnp.where(lane < 64, head_slab(fq_ref),
                       jnp.where(lane == 64, chi, jnp.where(lane == 65, cmid, jnp.where(lane == 66, clo,
                                 jnp.where(lane < 70, one, zero)))))
        ka = jnp.where(lane < 64, head_slab(fk_ref),
                       jnp.where(lane < 67, one, jnp.where(lane == 67, -chi, jnp.where(lane == 68, -cmid,
                                 jnp.where(lane == 69, -clo, zero)))))
        qa_ref[h] = qa.astype(BF16)
        ka_ref[h] = ka.astype(BF16)
        vslab = fv_ref[:, p * LANES:(p + 1) * LANES].astype(F32)
        if odd:
            va = jnp.where(lane >= 64, vslab, jnp.where(lane == 0, one, zero))
        else:
            va = jnp.where(lane < 64, vslab, jnp.where(lane == 64, one, zero))
        va_ref[h] = va.astype(BF16)


def _prep(fq, fk, fv, fg, bf, *, tb):
    S = fq.shape[0]
    row = lambda n: pl.BlockSpec((tb, n), lambda i: (i, 0))
    hspec = pl.BlockSpec((FOX_HEADS, tb, LANES), lambda i: (0, i, 0))
    hshape = jax.ShapeDtypeStruct((FOX_HEADS, S, LANES), BF16)
    return pl.pallas_call(
        functools.partial(_prep_kernel, tb=tb),
        grid=(S // tb,),
        in_specs=[row(512), row(512), row(512), row(LANES), pl.BlockSpec((1, LANES), lambda i: (0, 0))],
        out_specs=[hspec] * 3,
        out_shape=[hshape] * 3,
        scratch_shapes=[pltpu.VMEM((8, LANES), F32)],
        compiler_params=_cparams(("arbitrary",)),
        name="fox_prep",
    )(fq, fk, fv, fg, bf)


def _nt_dot(a, b):
    return lax.dot_general(a, b, (((1,), (1,)), ((), ())), preferred_element_type=F32)


def _fox_kernel(q_ref, k_ref, v_ref, o_ref, m_sc, acc_sc, *, tq, tk):
    i = pl.program_id(1)
    m_sc[...] = jnp.full_like(m_sc, NEG)
    acc_sc[...] = jnp.zeros_like(acc_sc)
    nsub = tq // tk

    def tile(j, r0, masked):
        off = pl.multiple_of(j * tk, tk)
        rows = tq - r0
        for hh in range(2):
            s = _nt_dot(q_ref[hh, r0:, :], k_ref[hh, pl.ds(off, tk), :])
            if masked:
                r = lax.broadcasted_iota(jnp.int32, (rows, tk), 0)
                c = lax.broadcasted_iota(jnp.int32, (rows, tk), 1)
                s = jnp.where(c <= r, s, NEG)
            m_prev = m_sc[hh, r0:, :]
            m_next = jnp.maximum(m_prev, jnp.max(s, axis=1, keepdims=True))
            alpha = jnp.exp2(m_prev - m_next)
            p = jnp.exp2(s - jnp.concatenate([m_next] * (tk // LANES), axis=1)).astype(BF16)
            pv = jnp.dot(p, v_ref[hh, pl.ds(off, tk), :], preferred_element_type=F32)
            acc_sc[hh, r0:, :] = alpha * acc_sc[hh, r0:, :] + pv
            m_sc[hh, r0:, :] = m_next

    def body(j, carry):
        tile(j, 0, False)
        return carry

    lax.fori_loop(0, i * nsub, body, 0)
    for u in range(nsub):
        tile(i * nsub + u, u * tk, True)

    a0 = acc_sc[0]
    a1 = acc_sc[1]
    lane = lax.broadcasted_iota(jnp.int32, (tq, LANES), 1)
    o_ref[...] = jnp.where(lane < 64, a0 / a0[:, 64:65], a1 / a1[:, 0:1]).astype(o_ref.dtype)


def _fox_attention(qa, ka, va, *, tq, tk):
    S = qa.shape[1]
    return pl.pallas_call(
        functools.partial(_fox_kernel, tq=tq, tk=tk),
        grid=(FOX_HEADS // 2, S // tq),
        in_specs=[pl.BlockSpec((2, tq, LANES), lambda p, i: (p, i, 0)),
                  pl.BlockSpec((2, S, LANES), lambda p, i: (p, 0, 0)),
                  pl.BlockSpec((2, S, LANES), lambda p, i: (p, 0, 0))],
        out_specs=pl.BlockSpec((tq, LANES), lambda p, i: (i, p)),
        out_shape=jax.ShapeDtypeStruct((S, FOX_WIDTH), BF16),
        scratch_shapes=[pltpu.VMEM((2, tq, LANES), F32), pltpu.VMEM((2, tq, LANES), F32)],
        compiler_params=_cparams(("arbitrary", "arbitrary")),
        name="fox_attn",
    )(qa, ka, va)


BIAS_KINDS = 3


def _bias_tile_kernel(table_ref, o_ref, *, t):
    h = pl.program_id(0)
    kind = pl.program_id(1)
    r = lax.broadcasted_iota(jnp.int32, (t, t), 0)
    c = lax.broadcasted_iota(jnp.int32, (t, t), 1)
    dist = r - c + kind * t
    n = jnp.maximum(dist, 0)
    max_exact = REL_BUCKETS // 2
    nf = jnp.maximum(n, 1).astype(F32)
    log_part = jnp.log(nf / max_exact) / math.log(REL_MAX_DISTANCE / max_exact) * (REL_BUCKETS - max_exact)
    large = jnp.minimum(max_exact + log_part.astype(jnp.int32), REL_BUCKETS - 1)
    bucket = jnp.where(n < max_exact, n, large)
    val = jnp.zeros((t, t), F32)
    for b in range(REL_BUCKETS):
        val = jnp.where(bucket == b, table_ref[h * REL_BUCKETS + b], val)
    o_ref[0, 0] = jnp.where(dist >= 0, val * LOG2E, NEG)


def _bias_tiles(table_flat, *, t):
    return pl.pallas_call(
        functools.partial(_bias_tile_kernel, t=t),
        grid=(DIFF_HEADS, BIAS_KINDS),
        in_specs=[pl.BlockSpec(memory_space=pltpu.SMEM)],
        out_specs=pl.BlockSpec((1, 1, t, t), lambda h, k: (h, k, 0, 0)),
        out_shape=jax.ShapeDtypeStruct((DIFF_HEADS, BIAS_KINDS, t, t), F32),
        compiler_params=_cparams(("arbitrary", "arbitrary")),
        name="bias_tiles",
    )(table_flat)


def _diff_kernel(q_ref, k_ref, v_ref, bias_ref, far_ref, lam_ref, g_ref, o_ref, m_sc, acc_sc,
                 *, tq, tk, lam_init):
    i = pl.program_id(1)
    m_sc[...] = jnp.full_like(m_sc, NEG)
    acc_sc[...] = jnp.zeros_like(acc_sc)
    nsub = tq // tk

    lane = lax.broadcasted_iota(jnp.int32, (tq, LANES), 1)
    q = q_ref[...]
    zero = jnp.zeros_like(q)
    qs = (jnp.where(lane < 64, q, zero), jnp.where(lane >= 64, q, zero))
    far = far_ref[0]

    def tile(j, r0, kinds):
        off = pl.multiple_of(j * tk, tk)
        k = k_ref[pl.ds(off, tk), :]
        v = v_ref[pl.ds(off, tk), :]
        for c in range(2):
            s = _nt_dot(qs[c][r0:], k)
            m_prev = m_sc[c, r0:, :]
            if kinds is None:
                m_next = jnp.maximum(m_prev, jnp.max(s, axis=1, keepdims=True) + far)
                shift = m_next - far
            else:
                s = jnp.concatenate([s[a * tk:(a + 1) * tk] + bias_ref[0, kd] for a, kd in enumerate(kinds)],
                                    axis=0)
                m_next = jnp.maximum(m_prev, jnp.max(s, axis=1, keepdims=True))
                shift = m_next
            p = jnp.exp2(s - jnp.concatenate([shift] * (tk // LANES), axis=1)).astype(BF16)
            alpha = jnp.exp2(m_prev - m_next)
            pv = jnp.dot(p, v, preferred_element_type=F32)
            acc_sc[c, r0:, :] = jnp.concatenate([alpha, alpha], axis=1) * acc_sc[c, r0:, :] + pv
            m_sc[c, r0:, :] = m_next

    def body(j, carry):
        tile(j, 0, None)
        return carry

    lax.fori_loop(0, i * nsub - 1, body, 0)

    @pl.when(i > 0)
    def _():
        tile(i * nsub - 1, 0, (1,) + (2,) * (nsub - 1))

    for u in range(nsub):
        tile(i * nsub + u, u * tk, ((0, 1) + (2,) * nsub)[:nsub - u])

    lv = lam_ref[...]
    lam = (jnp.exp(jnp.sum(lv[0:1] * lv[1:2], axis=1, keepdims=True))
           - jnp.exp(jnp.sum(lv[2:3] * lv[3:4], axis=1, keepdims=True)) + lam_init)
    a0 = acc_sc[0]
    a1 = acc_sc[1]
    o = a0[:, :LANES] / a0[:, LANES:LANES + 1] - lam * (a1[:, :LANES] / a1[:, LANES:LANES + 1])
    o = o * lax.rsqrt(jnp.mean(o * o, axis=1, keepdims=True) + SUBLN_EPS) * g_ref[...] * (1.0 - lam_init)
    o_ref[...] = o.astype(o_ref.dtype)


def _diff_attention(dq, dk, dv, bias, far, lamvec, g, *, tq, tk, lam_init):
    S = dq.shape[0]
    return pl.pallas_call(
        functools.partial(_diff_kernel, tq=tq, tk=tk, lam_init=lam_init),
        grid=(DIFF_HEADS, S // tq),
        in_specs=[pl.BlockSpec((tq, LANES), lambda h, i: (i, h)),
                  pl.BlockSpec((S, LANES), lambda h, i: (0, h)),
                  pl.BlockSpec((S, 2 * LANES), lambda h, i: (0, h)),
                  pl.BlockSpec((1, BIAS_KINDS, tk, tk), lambda h, i: (h, 0, 0, 0)),
                  pl.BlockSpec((1, 1, LANES), lambda h, i: (h, 0, 0)),
                  pl.BlockSpec((8, LANES), lambda h, i: (0, 0)),
                  pl.BlockSpec((1, LANES), lambda h, i: (0, 0))],
        out_specs=pl.BlockSpec((tq, LANES), lambda h, i: (i, h)),
        out_shape=jax.ShapeDtypeStruct((S, DIFF_V_WIDTH), BF16),
        scratch_shapes=[pltpu.VMEM((2, tq, LANES), F32), pltpu.VMEM((2, tq, 2 * LANES), F32)],
        compiler_params=_cparams(("arbitrary", "arbitrary")),
        name="diff_attn",
    )(dq, dk, dv, bias, far, lamvec, g)


def _merge_kernel(yf_ref, yd_ref, sa_ref, sb_ref, h_ref, wf_ref, wd_ref, wo_ref, g_ref, b_ref, o_ref, *, alpha):
    bf = jnp.dot(yf_ref[...], wf_ref[...], preferred_element_type=F32)
    bd = jnp.dot(yd_ref[...], wd_ref[...], preferred_element_type=F32)
    merged = sa_ref[...].astype(F32) * bf + sb_ref[...].astype(F32) * bd
    mix = jnp.dot(merged.astype(BF16), wo_ref[...], preferred_element_type=F32)
    o_ref[...] = _layer_norm(alpha * h_ref[...] + mix, g_ref[...], b_ref[...])


def _merge(yf, yd, sa, sb, h, wf, wd, wo, g, b, *, alpha, tm):
    S = h.shape[0]
    row = lambda n: pl.BlockSpec((tm, n), lambda i: (i, 0))
    full = lambda a: pl.BlockSpec(a.shape, lambda i: (0,) * a.ndim)
    return pl.pallas_call(
        functools.partial(_merge_kernel, alpha=alpha),
        grid=(S // tm,),
        in_specs=[row(512), row(512), row(D_MODEL), row(D_MODEL), row(D_MODEL),
                  full(wf), full(wd), full(wo), full(g), full(b)],
        out_specs=row(D_MODEL),
        out_shape=jax.ShapeDtypeStruct((S, D_MODEL), F32),
        compiler_params=_cparams(("arbitrary",)),
        name="merge_out",
    )(yf, yd, sa, sb, h, wf, wd, wo, g, b)


MOE_ROWS = 256
RANK_CHUNK = 256


def _router_kernel(h_ref, whi_ref, wlo_ref, cw_ref, rank_ref, rank_t_ref, cnt_ref):
    tm = h_ref.shape[0]
    h = h_ref[...]
    hhi = h.astype(BF16)
    hlo = (h - hhi.astype(F32)).astype(BF16)
    whi = whi_ref[...]
    dot = lambda a, b: jnp.dot(a, b, preferred_element_type=F32)
    logits = dot(hhi, whi) + dot(hhi, wlo_ref[...]) + dot(hlo, whi)
    lane = lax.broadcasted_iota(jnp.int32, logits.shape, 1).astype(F32)
    lg = jnp.where(lane < N_EXPERTS, logits, NEG)
    m1 = jnp.max(lg, axis=1, keepdims=True)
    i1 = jnp.min(jnp.where(lg == m1, lane, float(LANES)), axis=1, keepdims=True)
    lg2 = jnp.where(lane == i1, NEG, lg)
    m2 = jnp.max(lg2, axis=1, keepdims=True)
    i2 = jnp.min(jnp.where(lg2 == m2, lane, float(LANES)), axis=1, keepdims=True)
    e = jnp.exp(m2 - m1)
    g1 = 1.0 / (1.0 + e)
    chosen = jnp.logical_or(lane == i1, lane == i2)
    cw_ref[...] = jnp.where(lane == i1, g1, 0.0) + jnp.where(lane == i2, e * g1, 0.0)

    a = chosen.astype(BF16)
    r = lax.broadcasted_iota(jnp.int32, (RANK_CHUNK, RANK_CHUNK), 0)
    c = lax.broadcasted_iota(jnp.int32, (RANK_CHUNK, RANK_CHUNK), 1)
    strict_lower = (c < r).astype(BF16)
    ones_row = jnp.ones((8, RANK_CHUNK), BF16)
    carry = jnp.zeros((1, LANES), F32)
    for q in range(tm // RANK_CHUNK):
        rows = slice(q * RANK_CHUNK, (q + 1) * RANK_CHUNK)
        rank = jnp.where(chosen[rows], dot(strict_lower, a[rows]) + carry, -1.0)
        rank_ref[rows, :] = rank
        rank_t_ref[0, :, rows] = jnp.transpose(rank)[0:N_EXPERTS, :]
        carry = carry + dot(ones_row, a[rows])[0:1, :]
    cnt_ref[0] = jnp.broadcast_to(carry, (8, LANES)).astype(jnp.int32)


def _router(h, whi, wlo, *, tm):
    S = h.shape[0]
    nb = S // tm
    return pl.pallas_call(
        _router_kernel,
        grid=(nb,),
        in_specs=[pl.BlockSpec((tm, D_MODEL), lambda i: (i, 0)),
                  pl.BlockSpec(whi.shape, lambda i: (0, 0)), pl.BlockSpec(wlo.shape, lambda i: (0, 0))],
        out_specs=[pl.BlockSpec((tm, LANES), lambda i: (i, 0)),
                   pl.BlockSpec((tm, LANES), lambda i: (i, 0)),
                   pl.BlockSpec((1, N_EXPERTS, tm), lambda i: (i, 0, 0)),
                   pl.BlockSpec((1, 8, LANES), lambda i: (i, 0, 0))],
        out_shape=[jax.ShapeDtypeStruct((S, LANES), F32),
                   jax.ShapeDtypeStruct((S, LANES), F32),
                   jax.ShapeDtypeStruct((nb, N_EXPERTS, tm), F32),
                   jax.ShapeDtypeStruct((nb, 8, LANES), jnp.int32)],
        compiler_params=_cparams(("arbitrary",)),
        name="router",
    )(h, whi, wlo)


def _moe_kernel(cnt_ref, h_ref, cw_ref, rank_ref, rank_t_ref, wg_ref, wu_ref, wd_ref, g_ref, b_ref, o_ref,
                hb_sc, xs_sc, acc_sc, *, alpha):
    tm = h_ref.shape[0]
    i = pl.program_id(0)
    e = pl.program_id(1)
    c = pl.program_id(2)
    last_c = c == pl.num_programs(2) - 1
    n_steps = (cnt_ref[i * N_EXPERTS + e] + MOE_ROWS - 1) // MOE_ROWS

    @pl.when(jnp.logical_and(e == 0, c == 0))
    def _():
        hb_sc[...] = h_ref[...].astype(BF16)
        o_ref[...] = jnp.zeros_like(o_ref)

    @pl.when(c == 0)
    def _():
        rank_row = rank_t_ref[0, pl.ds(e, 1), :]

        def gather(t, carry):
            r0 = pl.multiple_of(t * MOE_ROWS, MOE_ROWS)
            slot = (lax.broadcasted_iota(jnp.int32, (MOE_ROWS, tm), 0) + r0).astype(F32)
            onehot = (rank_row == slot).astype(BF16)
            xs_sc[pl.ds(r0, MOE_ROWS), :] = jnp.dot(onehot, hb_sc[...], preferred_element_type=F32).astype(BF16)
            acc_sc[pl.ds(r0, MOE_ROWS), :] = jnp.zeros((MOE_ROWS, D_MODEL), F32)
            return carry

        lax.fori_loop(0, n_steps, gather, 0)

    def swiglu(t, carry):
        r0 = pl.multiple_of(t * MOE_ROWS, MOE_ROWS)
        x = xs_sc[pl.ds(r0, MOE_ROWS), :]
        gate = jnp.dot(x, wg_ref[0], preferred_element_type=F32)
        up = jnp.dot(x, wu_ref[0], preferred_element_type=F32)
        a = (gate * _sigmoid(gate) * up).astype(BF16)
        acc_sc[pl.ds(r0, MOE_ROWS), :] += jnp.dot(a, wd_ref[0], preferred_element_type=F32)
        return carry

    lax.fori_loop(0, n_steps, swiglu, 0)

    @pl.when(last_c)
    def _():
        lane = lax.broadcasted_iota(jnp.int32, (tm, LANES), 1)
        pick = lambda ref: jnp.sum(jnp.where(lane == e, ref[...], 0.0), axis=1, keepdims=True)
        rank_col = pick(rank_ref)
        gate_col = pick(cw_ref)

        def scatter(t, carry):
            r0 = pl.multiple_of(t * MOE_ROWS, MOE_ROWS)
            slot = (lax.broadcasted_iota(jnp.int32, (tm, MOE_ROWS), 1) + r0).astype(F32)
            onehot = (rank_col == slot).astype(BF16)
            y = acc_sc[pl.ds(r0, MOE_ROWS), :].astype(BF16)
            o_ref[...] += gate_col * jnp.dot(onehot, y, preferred_element_type=F32)
            return carry

        lax.fori_loop(0, n_steps, scatter, 0)

    @pl.when(jnp.logical_and(e == pl.num_programs(1) - 1, last_c))
    def _():
        o_ref[...] = _layer_norm(alpha * h_ref[...] + o_ref[...], g_ref[...], b_ref[...])


def _moe(h, cw, rank, rank_t, counts, w_gu, w_dn, g, b, *, alpha, tm, fc):
    S = h.shape[0]
    E, _, F2 = w_gu.shape
    nc = F2 // 2 // fc
    once = pl.Buffered(1)
    grid_spec = pltpu.PrefetchScalarGridSpec(
        num_scalar_prefetch=1,
        grid=(S // tm, E, nc),
        in_specs=[pl.BlockSpec((tm, D_MODEL), lambda i, e, c, cnt: (i, 0), pipeline_mode=once),
                  pl.BlockSpec((tm, LANES), lambda i, e, c, cnt: (i, 0), pipeline_mode=once),
                  pl.BlockSpec((tm, LANES), lambda i, e, c, cnt: (i, 0), pipeline_mode=once),
                  pl.BlockSpec((1, N_EXPERTS, tm), lambda i, e, c, cnt: (i, 0, 0), pipeline_mode=once),
                  pl.BlockSpec((1, D_MODEL, fc), lambda i, e, c, cnt: (e, 0, c)),
                  pl.BlockSpec((1, D_MODEL, fc), lambda i, e, c, cnt: (e, 0, nc + c)),
                  pl.BlockSpec((1, fc, D_MODEL), lambda i, e, c, cnt: (e, c, 0)),
                  pl.BlockSpec(g.shape, lambda i, e, c, cnt: (0, 0)),
                  pl.BlockSpec(b.shape, lambda i, e, c, cnt: (0, 0))],
        out_specs=pl.BlockSpec((tm, D_MODEL), lambda i, e, c, cnt: (i, 0), pipeline_mode=once),
        scratch_shapes=[pltpu.VMEM((tm, D_MODEL), BF16), pltpu.VMEM((tm, D_MODEL), BF16),
                        pltpu.VMEM((tm, D_MODEL), F32)],
    )
    return pl.pallas_call(
        functools.partial(_moe_kernel, alpha=alpha),
        grid_spec=grid_spec,
        out_shape=jax.ShapeDtypeStruct((S, D_MODEL), F32),
        compiler_params=_cparams(("arbitrary", "arbitrary", "arbitrary")),
        name="moe_ffn",
    )(counts, h, cw, rank, rank_t, w_gu, w_gu, w_dn, g, b)


def _ffn_kernel(h_ref, wg_ref, wu_ref, wd_ref, g_ref, b_ref, o_ref, hb_sc, acc_sc, *, alpha):
    c = pl.program_id(1)

    @pl.when(c == 0)
    def _():
        hb_sc[...] = h_ref[...].astype(BF16)
        acc_sc[...] = jnp.zeros_like(acc_sc)

    hb = hb_sc[...]
    gate = jnp.dot(hb, wg_ref[...], preferred_element_type=F32)
    up = jnp.dot(hb, wu_ref[...], preferred_element_type=F32)
    a = gate * _sigmoid(gate) * up
    acc_sc[...] += jnp.dot(a.astype(BF16), wd_ref[...], preferred_element_type=F32)

    @pl.when(c == pl.num_programs(1) - 1)
    def _():
        o_ref[...] = _layer_norm(alpha * h_ref[...] + acc_sc[...], g_ref[...], b_ref[...])


def _ffn(h, w_gu, w_dn, g, b, *, alpha, tm, fc):
    S = h.shape[0]
    nc = w_gu.shape[1] // 2 // fc
    return pl.pallas_call(
        functools.partial(_ffn_kernel, alpha=alpha),
        grid=(S // tm, nc),
        in_specs=[pl.BlockSpec((tm, D_MODEL), lambda i, c: (i, 0)),
                  pl.BlockSpec((D_MODEL, fc), lambda i, c: (0, c)),
                  pl.BlockSpec((D_MODEL, fc), lambda i, c: (0, nc + c)),
                  pl.BlockSpec((fc, D_MODEL), lambda i, c: (c, 0)),
                  pl.BlockSpec(g.shape, lambda i, c: (0, 0)), pl.BlockSpec(b.shape, lambda i, c: (0, 0))],
        out_specs=pl.BlockSpec((tm, D_MODEL), lambda i, c: (i, 0)),
        out_shape=jax.ShapeDtypeStruct((S, D_MODEL), F32),
        scratch_shapes=[pltpu.VMEM((tm, D_MODEL), BF16), pltpu.VMEM((tm, D_MODEL), F32)],
        compiler_params=_cparams(("arbitrary", "arbitrary")),
        name="dense_ffn",
    )(h, w_gu, w_gu, w_dn, g, b)


def _pad_lanes(a, n=LANES):
    return jnp.pad(a, [(0, 0)] * (a.ndim - 1) + [(0, n - a.shape[-1])])


def kernel(x, ln_in_g, ln_in_b, w_in, b_fgate, lam_q1, lam_k1, lam_q2, lam_k2, subln_g, w_branch_fox, w_branch_diff, w_out, ln_mix_g, ln_mix_b, rel_bias, w_ffn_gate_up, w_ffn_down, w_router, w_expert_gate_up, w_expert_down, ln_ffn_g, ln_ffn_b):
    B, S, _ = x.shape
    assert B == 1
    depth = w_in.shape[0]
    alpha = (2.0 * depth) ** 0.25
    tq = min(ATTN_TQ, S)
    tk = min(ATTN_TK, tq)
    tm = min(512, S)
    tf = min(1024, S)
    te = min(MOE_TM, S)
    assert tk >= REL_MAX_DISTANCE and tq % tk == 0 and S % tq == 0 and S % tf == 0 and S % tm == 0 and S % te == 0
    assert te % MOE_ROWS == 0 and te % RANK_CHUNK == 0
    row = lambda v: v.reshape(1, -1).astype(F32)

    table = rel_bias.astype(F32)
    bias = _bias_tiles(table.T.reshape(-1), t=tk)
    far = jnp.broadcast_to((table[REL_BUCKETS - 1] * LOG2E)[:, None, None], (DIFF_HEADS, 1, LANES))

    h = x[0]
    for l in range(depth):
        lam_init = 0.8 - 0.6 * math.exp(-0.3 * l)
        fq, fk, fv, fg, dq, dk, dv, ga, gb = jnp.split(w_in[l], list(_cumsum(IN_SPLIT_SIZES))[:-1], axis=-1)
        w_cat = jnp.concatenate([fq, fk, fv, dq, dk, dv, ga, gb, _pad_lanes(fg)], axis=-1).astype(BF16)
        outs = _inproj(h, row(ln_in_g), row(ln_in_b), w_cat, do_ln=(l == 0), tm=tm)
        fq, fk, fv, dq, dk, dv, sa, sb, fg = outs[:9]
        if l == 0:
            h = outs[9]
        qa, ka, va = _prep(fq, fk, fv, fg, _pad_lanes(row(b_fgate[l])), tb=tm)
        y_fox = _fox_attention(qa, ka, va, tq=tq, tk=tk)
        lamvec = jnp.zeros((8, LANES), F32).at[0:4, 0:HEAD_DIM].set(
            jnp.stack([lam_q1[l], lam_k1[l], lam_q2[l], lam_k2[l]]).astype(F32))
        y_diff = _diff_attention(dq, dk, dv, bias, far, lamvec, row(subln_g[l]), tq=tq, tk=tk,
                                 lam_init=lam_init)
        h = _merge(y_fox, y_diff, sa, sb, h, w_branch_fox[l].astype(BF16), w_branch_diff[l].astype(BF16),
                   w_out[l].astype(BF16), row(ln_mix_g[l]), row(ln_mix_b[l]), alpha=alpha, tm=tm)
        g, b = row(ln_ffn_g[l]), row(ln_ffn_b[l])
        if l % 2 == 0:
            h = _ffn(h, w_ffn_gate_up[l // 2].astype(BF16), w_ffn_down[l // 2].astype(BF16), g, b,
                     alpha=alpha, tm=tf, fc=1408)
        else:
            wr = _pad_lanes(w_router[l // 2].astype(F32))
            whi = wr.astype(BF16)
            wlo = (wr - whi.astype(F32)).astype(BF16)
            cw, rank, rank_t, counts = _router(h, whi, wlo, tm=te)
            h = _moe(h, cw, rank, rank_t, counts[:, 0, :N_EXPERTS].reshape(-1),
                     w_expert_gate_up[l // 2].astype(BF16), w_expert_down[l // 2].astype(BF16),
                     g, b, alpha=alpha, tm=te, fc=896)
    return h[None]


def _cumsum(sizes):
    tot = 0
    for s in sizes:
        tot += s
        yield tot
```

```python
import functools
import math

import jax
import jax.numpy as jnp
from jax import lax
from jax.experimental import pallas as pl
from jax.experimental.pallas import tpu as pltpu

F32 = jnp.float32
BF16 = jnp.bfloat16

D_MODEL = 1024
HEAD_DIM = 64
FOX_HEADS = 8
FOX_WIDTH = FOX_HEADS * HEAD_DIM
DIFF_HEADS = 4
DIFF_QK_WIDTH = DIFF_HEADS * 2 * HEAD_DIM
DIFF_V_DIM = 2 * HEAD_DIM
DIFF_V_WIDTH = DIFF_HEADS * DIFF_V_DIM
IN_SPLIT_SIZES = (FOX_WIDTH, FOX_WIDTH, FOX_WIDTH, FOX_HEADS, DIFF_QK_WIDTH, DIFF_QK_WIDTH, DIFF_V_WIDTH,
                  D_MODEL, D_MODEL)
REL_BUCKETS = 32
REL_MAX_DISTANCE = 128
N_EXPERTS = 8
LN_EPS = 1e-5
SUBLN_EPS = 1e-5

LANES = 128
LOG2E = 1.4426950408889634
Q_SCALE = HEAD_DIM ** -0.5 * LOG2E
NEG = -1e30
VMEM_LIMIT = 56 * 1024 * 1024
ATTN_TQ = 2048
ATTN_TK = 512
MOE_TM = 2048


def _cparams(sem):
    return pltpu.CompilerParams(dimension_semantics=sem, vmem_limit_bytes=VMEM_LIMIT)


def _layer_norm(x, g, b):
    mu = jnp.mean(x, axis=-1, keepdims=True)
    xc = x - mu
    var = jnp.mean(xc * xc, axis=-1, keepdims=True)
    return xc * lax.rsqrt(var + LN_EPS) * g + b


def _sigmoid(x):
    return 1.0 / (1.0 + jnp.exp(-x))


def _split3(x):
    hi = x.astype(BF16)
    r1 = x - hi.astype(F32)
    mid = r1.astype(BF16)
    lo = (r1 - mid.astype(F32)).astype(BF16)
    return hi, mid, lo


_C_FQ, _C_FK, _C_FV, _C_DQ, _C_DK, _C_DV, _C_GA, _C_GB, _C_FG, _C_END = (
    0, 512, 1024, 1536, 2048, 2560, 3072, 4096, 5120, 5248)


def _inproj_kernel(x_ref, g_ref, b_ref, w_ref, fq, fk, fv, dq, dk, dv, sa, sb, fg, *h_out, do_ln):
    x = x_ref[...]
    if do_ln:
        x = _layer_norm(x, g_ref[...], b_ref[...])
        h_out[0][...] = x
    xb = x.astype(BF16)

    def mm(a, b):
        return jnp.dot(xb, w_ref[:, a:b], preferred_element_type=F32)

    fq[...] = (mm(_C_FQ, _C_FK) * Q_SCALE).astype(BF16)
    fk[...] = mm(_C_FK, _C_FV).astype(BF16)
    fv[...] = mm(_C_FV, _C_DQ).astype(BF16)
    dq[...] = (mm(_C_DQ, _C_DK) * Q_SCALE).astype(BF16)
    dk[...] = mm(_C_DK, _C_DV).astype(BF16)
    dv[...] = mm(_C_DV, _C_GA).astype(BF16)
    sa[...] = _sigmoid(mm(_C_GA, _C_GB)).astype(BF16)
    sb[...] = _sigmoid(mm(_C_GB, _C_FG)).astype(BF16)
    fg[...] = mm(_C_FG, _C_END)


def _inproj(x, g, b, w, *, do_ln, tm):
    S = x.shape[0]
    row = lambda n: pl.BlockSpec((tm, n), lambda i: (i, 0))
    full = lambda a: pl.BlockSpec(a.shape, lambda i: (0,) * a.ndim)
    out_shape = [jax.ShapeDtypeStruct((S, 512), BF16)] * 6 + [jax.ShapeDtypeStruct((S, D_MODEL), BF16)] * 2 + [
        jax.ShapeDtypeStruct((S, LANES), F32)]
    out_specs = [row(512)] * 6 + [row(D_MODEL)] * 2 + [row(LANES)]
    if do_ln:
        out_shape.append(jax.ShapeDtypeStruct((S, D_MODEL), F32))
        out_specs.append(row(D_MODEL))
    return pl.pallas_call(
        functools.partial(_inproj_kernel, do_ln=do_ln),
        grid=(S // tm,),
        in_specs=[row(D_MODEL), full(g), full(b), full(w)],
        out_specs=out_specs,
        out_shape=out_shape,
        compiler_params=_cparams(("arbitrary",)),
        name="inproj",
    )(x, g, b, w)


def _prep_kernel(fq_ref, fk_ref, fv_ref, fg_ref, bf_ref, qa_ref, ka_ref, va_ref, carry_ref, *, tb):
    @pl.when(pl.program_id(0) == 0)
    def _():
        carry_ref[...] = jnp.zeros_like(carry_ref)

    z = fg_ref[...] + bf_ref[...]
    logf = jnp.minimum(z, 0.0) - jnp.log1p(jnp.exp(-jnp.abs(z)))
    r = lax.broadcasted_iota(jnp.int32, (tb, tb), 0)
    c = lax.broadcasted_iota(jnp.int32, (tb, tb), 1)
    tri = (c <= r).astype(BF16)
    hi, mid, lo = _split3(logf)
    dot = lambda a: jnp.dot(tri, a, preferred_element_type=F32)
    cum = (dot(hi) + dot(mid)) + dot(lo) + carry_ref[0:1, :]
    carry_ref[0:1, :] = cum[tb - 1:tb, :]

    c_hi, c_mid, c_lo = (t.astype(F32) for t in _split3(cum * LOG2E))
    lane = lax.broadcasted_iota(jnp.int32, (tb, LANES), 1)
    one = jnp.ones((tb, LANES), F32)
    zero = jnp.zeros((tb, LANES), F32)
    for h in range(FOX_HEADS):
        p, odd = divmod(h, 2)
        col = lambda t: jnp.broadcast_to(t[:, h:h + 1], (tb, LANES))
        chi, cmid, clo = col(c_hi), col(c_mid), col(c_lo)

        def head_slab(ref):
            slab = ref[:, p * LANES:(p + 1) * LANES].astype(F32)
            return pltpu.roll(slab, HEAD_DIM, 1) if odd else slab

        qa = jnp.where(lane < 64, head_slab(fq_ref),
                       jnp.where(lane == 64, chi, jnp.where(lane == 65, cmid, jnp.where(lane == 66, clo,
                                 jnp.where(lane < 70, one, zero)))))
        ka = jnp.where(lane < 64, head_slab(fk_ref),
                       jnp.where(lane < 67, one, jnp.where(lane == 67, -chi, jnp.where(lane == 68, -cmid,
                                 jnp.where(lane == 69, -clo, zero)))))
        qa_ref[h] = qa.astype(BF16)
        ka_ref[h] = ka.astype(BF16)
        vslab = fv_ref[:, p * LANES:(p + 1) * LANES].astype(F32)
        if odd:
            va = jnp.where(lane >= 64, vslab, jnp.where(lane == 0, one, zero))
        else:
            va = jnp.where(lane < 64, vslab, jnp.where(lane == 64, one, zero))
        va_ref[h] = va.astype(BF16)


def _prep(fq, fk, fv, fg, bf, *, tb):
    S = fq.shape[0]
    row = lambda n: pl.BlockSpec((tb, n), lambda i: (i, 0))
    hspec = pl.BlockSpec((FOX_HEADS, tb, LANES), lambda i: (0, i, 0))
    hshape = jax.ShapeDtypeStruct((FOX_HEADS, S, LANES), BF16)
    return pl.pallas_call(
        functools.partial(_prep_kernel, tb=tb),
        grid=(S // tb,),
        in_specs=[row(512), row(512), row(512), row(LANES), pl.BlockSpec((1, LANES), lambda i: (0, 0))],
        out_specs=[hspec] * 3,
        out_shape=[hshape] * 3,
        scratch_shapes=[pltpu.VMEM((8, LANES), F32)],
        compiler_params=_cparams(("arbitrary",)),
        name="fox_prep",
    )(fq, fk, fv, fg, bf)


def _nt_dot(a, b):
    return lax.dot_general(a, b, (((1,), (1,)), ((), ())), preferred_element_type=F32)


def _fox_kernel(q_ref, k_ref, v_ref, o_ref, m_sc, acc_sc, *, tq, tk):
    i = pl.program_id(1)
    m_sc[...] = jnp.full_like(m_sc, NEG)
    acc_sc[...] = jnp.zeros_like(acc_sc)
    nsub = tq // tk

    def tile(j, r0, masked):
        off = pl.multiple_of(j * tk, tk)
        rows = tq - r0
        for hh in range(2):
            s = _nt_dot(q_ref[hh, r0:, :], k_ref[hh, pl.ds(off, tk), :])
            if masked:
                r = lax.broadcasted_iota(jnp.int32, (rows, tk), 0)
                c = lax.broadcasted_iota(jnp.int32, (rows, tk), 1)
                s = jnp.where(c <= r, s, NEG)
            m_prev = m_sc[hh, r0:, :]
            m_next = jnp.maximum(m_prev, jnp.max(s, axis=1, keepdims=True))
            alpha = jnp.exp2(m_prev - m_next)
            p = jnp.exp2(s - jnp.concatenate([m_next] * (tk // LANES), axis=1)).astype(BF16)
            pv = jnp.dot(p, v_ref[hh, pl.ds(off, tk), :], preferred_element_type=F32)
            acc_sc[hh, r0:, :] = alpha * acc_sc[hh, r0:, :] + pv
            m_sc[hh, r0:, :] = m_next

    def body(j, carry):
        tile(j, 0, False)
        return carry

    lax.fori_loop(0, i * nsub, body, 0)
    for u in range(nsub):
        tile(i * nsub + u, u * tk, True)

    a0 = acc_sc[0]
    a1 = acc_sc[1]
    lane = lax.broadcasted_iota(jnp.int32, (tq, LANES), 1)
    o_ref[...] = jnp.where(lane < 64, a0 / a0[:, 64:65], a1 / a1[:, 0:1]).astype(o_ref.dtype)


def _fox_attention(qa, ka, va, *, tq, tk):
    S = qa.shape[1]
    return pl.pallas_call(
        functools.partial(_fox_kernel, tq=tq, tk=tk),
        grid=(FOX_HEADS // 2, S // tq),
        in_specs=[pl.BlockSpec((2, tq, LANES), lambda p, i: (p, i, 0)),
                  pl.BlockSpec((2, S, LANES), lambda p, i: (p, 0, 0)),
                  pl.BlockSpec((2, S, LANES), lambda p, i: (p, 0, 0))],
        out_specs=pl.BlockSpec((tq, LANES), lambda p, i: (i, p)),
        out_shape=jax.ShapeDtypeStruct((S, FOX_WIDTH), BF16),
        scratch_shapes=[pltpu.VMEM((2, tq, LANES), F32), pltpu.VMEM((2, tq, LANES), F32)],
        compiler_params=_cparams(("arbitrary", "arbitrary")),
        name="fox_attn",
    )(qa, ka, va)


BIAS_KINDS = 3


def _bias_tile_kernel(table_ref, o_ref, *, t):
    h = pl.program_id(0)
    kind = pl.program_id(1)
    r = lax.broadcasted_iota(jnp.int32, (t, t), 0)
    c = lax.broadcasted_iota(jnp.int32, (t, t), 1)
    dist = r - c + kind * t
    n = jnp.maximum(dist, 0)
    max_exact = REL_BUCKETS // 2
    nf = jnp.maximum(n, 1).astype(F32)
    log_part = jnp.log(nf / max_exact) / math.log(REL_MAX_DISTANCE / max_exact) * (REL_BUCKETS - max_exact)
    large = jnp.minimum(max_exact + log_part.astype(jnp.int32), REL_BUCKETS - 1)
    bucket = jnp.where(n < max_exact, n, large)
    val = jnp.zeros((t, t), F32)
    for b in range(REL_BUCKETS):
        val = jnp.where(bucket == b, table_ref[h * REL_BUCKETS + b], val)
    o_ref[0, 0] = jnp.where(dist >= 0, val * LOG2E, NEG)


def _bias_tiles(table_flat, *, t):
    return pl.pallas_call(
        functools.partial(_bias_tile_kernel, t=t),
        grid=(DIFF_HEADS, BIAS_KINDS),
        in_specs=[pl.BlockSpec(memory_space=pltpu.SMEM)],
        out_specs=pl.BlockSpec((1, 1, t, t), lambda h, k: (h, k, 0, 0)),
        out_shape=jax.ShapeDtypeStruct((DIFF_HEADS, BIAS_KINDS, t, t), F32),
        compiler_params=_cparams(("arbitrary", "arbitrary")),
        name="bias_tiles",
    )(table_flat)


def _diff_kernel(q_ref, k_ref, v_ref, bias_ref, far_ref, lam_ref, g_ref, o_ref, m_sc, l_sc, acc_sc,
                 *, tq, tk, lam_init):
    i = pl.program_id(1)
    m_sc[...] = jnp.full_like(m_sc, NEG)
    l_sc[...] = jnp.zeros_like(l_sc)
    acc_sc[...] = jnp.zeros_like(acc_sc)
    nsub = tq // tk

    lane = lax.broadcasted_iota(jnp.int32, (tq, LANES), 1)
    q = q_ref[...]
    zero = jnp.zeros_like(q)
    qs = (jnp.where(lane < 64, q, zero), jnp.where(lane >= 64, q, zero))
    far = far_ref[0]

    def tile(j, r0, kinds):
        off = pl.multiple_of(j * tk, tk)
        k = k_ref[pl.ds(off, tk), :]
        v = v_ref[pl.ds(off, tk), :]
        for c in range(2):
            s = _nt_dot(qs[c][r0:], k)
            m_prev = m_sc[c, r0:, :]
            if kinds is None:
                m_next = jnp.maximum(m_prev, jnp.max(s, axis=1, keepdims=True) + far)
                shift = m_next - far
            else:
                s = jnp.concatenate([s[a * tk:(a + 1) * tk] + bias_ref[0, kd] for a, kd in enumerate(kinds)],
                                    axis=0)
                m_next = jnp.maximum(m_prev, jnp.max(s, axis=1, keepdims=True))
                shift = m_next
            p = jnp.exp2(s - jnp.concatenate([shift] * (tk // LANES), axis=1))
            alpha = jnp.exp2(m_prev - m_next)
            l_sc[c, r0:, :] = alpha * l_sc[c, r0:, :] + sum(p[:, a * LANES:(a + 1) * LANES]
                                                            for a in range(tk // LANES))
            pv = jnp.dot(p.astype(BF16), v, preferred_element_type=F32)
            acc_sc[c, r0:, :] = alpha * acc_sc[c, r0:, :] + pv
            m_sc[c, r0:, :] = m_next

    def body(j, carry):
        tile(j, 0, None)
        return carry

    lax.fori_loop(0, i * nsub - 1, body, 0)

    @pl.when(i > 0)
    def _():
        tile(i * nsub - 1, 0, (1,) + (2,) * (nsub - 1))

    for u in range(nsub):
        tile(i * nsub + u, u * tk, ((0, 1) + (2,) * nsub)[:nsub - u])

    lv = lam_ref[...]
    lam = (jnp.exp(jnp.sum(lv[0:1] * lv[1:2], axis=1, keepdims=True))
           - jnp.exp(jnp.sum(lv[2:3] * lv[3:4], axis=1, keepdims=True)) + lam_init)
    l0 = jnp.sum(l_sc[0], axis=1, keepdims=True)
    l1 = jnp.sum(l_sc[1], axis=1, keepdims=True)
    o = acc_sc[0] / l0 - lam * (acc_sc[1] / l1)
    o = o * lax.rsqrt(jnp.mean(o * o, axis=1, keepdims=True) + SUBLN_EPS) * g_ref[...] * (1.0 - lam_init)
    o_ref[...] = o.astype(o_ref.dtype)


def _diff_attention(dq, dk, dv, bias, far, lamvec, g, *, tq, tk, lam_init):
    S = dq.shape[0]
    return pl.pallas_call(
        functools.partial(_diff_kernel, tq=tq, tk=tk, lam_init=lam_init),
        grid=(DIFF_HEADS, S // tq),
        in_specs=[pl.BlockSpec((tq, LANES), lambda h, i: (i, h)),
                  pl.BlockSpec((S, LANES), lambda h, i: (0, h)),
                  pl.BlockSpec((S, LANES), lambda h, i: (0, h)),
                  pl.BlockSpec((1, BIAS_KINDS, tk, tk), lambda h, i: (h, 0, 0, 0)),
                  pl.BlockSpec((1, 1, LANES), lambda h, i: (h, 0, 0)),
                  pl.BlockSpec((8, LANES), lambda h, i: (0, 0)),
                  pl.BlockSpec((1, LANES), lambda h, i: (0, 0))],
        out_specs=pl.BlockSpec((tq, LANES), lambda h, i: (i, h)),
        out_shape=jax.ShapeDtypeStruct((S, DIFF_V_WIDTH), BF16),
        scratch_shapes=[pltpu.VMEM((2, tq, LANES), F32), pltpu.VMEM((2, tq, LANES), F32),
                        pltpu.VMEM((2, tq, LANES), F32)],
        compiler_params=_cparams(("arbitrary", "arbitrary")),
        name="diff_attn",
    )(dq, dk, dv, bias, far, lamvec, g)


def _merge_kernel(yf_ref, yd_ref, sa_ref, sb_ref, h_ref, wf_ref, wd_ref, wo_ref, g_ref, b_ref, o_ref, *, alpha):
    bf = jnp.dot(yf_ref[...], wf_ref[...], preferred_element_type=F32)
    bd = jnp.dot(yd_ref[...], wd_ref[...], preferred_element_type=F32)
    merged = sa_ref[...].astype(F32) * bf + sb_ref[...].astype(F32) * bd
    mix = jnp.dot(merged.astype(BF16), wo_ref[...], preferred_element_type=F32)
    o_ref[...] = _layer_norm(alpha * h_ref[...] + mix, g_ref[...], b_ref[...])


def _merge(yf, yd, sa, sb, h, wf, wd, wo, g, b, *, alpha, tm):
    S = h.shape[0]
    row = lambda n: pl.BlockSpec((tm, n), lambda i: (i, 0))
    full = lambda a: pl.BlockSpec(a.shape, lambda i: (0,) * a.ndim)
    return pl.pallas_call(
        functools.partial(_merge_kernel, alpha=alpha),
        grid=(S // tm,),
        in_specs=[row(512), row(512), row(D_MODEL), row(D_MODEL), row(D_MODEL),
                  full(wf), full(wd), full(wo), full(g), full(b)],
        out_specs=row(D_MODEL),
        out_shape=jax.ShapeDtypeStruct((S, D_MODEL), F32),
        compiler_params=_cparams(("arbitrary",)),
        name="merge_out",
    )(yf, yd, sa, sb, h, wf, wd, wo, g, b)


MOE_ROWS = 256
RANK_CHUNK = 256


def _router_kernel(h_ref, whi_ref, wlo_ref, cw_ref, rank_ref, rank_t_ref, cnt_ref):
    tm = h_ref.shape[0]
    h = h_ref[...]
    hhi = h.astype(BF16)
    hlo = (h - hhi.astype(F32)).astype(BF16)
    whi = whi_ref[...]
    dot = lambda a, b: jnp.dot(a, b, preferred_element_type=F32)
    logits = dot(hhi, whi) + dot(hhi, wlo_ref[...]) + dot(hlo, whi)
    lane = lax.broadcasted_iota(jnp.int32, logits.shape, 1).astype(F32)
    lg = jnp.where(lane < N_EXPERTS, logits, NEG)
    m1 = jnp.max(lg, axis=1, keepdims=True)
    i1 = jnp.min(jnp.where(lg == m1, lane, float(LANES)), axis=1, keepdims=True)
    lg2 = jnp.where(lane == i1, NEG, lg)
    m2 = jnp.max(lg2, axis=1, keepdims=True)
    i2 = jnp.min(jnp.where(lg2 == m2, lane, float(LANES)), axis=1, keepdims=True)
    e = jnp.exp(m2 - m1)
    g1 = 1.0 / (1.0 + e)
    chosen = jnp.logical_or(lane == i1, lane == i2)
    cw_ref[...] = jnp.where(lane == i1, g1, 0.0) + jnp.where(lane == i2, e * g1, 0.0)

    a = chosen.astype(BF16)
    r = lax.broadcasted_iota(jnp.int32, (RANK_CHUNK, RANK_CHUNK), 0)
    c = lax.broadcasted_iota(jnp.int32, (RANK_CHUNK, RANK_CHUNK), 1)
    strict_lower = (c < r).astype(BF16)
    ones_row = jnp.ones((8, RANK_CHUNK), BF16)
    carry = jnp.zeros((1, LANES), F32)
    for q in range(tm // RANK_CHUNK):
        rows = slice(q * RANK_CHUNK, (q + 1) * RANK_CHUNK)
        rank = jnp.where(chosen[rows], dot(strict_lower, a[rows]) + carry, -1.0)
        rank_ref[rows, :] = rank
        rank_t_ref[0, :, rows] = jnp.transpose(rank)[0:N_EXPERTS, :]
        carry = carry + dot(ones_row, a[rows])[0:1, :]
    cnt_ref[0] = jnp.broadcast_to(carry, (8, LANES)).astype(jnp.int32)


def _router(h, whi, wlo, *, tm):
    S = h.shape[0]
    nb = S // tm
    return pl.pallas_call(
        _router_kernel,
        grid=(nb,),
        in_specs=[pl.BlockSpec((tm, D_MODEL), lambda i: (i, 0)),
                  pl.BlockSpec(whi.shape, lambda i: (0, 0)), pl.BlockSpec(wlo.shape, lambda i: (0, 0))],
        out_specs=[pl.BlockSpec((tm, LANES), lambda i: (i, 0)),
                   pl.BlockSpec((tm, LANES), lambda i: (i, 0)),
                   pl.BlockSpec((1, N_EXPERTS, tm), lambda i: (i, 0, 0)),
                   pl.BlockSpec((1, 8, LANES), lambda i: (i, 0, 0))],
        out_shape=[jax.ShapeDtypeStruct((S, LANES), F32),
                   jax.ShapeDtypeStruct((S, LANES), F32),
                   jax.ShapeDtypeStruct((nb, N_EXPERTS, tm), F32),
                   jax.ShapeDtypeStruct((nb, 8, LANES), jnp.int32)],
        compiler_params=_cparams(("arbitrary",)),
        name="router",
    )(h, whi, wlo)


def _moe_kernel(cnt_ref, h_ref, cw_ref, rank_ref, rank_t_ref, wg_ref, wu_ref, wd_ref, g_ref, b_ref, o_ref,
                hb_sc, xs_sc, acc_sc, *, alpha):
    tm = h_ref.shape[0]
    i = pl.program_id(0)
    e = pl.program_id(1)
    c = pl.program_id(2)
    last_c = c == pl.num_programs(2) - 1
    n_steps = (cnt_ref[i * N_EXPERTS + e] + MOE_ROWS - 1) // MOE_ROWS

    @pl.when(jnp.logical_and(e == 0, c == 0))
    def _():
        hb_sc[...] = h_ref[...].astype(BF16)
        o_ref[...] = jnp.zeros_like(o_ref)

    @pl.when(c == 0)
    def _():
        rank_row = rank_t_ref[0, pl.ds(e, 1), :]

        def gather(t, carry):
            r0 = pl.multiple_of(t * MOE_ROWS, MOE_ROWS)
            slot = (lax.broadcasted_iota(jnp.int32, (MOE_ROWS, tm), 0) + r0).astype(F32)
            onehot = (rank_row == slot).astype(BF16)
            xs_sc[pl.ds(r0, MOE_ROWS), :] = jnp.dot(onehot, hb_sc[...], preferred_element_type=F32).astype(BF16)
            acc_sc[pl.ds(r0, MOE_ROWS), :] = jnp.zeros((MOE_ROWS, D_MODEL), F32)
            return carry

        lax.fori_loop(0, n_steps, gather, 0)

    def swiglu(t, carry):
        r0 = pl.multiple_of(t * MOE_ROWS, MOE_ROWS)
        x = xs_sc[pl.ds(r0, MOE_ROWS), :]
        gate = jnp.dot(x, wg_ref[0], preferred_element_type=F32)
        up = jnp.dot(x, wu_ref[0], preferred_element_type=F32)
        a = (gate * _sigmoid(gate) * up).astype(BF16)
        acc_sc[pl.ds(r0, MOE_ROWS), :] += jnp.dot(a, wd_ref[0], preferred_element_type=F32)
        return carry

    lax.fori_loop(0, n_steps, swiglu, 0)

    @pl.when(last_c)
    def _():
        lane = lax.broadcasted_iota(jnp.int32, (tm, LANES), 1)
        pick = lambda ref: jnp.sum(jnp.where(lane == e, ref[...], 0.0), axis=1, keepdims=True)
        rank_col = pick(rank_ref)
        gate_col = pick(cw_ref)

        def scatter(t, carry):
            r0 = pl.multiple_of(t * MOE_ROWS, MOE_ROWS)
            slot = (lax.broadcasted_iota(jnp.int32, (tm, MOE_ROWS), 1) + r0).astype(F32)
            onehot = (rank_col == slot).astype(BF16)
            y = acc_sc[pl.ds(r0, MOE_ROWS), :].astype(BF16)
            o_ref[...] += gate_col * jnp.dot(onehot, y, preferred_element_type=F32)
            return carry

        lax.fori_loop(0, n_steps, scatter, 0)

    @pl.when(jnp.logical_and(e == pl.num_programs(1) - 1, last_c))
    def _():
        o_ref[...] = _layer_norm(alpha * h_ref[...] + o_ref[...], g_ref[...], b_ref[...])


def _moe(h, cw, rank, rank_t, counts, w_gu, w_dn, g, b, *, alpha, tm, fc):
    S = h.shape[0]
    E, _, F2 = w_gu.shape
    nc = F2 // 2 // fc
    once = pl.Buffered(1)
    grid_spec = pltpu.PrefetchScalarGridSpec(
        num_scalar_prefetch=1,
        grid=(S // tm, E, nc),
        in_specs=[pl.BlockSpec((tm, D_MODEL), lambda i, e, c, cnt: (i, 0), pipeline_mode=once),
                  pl.BlockSpec((tm, LANES), lambda i, e, c, cnt: (i, 0), pipeline_mode=once),
                  pl.BlockSpec((tm, LANES), lambda i, e, c, cnt: (i, 0), pipeline_mode=once),
                  pl.BlockSpec((1, N_EXPERTS, tm), lambda i, e, c, cnt: (i, 0, 0), pipeline_mode=once),
                  pl.BlockSpec((1, D_MODEL, fc), lambda i, e, c, cnt: (e, 0, c)),
                  pl.BlockSpec((1, D_MODEL, fc), lambda i, e, c, cnt: (e, 0, nc + c)),
                  pl.BlockSpec((1, fc, D_MODEL), lambda i, e, c, cnt: (e, c, 0)),
                  pl.BlockSpec(g.shape, lambda i, e, c, cnt: (0, 0)),
                  pl.BlockSpec(b.shape, lambda i, e, c, cnt: (0, 0))],
        out_specs=pl.BlockSpec((tm, D_MODEL), lambda i, e, c, cnt: (i, 0), pipeline_mode=once),
        scratch_shapes=[pltpu.VMEM((tm, D_MODEL), BF16), pltpu.VMEM((tm, D_MODEL), BF16),
                        pltpu.VMEM((tm, D_MODEL), F32)],
    )
    return pl.pallas_call(
        functools.partial(_moe_kernel, alpha=alpha),
        grid_spec=grid_spec,
        out_shape=jax.ShapeDtypeStruct((S, D_MODEL), F32),
        compiler_params=_cparams(("arbitrary", "arbitrary", "arbitrary")),
        name="moe_ffn",
    )(counts, h, cw, rank, rank_t, w_gu, w_gu, w_dn, g, b)


def _ffn_kernel(h_ref, wg_ref, wu_ref, wd_ref, g_ref, b_ref, o_ref, hb_sc, acc_sc, *, alpha):
    c = pl.program_id(1)

    @pl.when(c == 0)
    def _():
        hb_sc[...] = h_ref[...].astype(BF16)
        acc_sc[...] = jnp.zeros_like(acc_sc)

    hb = hb_sc[...]
    gate = jnp.dot(hb, wg_ref[...], preferred_element_type=F32)
    up = jnp.dot(hb, wu_ref[...], preferred_element_type=F32)
    a = gate * _sigmoid(gate) * up
    acc_sc[...] += jnp.dot(a.astype(BF16), wd_ref[...], preferred_element_type=F32)

    @pl.when(c == pl.num_programs(1) - 1)
    def _():
        o_ref[...] = _layer_norm(alpha * h_ref[...] + acc_sc[...], g_ref[...], b_ref[...])


def _ffn(h, w_gu, w_dn, g, b, *, alpha, tm, fc):
    S = h.shape[0]
    nc = w_gu.shape[1] // 2 // fc
    return pl.pallas_call(
        functools.partial(_ffn_kernel, alpha=alpha),
        grid=(S // tm, nc),
        in_specs=[pl.BlockSpec((tm, D_MODEL), lambda i, c: (i, 0)),
                  pl.BlockSpec((D_MODEL, fc), lambda i, c: (0, c)),
                  pl.BlockSpec((D_MODEL, fc), lambda i, c: (0, nc + c)),
                  pl.BlockSpec((fc, D_MODEL), lambda i, c: (c, 0)),
                  pl.BlockSpec(g.shape, lambda i, c: (0, 0)), pl.BlockSpec(b.shape, lambda i, c: (0, 0))],
        out_specs=pl.BlockSpec((tm, D_MODEL), lambda i, c: (i, 0)),
        out_shape=jax.ShapeDtypeStruct((S, D_MODEL), F32),
        scratch_shapes=[pltpu.VMEM((tm, D_MODEL), BF16), pltpu.VMEM((tm, D_MODEL), F32)],
        compiler_params=_cparams(("arbitrary", "arbitrary")),
        name="dense_ffn",
    )(h, w_gu, w_gu, w_dn, g, b)


def _pad_lanes(a, n=LANES):
    return jnp.pad(a, [(0, 0)] * (a.ndim - 1) + [(0, n - a.shape[-1])])


def kernel(x, ln_in_g, ln_in_b, w_in, b_fgate, lam_q1, lam_k1, lam_q2, lam_k2, subln_g, w_branch_fox, w_branch_diff, w_out, ln_mix_g, ln_mix_b, rel_bias, w_ffn_gate_up, w_ffn_down, w_router, w_expert_gate_up, w_expert_down, ln_ffn_g, ln_ffn_b):
    B, S, _ = x.shape
    assert B == 1
    depth = w_in.shape[0]
    alpha = (2.0 * depth) ** 0.25
    tq = min(ATTN_TQ, S)
    tk = min(ATTN_TK, tq)
    tm = min(512, S)
    tf = min(1024, S)
    te = min(MOE_TM, S)
    assert tk >= REL_MAX_DISTANCE and tq % tk == 0 and S % tq == 0 and S % tf == 0 and S % tm == 0 and S % te == 0
    assert te % MOE_ROWS == 0 and te % RANK_CHUNK == 0
    row = lambda v: v.reshape(1, -1).astype(F32)

    table = rel_bias.astype(F32)
    bias = _bias_tiles(table.T.reshape(-1), t=tk)
    far = jnp.broadcast_to((table[REL_BUCKETS - 1] * LOG2E)[:, None, None], (DIFF_HEADS, 1, LANES))

    h = x[0]
    for l in range(depth):
        lam_init = 0.8 - 0.6 * math.exp(-0.3 * l)
        fq, fk, fv, fg, dq, dk, dv, ga, gb = jnp.split(w_in[l], list(_cumsum(IN_SPLIT_SIZES))[:-1], axis=-1)
        w_cat = jnp.concatenate([fq, fk, fv, dq, dk, dv, ga, gb, _pad_lanes(fg)], axis=-1).astype(BF16)
        outs = _inproj(h, row(ln_in_g), row(ln_in_b), w_cat, do_ln=(l == 0), tm=tm)
        fq, fk, fv, dq, dk, dv, sa, sb, fg = outs[:9]
        if l == 0:
            h = outs[9]
        qa, ka, va = _prep(fq, fk, fv, fg, _pad_lanes(row(b_fgate[l])), tb=tm)
        y_fox = _fox_attention(qa, ka, va, tq=tq, tk=tk)
        lamvec = jnp.zeros((8, LANES), F32).at[0:4, 0:HEAD_DIM].set(
            jnp.stack([lam_q1[l], lam_k1[l], lam_q2[l], lam_k2[l]]).astype(F32))
        y_diff = _diff_attention(dq, dk, dv, bias, far, lamvec, row(subln_g[l]), tq=tq, tk=tk,
                                 lam_init=lam_init)
        h = _merge(y_fox, y_diff, sa, sb, h, w_branch_fox[l].astype(BF16), w_branch_diff[l].astype(BF16),
                   w_out[l].astype(BF16), row(ln_mix_g[l]), row(ln_mix_b[l]), alpha=alpha, tm=tm)
        g, b = row(ln_ffn_g[l]), row(ln_ffn_b[l])
        if l % 2 == 0:
            h = _ffn(h, w_ffn_gate_up[l // 2].astype(BF16), w_ffn_down[l // 2].astype(BF16), g, b,
                     alpha=alpha, tm=tf, fc=1408)
        else:
            wr = _pad_lanes(w_router[l // 2].astype(F32))
            whi = wr.astype(BF16)
            wlo = (wr - whi.astype(F32)).astype(BF16)
            cw, rank, rank_t, counts = _router(h, whi, wlo, tm=te)
            h = _moe(h, cw, rank, rank_t, counts[:, 0, :N_EXPERTS].reshape(-1),
                     w_expert_gate_up[l // 2].astype(BF16), w_expert_down[l // 2].astype(BF16),
                     g, b, alpha=alpha, tm=te, fc=896)
    return h[None]


def _cumsum(sizes):
    tot = 0
    for s in sizes:
        tot += s
        yield tot
```

```python
import functools
import math

import jax
import jax.numpy as jnp
from jax import lax
from jax.experimental import pallas as pl
from jax.experimental.pallas import tpu as pltpu

F32 = jnp.float32
BF16 = jnp.bfloat16

D_MODEL = 1024
HEAD_DIM = 64
FOX_HEADS = 8
FOX_WIDTH = FOX_HEADS * HEAD_DIM
DIFF_HEADS = 4
DIFF_QK_WIDTH = DIFF_HEADS * 2 * HEAD_DIM
DIFF_V_DIM = 2 * HEAD_DIM
DIFF_V_WIDTH = DIFF_HEADS * DIFF_V_DIM
IN_SPLIT_SIZES = (FOX_WIDTH, FOX_WIDTH, FOX_WIDTH, FOX_HEADS, DIFF_QK_WIDTH, DIFF_QK_WIDTH, DIFF_V_WIDTH,
                  D_MODEL, D_MODEL)
REL_BUCKETS = 32
REL_MAX_DISTANCE = 128
N_EXPERTS = 8
LN_EPS = 1e-5
SUBLN_EPS = 1e-5

LANES = 128
LOG2E = 1.4426950408889634
Q_SCALE = HEAD_DIM ** -0.5 * LOG2E
NEG = -1e30
VMEM_LIMIT = 56 * 1024 * 1024
ATTN_TQ = 2048
ATTN_TK = 512
MOE_TM = 2048


def _cparams(sem):
    return pltpu.CompilerParams(dimension_semantics=sem, vmem_limit_bytes=VMEM_LIMIT)


def _layer_norm(x, g, b):
    mu = jnp.mean(x, axis=-1, keepdims=True)
    xc = x - mu
    var = jnp.mean(xc * xc, axis=-1, keepdims=True)
    return xc * lax.rsqrt(var + LN_EPS) * g + b


def _sigmoid(x):
    return 1.0 / (1.0 + jnp.exp(-x))


def _split3(x):
    hi = x.astype(BF16)
    r1 = x - hi.astype(F32)
    mid = r1.astype(BF16)
    lo = (r1 - mid.astype(F32)).astype(BF16)
    return hi, mid, lo


_C_FQ, _C_FK, _C_FV, _C_DQ, _C_DK, _C_DV, _C_GA, _C_GB, _C_FG, _C_END = (
    0, 512, 1024, 1536, 2048, 2560, 3072, 4096, 5120, 5248)


def _inproj_kernel(x_ref, g_ref, b_ref, w_ref, fq, fk, fv, dq, dk, dv, sa, sb, fg, *h_out, do_ln):
    x = x_ref[...]
    if do_ln:
        x = _layer_norm(x, g_ref[...], b_ref[...])
        h_out[0][...] = x
    xb = x.astype(BF16)

    def mm(a, b):
        return jnp.dot(xb, w_ref[:, a:b], preferred_element_type=F32)

    fq[...] = (mm(_C_FQ, _C_FK) * Q_SCALE).astype(BF16)
    fk[...] = mm(_C_FK, _C_FV).astype(BF16)
    fv[...] = mm(_C_FV, _C_DQ).astype(BF16)
    dq[...] = (mm(_C_DQ, _C_DK) * Q_SCALE).astype(BF16)
    dk[...] = mm(_C_DK, _C_DV).astype(BF16)
    dv[...] = mm(_C_DV, _C_GA).astype(BF16)
    sa[...] = _sigmoid(mm(_C_GA, _C_GB)).astype(BF16)
    sb[...] = _sigmoid(mm(_C_GB, _C_FG)).astype(BF16)
    fg[...] = mm(_C_FG, _C_END)


def _inproj(x, g, b, w, *, do_ln, tm):
    S = x.shape[0]
    row = lambda n: pl.BlockSpec((tm, n), lambda i: (i, 0))
    full = lambda a: pl.BlockSpec(a.shape, lambda i: (0,) * a.ndim)
    out_shape = [jax.ShapeDtypeStruct((S, 512), BF16)] * 6 + [jax.ShapeDtypeStruct((S, D_MODEL), BF16)] * 2 + [
        jax.ShapeDtypeStruct((S, LANES), F32)]
    out_specs = [row(512)] * 6 + [row(D_MODEL)] * 2 + [row(LANES)]
    if do_ln:
        out_shape.append(jax.ShapeDtypeStruct((S, D_MODEL), F32))
        out_specs.append(row(D_MODEL))
    return pl.pallas_call(
        functools.partial(_inproj_kernel, do_ln=do_ln),
        grid=(S // tm,),
        in_specs=[row(D_MODEL), full(g), full(b), full(w)],
        out_specs=out_specs,
        out_shape=out_shape,
        compiler_params=_cparams(("arbitrary",)),
        name="inproj",
    )(x, g, b, w)


def _prep_kernel(fq_ref, fk_ref, fv_ref, fg_ref, bf_ref, qa_ref, ka_ref, va_ref, carry_ref, *, tb):
    @pl.when(pl.program_id(0) == 0)
    def _():
        carry_ref[...] = jnp.zeros_like(carry_ref)

    z = fg_ref[...] + bf_ref[...]
    logf = jnp.minimum(z, 0.0) - jnp.log1p(jnp.exp(-jnp.abs(z)))
    r = lax.broadcasted_iota(jnp.int32, (tb, tb), 0)
    c = lax.broadcasted_iota(jnp.int32, (tb, tb), 1)
    tri = (c <= r).astype(BF16)
    hi, mid, lo = _split3(logf)
    dot = lambda a: jnp.dot(tri, a, preferred_element_type=F32)
    cum = (dot(hi) + dot(mid)) + dot(lo) + carry_ref[0:1, :]
    carry_ref[0:1, :] = cum[tb - 1:tb, :]

    c_hi, c_mid, c_lo = (t.astype(F32) for t in _split3(cum * LOG2E))
    lane = lax.broadcasted_iota(jnp.int32, (tb, LANES), 1)
    one = jnp.ones((tb, LANES), F32)
    zero = jnp.zeros((tb, LANES), F32)
    for h in range(FOX_HEADS):
        p, odd = divmod(h, 2)
        col = lambda t: jnp.broadcast_to(t[:, h:h + 1], (tb, LANES))
        chi, cmid, clo = col(c_hi), col(c_mid), col(c_lo)

        def head_slab(ref):
            slab = ref[:, p * LANES:(p + 1) * LANES].astype(F32)
            return pltpu.roll(slab, HEAD_DIM, 1) if odd else slab

        qa = jnp.where(lane < 64, head_slab(fq_ref),
                       jnp.where(lane == 64, chi, jnp.where(lane == 65, cmid, jnp.where(lane == 66, clo,
                                 jnp.where(lane < 70, one, zero)))))
        ka = jnp.where(lane < 64, head_slab(fk_ref),
                       jnp.where(lane < 67, one, jnp.where(lane == 67, -chi, jnp.where(lane == 68, -cmid,
                                 jnp.where(lane == 69, -clo, zero)))))
        qa_ref[h] = qa.astype(BF16)
        ka_ref[h] = ka.astype(BF16)
        vslab = fv_ref[:, p * LANES:(p + 1) * LANES].astype(F32)
        if odd:
            va = jnp.where(lane >= 64, vslab, jnp.where(lane == 0, one, zero))
        else:
            va = jnp.where(lane < 64, vslab, jnp.where(lane == 64, one, zero))
        va_ref[h] = va.astype(BF16)


def _prep(fq, fk, fv, fg, bf, *, tb):
    S = fq.shape[0]
    row = lambda n: pl.BlockSpec((tb, n), lambda i: (i, 0))
    hspec = pl.BlockSpec((FOX_HEADS, tb, LANES), lambda i: (0, i, 0))
    hshape = jax.ShapeDtypeStruct((FOX_HEADS, S, LANES), BF16)
    return pl.pallas_call(
        functools.partial(_prep_kernel, tb=tb),
        grid=(S // tb,),
        in_specs=[row(512), row(512), row(512), row(LANES), pl.BlockSpec((1, LANES), lambda i: (0, 0))],
        out_specs=[hspec] * 3,
        out_shape=[hshape] * 3,
        scratch_shapes=[pltpu.VMEM((8, LANES), F32)],
        compiler_params=_cparams(("arbitrary",)),
        name="fox_prep",
    )(fq, fk, fv, fg, bf)


def _nt_dot(a, b):
    return lax.dot_general(a, b, (((1,), (1,)), ((), ())), preferred_element_type=F32)


def _fox_kernel(q_ref, k_ref, v_ref, o_ref, m_sc, acc_sc, *, tq, tk):
    i = pl.program_id(1)
    m_sc[...] = jnp.full_like(m_sc, NEG)
    acc_sc[...] = jnp.zeros_like(acc_sc)
    nsub = tq // tk

    def tile(j, r0, masked):
        off = pl.multiple_of(j * tk, tk)
        rows = tq - r0
        for hh in range(2):
            s = _nt_dot(q_ref[hh, r0:, :], k_ref[hh, pl.ds(off, tk), :])
            if masked:
                r = lax.broadcasted_iota(jnp.int32, (rows, tk), 0)
                c = lax.broadcasted_iota(jnp.int32, (rows, tk), 1)
                s = jnp.where(c <= r, s, NEG)
            m_prev = m_sc[hh, r0:, :]
            m_next = jnp.maximum(m_prev, jnp.max(s, axis=1, keepdims=True))
            alpha = jnp.exp2(m_prev - m_next)
            p = jnp.exp2(s - jnp.concatenate([m_next] * (tk // LANES), axis=1)).astype(BF16)
            pv = jnp.dot(p, v_ref[hh, pl.ds(off, tk), :], preferred_element_type=F32)
            acc_sc[hh, r0:, :] = alpha * acc_sc[hh, r0:, :] + pv
            m_sc[hh, r0:, :] = m_next

    def body(j, carry):
        tile(j, 0, False)
        return carry

    lax.fori_loop(0, i * nsub, body, 0)
    for u in range(nsub):
        tile(i * nsub + u, u * tk, True)

    a0 = acc_sc[0]
    a1 = acc_sc[1]
    lane = lax.broadcasted_iota(jnp.int32, (tq, LANES), 1)
    o_ref[...] = jnp.where(lane < 64, a0 / a0[:, 64:65], a1 / a1[:, 0:1]).astype(o_ref.dtype)


def _fox_attention(qa, ka, va, *, tq, tk):
    S = qa.shape[1]
    return pl.pallas_call(
        functools.partial(_fox_kernel, tq=tq, tk=tk),
        grid=(FOX_HEADS // 2, S // tq),
        in_specs=[pl.BlockSpec((2, tq, LANES), lambda p, i: (p, i, 0)),
                  pl.BlockSpec((2, S, LANES), lambda p, i: (p, 0, 0)),
                  pl.BlockSpec((2, S, LANES), lambda p, i: (p, 0, 0))],
        out_specs=pl.BlockSpec((tq, LANES), lambda p, i: (i, p)),
        out_shape=jax.ShapeDtypeStruct((S, FOX_WIDTH), BF16),
        scratch_shapes=[pltpu.VMEM((2, tq, LANES), F32), pltpu.VMEM((2, tq, LANES), F32)],
        compiler_params=_cparams(("arbitrary", "arbitrary")),
        name="fox_attn",
    )(qa, ka, va)


BIAS_KINDS = 2


def _bias_tile_kernel(table_ref, o_ref, *, t):
    h = pl.program_id(0)
    kind = pl.program_id(1)
    r = lax.broadcasted_iota(jnp.int32, (t, t), 0)
    c = lax.broadcasted_iota(jnp.int32, (t, t), 1)
    dist = r - c + kind * t
    n = jnp.maximum(dist, 0)
    max_exact = REL_BUCKETS // 2
    nf = jnp.maximum(n, 1).astype(F32)
    log_part = jnp.log(nf / max_exact) / math.log(REL_MAX_DISTANCE / max_exact) * (REL_BUCKETS - max_exact)
    large = jnp.minimum(max_exact + log_part.astype(jnp.int32), REL_BUCKETS - 1)
    bucket = jnp.where(n < max_exact, n, large)
    val = jnp.zeros((t, t), F32)
    for b in range(REL_BUCKETS):
        val = jnp.where(bucket == b, table_ref[h * REL_BUCKETS + b], val)
    far = table_ref[h * REL_BUCKETS + REL_BUCKETS - 1]
    o_ref[0, 0] = jnp.where(dist >= 0, (val - far) * LOG2E, NEG)


def _bias_tiles(table_flat, *, t):
    return pl.pallas_call(
        functools.partial(_bias_tile_kernel, t=t),
        grid=(DIFF_HEADS, BIAS_KINDS),
        in_specs=[pl.BlockSpec(memory_space=pltpu.SMEM)],
        out_specs=pl.BlockSpec((1, 1, t, t), lambda h, k: (h, k, 0, 0)),
        out_shape=jax.ShapeDtypeStruct((DIFF_HEADS, BIAS_KINDS, t, t), F32),
        compiler_params=_cparams(("arbitrary", "arbitrary")),
        name="bias_tiles",
    )(table_flat)


def _diff_kernel(q_ref, k_ref, v_ref, bias_ref, lam_ref, g_ref, o_ref, m_sc, l_sc, acc_sc,
                 *, tq, tk, lam_init):
    i = pl.program_id(1)
    m_sc[...] = jnp.full_like(m_sc, NEG)
    l_sc[...] = jnp.zeros_like(l_sc)
    acc_sc[...] = jnp.zeros_like(acc_sc)
    nsub = tq // tk

    lane = lax.broadcasted_iota(jnp.int32, (tq, LANES), 1)
    q = q_ref[...]
    zero = jnp.zeros_like(q)
    qs = (jnp.where(lane < 64, q, zero), jnp.where(lane >= 64, q, zero))

    def tile(j, r0, kinds):
        off = pl.multiple_of(j * tk, tk)
        k = k_ref[pl.ds(off, tk), :]
        v = v_ref[pl.ds(off, tk), :]
        for c in range(2):
            s = _nt_dot(qs[c][r0:], k)
            m_prev = m_sc[c, r0:, :]
            if kinds is not None:
                s = jnp.concatenate([s[a * tk:(a + 1) * tk] + bias_ref[0, kd] if kd < 2 else s[a * tk:(a + 1) * tk]
                                     for a, kd in enumerate(kinds)], axis=0)
            m_next = jnp.maximum(m_prev, jnp.max(s, axis=1, keepdims=True))
            p = jnp.exp2(s - jnp.concatenate([m_next] * (tk // LANES), axis=1))
            alpha = jnp.exp2(m_prev - m_next)
            l_sc[c, r0:, :] = alpha * l_sc[c, r0:, :] + sum(p[:, a * LANES:(a + 1) * LANES]
                                                            for a in range(tk // LANES))
            pv = jnp.dot(p.astype(BF16), v, preferred_element_type=F32)
            acc_sc[c, r0:, :] = alpha * acc_sc[c, r0:, :] + pv
            m_sc[c, r0:, :] = m_next

    def body(j, carry):
        tile(j, 0, None)
        return carry

    lax.fori_loop(0, i * nsub - 1, body, 0)

    @pl.when(i > 0)
    def _():
        tile(i * nsub - 1, 0, (1,) + (2,) * (nsub - 1))

    for u in range(nsub):
        tile(i * nsub + u, u * tk, ((0, 1) + (2,) * nsub)[:nsub - u])

    lv = lam_ref[...]
    lam = (jnp.exp(jnp.sum(lv[0:1] * lv[1:2], axis=1, keepdims=True))
           - jnp.exp(jnp.sum(lv[2:3] * lv[3:4], axis=1, keepdims=True)) + lam_init)
    l0 = jnp.sum(l_sc[0], axis=1, keepdims=True)
    l1 = jnp.sum(l_sc[1], axis=1, keepdims=True)
    o = acc_sc[0] / l0 - lam * (acc_sc[1] / l1)
    o = o * lax.rsqrt(jnp.mean(o * o, axis=1, keepdims=True) + SUBLN_EPS) * g_ref[...] * (1.0 - lam_init)
    o_ref[...] = o.astype(o_ref.dtype)


def _diff_attention(dq, dk, dv, bias, lamvec, g, *, tq, tk, lam_init):
    S = dq.shape[0]
    return pl.pallas_call(
        functools.partial(_diff_kernel, tq=tq, tk=tk, lam_init=lam_init),
        grid=(DIFF_HEADS, S // tq),
        in_specs=[pl.BlockSpec((tq, LANES), lambda h, i: (i, h)),
                  pl.BlockSpec((S, LANES), lambda h, i: (0, h)),
                  pl.BlockSpec((S, LANES), lambda h, i: (0, h)),
                  pl.BlockSpec((1, BIAS_KINDS, tk, tk), lambda h, i: (h, 0, 0, 0)),
                  pl.BlockSpec((8, LANES), lambda h, i: (0, 0)),
                  pl.BlockSpec((1, LANES), lambda h, i: (0, 0))],
        out_specs=pl.BlockSpec((tq, LANES), lambda h, i: (i, h)),
        out_shape=jax.ShapeDtypeStruct((S, DIFF_V_WIDTH), BF16),
        scratch_shapes=[pltpu.VMEM((2, tq, LANES), F32), pltpu.VMEM((2, tq, LANES), F32),
                        pltpu.VMEM((2, tq, LANES), F32)],
        compiler_params=_cparams(("arbitrary", "arbitrary")),
        name="diff_attn",
    )(dq, dk, dv, bias, lamvec, g)


def _merge_kernel(yf_ref, yd_ref, sa_ref, sb_ref, h_ref, wf_ref, wd_ref, wo_ref, g_ref, b_ref, o_ref, *, alpha):
    bf = jnp.dot(yf_ref[...], wf_ref[...], preferred_element_type=F32)
    bd = jnp.dot(yd_ref[...], wd_ref[...], preferred_element_type=F32)
    merged = sa_ref[...].astype(F32) * bf + sb_ref[...].astype(F32) * bd
    mix = jnp.dot(merged.astype(BF16), wo_ref[...], preferred_element_type=F32)
    o_ref[...] = _layer_norm(alpha * h_ref[...] + mix, g_ref[...], b_ref[...])


def _merge(yf, yd, sa, sb, h, wf, wd, wo, g, b, *, alpha, tm):
    S = h.shape[0]
    row = lambda n: pl.BlockSpec((tm, n), lambda i: (i, 0))
    full = lambda a: pl.BlockSpec(a.shape, lambda i: (0,) * a.ndim)
    return pl.pallas_call(
        functools.partial(_merge_kernel, alpha=alpha),
        grid=(S // tm,),
        in_specs=[row(512), row(512), row(D_MODEL), row(D_MODEL), row(D_MODEL),
                  full(wf), full(wd), full(wo), full(g), full(b)],
        out_specs=row(D_MODEL),
        out_shape=jax.ShapeDtypeStruct((S, D_MODEL), F32),
        compiler_params=_cparams(("arbitrary",)),
        name="merge_out",
    )(yf, yd, sa, sb, h, wf, wd, wo, g, b)


MOE_ROWS = 256
RANK_CHUNK = 256


def _router_kernel(h_ref, whi_ref, wlo_ref, cw_ref, rank_ref, rank_t_ref, cnt_ref):
    tm = h_ref.shape[0]
    h = h_ref[...]
    hhi = h.astype(BF16)
    hlo = (h - hhi.astype(F32)).astype(BF16)
    whi = whi_ref[...]
    dot = lambda a, b: jnp.dot(a, b, preferred_element_type=F32)
    logits = dot(hhi, whi) + dot(hhi, wlo_ref[...]) + dot(hlo, whi)
    lane = lax.broadcasted_iota(jnp.int32, logits.shape, 1).astype(F32)
    lg = jnp.where(lane < N_EXPERTS, logits, NEG)
    m1 = jnp.max(lg, axis=1, keepdims=True)
    i1 = jnp.min(jnp.where(lg == m1, lane, float(LANES)), axis=1, keepdims=True)
    lg2 = jnp.where(lane == i1, NEG, lg)
    m2 = jnp.max(lg2, axis=1, keepdims=True)
    i2 = jnp.min(jnp.where(lg2 == m2, lane, float(LANES)), axis=1, keepdims=True)
    e = jnp.exp(m2 - m1)
    g1 = 1.0 / (1.0 + e)
    chosen = jnp.logical_or(lane == i1, lane == i2)
    cw_ref[...] = jnp.where(lane == i1, g1, 0.0) + jnp.where(lane == i2, e * g1, 0.0)

    a = chosen.astype(BF16)
    r = lax.broadcasted_iota(jnp.int32, (RANK_CHUNK, RANK_CHUNK), 0)
    c = lax.broadcasted_iota(jnp.int32, (RANK_CHUNK, RANK_CHUNK), 1)
    strict_lower = (c < r).astype(BF16)
    ones_row = jnp.ones((8, RANK_CHUNK), BF16)
    carry = jnp.zeros((1, LANES), F32)
    for q in range(tm // RANK_CHUNK):
        rows = slice(q * RANK_CHUNK, (q + 1) * RANK_CHUNK)
        rank = jnp.where(chosen[rows], dot(strict_lower, a[rows]) + carry, -1.0)
        rank_ref[rows, :] = rank
        rank_t_ref[0, :, rows] = jnp.transpose(rank)[0:N_EXPERTS, :]
        carry = carry + dot(ones_row, a[rows])[0:1, :]
    cnt_ref[0] = jnp.broadcast_to(carry, (8, LANES)).astype(jnp.int32)


def _router(h, whi, wlo, *, tm):
    S = h.shape[0]
    nb = S // tm
    return pl.pallas_call(
        _router_kernel,
        grid=(nb,),
        in_specs=[pl.BlockSpec((tm, D_MODEL), lambda i: (i, 0)),
                  pl.BlockSpec(whi.shape, lambda i: (0, 0)), pl.BlockSpec(wlo.shape, lambda i: (0, 0))],
        out_specs=[pl.BlockSpec((tm, LANES), lambda i: (i, 0)),
                   pl.BlockSpec((tm, LANES), lambda i: (i, 0)),
                   pl.BlockSpec((1, N_EXPERTS, tm), lambda i: (i, 0, 0)),
                   pl.BlockSpec((1, 8, LANES), lambda i: (i, 0, 0))],
        out_shape=[jax.ShapeDtypeStruct((S, LANES), F32),
                   jax.ShapeDtypeStruct((S, LANES), F32),
                   jax.ShapeDtypeStruct((nb, N_EXPERTS, tm), F32),
                   jax.ShapeDtypeStruct((nb, 8, LANES), jnp.int32)],
        compiler_params=_cparams(("arbitrary",)),
        name="router",
    )(h, whi, wlo)


def _moe_kernel(cnt_ref, h_ref, cw_ref, rank_ref, rank_t_ref, wg_ref, wu_ref, wd_ref, g_ref, b_ref, o_ref,
                hb_sc, xs_sc, acc_sc, *, alpha):
    tm = h_ref.shape[0]
    i = pl.program_id(0)
    e = pl.program_id(1)
    c = pl.program_id(2)
    last_c = c == pl.num_programs(2) - 1
    n_steps = (cnt_ref[i * N_EXPERTS + e] + MOE_ROWS - 1) // MOE_ROWS

    @pl.when(jnp.logical_and(e == 0, c == 0))
    def _():
        hb_sc[...] = h_ref[...].astype(BF16)
        o_ref[...] = jnp.zeros_like(o_ref)

    @pl.when(c == 0)
    def _():
        rank_row = rank_t_ref[0, pl.ds(e, 1), :]

        def gather(t, carry):
            r0 = pl.multiple_of(t * MOE_ROWS, MOE_ROWS)
            slot = (lax.broadcasted_iota(jnp.int32, (MOE_ROWS, tm), 0) + r0).astype(F32)
            onehot = (rank_row == slot).astype(BF16)
            xs_sc[pl.ds(r0, MOE_ROWS), :] = jnp.dot(onehot, hb_sc[...], preferred_element_type=F32).astype(BF16)
            acc_sc[pl.ds(r0, MOE_ROWS), :] = jnp.zeros((MOE_ROWS, D_MODEL), F32)
            return carry

        lax.fori_loop(0, n_steps, gather, 0)

    def swiglu(t, carry):
        r0 = pl.multiple_of(t * MOE_ROWS, MOE_ROWS)
        x = xs_sc[pl.ds(r0, MOE_ROWS), :]
        gate = jnp.dot(x, wg_ref[0], preferred_element_type=F32)
        up = jnp.dot(x, wu_ref[0], preferred_element_type=F32)
        a = (gate * _sigmoid(gate) * up).astype(BF16)
        acc_sc[pl.ds(r0, MOE_ROWS), :] += jnp.dot(a, wd_ref[0], preferred_element_type=F32)
        return carry

    lax.fori_loop(0, n_steps, swiglu, 0)

    @pl.when(last_c)
    def _():
        lane = lax.broadcasted_iota(jnp.int32, (tm, LANES), 1)
        pick = lambda ref: jnp.sum(jnp.where(lane == e, ref[...], 0.0), axis=1, keepdims=True)
        rank_col = pick(rank_ref)
        gate_col = pick(cw_ref)

        def scatter(t, carry):
            r0 = pl.multiple_of(t * MOE_ROWS, MOE_ROWS)
            slot = (lax.broadcasted_iota(jnp.int32, (tm, MOE_ROWS), 1) + r0).astype(F32)
            onehot = (rank_col == slot).astype(BF16)
            y = acc_sc[pl.ds(r0, MOE_ROWS), :].astype(BF16)
            o_ref[...] += gate_col * jnp.dot(onehot, y, preferred_element_type=F32)
            return carry

        lax.fori_loop(0, n_steps, scatter, 0)

    @pl.when(jnp.logical_and(e == pl.num_programs(1) - 1, last_c))
    def _():
        o_ref[...] = _layer_norm(alpha * h_ref[...] + o_ref[...], g_ref[...], b_ref[...])


def _moe(h, cw, rank, rank_t, counts, w_gu, w_dn, g, b, *, alpha, tm, fc):
    S = h.shape[0]
    E, _, F2 = w_gu.shape
    nc = F2 // 2 // fc
    once = pl.Buffered(1)
    grid_spec = pltpu.PrefetchScalarGridSpec(
        num_scalar_prefetch=1,
        grid=(S // tm, E, nc),
        in_specs=[pl.BlockSpec((tm, D_MODEL), lambda i, e, c, cnt: (i, 0), pipeline_mode=once),
                  pl.BlockSpec((tm, LANES), lambda i, e, c, cnt: (i, 0), pipeline_mode=once),
                  pl.BlockSpec((tm, LANES), lambda i, e, c, cnt: (i, 0), pipeline_mode=once),
                  pl.BlockSpec((1, N_EXPERTS, tm), lambda i, e, c, cnt: (i, 0, 0), pipeline_mode=once),
                  pl.BlockSpec((1, D_MODEL, fc), lambda i, e, c, cnt: (e, 0, c)),
                  pl.BlockSpec((1, D_MODEL, fc), lambda i, e, c, cnt: (e, 0, nc + c)),
                  pl.BlockSpec((1, fc, D_MODEL), lambda i, e, c, cnt: (e, c, 0)),
                  pl.BlockSpec(g.shape, lambda i, e, c, cnt: (0, 0)),
                  pl.BlockSpec(b.shape, lambda i, e, c, cnt: (0, 0))],
        out_specs=pl.BlockSpec((tm, D_MODEL), lambda i, e, c, cnt: (i, 0), pipeline_mode=once),
        scratch_shapes=[pltpu.VMEM((tm, D_MODEL), BF16), pltpu.VMEM((tm, D_MODEL), BF16),
                        pltpu.VMEM((tm, D_MODEL), F32)],
    )
    return pl.pallas_call(
        functools.partial(_moe_kernel, alpha=alpha),
        grid_spec=grid_spec,
        out_shape=jax.ShapeDtypeStruct((S, D_MODEL), F32),
        compiler_params=_cparams(("arbitrary", "arbitrary", "arbitrary")),
        name="moe_ffn",
    )(counts, h, cw, rank, rank_t, w_gu, w_gu, w_dn, g, b)


def _ffn_kernel(h_ref, wg_ref, wu_ref, wd_ref, g_ref, b_ref, o_ref, hb_sc, acc_sc, *, alpha):
    c = pl.program_id(1)

    @pl.when(c == 0)
    def _():
        hb_sc[...] = h_ref[...].astype(BF16)
        acc_sc[...] = jnp.zeros_like(acc_sc)

    hb = hb_sc[...]
    gate = jnp.dot(hb, wg_ref[...], preferred_element_type=F32)
    up = jnp.dot(hb, wu_ref[...], preferred_element_type=F32)
    a = gate * _sigmoid(gate) * up
    acc_sc[...] += jnp.dot(a.astype(BF16), wd_ref[...], preferred_element_type=F32)

    @pl.when(c == pl.num_programs(1) - 1)
    def _():
        o_ref[...] = _layer_norm(alpha * h_ref[...] + acc_sc[...], g_ref[...], b_ref[...])


def _ffn(h, w_gu, w_dn, g, b, *, alpha, tm, fc):
    S = h.shape[0]
    nc = w_gu.shape[1] // 2 // fc
    return pl.pallas_call(
        functools.partial(_ffn_kernel, alpha=alpha),
        grid=(S // tm, nc),
        in_specs=[pl.BlockSpec((tm, D_MODEL), lambda i, c: (i, 0)),
                  pl.BlockSpec((D_MODEL, fc), lambda i, c: (0, c)),
                  pl.BlockSpec((D_MODEL, fc), lambda i, c: (0, nc + c)),
                  pl.BlockSpec((fc, D_MODEL), lambda i, c: (c, 0)),
                  pl.BlockSpec(g.shape, lambda i, c: (0, 0)), pl.BlockSpec(b.shape, lambda i, c: (0, 0))],
        out_specs=pl.BlockSpec((tm, D_MODEL), lambda i, c: (i, 0)),
        out_shape=jax.ShapeDtypeStruct((S, D_MODEL), F32),
        scratch_shapes=[pltpu.VMEM((tm, D_MODEL), BF16), pltpu.VMEM((tm, D_MODEL), F32)],
        compiler_params=_cparams(("arbitrary", "arbitrary")),
        name="dense_ffn",
    )(h, w_gu, w_gu, w_dn, g, b)


def _pad_lanes(a, n=LANES):
    return jnp.pad(a, [(0, 0)] * (a.ndim - 1) + [(0, n - a.shape[-1])])


def kernel(x, ln_in_g, ln_in_b, w_in, b_fgate, lam_q1, lam_k1, lam_q2, lam_k2, subln_g, w_branch_fox, w_branch_diff, w_out, ln_mix_g, ln_mix_b, rel_bias, w_ffn_gate_up, w_ffn_down, w_router, w_expert_gate_up, w_expert_down, ln_ffn_g, ln_ffn_b):
    B, S, _ = x.shape
    assert B == 1
    depth = w_in.shape[0]
    alpha = (2.0 * depth) ** 0.25
    tq = min(ATTN_TQ, S)
    tk = min(ATTN_TK, tq)
    tm = min(512, S)
    tf = min(1024, S)
    te = min(MOE_TM, S)
    assert tk >= REL_MAX_DISTANCE and tq % tk == 0 and S % tq == 0 and S % tf == 0 and S % tm == 0 and S % te == 0
    assert te % MOE_ROWS == 0 and te % RANK_CHUNK == 0
    row = lambda v: v.reshape(1, -1).astype(F32)

    table = rel_bias.astype(F32)
    bias = _bias_tiles(table.T.reshape(-1), t=tk)

    h = x[0]
    for l in range(depth):
        lam_init = 0.8 - 0.6 * math.exp(-0.3 * l)
        fq, fk, fv, fg, dq, dk, dv, ga, gb = jnp.split(w_in[l], list(_cumsum(IN_SPLIT_SIZES))[:-1], axis=-1)
        w_cat = jnp.concatenate([fq, fk, fv, dq, dk, dv, ga, gb, _pad_lanes(fg)], axis=-1).astype(BF16)
        outs = _inproj(h, row(ln_in_g), row(ln_in_b), w_cat, do_ln=(l == 0), tm=tm)
        fq, fk, fv, dq, dk, dv, sa, sb, fg = outs[:9]
        if l == 0:
            h = outs[9]
        qa, ka, va = _prep(fq, fk, fv, fg, _pad_lanes(row(b_fgate[l])), tb=tm)
        y_fox = _fox_attention(qa, ka, va, tq=tq, tk=tk)
        lamvec = jnp.zeros((8, LANES), F32).at[0:4, 0:HEAD_DIM].set(
            jnp.stack([lam_q1[l], lam_k1[l], lam_q2[l], lam_k2[l]]).astype(F32))
        y_diff = _diff_attention(dq, dk, dv, bias, lamvec, row(subln_g[l]), tq=tq, tk=tk,
                                 lam_init=lam_init)
        h = _merge(y_fox, y_diff, sa, sb, h, w_branch_fox[l].astype(BF16), w_branch_diff[l].astype(BF16),
                   w_out[l].astype(BF16), row(ln_mix_g[l]), row(ln_mix_b[l]), alpha=alpha, tm=tm)
        g, b = row(ln_ffn_g[l]), row(ln_ffn_b[l])
        if l % 2 == 0:
            h = _ffn(h, w_ffn_gate_up[l // 2].astype(BF16), w_ffn_down[l // 2].astype(BF16), g, b,
                     alpha=alpha, tm=tf, fc=1408)
        else:
            wr = _pad_lanes(w_router[l // 2].astype(F32))
            whi = wr.astype(BF16)
            wlo = (wr - whi.astype(F32)).astype(BF16)
            cw, rank, rank_t, counts = _router(h, whi, wlo, tm=te)
            h = _moe(h, cw, rank, rank_t, counts[:, 0, :N_EXPERTS].reshape(-1),
                     w_expert_gate_up[l // 2].astype(BF16), w_expert_down[l // 2].astype(BF16),
                     g, b, alpha=alpha, tm=te, fc=896)
    return h[None]


def _cumsum(sizes):
    tot = 0
    for s in sizes:
        tot += s
        yield tot
```

```python
import functools
import math

import jax
import jax.numpy as jnp
from jax import lax
from jax.experimental import pallas as pl
from jax.experimental.pallas import tpu as pltpu

F32 = jnp.float32
BF16 = jnp.bfloat16

D_MODEL = 1024
HEAD_DIM = 64
FOX_HEADS = 8
FOX_WIDTH = FOX_HEADS * HEAD_DIM
DIFF_HEADS = 4
DIFF_QK_WIDTH = DIFF_HEADS * 2 * HEAD_DIM
DIFF_V_DIM = 2 * HEAD_DIM
DIFF_V_WIDTH = DIFF_HEADS * DIFF_V_DIM
IN_SPLIT_SIZES = (FOX_WIDTH, FOX_WIDTH, FOX_WIDTH, FOX_HEADS, DIFF_QK_WIDTH, DIFF_QK_WIDTH, DIFF_V_WIDTH,
                  D_MODEL, D_MODEL)
REL_BUCKETS = 32
REL_MAX_DISTANCE = 128
N_EXPERTS = 8
LN_EPS = 1e-5
SUBLN_EPS = 1e-5

LANES = 128
LOG2E = 1.4426950408889634
Q_SCALE = HEAD_DIM ** -0.5 * LOG2E
NEG = -1e30
VMEM_LIMIT = 56 * 1024 * 1024
ATTN_TQ = 2048
ATTN_TK = 512
MOE_TM = 2048


def _cparams(sem):
    return pltpu.CompilerParams(dimension_semantics=sem, vmem_limit_bytes=VMEM_LIMIT)


def _layer_norm(x, g, b):
    mu = jnp.mean(x, axis=-1, keepdims=True)
    xc = x - mu
    var = jnp.mean(xc * xc, axis=-1, keepdims=True)
    return xc * lax.rsqrt(var + LN_EPS) * g + b


def _sigmoid(x):
    return 1.0 / (1.0 + jnp.exp(-x))


def _split3(x):
    hi = x.astype(BF16)
    r1 = x - hi.astype(F32)
    mid = r1.astype(BF16)
    lo = (r1 - mid.astype(F32)).astype(BF16)
    return hi, mid, lo


_C_FQ, _C_FK, _C_FV, _C_DQ, _C_DK, _C_DV, _C_GA, _C_GB, _C_FG, _C_END = (
    0, 512, 1024, 1536, 2048, 2560, 3072, 4096, 5120, 5248)


def _inproj_kernel(x_ref, g_ref, b_ref, w_ref, fq, fk, fv, dq, dk, dv, sa, sb, fg, *h_out, do_ln):
    x = x_ref[...]
    if do_ln:
        x = _layer_norm(x, g_ref[...], b_ref[...])
        h_out[0][...] = x
    xb = x.astype(BF16)

    def mm(a, b):
        return jnp.dot(xb, w_ref[:, a:b], preferred_element_type=F32)

    fq[...] = (mm(_C_FQ, _C_FK) * Q_SCALE).astype(BF16)
    fk[...] = mm(_C_FK, _C_FV).astype(BF16)
    fv[...] = mm(_C_FV, _C_DQ).astype(BF16)
    dq[...] = (mm(_C_DQ, _C_DK) * Q_SCALE).astype(BF16)
    dk[...] = mm(_C_DK, _C_DV).astype(BF16)
    dv[...] = mm(_C_DV, _C_GA).astype(BF16)
    sa[...] = _sigmoid(mm(_C_GA, _C_GB)).astype(BF16)
    sb[...] = _sigmoid(mm(_C_GB, _C_FG)).astype(BF16)
    fg[...] = mm(_C_FG, _C_END)


def _inproj(x, g, b, w, *, do_ln, tm):
    S = x.shape[0]
    row = lambda n: pl.BlockSpec((tm, n), lambda i: (i, 0))
    full = lambda a: pl.BlockSpec(a.shape, lambda i: (0,) * a.ndim)
    out_shape = [jax.ShapeDtypeStruct((S, 512), BF16)] * 6 + [jax.ShapeDtypeStruct((S, D_MODEL), BF16)] * 2 + [
        jax.ShapeDtypeStruct((S, LANES), F32)]
    out_specs = [row(512)] * 6 + [row(D_MODEL)] * 2 + [row(LANES)]
    if do_ln:
        out_shape.append(jax.ShapeDtypeStruct((S, D_MODEL), F32))
        out_specs.append(row(D_MODEL))
    return pl.pallas_call(
        functools.partial(_inproj_kernel, do_ln=do_ln),
        grid=(S // tm,),
        in_specs=[row(D_MODEL), full(g), full(b), full(w)],
        out_specs=out_specs,
        out_shape=out_shape,
        compiler_params=_cparams(("arbitrary",)),
        name="inproj",
    )(x, g, b, w)


def _prep_kernel(fq_ref, fk_ref, fv_ref, fg_ref, bf_ref, qa_ref, ka_ref, va_ref, st_ref, carry_ref, *, tb):
    @pl.when(pl.program_id(0) == 0)
    def _():
        carry_ref[...] = jnp.zeros_like(carry_ref)

    z = fg_ref[...] + bf_ref[...]
    logf = jnp.minimum(z, 0.0) - jnp.log1p(jnp.exp(-jnp.abs(z)))
    r = lax.broadcasted_iota(jnp.int32, (tb, tb), 0)
    c = lax.broadcasted_iota(jnp.int32, (tb, tb), 1)
    tri = (c <= r).astype(BF16)
    hi, mid, lo = _split3(logf)
    dot = lambda a: jnp.dot(tri, a, preferred_element_type=F32)
    cum = (dot(hi) + dot(mid)) + dot(lo) + carry_ref[0:1, :]
    carry_ref[0:1, :] = cum[tb - 1:tb, :]

    c_hi, c_mid, c_lo = (t.astype(F32) for t in _split3(cum * LOG2E))
    lane = lax.broadcasted_iota(jnp.int32, (tb, LANES), 1)
    one = jnp.ones((tb, LANES), F32)
    zero = jnp.zeros((tb, LANES), F32)
    lane1 = lax.broadcasted_iota(jnp.int32, (1, LANES), 1)
    q_norm2 = jnp.zeros((1, LANES), F32)
    k_norm2 = jnp.zeros((1, LANES), F32)

    def max_norm2(slab):
        n2 = jnp.sum(jnp.where(lane < 64, slab * slab, 0.0), axis=1, keepdims=True)
        return jnp.max(n2, axis=0, keepdims=True)

    for h in range(FOX_HEADS):
        p, odd = divmod(h, 2)
        col = lambda t: jnp.broadcast_to(t[:, h:h + 1], (tb, LANES))
        chi, cmid, clo = col(c_hi), col(c_mid), col(c_lo)

        def head_slab(ref):
            slab = ref[:, p * LANES:(p + 1) * LANES].astype(F32)
            return pltpu.roll(slab, HEAD_DIM, 1) if odd else slab

        qslab = head_slab(fq_ref)
        kslab = head_slab(fk_ref)
        q_norm2 = jnp.where(lane1 == h, max_norm2(qslab), q_norm2)
        k_norm2 = jnp.where(lane1 == h, max_norm2(kslab), k_norm2)
        qa = jnp.where(lane < 64, qslab,
                       jnp.where(lane == 64, chi, jnp.where(lane == 65, cmid, jnp.where(lane == 66, clo,
                                 jnp.where(lane < 70, one, zero)))))
        ka = jnp.where(lane < 64, kslab,
                       jnp.where(lane < 67, one, jnp.where(lane == 67, -chi, jnp.where(lane == 68, -cmid,
                                 jnp.where(lane == 69, -clo, zero)))))
        qa_ref[h] = qa.astype(BF16)
        ka_ref[h] = ka.astype(BF16)
        vslab = fv_ref[:, p * LANES:(p + 1) * LANES].astype(F32)
        if odd:
            va = jnp.where(lane >= 64, vslab, jnp.where(lane == 0, one, zero))
        else:
            va = jnp.where(lane < 64, vslab, jnp.where(lane == 64, one, zero))
        va_ref[h] = va.astype(BF16)

    st_ref[0] = jnp.concatenate([q_norm2, k_norm2, cum[tb - 1:tb, :] * LOG2E, jnp.zeros((5, LANES), F32)], axis=0)


def _prep(fq, fk, fv, fg, bf, *, tb):
    S = fq.shape[0]
    row = lambda n: pl.BlockSpec((tb, n), lambda i: (i, 0))
    hspec = pl.BlockSpec((FOX_HEADS, tb, LANES), lambda i: (0, i, 0))
    hshape = jax.ShapeDtypeStruct((FOX_HEADS, S, LANES), BF16)
    return pl.pallas_call(
        functools.partial(_prep_kernel, tb=tb),
        grid=(S // tb,),
        in_specs=[row(512), row(512), row(512), row(LANES), pl.BlockSpec((1, LANES), lambda i: (0, 0))],
        out_specs=[hspec] * 3 + [pl.BlockSpec((1, 8, LANES), lambda i: (i, 0, 0))],
        out_shape=[hshape] * 3 + [jax.ShapeDtypeStruct((S // tb, 8, LANES), F32)],
        scratch_shapes=[pltpu.VMEM((8, LANES), F32)],
        compiler_params=_cparams(("arbitrary",)),
        name="fox_prep",
    )(fq, fk, fv, fg, bf)


SKIP_LOG2 = 150.0


def _plan_kernel(st_ref, start_ref, *, nt, nsub):
    qn = jnp.sqrt(st_ref[pl.ds(0, nt, stride=8), :])
    kn = jnp.sqrt(st_ref[pl.ds(1, nt, stride=8), :])
    cum_end = st_ref[pl.ds(2, nt, stride=8), :]
    tile = lax.broadcasted_iota(jnp.int32, (nt, LANES), 0)
    rows = []
    for i in range(nt // nsub):
        own = jnp.logical_and(tile >= i * nsub, tile < (i + 1) * nsub)
        q_max = jnp.max(jnp.where(own, qn, 0.0), axis=0, keepdims=True)
        k_own = jnp.max(jnp.where(own, kn, 0.0), axis=0, keepdims=True)
        cum_q = cum_end[i * nsub - 1:i * nsub, :] if i > 0 else jnp.zeros((1, LANES), F32)
        upper = q_max * kn + (cum_q - cum_end) + q_max * k_own
        slack = SKIP_LOG2 + 1e-4 * jnp.abs(cum_end)
        needed = jnp.logical_and(upper >= -slack, tile < i * nsub)
        rows.append(jnp.min(jnp.where(needed, tile, i * nsub), axis=0, keepdims=True))
    rows += [jnp.zeros((1, LANES), jnp.int32)] * (8 - len(rows))
    start_ref[...] = jnp.concatenate(rows, axis=0)


def _plan(stats, *, nsub):
    nt = stats.shape[0]
    assert nt // nsub <= 8
    return pl.pallas_call(
        functools.partial(_plan_kernel, nt=nt, nsub=nsub),
        out_shape=jax.ShapeDtypeStruct((8, LANES), jnp.int32),
        compiler_params=pltpu.CompilerParams(vmem_limit_bytes=VMEM_LIMIT),
        name="fox_plan",
    )(stats.reshape(nt * 8, LANES))


def _nt_dot(a, b):
    return lax.dot_general(a, b, (((1,), (1,)), ((), ())), preferred_element_type=F32)


def _fox_kernel(start_ref, q_ref, k_ref, v_ref, o_ref, m_sc, acc_sc, *, tq, tk):
    i = pl.program_id(1)
    pair = pl.program_id(0)
    m_sc[...] = jnp.full_like(m_sc, NEG)
    acc_sc[...] = jnp.zeros_like(acc_sc)
    nsub = tq // tk

    def tile(j, r0, masked):
        off = pl.multiple_of(j * tk, tk)
        rows = tq - r0
        for hh in range(2):
            s = _nt_dot(q_ref[hh, r0:, :], k_ref[hh, pl.ds(off, tk), :])
            if masked:
                r = lax.broadcasted_iota(jnp.int32, (rows, tk), 0)
                c = lax.broadcasted_iota(jnp.int32, (rows, tk), 1)
                s = jnp.where(c <= r, s, NEG)
            m_prev = m_sc[hh, r0:, :]
            m_next = jnp.maximum(m_prev, jnp.max(s, axis=1, keepdims=True))
            alpha = jnp.exp2(m_prev - m_next)
            p = jnp.exp2(s - jnp.concatenate([m_next] * (tk // LANES), axis=1)).astype(BF16)
            pv = jnp.dot(p, v_ref[hh, pl.ds(off, tk), :], preferred_element_type=F32)
            acc_sc[hh, r0:, :] = alpha * acc_sc[hh, r0:, :] + pv
            m_sc[hh, r0:, :] = m_next

    def body(j, carry):
        tile(j, 0, False)
        return carry

    first = jnp.minimum(start_ref[i * FOX_HEADS + 2 * pair], start_ref[i * FOX_HEADS + 2 * pair + 1])
    lax.fori_loop(first, i * nsub, body, 0)
    for u in range(nsub):
        tile(i * nsub + u, u * tk, True)

    a0 = acc_sc[0]
    a1 = acc_sc[1]
    lane = lax.broadcasted_iota(jnp.int32, (tq, LANES), 1)
    o_ref[...] = jnp.where(lane < 64, a0 / a0[:, 64:65], a1 / a1[:, 0:1]).astype(o_ref.dtype)


def _fox_attention(start, qa, ka, va, *, tq, tk):
    S = qa.shape[1]
    grid_spec = pltpu.PrefetchScalarGridSpec(
        num_scalar_prefetch=1,
        grid=(FOX_HEADS // 2, S // tq),
        in_specs=[pl.BlockSpec((2, tq, LANES), lambda p, i, st: (p, i, 0)),
                  pl.BlockSpec((2, S, LANES), lambda p, i, st: (p, 0, 0)),
                  pl.BlockSpec((2, S, LANES), lambda p, i, st: (p, 0, 0))],
        out_specs=pl.BlockSpec((tq, LANES), lambda p, i, st: (i, p)),
        scratch_shapes=[pltpu.VMEM((2, tq, LANES), F32), pltpu.VMEM((2, tq, LANES), F32)],
    )
    return pl.pallas_call(
        functools.partial(_fox_kernel, tq=tq, tk=tk),
        grid_spec=grid_spec,
        out_shape=jax.ShapeDtypeStruct((S, FOX_WIDTH), BF16),
        compiler_params=_cparams(("arbitrary", "arbitrary")),
        name="fox_attn",
    )(start, qa, ka, va)


BIAS_KINDS = 2


def _bias_tile_kernel(table_ref, o_ref, *, t):
    h = pl.program_id(0)
    kind = pl.program_id(1)
    r = lax.broadcasted_iota(jnp.int32, (t, t), 0)
    c = lax.broadcasted_iota(jnp.int32, (t, t), 1)
    dist = r - c + kind * t
    n = jnp.maximum(dist, 0)
    max_exact = REL_BUCKETS // 2
    nf = jnp.maximum(n, 1).astype(F32)
    log_part = jnp.log(nf / max_exact) / math.log(REL_MAX_DISTANCE / max_exact) * (REL_BUCKETS - max_exact)
    large = jnp.minimum(max_exact + log_part.astype(jnp.int32), REL_BUCKETS - 1)
    bucket = jnp.where(n < max_exact, n, large)
    val = jnp.zeros((t, t), F32)
    for b in range(REL_BUCKETS):
        val = jnp.where(bucket == b, table_ref[h * REL_BUCKETS + b], val)
    far = table_ref[h * REL_BUCKETS + REL_BUCKETS - 1]
    o_ref[0, 0] = jnp.where(dist >= 0, (val - far) * LOG2E, NEG)


def _bias_tiles(table_flat, *, t):
    return pl.pallas_call(
        functools.partial(_bias_tile_kernel, t=t),
        grid=(DIFF_HEADS, BIAS_KINDS),
        in_specs=[pl.BlockSpec(memory_space=pltpu.SMEM)],
        out_specs=pl.BlockSpec((1, 1, t, t), lambda h, k: (h, k, 0, 0)),
        out_shape=jax.ShapeDtypeStruct((DIFF_HEADS, BIAS_KINDS, t, t), F32),
        compiler_params=_cparams(("arbitrary", "arbitrary")),
        name="bias_tiles",
    )(table_flat)


def _diff_kernel(q_ref, k_ref, v_ref, bias_ref, lam_ref, g_ref, o_ref, m_sc, l_sc, acc_sc,
                 *, tq, tk, lam_init):
    i = pl.program_id(1)
    m_sc[...] = jnp.full_like(m_sc, NEG)
    l_sc[...] = jnp.zeros_like(l_sc)
    acc_sc[...] = jnp.zeros_like(acc_sc)
    nsub = tq // tk

    lane = lax.broadcasted_iota(jnp.int32, (tq, LANES), 1)
    q = q_ref[...]
    zero = jnp.zeros_like(q)
    qs = (jnp.where(lane < 64, q, zero), jnp.where(lane >= 64, q, zero))

    def tile(j, r0, kinds):
        off = pl.multiple_of(j * tk, tk)
        k = k_ref[pl.ds(off, tk), :]
        v = v_ref[pl.ds(off, tk), :]
        for c in range(2):
            s = _nt_dot(qs[c][r0:], k)
            m_prev = m_sc[c, r0:, :]
            if kinds is not None:
                s = jnp.concatenate([s[a * tk:(a + 1) * tk] + bias_ref[0, kd] if kd < 2 else s[a * tk:(a + 1) * tk]
                                     for a, kd in enumerate(kinds)], axis=0)
            m_next = jnp.maximum(m_prev, jnp.max(s, axis=1, keepdims=True))
            p = jnp.exp2(s - jnp.concatenate([m_next] * (tk // LANES), axis=1))
            alpha = jnp.exp2(m_prev - m_next)
            l_sc[c, r0:, :] = alpha * l_sc[c, r0:, :] + sum(p[:, a * LANES:(a + 1) * LANES]
                                                            for a in range(tk // LANES))
            pv = jnp.dot(p.astype(BF16), v, preferred_element_type=F32)
            acc_sc[c, r0:, :] = alpha * acc_sc[c, r0:, :] + pv
            m_sc[c, r0:, :] = m_next

    def body(j, carry):
        tile(j, 0, None)
        return carry

    lax.fori_loop(0, i * nsub - 1, body, 0)

    @pl.when(i > 0)
    def _():
        tile(i * nsub - 1, 0, (1,) + (2,) * (nsub - 1))

    for u in range(nsub):
        tile(i * nsub + u, u * tk, ((0, 1) + (2,) * nsub)[:nsub - u])

    lv = lam_ref[...]
    lam = (jnp.exp(jnp.sum(lv[0:1] * lv[1:2], axis=1, keepdims=True))
           - jnp.exp(jnp.sum(lv[2:3] * lv[3:4], axis=1, keepdims=True)) + lam_init)
    l0 = jnp.sum(l_sc[0], axis=1, keepdims=True)
    l1 = jnp.sum(l_sc[1], axis=1, keepdims=True)
    o = acc_sc[0] / l0 - lam * (acc_sc[1] / l1)
    o = o * lax.rsqrt(jnp.mean(o * o, axis=1, keepdims=True) + SUBLN_EPS) * g_ref[...] * (1.0 - lam_init)
    o_ref[...] = o.astype(o_ref.dtype)


def _diff_attention(dq, dk, dv, bias, lamvec, g, *, tq, tk, lam_init):
    S = dq.shape[0]
    return pl.pallas_call(
        functools.partial(_diff_kernel, tq=tq, tk=tk, lam_init=lam_init),
        grid=(DIFF_HEADS, S // tq),
        in_specs=[pl.BlockSpec((tq, LANES), lambda h, i: (i, h)),
                  pl.BlockSpec((S, LANES), lambda h, i: (0, h)),
                  pl.BlockSpec((S, LANES), lambda h, i: (0, h)),
                  pl.BlockSpec((1, BIAS_KINDS, tk, tk), lambda h, i: (h, 0, 0, 0)),
                  pl.BlockSpec((8, LANES), lambda h, i: (0, 0)),
                  pl.BlockSpec((1, LANES), lambda h, i: (0, 0))],
        out_specs=pl.BlockSpec((tq, LANES), lambda h, i: (i, h)),
        out_shape=jax.ShapeDtypeStruct((S, DIFF_V_WIDTH), BF16),
        scratch_shapes=[pltpu.VMEM((2, tq, LANES), F32), pltpu.VMEM((2, tq, LANES), F32),
                        pltpu.VMEM((2, tq, LANES), F32)],
        compiler_params=_cparams(("arbitrary", "arbitrary")),
        name="diff_attn",
    )(dq, dk, dv, bias, lamvec, g)


def _merge_kernel(yf_ref, yd_ref, sa_ref, sb_ref, h_ref, wf_ref, wd_ref, wo_ref, g_ref, b_ref, o_ref, *, alpha):
    bf = jnp.dot(yf_ref[...], wf_ref[...], preferred_element_type=F32)
    bd = jnp.dot(yd_ref[...], wd_ref[...], preferred_element_type=F32)
    merged = sa_ref[...].astype(F32) * bf + sb_ref[...].astype(F32) * bd
    mix = jnp.dot(merged.astype(BF16), wo_ref[...], preferred_element_type=F32)
    o_ref[...] = _layer_norm(alpha * h_ref[...] + mix, g_ref[...], b_ref[...])


def _merge(yf, yd, sa, sb, h, wf, wd, wo, g, b, *, alpha, tm):
    S = h.shape[0]
    row = lambda n: pl.BlockSpec((tm, n), lambda i: (i, 0))
    full = lambda a: pl.BlockSpec(a.shape, lambda i: (0,) * a.ndim)
    return pl.pallas_call(
        functools.partial(_merge_kernel, alpha=alpha),
        grid=(S // tm,),
        in_specs=[row(512), row(512), row(D_MODEL), row(D_MODEL), row(D_MODEL),
                  full(wf), full(wd), full(wo), full(g), full(b)],
        out_specs=row(D_MODEL),
        out_shape=jax.ShapeDtypeStruct((S, D_MODEL), F32),
        compiler_params=_cparams(("arbitrary",)),
        name="merge_out",
    )(yf, yd, sa, sb, h, wf, wd, wo, g, b)


MOE_ROWS = 256
RANK_CHUNK = 256


def _router_kernel(h_ref, whi_ref, wlo_ref, cw_ref, rank_ref, rank_t_ref, cnt_ref):
    tm = h_ref.shape[0]
    h = h_ref[...]
    hhi = h.astype(BF16)
    hlo = (h - hhi.astype(F32)).astype(BF16)
    whi = whi_ref[...]
    dot = lambda a, b: jnp.dot(a, b, preferred_element_type=F32)
    logits = dot(hhi, whi) + dot(hhi, wlo_ref[...]) + dot(hlo, whi)
    lane = lax.broadcasted_iota(jnp.int32, logits.shape, 1).astype(F32)
    lg = jnp.where(lane < N_EXPERTS, logits, NEG)
    m1 = jnp.max(lg, axis=1, keepdims=True)
    i1 = jnp.min(jnp.where(lg == m1, lane, float(LANES)), axis=1, keepdims=True)
    lg2 = jnp.where(lane == i1, NEG, lg)
    m2 = jnp.max(lg2, axis=1, keepdims=True)
    i2 = jnp.min(jnp.where(lg2 == m2, lane, float(LANES)), axis=1, keepdims=True)
    e = jnp.exp(m2 - m1)
    g1 = 1.0 / (1.0 + e)
    chosen = jnp.logical_or(lane == i1, lane == i2)
    cw_ref[...] = jnp.where(lane == i1, g1, 0.0) + jnp.where(lane == i2, e * g1, 0.0)

    a = chosen.astype(BF16)
    r = lax.broadcasted_iota(jnp.int32, (RANK_CHUNK, RANK_CHUNK), 0)
    c = lax.broadcasted_iota(jnp.int32, (RANK_CHUNK, RANK_CHUNK), 1)
    strict_lower = (c < r).astype(BF16)
    ones_row = jnp.ones((8, RANK_CHUNK), BF16)
    carry = jnp.zeros((1, LANES), F32)
    for q in range(tm // RANK_CHUNK):
        rows = slice(q * RANK_CHUNK, (q + 1) * RANK_CHUNK)
        rank = jnp.where(chosen[rows], dot(strict_lower, a[rows]) + carry, -1.0)
        rank_ref[rows, :] = rank
        rank_t_ref[0, :, rows] = jnp.transpose(rank)[0:N_EXPERTS, :]
        carry = carry + dot(ones_row, a[rows])[0:1, :]
    cnt_ref[0] = jnp.broadcast_to(carry, (8, LANES)).astype(jnp.int32)


def _router(h, whi, wlo, *, tm):
    S = h.shape[0]
    nb = S // tm
    return pl.pallas_call(
        _router_kernel,
        grid=(nb,),
        in_specs=[pl.BlockSpec((tm, D_MODEL), lambda i: (i, 0)),
                  pl.BlockSpec(whi.shape, lambda i: (0, 0)), pl.BlockSpec(wlo.shape, lambda i: (0, 0))],
        out_specs=[pl.BlockSpec((tm, LANES), lambda i: (i, 0)),
                   pl.BlockSpec((tm, LANES), lambda i: (i, 0)),
                   pl.BlockSpec((1, N_EXPERTS, tm), lambda i: (i, 0, 0)),
                   pl.BlockSpec((1, 8, LANES), lambda i: (i, 0, 0))],
        out_shape=[jax.ShapeDtypeStruct((S, LANES), F32),
                   jax.ShapeDtypeStruct((S, LANES), F32),
                   jax.ShapeDtypeStruct((nb, N_EXPERTS, tm), F32),
                   jax.ShapeDtypeStruct((nb, 8, LANES), jnp.int32)],
        compiler_params=_cparams(("arbitrary",)),
        name="router",
    )(h, whi, wlo)


def _moe_kernel(cnt_ref, h_ref, cw_ref, rank_ref, rank_t_ref, wg_ref, wu_ref, wd_ref, g_ref, b_ref, o_ref,
                hb_sc, xs_sc, acc_sc, *, alpha):
    tm = h_ref.shape[0]
    i = pl.program_id(0)
    e = pl.program_id(1)
    c = pl.program_id(2)
    last_c = c == pl.num_programs(2) - 1
    n_steps = (cnt_ref[i * N_EXPERTS + e] + MOE_ROWS - 1) // MOE_ROWS

    @pl.when(jnp.logical_and(e == 0, c == 0))
    def _():
        hb_sc[...] = h_ref[...].astype(BF16)
        o_ref[...] = jnp.zeros_like(o_ref)

    @pl.when(c == 0)
    def _():
        rank_row = rank_t_ref[0, pl.ds(e, 1), :]

        def gather(t, carry):
            r0 = pl.multiple_of(t * MOE_ROWS, MOE_ROWS)
            slot = (lax.broadcasted_iota(jnp.int32, (MOE_ROWS, tm), 0) + r0).astype(F32)
            onehot = (rank_row == slot).astype(BF16)
            xs_sc[pl.ds(r0, MOE_ROWS), :] = jnp.dot(onehot, hb_sc[...], preferred_element_type=F32).astype(BF16)
            acc_sc[pl.ds(r0, MOE_ROWS), :] = jnp.zeros((MOE_ROWS, D_MODEL), F32)
            return carry

        lax.fori_loop(0, n_steps, gather, 0)

    def swiglu(t, carry):
        r0 = pl.multiple_of(t * MOE_ROWS, MOE_ROWS)
        x = xs_sc[pl.ds(r0, MOE_ROWS), :]
        gate = jnp.dot(x, wg_ref[0], preferred_element_type=F32)
        up = jnp.dot(x, wu_ref[0], preferred_element_type=F32)
        a = (gate * _sigmoid(gate) * up).astype(BF16)
        acc_sc[pl.ds(r0, MOE_ROWS), :] += jnp.dot(a, wd_ref[0], preferred_element_type=F32)
        return carry

    lax.fori_loop(0, n_steps, swiglu, 0)

    @pl.when(last_c)
    def _():
        lane = lax.broadcasted_iota(jnp.int32, (tm, LANES), 1)
        pick = lambda ref: jnp.sum(jnp.where(lane == e, ref[...], 0.0), axis=1, keepdims=True)
        rank_col = pick(rank_ref)
        gate_col = pick(cw_ref)

        def scatter(t, carry):
            r0 = pl.multiple_of(t * MOE_ROWS, MOE_ROWS)
            slot = (lax.broadcasted_iota(jnp.int32, (tm, MOE_ROWS), 1) + r0).astype(F32)
            onehot = (rank_col == slot).astype(BF16)
            y = acc_sc[pl.ds(r0, MOE_ROWS), :].astype(BF16)
            o_ref[...] += gate_col * jnp.dot(onehot, y, preferred_element_type=F32)
            return carry

        lax.fori_loop(0, n_steps, scatter, 0)

    @pl.when(jnp.logical_and(e == pl.num_programs(1) - 1, last_c))
    def _():
        o_ref[...] = _layer_norm(alpha * h_ref[...] + o_ref[...], g_ref[...], b_ref[...])


def _moe(h, cw, rank, rank_t, counts, w_gu, w_dn, g, b, *, alpha, tm, fc):
    S = h.shape[0]
    E, _, F2 = w_gu.shape
    nc = F2 // 2 // fc
    once = pl.Buffered(1)
    grid_spec = pltpu.PrefetchScalarGridSpec(
        num_scalar_prefetch=1,
        grid=(S // tm, E, nc),
        in_specs=[pl.BlockSpec((tm, D_MODEL), lambda i, e, c, cnt: (i, 0), pipeline_mode=once),
                  pl.BlockSpec((tm, LANES), lambda i, e, c, cnt: (i, 0), pipeline_mode=once),
                  pl.BlockSpec((tm, LANES), lambda i, e, c, cnt: (i, 0), pipeline_mode=once),
                  pl.BlockSpec((1, N_EXPERTS, tm), lambda i, e, c, cnt: (i, 0, 0), pipeline_mode=once),
                  pl.BlockSpec((1, D_MODEL, fc), lambda i, e, c, cnt: (e, 0, c)),
                  pl.BlockSpec((1, D_MODEL, fc), lambda i, e, c, cnt: (e, 0, nc + c)),
                  pl.BlockSpec((1, fc, D_MODEL), lambda i, e, c, cnt: (e, c, 0)),
                  pl.BlockSpec(g.shape, lambda i, e, c, cnt: (0, 0)),
                  pl.BlockSpec(b.shape, lambda i, e, c, cnt: (0, 0))],
        out_specs=pl.BlockSpec((tm, D_MODEL), lambda i, e, c, cnt: (i, 0), pipeline_mode=once),
        scratch_shapes=[pltpu.VMEM((tm, D_MODEL), BF16), pltpu.VMEM((tm, D_MODEL), BF16),
                        pltpu.VMEM((tm, D_MODEL), F32)],
    )
    return pl.pallas_call(
        functools.partial(_moe_kernel, alpha=alpha),
        grid_spec=grid_spec,
        out_shape=jax.ShapeDtypeStruct((S, D_MODEL), F32),
        compiler_params=_cparams(("arbitrary", "arbitrary", "arbitrary")),
        name="moe_ffn",
    )(counts, h, cw, rank, rank_t, w_gu, w_gu, w_dn, g, b)


def _ffn_kernel(h_ref, wg_ref, wu_ref, wd_ref, g_ref, b_ref, o_ref, hb_sc, acc_sc, *, alpha):
    c = pl.program_id(1)

    @pl.when(c == 0)
    def _():
        hb_sc[...] = h_ref[...].astype(BF16)
        acc_sc[...] = jnp.zeros_like(acc_sc)

    hb = hb_sc[...]
    gate = jnp.dot(hb, wg_ref[...], preferred_element_type=F32)
    up = jnp.dot(hb, wu_ref[...], preferred_element_type=F32)
    a = gate * _sigmoid(gate) * up
    acc_sc[...] += jnp.dot(a.astype(BF16), wd_ref[...], preferred_element_type=F32)

    @pl.when(c == pl.num_programs(1) - 1)
    def _():
        o_ref[...] = _layer_norm(alpha * h_ref[...] + acc_sc[...], g_ref[...], b_ref[...])


def _ffn(h, w_gu, w_dn, g, b, *, alpha, tm, fc):
    S = h.shape[0]
    nc = w_gu.shape[1] // 2 // fc
    return pl.pallas_call(
        functools.partial(_ffn_kernel, alpha=alpha),
        grid=(S // tm, nc),
        in_specs=[pl.BlockSpec((tm, D_MODEL), lambda i, c: (i, 0)),
                  pl.BlockSpec((D_MODEL, fc), lambda i, c: (0, c)),
                  pl.BlockSpec((D_MODEL, fc), lambda i, c: (0, nc + c)),
                  pl.BlockSpec((fc, D_MODEL), lambda i, c: (c, 0)),
                  pl.BlockSpec(g.shape, lambda i, c: (0, 0)), pl.BlockSpec(b.shape, lambda i, c: (0, 0))],
        out_specs=pl.BlockSpec((tm, D_MODEL), lambda i, c: (i, 0)),
        out_shape=jax.ShapeDtypeStruct((S, D_MODEL), F32),
        scratch_shapes=[pltpu.VMEM((tm, D_MODEL), BF16), pltpu.VMEM((tm, D_MODEL), F32)],
        compiler_params=_cparams(("arbitrary", "arbitrary")),
        name="dense_ffn",
    )(h, w_gu, w_gu, w_dn, g, b)


def _pad_lanes(a, n=LANES):
    return jnp.pad(a, [(0, 0)] * (a.ndim - 1) + [(0, n - a.shape[-1])])


def kernel(x, ln_in_g, ln_in_b, w_in, b_fgate, lam_q1, lam_k1, lam_q2, lam_k2, subln_g, w_branch_fox, w_branch_diff, w_out, ln_mix_g, ln_mix_b, rel_bias, w_ffn_gate_up, w_ffn_down, w_router, w_expert_gate_up, w_expert_down, ln_ffn_g, ln_ffn_b):
    B, S, _ = x.shape
    assert B == 1
    depth = w_in.shape[0]
    alpha = (2.0 * depth) ** 0.25
    tq = min(ATTN_TQ, S)
    tk = min(ATTN_TK, tq)
    tm = min(512, S)
    tf = min(1024, S)
    te = min(MOE_TM, S)
    assert tk >= REL_MAX_DISTANCE and tq % tk == 0 and S % tq == 0 and S % tf == 0 and S % tm == 0 and S % te == 0
    assert te % MOE_ROWS == 0 and te % RANK_CHUNK == 0
    row = lambda v: v.reshape(1, -1).astype(F32)

    table = rel_bias.astype(F32)
    bias = _bias_tiles(table.T.reshape(-1), t=tk)

    h = x[0]
    for l in range(depth):
        lam_init = 0.8 - 0.6 * math.exp(-0.3 * l)
        fq, fk, fv, fg, dq, dk, dv, ga, gb = jnp.split(w_in[l], list(_cumsum(IN_SPLIT_SIZES))[:-1], axis=-1)
        w_cat = jnp.concatenate([fq, fk, fv, dq, dk, dv, ga, gb, _pad_lanes(fg)], axis=-1).astype(BF16)
        outs = _inproj(h, row(ln_in_g), row(ln_in_b), w_cat, do_ln=(l == 0), tm=tm)
        fq, fk, fv, dq, dk, dv, sa, sb, fg = outs[:9]
        if l == 0:
            h = outs[9]
        qa, ka, va, stats = _prep(fq, fk, fv, fg, _pad_lanes(row(b_fgate[l])), tb=tk)
        start = _plan(stats, nsub=tq // tk)[:S // tq, :FOX_HEADS].reshape(-1)
        y_fox = _fox_attention(start, qa, ka, va, tq=tq, tk=tk)
        lamvec = jnp.zeros((8, LANES), F32).at[0:4, 0:HEAD_DIM].set(
            jnp.stack([lam_q1[l], lam_k1[l], lam_q2[l], lam_k2[l]]).astype(F32))
        y_diff = _diff_attention(dq, dk, dv, bias, lamvec, row(subln_g[l]), tq=tq, tk=tk,
                                 lam_init=lam_init)
        h = _merge(y_fox, y_diff, sa, sb, h, w_branch_fox[l].astype(BF16), w_branch_diff[l].astype(BF16),
                   w_out[l].astype(BF16), row(ln_mix_g[l]), row(ln_mix_b[l]), alpha=alpha, tm=tm)
        g, b = row(ln_ffn_g[l]), row(ln_ffn_b[l])
        if l % 2 == 0:
            h = _ffn(h, w_ffn_gate_up[l // 2].astype(BF16), w_ffn_down[l // 2].astype(BF16), g, b,
                     alpha=alpha, tm=tf, fc=1408)
        else:
            wr = _pad_lanes(w_router[l // 2].astype(F32))
            whi = wr.astype(BF16)
            wlo = (wr - whi.astype(F32)).astype(BF16)
            cw, rank, rank_t, counts = _router(h, whi, wlo, tm=te)
            h = _moe(h, cw, rank, rank_t, counts[:, 0, :N_EXPERTS].reshape(-1),
                     w_expert_gate_up[l // 2].astype(BF16), w_expert_down[l // 2].astype(BF16),
                     g, b, alpha=alpha, tm=te, fc=896)
    return h[None]


def _cumsum(sizes):
    tot = 0
    for s in sizes:
        tot += s
        yield tot
```

```python
import functools
import math

import jax
import jax.numpy as jnp
from jax import lax
from jax.experimental import pallas as pl
from jax.experimental.pallas import tpu as pltpu

F32 = jnp.float32
BF16 = jnp.bfloat16

D_MODEL = 1024
HEAD_DIM = 64
FOX_HEADS = 8
FOX_WIDTH = FOX_HEADS * HEAD_DIM
DIFF_HEADS = 4
DIFF_QK_WIDTH = DIFF_HEADS * 2 * HEAD_DIM
DIFF_V_DIM = 2 * HEAD_DIM
DIFF_V_WIDTH = DIFF_HEADS * DIFF_V_DIM
IN_SPLIT_SIZES = (FOX_WIDTH, FOX_WIDTH, FOX_WIDTH, FOX_HEADS, DIFF_QK_WIDTH, DIFF_QK_WIDTH, DIFF_V_WIDTH,
                  D_MODEL, D_MODEL)
REL_BUCKETS = 32
REL_MAX_DISTANCE = 128
N_EXPERTS = 8
LN_EPS = 1e-5
SUBLN_EPS = 1e-5

LANES = 128
LOG2E = 1.4426950408889634
Q_SCALE = HEAD_DIM ** -0.5 * LOG2E
NEG = -1e30
VMEM_LIMIT = 56 * 1024 * 1024
ATTN_TQ = 2048
ATTN_TK = 512
MOE_TM = 2048


def _cparams(sem):
    return pltpu.CompilerParams(dimension_semantics=sem, vmem_limit_bytes=VMEM_LIMIT)


def _layer_norm(x, g, b):
    mu = jnp.mean(x, axis=-1, keepdims=True)
    xc = x - mu
    var = jnp.mean(xc * xc, axis=-1, keepdims=True)
    return xc * lax.rsqrt(var + LN_EPS) * g + b


def _sigmoid(x):
    return 1.0 / (1.0 + jnp.exp(-x))


def _split3(x):
    hi = x.astype(BF16)
    r1 = x - hi.astype(F32)
    mid = r1.astype(BF16)
    lo = (r1 - mid.astype(F32)).astype(BF16)
    return hi, mid, lo


_C_FQ, _C_FK, _C_FV, _C_DQ, _C_DK, _C_DV, _C_GA, _C_GB, _C_FG, _C_END = (
    0, 512, 1024, 1536, 2048, 2560, 3072, 4096, 5120, 5248)


def _inproj_kernel(x_ref, g_ref, b_ref, w_ref, fq, fk, fv, dq, dk, dv, sa, sb, fg, *h_out, do_ln):
    x = x_ref[...]
    if do_ln:
        x = _layer_norm(x, g_ref[...], b_ref[...])
        h_out[0][...] = x
    xb = x.astype(BF16)

    def mm(a, b):
        return jnp.dot(xb, w_ref[:, a:b], preferred_element_type=F32)

    fq[...] = (mm(_C_FQ, _C_FK) * Q_SCALE).astype(BF16)
    fk[...] = mm(_C_FK, _C_FV).astype(BF16)
    fv[...] = mm(_C_FV, _C_DQ).astype(BF16)
    dq[...] = (mm(_C_DQ, _C_DK) * Q_SCALE).astype(BF16)
    dk[...] = mm(_C_DK, _C_DV).astype(BF16)
    dv[...] = mm(_C_DV, _C_GA).astype(BF16)
    sa[...] = _sigmoid(mm(_C_GA, _C_GB)).astype(BF16)
    sb[...] = _sigmoid(mm(_C_GB, _C_FG)).astype(BF16)
    fg[...] = mm(_C_FG, _C_END)


def _inproj(x, g, b, w, *, do_ln, tm):
    S = x.shape[0]
    row = lambda n: pl.BlockSpec((tm, n), lambda i: (i, 0))
    full = lambda a: pl.BlockSpec(a.shape, lambda i: (0,) * a.ndim)
    out_shape = [jax.ShapeDtypeStruct((S, 512), BF16)] * 6 + [jax.ShapeDtypeStruct((S, D_MODEL), BF16)] * 2 + [
        jax.ShapeDtypeStruct((S, LANES), F32)]
    out_specs = [row(512)] * 6 + [row(D_MODEL)] * 2 + [row(LANES)]
    if do_ln:
        out_shape.append(jax.ShapeDtypeStruct((S, D_MODEL), F32))
        out_specs.append(row(D_MODEL))
    return pl.pallas_call(
        functools.partial(_inproj_kernel, do_ln=do_ln),
        grid=(S // tm,),
        in_specs=[row(D_MODEL), full(g), full(b), full(w)],
        out_specs=out_specs,
        out_shape=out_shape,
        compiler_params=_cparams(("arbitrary",)),
        name="inproj",
    )(x, g, b, w)


def _prep_kernel(fq_ref, fk_ref, fv_ref, fg_ref, bf_ref, qa_ref, ka_ref, va_ref, st_ref, carry_ref, *, tb):
    @pl.when(pl.program_id(0) == 0)
    def _():
        carry_ref[...] = jnp.zeros_like(carry_ref)

    z = fg_ref[...] + bf_ref[...]
    logf = jnp.minimum(z, 0.0) - jnp.log1p(jnp.exp(-jnp.abs(z)))
    r = lax.broadcasted_iota(jnp.int32, (tb, tb), 0)
    c = lax.broadcasted_iota(jnp.int32, (tb, tb), 1)
    tri = (c <= r).astype(BF16)
    hi, mid, lo = _split3(logf)
    dot = lambda a: jnp.dot(tri, a, preferred_element_type=F32)
    cum = (dot(hi) + dot(mid)) + dot(lo) + carry_ref[0:1, :]
    carry_ref[0:1, :] = cum[tb - 1:tb, :]

    c_hi, c_mid, c_lo = (t.astype(F32) for t in _split3(cum * LOG2E))
    lane = lax.broadcasted_iota(jnp.int32, (tb, LANES), 1)
    one = jnp.ones((tb, LANES), F32)
    zero = jnp.zeros((tb, LANES), F32)
    lane1 = lax.broadcasted_iota(jnp.int32, (1, LANES), 1)
    q_norm2 = jnp.zeros((1, LANES), F32)
    k_norm2 = jnp.zeros((1, LANES), F32)

    def max_norm2(slab):
        n2 = jnp.sum(jnp.where(lane < 64, slab * slab, 0.0), axis=1, keepdims=True)
        return jnp.max(n2, axis=0, keepdims=True)

    for h in range(FOX_HEADS):
        p, odd = divmod(h, 2)
        col = lambda t: jnp.broadcast_to(t[:, h:h + 1], (tb, LANES))
        chi, cmid, clo = col(c_hi), col(c_mid), col(c_lo)

        def head_slab(ref):
            slab = ref[:, p * LANES:(p + 1) * LANES].astype(F32)
            return pltpu.roll(slab, HEAD_DIM, 1) if odd else slab

        qslab = head_slab(fq_ref)
        kslab = head_slab(fk_ref)
        q_norm2 = jnp.where(lane1 == h, max_norm2(qslab), q_norm2)
        k_norm2 = jnp.where(lane1 == h, max_norm2(kslab), k_norm2)
        qa = jnp.where(lane < 64, qslab,
                       jnp.where(lane == 64, chi, jnp.where(lane == 65, cmid, jnp.where(lane == 66, clo,
                                 jnp.where(lane < 70, one, zero)))))
        ka = jnp.where(lane < 64, kslab,
                       jnp.where(lane < 67, one, jnp.where(lane == 67, -chi, jnp.where(lane == 68, -cmid,
                                 jnp.where(lane == 69, -clo, zero)))))
        qa_ref[h] = qa.astype(BF16)
        ka_ref[h] = ka.astype(BF16)
        vslab = fv_ref[:, p * LANES:(p + 1) * LANES].astype(F32)
        if odd:
            va = jnp.where(lane >= 64, vslab, jnp.where(lane == 0, one, zero))
        else:
            va = jnp.where(lane < 64, vslab, jnp.where(lane == 64, one, zero))
        va_ref[h] = va.astype(BF16)

    st_ref[0] = jnp.concatenate([q_norm2, k_norm2, cum[tb - 1:tb, :] * LOG2E, jnp.zeros((5, LANES), F32)], axis=0)


def _prep(fq, fk, fv, fg, bf, *, tb):
    S = fq.shape[0]
    row = lambda n: pl.BlockSpec((tb, n), lambda i: (i, 0))
    hspec = pl.BlockSpec((FOX_HEADS, tb, LANES), lambda i: (0, i, 0))
    hshape = jax.ShapeDtypeStruct((FOX_HEADS, S, LANES), BF16)
    return pl.pallas_call(
        functools.partial(_prep_kernel, tb=tb),
        grid=(S // tb,),
        in_specs=[row(512), row(512), row(512), row(LANES), pl.BlockSpec((1, LANES), lambda i: (0, 0))],
        out_specs=[hspec] * 3 + [pl.BlockSpec((1, 8, LANES), lambda i: (i, 0, 0))],
        out_shape=[hshape] * 3 + [jax.ShapeDtypeStruct((S // tb, 8, LANES), F32)],
        scratch_shapes=[pltpu.VMEM((8, LANES), F32)],
        compiler_params=_cparams(("arbitrary",)),
        name="fox_prep",
    )(fq, fk, fv, fg, bf)


SKIP_LOG2 = 150.0


def _plan_kernel(st_ref, start_ref, *, nt, nsub):
    qn = jnp.sqrt(st_ref[pl.ds(0, nt, stride=8), :])
    kn = jnp.sqrt(st_ref[pl.ds(1, nt, stride=8), :])
    cum_end = st_ref[pl.ds(2, nt, stride=8), :]
    tile = lax.broadcasted_iota(jnp.int32, (nt, LANES), 0)
    rows = []
    for i in range(nt // nsub):
        own = jnp.logical_and(tile >= i * nsub, tile < (i + 1) * nsub)
        q_max = jnp.max(jnp.where(own, qn, 0.0), axis=0, keepdims=True)
        k_own = jnp.max(jnp.where(own, kn, 0.0), axis=0, keepdims=True)
        cum_q = cum_end[i * nsub - 1:i * nsub, :] if i > 0 else jnp.zeros((1, LANES), F32)
        upper = q_max * kn + (cum_q - cum_end) + q_max * k_own
        slack = SKIP_LOG2 + 1e-4 * jnp.abs(cum_end)
        needed = jnp.logical_and(upper >= -slack, tile < i * nsub)
        rows.append(jnp.min(jnp.where(needed, tile, i * nsub), axis=0, keepdims=True))
    rows += [jnp.zeros((1, LANES), jnp.int32)] * (8 - len(rows))
    start_ref[...] = jnp.concatenate(rows, axis=0)


def _plan(stats, *, nsub):
    nt = stats.shape[0]
    assert nt // nsub <= 8
    return pl.pallas_call(
        functools.partial(_plan_kernel, nt=nt, nsub=nsub),
        out_shape=jax.ShapeDtypeStruct((8, LANES), jnp.int32),
        compiler_params=pltpu.CompilerParams(vmem_limit_bytes=VMEM_LIMIT),
        name="fox_plan",
    )(stats.reshape(nt * 8, LANES))


def _nt_dot(a, b):
    return lax.dot_general(a, b, (((1,), (1,)), ((), ())), preferred_element_type=F32)


def _fox_kernel(start_ref, q_ref, k_ref, v_ref, o_ref, m_sc, acc_sc, *, tq, tk):
    i = pl.program_id(1)
    pair = pl.program_id(0)
    m_sc[...] = jnp.full_like(m_sc, NEG)
    acc_sc[...] = jnp.zeros_like(acc_sc)
    nsub = tq // tk

    def tile(j, r0, masked):
        off = pl.multiple_of(j * tk, tk)
        rows = tq - r0
        for hh in range(2):
            s = _nt_dot(q_ref[hh, r0:, :], k_ref[hh, pl.ds(off, tk), :])
            if masked:
                r = lax.broadcasted_iota(jnp.int32, (rows, tk), 0)
                c = lax.broadcasted_iota(jnp.int32, (rows, tk), 1)
                s = jnp.where(c <= r, s, NEG)
            m_prev = m_sc[hh, r0:, :]
            m_next = jnp.maximum(m_prev, jnp.max(s, axis=1, keepdims=True))
            alpha = jnp.exp2(m_prev - m_next)
            p = jnp.exp2(s - jnp.concatenate([m_next] * (tk // LANES), axis=1)).astype(BF16)
            pv = jnp.dot(p, v_ref[hh, pl.ds(off, tk), :], preferred_element_type=F32)
            acc_sc[hh, r0:, :] = alpha * acc_sc[hh, r0:, :] + pv
            m_sc[hh, r0:, :] = m_next

    def body(j, carry):
        tile(j, 0, False)
        return carry

    first = jnp.minimum(start_ref[i * FOX_HEADS + 2 * pair], start_ref[i * FOX_HEADS + 2 * pair + 1])
    lax.fori_loop(first, i * nsub, body, 0)
    for u in range(nsub):
        tile(i * nsub + u, u * tk, True)

    a0 = acc_sc[0]
    a1 = acc_sc[1]
    lane = lax.broadcasted_iota(jnp.int32, (tq, LANES), 1)
    o_ref[...] = jnp.where(lane < 64, a0 / a0[:, 64:65], a1 / a1[:, 0:1]).astype(o_ref.dtype)


def _fox_attention(start, qa, ka, va, *, tq, tk):
    S = qa.shape[1]
    grid_spec = pltpu.PrefetchScalarGridSpec(
        num_scalar_prefetch=1,
        grid=(FOX_HEADS // 2, S // tq),
        in_specs=[pl.BlockSpec((2, tq, LANES), lambda p, i, st: (p, i, 0)),
                  pl.BlockSpec((2, S, LANES), lambda p, i, st: (p, 0, 0)),
                  pl.BlockSpec((2, S, LANES), lambda p, i, st: (p, 0, 0))],
        out_specs=pl.BlockSpec((tq, LANES), lambda p, i, st: (i, p)),
        scratch_shapes=[pltpu.VMEM((2, tq, LANES), F32), pltpu.VMEM((2, tq, LANES), F32)],
    )
    return pl.pallas_call(
        functools.partial(_fox_kernel, tq=tq, tk=tk),
        grid_spec=grid_spec,
        out_shape=jax.ShapeDtypeStruct((S, FOX_WIDTH), BF16),
        compiler_params=_cparams(("arbitrary", "arbitrary")),
        name="fox_attn",
    )(start, qa, ka, va)


BIAS_KINDS = 2


def _bias_tile_kernel(table_ref, o_ref, *, t):
    h = pl.program_id(0)
    kind = pl.program_id(1)
    r = lax.broadcasted_iota(jnp.int32, (t, t), 0)
    c = lax.broadcasted_iota(jnp.int32, (t, t), 1)
    dist = r - c + kind * t
    n = jnp.maximum(dist, 0)
    max_exact = REL_BUCKETS // 2
    nf = jnp.maximum(n, 1).astype(F32)
    log_part = jnp.log(nf / max_exact) / math.log(REL_MAX_DISTANCE / max_exact) * (REL_BUCKETS - max_exact)
    large = jnp.minimum(max_exact + log_part.astype(jnp.int32), REL_BUCKETS - 1)
    bucket = jnp.where(n < max_exact, n, large)
    val = jnp.zeros((t, t), F32)
    for b in range(REL_BUCKETS):
        val = jnp.where(bucket == b, table_ref[h * REL_BUCKETS + b], val)
    far = table_ref[h * REL_BUCKETS + REL_BUCKETS - 1]
    o_ref[0, 0] = jnp.where(dist >= 0, (val - far) * LOG2E, NEG)


def _bias_tiles(table_flat, *, t):
    return pl.pallas_call(
        functools.partial(_bias_tile_kernel, t=t),
        grid=(DIFF_HEADS, BIAS_KINDS),
        in_specs=[pl.BlockSpec(memory_space=pltpu.SMEM)],
        out_specs=pl.BlockSpec((1, 1, t, t), lambda h, k: (h, k, 0, 0)),
        out_shape=jax.ShapeDtypeStruct((DIFF_HEADS, BIAS_KINDS, t, t), F32),
        compiler_params=_cparams(("arbitrary", "arbitrary")),
        name="bias_tiles",
    )(table_flat)


def _diff_kernel(q_ref, k_ref, v_ref, bias_ref, lam_ref, g_ref, o_ref, m_sc, l_sc, acc_sc,
                 *, tq, tk, lam_init):
    i = pl.program_id(1)
    m_sc[...] = jnp.full_like(m_sc, NEG)
    l_sc[...] = jnp.zeros_like(l_sc)
    acc_sc[...] = jnp.zeros_like(acc_sc)
    nsub = tq // tk

    lane = lax.broadcasted_iota(jnp.int32, (tq, LANES), 1)
    q = q_ref[...]
    zero = jnp.zeros_like(q)
    qs = (jnp.where(lane < 64, q, zero), jnp.where(lane >= 64, q, zero))

    def tile(j, r0, kinds):
        off = pl.multiple_of(j * tk, tk)
        k = k_ref[pl.ds(off, tk), :]
        v = v_ref[pl.ds(off, tk), :]
        for c in range(2):
            s = _nt_dot(qs[c][r0:], k)
            m_prev = m_sc[c, r0:, :]
            if kinds is not None:
                s = jnp.concatenate([s[a * tk:(a + 1) * tk] + bias_ref[0, kd] if kd < 2 else s[a * tk:(a + 1) * tk]
                                     for a, kd in enumerate(kinds)], axis=0)
            m_next = jnp.maximum(m_prev, jnp.max(s, axis=1, keepdims=True))
            p = jnp.exp2(s - jnp.concatenate([m_next] * (tk // LANES), axis=1))
            alpha = jnp.exp2(m_prev - m_next)
            l_sc[c, r0:, :] = alpha * l_sc[c, r0:, :] + sum(p[:, a * LANES:(a + 1) * LANES]
                                                            for a in range(tk // LANES))
            pv = jnp.dot(p.astype(BF16), v, preferred_element_type=F32)
            acc_sc[c, r0:, :] = alpha * acc_sc[c, r0:, :] + pv
            m_sc[c, r0:, :] = m_next

    def body(j, carry):
        tile(j, 0, None)
        return carry

    lax.fori_loop(0, i * nsub - 1, body, 0)

    @pl.when(i > 0)
    def _():
        tile(i * nsub - 1, 0, (1,) + (2,) * (nsub - 1))

    for u in range(nsub):
        tile(i * nsub + u, u * tk, ((0, 1) + (2,) * nsub)[:nsub - u])

    lv = lam_ref[...]
    lam = (jnp.exp(jnp.sum(lv[0:1] * lv[1:2], axis=1, keepdims=True))
           - jnp.exp(jnp.sum(lv[2:3] * lv[3:4], axis=1, keepdims=True)) + lam_init)
    l0 = jnp.sum(l_sc[0], axis=1, keepdims=True)
    l1 = jnp.sum(l_sc[1], axis=1, keepdims=True)
    o = acc_sc[0] / l0 - lam * (acc_sc[1] / l1)
    o = o * lax.rsqrt(jnp.mean(o * o, axis=1, keepdims=True) + SUBLN_EPS) * g_ref[...] * (1.0 - lam_init)
    o_ref[...] = o.astype(o_ref.dtype)


def _diff_attention(dq, dk, dv, bias, lamvec, g, *, tq, tk, lam_init):
    S = dq.shape[0]
    return pl.pallas_call(
        functools.partial(_diff_kernel, tq=tq, tk=tk, lam_init=lam_init),
        grid=(DIFF_HEADS, S // tq),
        in_specs=[pl.BlockSpec((tq, LANES), lambda h, i: (i, h)),
                  pl.BlockSpec((S, LANES), lambda h, i: (0, h)),
                  pl.BlockSpec((S, LANES), lambda h, i: (0, h)),
                  pl.BlockSpec((1, BIAS_KINDS, tk, tk), lambda h, i: (h, 0, 0, 0)),
                  pl.BlockSpec((8, LANES), lambda h, i: (0, 0)),
                  pl.BlockSpec((1, LANES), lambda h, i: (0, 0))],
        out_specs=pl.BlockSpec((tq, LANES), lambda h, i: (i, h)),
        out_shape=jax.ShapeDtypeStruct((S, DIFF_V_WIDTH), BF16),
        scratch_shapes=[pltpu.VMEM((2, tq, LANES), F32), pltpu.VMEM((2, tq, LANES), F32),
                        pltpu.VMEM((2, tq, LANES), F32)],
        compiler_params=_cparams(("arbitrary", "arbitrary")),
        name="diff_attn",
    )(dq, dk, dv, bias, lamvec, g)


def _merge_kernel(yf_ref, yd_ref, sa_ref, sb_ref, h_ref, wf_ref, wd_ref, wo_ref, g_ref, b_ref, o_ref, *, alpha):
    bf = jnp.dot(yf_ref[...], wf_ref[...], preferred_element_type=F32)
    bd = jnp.dot(yd_ref[...], wd_ref[...], preferred_element_type=F32)
    merged = sa_ref[...].astype(F32) * bf + sb_ref[...].astype(F32) * bd
    mix = jnp.dot(merged.astype(BF16), wo_ref[...], preferred_element_type=F32)
    o_ref[...] = _layer_norm(alpha * h_ref[...] + mix, g_ref[...], b_ref[...])


def _merge(yf, yd, sa, sb, h, wf, wd, wo, g, b, *, alpha, tm):
    S = h.shape[0]
    row = lambda n: pl.BlockSpec((tm, n), lambda i: (i, 0))
    full = lambda a: pl.BlockSpec(a.shape, lambda i: (0,) * a.ndim)
    return pl.pallas_call(
        functools.partial(_merge_kernel, alpha=alpha),
        grid=(S // tm,),
        in_specs=[row(512), row(512), row(D_MODEL), row(D_MODEL), row(D_MODEL),
                  full(wf), full(wd), full(wo), full(g), full(b)],
        out_specs=row(D_MODEL),
        out_shape=jax.ShapeDtypeStruct((S, D_MODEL), F32),
        compiler_params=_cparams(("arbitrary",)),
        name="merge_out",
    )(yf, yd, sa, sb, h, wf, wd, wo, g, b)


MOE_ROWS = 272
SCATTER_ROWS = 256
RANK_CHUNK = 256


def _router_kernel(h_ref, whi_ref, wlo_ref, cw_ref, rank_ref, rank_t_ref, cnt_ref):
    tm = h_ref.shape[0]
    h = h_ref[...]
    hhi = h.astype(BF16)
    hlo = (h - hhi.astype(F32)).astype(BF16)
    whi = whi_ref[...]
    dot = lambda a, b: jnp.dot(a, b, preferred_element_type=F32)
    logits = dot(hhi, whi) + dot(hhi, wlo_ref[...]) + dot(hlo, whi)
    lane = lax.broadcasted_iota(jnp.int32, logits.shape, 1).astype(F32)
    lg = jnp.where(lane < N_EXPERTS, logits, NEG)
    m1 = jnp.max(lg, axis=1, keepdims=True)
    i1 = jnp.min(jnp.where(lg == m1, lane, float(LANES)), axis=1, keepdims=True)
    lg2 = jnp.where(lane == i1, NEG, lg)
    m2 = jnp.max(lg2, axis=1, keepdims=True)
    i2 = jnp.min(jnp.where(lg2 == m2, lane, float(LANES)), axis=1, keepdims=True)
    e = jnp.exp(m2 - m1)
    g1 = 1.0 / (1.0 + e)
    chosen = jnp.logical_or(lane == i1, lane == i2)
    cw_ref[...] = jnp.where(lane == i1, g1, 0.0) + jnp.where(lane == i2, e * g1, 0.0)

    a = chosen.astype(BF16)
    r = lax.broadcasted_iota(jnp.int32, (RANK_CHUNK, RANK_CHUNK), 0)
    c = lax.broadcasted_iota(jnp.int32, (RANK_CHUNK, RANK_CHUNK), 1)
    strict_lower = (c < r).astype(BF16)
    ones_row = jnp.ones((8, RANK_CHUNK), BF16)
    carry = jnp.zeros((1, LANES), F32)
    for q in range(tm // RANK_CHUNK):
        rows = slice(q * RANK_CHUNK, (q + 1) * RANK_CHUNK)
        rank = jnp.where(chosen[rows], dot(strict_lower, a[rows]) + carry, -1.0)
        rank_ref[rows, :] = rank
        rank_t_ref[0, :, rows] = jnp.transpose(rank)[0:N_EXPERTS, :]
        carry = carry + dot(ones_row, a[rows])[0:1, :]
    cnt_ref[0] = jnp.broadcast_to(carry, (8, LANES)).astype(jnp.int32)


def _router(h, whi, wlo, *, tm):
    S = h.shape[0]
    nb = S // tm
    return pl.pallas_call(
        _router_kernel,
        grid=(nb,),
        in_specs=[pl.BlockSpec((tm, D_MODEL), lambda i: (i, 0)),
                  pl.BlockSpec(whi.shape, lambda i: (0, 0)), pl.BlockSpec(wlo.shape, lambda i: (0, 0))],
        out_specs=[pl.BlockSpec((tm, LANES), lambda i: (i, 0)),
                   pl.BlockSpec((tm, LANES), lambda i: (i, 0)),
                   pl.BlockSpec((1, N_EXPERTS, tm), lambda i: (i, 0, 0)),
                   pl.BlockSpec((1, 8, LANES), lambda i: (i, 0, 0))],
        out_shape=[jax.ShapeDtypeStruct((S, LANES), F32),
                   jax.ShapeDtypeStruct((S, LANES), F32),
                   jax.ShapeDtypeStruct((nb, N_EXPERTS, tm), F32),
                   jax.ShapeDtypeStruct((nb, 8, LANES), jnp.int32)],
        compiler_params=_cparams(("arbitrary",)),
        name="router",
    )(h, whi, wlo)


def _moe_kernel(cnt_ref, h_ref, cw_ref, rank_ref, rank_t_ref, wg_ref, wu_ref, wd_ref, g_ref, b_ref, o_ref,
                hb_sc, xs_sc, acc_sc, *, alpha):
    tm = h_ref.shape[0]
    i = pl.program_id(0)
    e = pl.program_id(1)
    c = pl.program_id(2)
    last_c = c == pl.num_programs(2) - 1
    n_rows = cnt_ref[i * N_EXPERTS + e]
    n_steps = (n_rows + MOE_ROWS - 1) // MOE_ROWS
    n_scatter = (n_rows + SCATTER_ROWS - 1) // SCATTER_ROWS

    @pl.when(jnp.logical_and(e == 0, c == 0))
    def _():
        hb_sc[...] = h_ref[...].astype(BF16)
        o_ref[...] = jnp.zeros_like(o_ref)

    @pl.when(c == 0)
    def _():
        rank_row = rank_t_ref[0, pl.ds(e, 1), :]

        def gather(t, carry):
            r0 = pl.multiple_of(t * MOE_ROWS, 16)
            slot = (lax.broadcasted_iota(jnp.int32, (MOE_ROWS, tm), 0) + r0).astype(F32)
            onehot = (rank_row == slot).astype(BF16)
            xs_sc[pl.ds(r0, MOE_ROWS), :] = jnp.dot(onehot, hb_sc[...], preferred_element_type=F32).astype(BF16)
            acc_sc[pl.ds(r0, MOE_ROWS), :] = jnp.zeros((MOE_ROWS, D_MODEL), F32)
            return carry

        lax.fori_loop(0, n_steps, gather, 0)

        @pl.when(n_steps * MOE_ROWS < n_scatter * SCATTER_ROWS)
        def _():
            acc_sc[pl.ds(pl.multiple_of(n_steps * MOE_ROWS, 16), SCATTER_ROWS), :] = jnp.zeros(
                (SCATTER_ROWS, D_MODEL), F32)

    def swiglu(t, carry):
        r0 = pl.multiple_of(t * MOE_ROWS, 16)
        x = xs_sc[pl.ds(r0, MOE_ROWS), :]
        gate = jnp.dot(x, wg_ref[0], preferred_element_type=F32)
        up = jnp.dot(x, wu_ref[0], preferred_element_type=F32)
        a = (gate * _sigmoid(gate) * up).astype(BF16)
        acc_sc[pl.ds(r0, MOE_ROWS), :] += jnp.dot(a, wd_ref[0], preferred_element_type=F32)
        return carry

    lax.fori_loop(0, n_steps, swiglu, 0)

    @pl.when(last_c)
    def _():
        lane = lax.broadcasted_iota(jnp.int32, (tm, LANES), 1)
        pick = lambda ref: jnp.broadcast_to(
            jnp.sum(jnp.where(lane == e, ref[...], 0.0), axis=1, keepdims=True), (tm, LANES))
        rank_col = pick(rank_ref)
        gate_col = jnp.concatenate([pick(cw_ref)] * (D_MODEL // LANES), axis=1)

        def scatter(t, carry):
            r0 = pl.multiple_of(t * SCATTER_ROWS, SCATTER_ROWS)
            slot = (lane + r0).astype(F32)
            onehot = jnp.concatenate([(rank_col == slot + float(a * LANES)) for a in range(SCATTER_ROWS // LANES)],
                                     axis=1).astype(BF16)
            y = acc_sc[pl.ds(r0, SCATTER_ROWS), :].astype(BF16)
            o_ref[...] += gate_col * jnp.dot(onehot, y, preferred_element_type=F32)
            return carry

        lax.fori_loop(0, n_scatter, scatter, 0)

    @pl.when(jnp.logical_and(e == pl.num_programs(1) - 1, last_c))
    def _():
        o_ref[...] = _layer_norm(alpha * h_ref[...] + o_ref[...], g_ref[...], b_ref[...])


def _moe(h, cw, rank, rank_t, counts, w_gu, w_dn, g, b, *, alpha, tm, fc):
    S = h.shape[0]
    E, _, F2 = w_gu.shape
    nc = F2 // 2 // fc
    once = pl.Buffered(1)
    rows_cap = pl.cdiv(tm, MOE_ROWS) * MOE_ROWS
    grid_spec = pltpu.PrefetchScalarGridSpec(
        num_scalar_prefetch=1,
        grid=(S // tm, E, nc),
        in_specs=[pl.BlockSpec((tm, D_MODEL), lambda i, e, c, cnt: (i, 0), pipeline_mode=once),
                  pl.BlockSpec((tm, LANES), lambda i, e, c, cnt: (i, 0), pipeline_mode=once),
                  pl.BlockSpec((tm, LANES), lambda i, e, c, cnt: (i, 0), pipeline_mode=once),
                  pl.BlockSpec((1, N_EXPERTS, tm), lambda i, e, c, cnt: (i, 0, 0), pipeline_mode=once),
                  pl.BlockSpec((1, D_MODEL, fc), lambda i, e, c, cnt: (e, 0, c)),
                  pl.BlockSpec((1, D_MODEL, fc), lambda i, e, c, cnt: (e, 0, nc + c)),
                  pl.BlockSpec((1, fc, D_MODEL), lambda i, e, c, cnt: (e, c, 0)),
                  pl.BlockSpec(g.shape, lambda i, e, c, cnt: (0, 0)),
                  pl.BlockSpec(b.shape, lambda i, e, c, cnt: (0, 0))],
        out_specs=pl.BlockSpec((tm, D_MODEL), lambda i, e, c, cnt: (i, 0), pipeline_mode=once),
        scratch_shapes=[pltpu.VMEM((tm, D_MODEL), BF16), pltpu.VMEM((rows_cap, D_MODEL), BF16),
                        pltpu.VMEM((rows_cap, D_MODEL), F32)],
    )
    return pl.pallas_call(
        functools.partial(_moe_kernel, alpha=alpha),
        grid_spec=grid_spec,
        out_shape=jax.ShapeDtypeStruct((S, D_MODEL), F32),
        compiler_params=_cparams(("arbitrary", "arbitrary", "arbitrary")),
        name="moe_ffn",
    )(counts, h, cw, rank, rank_t, w_gu, w_gu, w_dn, g, b)


def _ffn_kernel(h_ref, wg_ref, wu_ref, wd_ref, g_ref, b_ref, o_ref, hb_sc, acc_sc, *, alpha):
    c = pl.program_id(1)

    @pl.when(c == 0)
    def _():
        hb_sc[...] = h_ref[...].astype(BF16)
        acc_sc[...] = jnp.zeros_like(acc_sc)

    hb = hb_sc[...]
    gate = jnp.dot(hb, wg_ref[...], preferred_element_type=F32)
    up = jnp.dot(hb, wu_ref[...], preferred_element_type=F32)
    a = gate * _sigmoid(gate) * up
    acc_sc[...] += jnp.dot(a.astype(BF16), wd_ref[...], preferred_element_type=F32)

    @pl.when(c == pl.num_programs(1) - 1)
    def _():
        o_ref[...] = _layer_norm(alpha * h_ref[...] + acc_sc[...], g_ref[...], b_ref[...])


def _ffn(h, w_gu, w_dn, g, b, *, alpha, tm, fc):
    S = h.shape[0]
    nc = w_gu.shape[1] // 2 // fc
    return pl.pallas_call(
        functools.partial(_ffn_kernel, alpha=alpha),
        grid=(S // tm, nc),
        in_specs=[pl.BlockSpec((tm, D_MODEL), lambda i, c: (i, 0)),
                  pl.BlockSpec((D_MODEL, fc), lambda i, c: (0, c)),
                  pl.BlockSpec((D_MODEL, fc), lambda i, c: (0, nc + c)),
                  pl.BlockSpec((fc, D_MODEL), lambda i, c: (c, 0)),
                  pl.BlockSpec(g.shape, lambda i, c: (0, 0)), pl.BlockSpec(b.shape, lambda i, c: (0, 0))],
        out_specs=pl.BlockSpec((tm, D_MODEL), lambda i, c: (i, 0)),
        out_shape=jax.ShapeDtypeStruct((S, D_MODEL), F32),
        scratch_shapes=[pltpu.VMEM((tm, D_MODEL), BF16), pltpu.VMEM((tm, D_MODEL), F32)],
        compiler_params=_cparams(("arbitrary", "arbitrary")),
        name="dense_ffn",
    )(h, w_gu, w_gu, w_dn, g, b)


def _pad_lanes(a, n=LANES):
    return jnp.pad(a, [(0, 0)] * (a.ndim - 1) + [(0, n - a.shape[-1])])


def kernel(x, ln_in_g, ln_in_b, w_in, b_fgate, lam_q1, lam_k1, lam_q2, lam_k2, subln_g, w_branch_fox, w_branch_diff, w_out, ln_mix_g, ln_mix_b, rel_bias, w_ffn_gate_up, w_ffn_down, w_router, w_expert_gate_up, w_expert_down, ln_ffn_g, ln_ffn_b):
    B, S, _ = x.shape
    assert B == 1
    depth = w_in.shape[0]
    alpha = (2.0 * depth) ** 0.25
    tq = min(ATTN_TQ, S)
    tk = min(ATTN_TK, tq)
    tm = min(512, S)
    tf = min(1024, S)
    te = min(MOE_TM, S)
    assert tk >= REL_MAX_DISTANCE and tq % tk == 0 and S % tq == 0 and S % tf == 0 and S % tm == 0 and S % te == 0
    assert te % SCATTER_ROWS == 0 and te % RANK_CHUNK == 0 and MOE_ROWS % 16 == 0
    row = lambda v: v.reshape(1, -1).astype(F32)

    table = rel_bias.astype(F32)
    bias = _bias_tiles(table.T.reshape(-1), t=tk)

    h = x[0]
    for l in range(depth):
        lam_init = 0.8 - 0.6 * math.exp(-0.3 * l)
        fq, fk, fv, fg, dq, dk, dv, ga, gb = jnp.split(w_in[l], list(_cumsum(IN_SPLIT_SIZES))[:-1], axis=-1)
        w_cat = jnp.concatenate([fq, fk, fv, dq, dk, dv, ga, gb, _pad_lanes(fg)], axis=-1).astype(BF16)
        outs = _inproj(h, row(ln_in_g), row(ln_in_b), w_cat, do_ln=(l == 0), tm=tm)
        fq, fk, fv, dq, dk, dv, sa, sb, fg = outs[:9]
        if l == 0:
            h = outs[9]
        qa, ka, va, stats = _prep(fq, fk, fv, fg, _pad_lanes(row(b_fgate[l])), tb=tk)
        start = _plan(stats, nsub=tq // tk)[:S // tq, :FOX_HEADS].reshape(-1)
        y_fox = _fox_attention(start, qa, ka, va, tq=tq, tk=tk)
        lamvec = jnp.zeros((8, LANES), F32).at[0:4, 0:HEAD_DIM].set(
            jnp.stack([lam_q1[l], lam_k1[l], lam_q2[l], lam_k2[l]]).astype(F32))
        y_diff = _diff_attention(dq, dk, dv, bias, lamvec, row(subln_g[l]), tq=tq, tk=tk,
                                 lam_init=lam_init)
        h = _merge(y_fox, y_diff, sa, sb, h, w_branch_fox[l].astype(BF16), w_branch_diff[l].astype(BF16),
                   w_out[l].astype(BF16), row(ln_mix_g[l]), row(ln_mix_b[l]), alpha=alpha, tm=tm)
        g, b = row(ln_ffn_g[l]), row(ln_ffn_b[l])
        if l % 2 == 0:
            h = _ffn(h, w_ffn_gate_up[l // 2].astype(BF16), w_ffn_down[l // 2].astype(BF16), g, b,
                     alpha=alpha, tm=tf, fc=1408)
        else:
            wr = _pad_lanes(w_router[l // 2].astype(F32))
            whi = wr.astype(BF16)
            wlo = (wr - whi.astype(F32)).astype(BF16)
            cw, rank, rank_t, counts = _router(h, whi, wlo, tm=te)
            h = _moe(h, cw, rank, rank_t, counts[:, 0, :N_EXPERTS].reshape(-1),
                     w_expert_gate_up[l // 2].astype(BF16), w_expert_down[l // 2].astype(BF16),
                     g, b, alpha=alpha, tm=te, fc=896)
    return h[None]


def _cumsum(sizes):
    tot = 0
    for s in sizes:
        tot += s
        yield tot
```

```python
import functools
import math

import jax
import jax.numpy as jnp
from jax import lax
from jax.experimental import pallas as pl
from jax.experimental.pallas import tpu as pltpu

F32 = jnp.float32
BF16 = jnp.bfloat16

D_MODEL = 1024
HEAD_DIM = 64
FOX_HEADS = 8
FOX_WIDTH = FOX_HEADS * HEAD_DIM
DIFF_HEADS = 4
DIFF_QK_WIDTH = DIFF_HEADS * 2 * HEAD_DIM
DIFF_V_DIM = 2 * HEAD_DIM
DIFF_V_WIDTH = DIFF_HEADS * DIFF_V_DIM
IN_SPLIT_SIZES = (FOX_WIDTH, FOX_WIDTH, FOX_WIDTH, FOX_HEADS, DIFF_QK_WIDTH, DIFF_QK_WIDTH, DIFF_V_WIDTH,
                  D_MODEL, D_MODEL)
REL_BUCKETS = 32
REL_MAX_DISTANCE = 128
N_EXPERTS = 8
LN_EPS = 1e-5
SUBLN_EPS = 1e-5

LANES = 128
LOG2E = 1.4426950408889634
Q_SCALE = HEAD_DIM ** -0.5 * LOG2E
NEG = -1e30
VMEM_LIMIT = 56 * 1024 * 1024
ATTN_TQ = 2048
ATTN_TK = 512
MOE_TM = 2048


def _cparams(sem):
    return pltpu.CompilerParams(dimension_semantics=sem, vmem_limit_bytes=VMEM_LIMIT)


def _layer_norm(x, g, b):
    mu = jnp.mean(x, axis=-1, keepdims=True)
    xc = x - mu
    var = jnp.mean(xc * xc, axis=-1, keepdims=True)
    return xc * lax.rsqrt(var + LN_EPS) * g + b


def _sigmoid(x):
    return 1.0 / (1.0 + jnp.exp(-x))


def _split3(x):
    hi = x.astype(BF16)
    r1 = x - hi.astype(F32)
    mid = r1.astype(BF16)
    lo = (r1 - mid.astype(F32)).astype(BF16)
    return hi, mid, lo


_C_FQ, _C_FK, _C_FV, _C_DQ, _C_DK, _C_DV, _C_GA, _C_GB, _C_FG, _C_END = (
    0, 512, 1024, 1536, 2048, 2560, 3072, 4096, 5120, 5248)


def _inproj_kernel(x_ref, g_ref, b_ref, w_ref, bf_ref, qa_ref, ka_ref, va_ref, st_ref, dq, dk, dv, sa, sb,
                   *rest, do_ln):
    carry_ref = rest[-1]
    tb = x_ref.shape[0]

    @pl.when(pl.program_id(0) == 0)
    def _():
        carry_ref[...] = jnp.zeros_like(carry_ref)

    x = x_ref[...]
    if do_ln:
        x = _layer_norm(x, g_ref[...], b_ref[...])
        rest[0][...] = x
    xb = x.astype(BF16)

    def mm(a, b):
        return jnp.dot(xb, w_ref[:, a:b], preferred_element_type=F32)

    dq[...] = (mm(_C_DQ, _C_DK) * Q_SCALE).astype(BF16)
    dk[...] = mm(_C_DK, _C_DV).astype(BF16)
    dv[...] = mm(_C_DV, _C_GA).astype(BF16)
    sa[...] = _sigmoid(mm(_C_GA, _C_GB)).astype(BF16)
    sb[...] = _sigmoid(mm(_C_GB, _C_FG)).astype(BF16)
    fq = (mm(_C_FQ, _C_FK) * Q_SCALE).astype(BF16)
    fk = mm(_C_FK, _C_FV).astype(BF16)
    fv = mm(_C_FV, _C_DQ).astype(BF16)

    z = mm(_C_FG, _C_END) + bf_ref[...]
    logf = jnp.minimum(z, 0.0) - jnp.log1p(jnp.exp(-jnp.abs(z)))
    r = lax.broadcasted_iota(jnp.int32, (tb, tb), 0)
    c = lax.broadcasted_iota(jnp.int32, (tb, tb), 1)
    tri = (c <= r).astype(BF16)
    hi, mid, lo = _split3(logf)
    dot = lambda a: jnp.dot(tri, a, preferred_element_type=F32)
    cum = (dot(hi) + dot(mid)) + dot(lo) + carry_ref[0:1, :]
    carry_ref[0:1, :] = cum[tb - 1:tb, :]

    c_hi, c_mid, c_lo = (t.astype(F32) for t in _split3(cum * LOG2E))
    lane = lax.broadcasted_iota(jnp.int32, (tb, LANES), 1)
    one = jnp.ones((tb, LANES), F32)
    zero = jnp.zeros((tb, LANES), F32)
    lane1 = lax.broadcasted_iota(jnp.int32, (1, LANES), 1)
    q_norm2 = jnp.zeros((1, LANES), F32)
    k_norm2 = jnp.zeros((1, LANES), F32)

    def max_norm2(slab):
        n2 = jnp.sum(jnp.where(lane < 64, slab * slab, 0.0), axis=1, keepdims=True)
        return jnp.max(n2, axis=0, keepdims=True)

    for h in range(FOX_HEADS):
        p, odd = divmod(h, 2)
        col = lambda t: jnp.broadcast_to(t[:, h:h + 1], (tb, LANES))
        chi, cmid, clo = col(c_hi), col(c_mid), col(c_lo)

        def head_slab(a):
            slab = a[:, p * LANES:(p + 1) * LANES].astype(F32)
            return pltpu.roll(slab, HEAD_DIM, 1) if odd else slab

        qslab = head_slab(fq)
        kslab = head_slab(fk)
        q_norm2 = jnp.where(lane1 == h, max_norm2(qslab), q_norm2)
        k_norm2 = jnp.where(lane1 == h, max_norm2(kslab), k_norm2)
        qa = jnp.where(lane < 64, qslab,
                       jnp.where(lane == 64, chi, jnp.where(lane == 65, cmid, jnp.where(lane == 66, clo,
                                 jnp.where(lane < 70, one, zero)))))
        ka = jnp.where(lane < 64, kslab,
                       jnp.where(lane < 67, one, jnp.where(lane == 67, -chi, jnp.where(lane == 68, -cmid,
                                 jnp.where(lane == 69, -clo, zero)))))
        qa_ref[h] = qa.astype(BF16)
        ka_ref[h] = ka.astype(BF16)
        vslab = fv[:, p * LANES:(p + 1) * LANES].astype(F32)
        if odd:
            va = jnp.where(lane >= 64, vslab, jnp.where(lane == 0, one, zero))
        else:
            va = jnp.where(lane < 64, vslab, jnp.where(lane == 64, one, zero))
        va_ref[h] = va.astype(BF16)

    st_ref[0] = jnp.concatenate([q_norm2, k_norm2, cum[tb - 1:tb, :] * LOG2E, jnp.zeros((5, LANES), F32)], axis=0)


def _inproj(x, g, b, w, bf, *, do_ln, tm):
    S = x.shape[0]
    row = lambda n: pl.BlockSpec((tm, n), lambda i: (i, 0))
    full = lambda a: pl.BlockSpec(a.shape, lambda i: (0,) * a.ndim)
    hspec = pl.BlockSpec((FOX_HEADS, tm, LANES), lambda i: (0, i, 0))
    hshape = jax.ShapeDtypeStruct((FOX_HEADS, S, LANES), BF16)
    out_shape = [hshape] * 3 + [jax.ShapeDtypeStruct((S // tm, 8, LANES), F32)] + [
        jax.ShapeDtypeStruct((S, 512), BF16)] * 3 + [jax.ShapeDtypeStruct((S, D_MODEL), BF16)] * 2
    out_specs = [hspec] * 3 + [pl.BlockSpec((1, 8, LANES), lambda i: (i, 0, 0))] + [row(512)] * 3 + [row(D_MODEL)] * 2
    if do_ln:
        out_shape.append(jax.ShapeDtypeStruct((S, D_MODEL), F32))
        out_specs.append(row(D_MODEL))
    return pl.pallas_call(
        functools.partial(_inproj_kernel, do_ln=do_ln),
        grid=(S // tm,),
        in_specs=[row(D_MODEL), full(g), full(b), full(w), full(bf)],
        out_specs=out_specs,
        out_shape=out_shape,
        scratch_shapes=[pltpu.VMEM((8, LANES), F32)],
        compiler_params=_cparams(("arbitrary",)),
        name="inproj",
    )(x, g, b, w, bf)


SKIP_LOG2 = 150.0


def _plan_kernel(st_ref, start_ref, *, nt, nsub):
    qn = jnp.sqrt(st_ref[pl.ds(0, nt, stride=8), :])
    kn = jnp.sqrt(st_ref[pl.ds(1, nt, stride=8), :])
    cum_end = st_ref[pl.ds(2, nt, stride=8), :]
    tile = lax.broadcasted_iota(jnp.int32, (nt, LANES), 0)
    rows = []
    for i in range(nt // nsub):
        own = jnp.logical_and(tile >= i * nsub, tile < (i + 1) * nsub)
        q_max = jnp.max(jnp.where(own, qn, 0.0), axis=0, keepdims=True)
        k_own = jnp.max(jnp.where(own, kn, 0.0), axis=0, keepdims=True)
        cum_q = cum_end[i * nsub - 1:i * nsub, :] if i > 0 else jnp.zeros((1, LANES), F32)
        upper = q_max * kn + (cum_q - cum_end) + q_max * k_own
        slack = SKIP_LOG2 + 1e-4 * jnp.abs(cum_end)
        needed = jnp.logical_and(upper >= -slack, tile < i * nsub)
        rows.append(jnp.min(jnp.where(needed, tile, i * nsub), axis=0, keepdims=True))
    rows += [jnp.zeros((1, LANES), jnp.int32)] * (8 - len(rows))
    start_ref[...] = jnp.concatenate(rows, axis=0)


def _plan(stats, *, nsub):
    nt = stats.shape[0]
    assert nt // nsub <= 8
    return pl.pallas_call(
        functools.partial(_plan_kernel, nt=nt, nsub=nsub),
        out_shape=jax.ShapeDtypeStruct((8, LANES), jnp.int32),
        compiler_params=pltpu.CompilerParams(vmem_limit_bytes=VMEM_LIMIT),
        name="fox_plan",
    )(stats.reshape(nt * 8, LANES))


def _nt_dot(a, b):
    return lax.dot_general(a, b, (((1,), (1,)), ((), ())), preferred_element_type=F32)


def _fox_kernel(start_ref, q_ref, k_ref, v_ref, o_ref, m_sc, acc_sc, *, tq, tk):
    i = pl.program_id(1)
    pair = pl.program_id(0)
    m_sc[...] = jnp.full_like(m_sc, NEG)
    acc_sc[...] = jnp.zeros_like(acc_sc)
    nsub = tq // tk

    def tile(j, r0, masked):
        off = pl.multiple_of(j * tk, tk)
        rows = tq - r0
        for hh in range(2):
            s = _nt_dot(q_ref[hh, r0:, :], k_ref[hh, pl.ds(off, tk), :])
            if masked:
                r = lax.broadcasted_iota(jnp.int32, (rows, tk), 0)
                c = lax.broadcasted_iota(jnp.int32, (rows, tk), 1)
                s = jnp.where(c <= r, s, NEG)
            m_prev = m_sc[hh, r0:, :]
            m_next = jnp.maximum(m_prev, jnp.max(s, axis=1, keepdims=True))
            alpha = jnp.exp2(m_prev - m_next)
            p = jnp.exp2(s - jnp.concatenate([m_next] * (tk // LANES), axis=1)).astype(BF16)
            pv = jnp.dot(p, v_ref[hh, pl.ds(off, tk), :], preferred_element_type=F32)
            acc_sc[hh, r0:, :] = alpha * acc_sc[hh, r0:, :] + pv
            m_sc[hh, r0:, :] = m_next

    def body(j, carry):
        tile(j, 0, False)
        return carry

    first = jnp.minimum(start_ref[i * FOX_HEADS + 2 * pair], start_ref[i * FOX_HEADS + 2 * pair + 1])
    lax.fori_loop(first, i * nsub, body, 0)
    for u in range(nsub):
        tile(i * nsub + u, u * tk, True)

    a0 = acc_sc[0]
    a1 = acc_sc[1]
    lane = lax.broadcasted_iota(jnp.int32, (tq, LANES), 1)
    o_ref[...] = jnp.where(lane < 64, a0 / a0[:, 64:65], a1 / a1[:, 0:1]).astype(o_ref.dtype)


def _fox_attention(start, qa, ka, va, *, tq, tk):
    S = qa.shape[1]
    grid_spec = pltpu.PrefetchScalarGridSpec(
        num_scalar_prefetch=1,
        grid=(FOX_HEADS // 2, S // tq),
        in_specs=[pl.BlockSpec((2, tq, LANES), lambda p, i, st: (p, i, 0)),
                  pl.BlockSpec((2, S, LANES), lambda p, i, st: (p, 0, 0)),
                  pl.BlockSpec((2, S, LANES), lambda p, i, st: (p, 0, 0))],
        out_specs=pl.BlockSpec((tq, LANES), lambda p, i, st: (i, p)),
        scratch_shapes=[pltpu.VMEM((2, tq, LANES), F32), pltpu.VMEM((2, tq, LANES), F32)],
    )
    return pl.pallas_call(
        functools.partial(_fox_kernel, tq=tq, tk=tk),
        grid_spec=grid_spec,
        out_shape=jax.ShapeDtypeStruct((S, FOX_WIDTH), BF16),
        compiler_params=_cparams(("arbitrary", "arbitrary")),
        name="fox_attn",
    )(start, qa, ka, va)


BIAS_KINDS = 2


def _bias_tile_kernel(table_ref, o_ref, *, t):
    h = pl.program_id(0)
    kind = pl.program_id(1)
    r = lax.broadcasted_iota(jnp.int32, (t, t), 0)
    c = lax.broadcasted_iota(jnp.int32, (t, t), 1)
    dist = r - c + kind * t
    n = jnp.maximum(dist, 0)
    max_exact = REL_BUCKETS // 2
    nf = jnp.maximum(n, 1).astype(F32)
    log_part = jnp.log(nf / max_exact) / math.log(REL_MAX_DISTANCE / max_exact) * (REL_BUCKETS - max_exact)
    large = jnp.minimum(max_exact + log_part.astype(jnp.int32), REL_BUCKETS - 1)
    bucket = jnp.where(n < max_exact, n, large)
    val = jnp.zeros((t, t), F32)
    for b in range(REL_BUCKETS):
        val = jnp.where(bucket == b, table_ref[h * REL_BUCKETS + b], val)
    far = table_ref[h * REL_BUCKETS + REL_BUCKETS - 1]
    o_ref[0, 0] = jnp.where(dist >= 0, (val - far) * LOG2E, NEG)


def _bias_tiles(table_flat, *, t):
    return pl.pallas_call(
        functools.partial(_bias_tile_kernel, t=t),
        grid=(DIFF_HEADS, BIAS_KINDS),
        in_specs=[pl.BlockSpec(memory_space=pltpu.SMEM)],
        out_specs=pl.BlockSpec((1, 1, t, t), lambda h, k: (h, k, 0, 0)),
        out_shape=jax.ShapeDtypeStruct((DIFF_HEADS, BIAS_KINDS, t, t), F32),
        compiler_params=_cparams(("arbitrary", "arbitrary")),
        name="bias_tiles",
    )(table_flat)


def _diff_kernel(q_ref, k_ref, v_ref, bias_ref, lam_ref, g_ref, o_ref, m_sc, l_sc, acc_sc,
                 *, tq, tk, lam_init):
    i = pl.program_id(1)
    m_sc[...] = jnp.full_like(m_sc, NEG)
    l_sc[...] = jnp.zeros_like(l_sc)
    acc_sc[...] = jnp.zeros_like(acc_sc)
    nsub = tq // tk

    lane = lax.broadcasted_iota(jnp.int32, (tq, LANES), 1)
    q = q_ref[...]
    zero = jnp.zeros_like(q)
    qs = (jnp.where(lane < 64, q, zero), jnp.where(lane >= 64, q, zero))

    def tile(j, r0, kinds):
        off = pl.multiple_of(j * tk, tk)
        k = k_ref[pl.ds(off, tk), :]
        v = v_ref[pl.ds(off, tk), :]
        for c in range(2):
            s = _nt_dot(qs[c][r0:], k)
            m_prev = m_sc[c, r0:, :]
            if kinds is not None:
                s = jnp.concatenate([s[a * tk:(a + 1) * tk] + bias_ref[0, kd] if kd < 2 else s[a * tk:(a + 1) * tk]
                                     for a, kd in enumerate(kinds)], axis=0)
            m_next = jnp.maximum(m_prev, jnp.max(s, axis=1, keepdims=True))
            p = jnp.exp2(s - jnp.concatenate([m_next] * (tk // LANES), axis=1))
            alpha = jnp.exp2(m_prev - m_next)
            l_sc[c, r0:, :] = alpha * l_sc[c, r0:, :] + sum(p[:, a * LANES:(a + 1) * LANES]
                                                            for a in range(tk // LANES))
            pv = jnp.dot(p.astype(BF16), v, preferred_element_type=F32)
            acc_sc[c, r0:, :] = alpha * acc_sc[c, r0:, :] + pv
            m_sc[c, r0:, :] = m_next

    def body(j, carry):
        tile(j, 0, None)
        return carry

    lax.fori_loop(0, i * nsub - 1, body, 0)

    @pl.when(i > 0)
    def _():
        tile(i * nsub - 1, 0, (1,) + (2,) * (nsub - 1))

    for u in range(nsub):
        tile(i * nsub + u, u * tk, ((0, 1) + (2,) * nsub)[:nsub - u])

    lv = lam_ref[...]
    lam = (jnp.exp(jnp.sum(lv[0:1] * lv[1:2], axis=1, keepdims=True))
           - jnp.exp(jnp.sum(lv[2:3] * lv[3:4], axis=1, keepdims=True)) + lam_init)
    l0 = jnp.sum(l_sc[0], axis=1, keepdims=True)
    l1 = jnp.sum(l_sc[1], axis=1, keepdims=True)
    o = acc_sc[0] / l0 - lam * (acc_sc[1] / l1)
    o = o * lax.rsqrt(jnp.mean(o * o, axis=1, keepdims=True) + SUBLN_EPS) * g_ref[...] * (1.0 - lam_init)
    o_ref[...] = o.astype(o_ref.dtype)


def _diff_attention(dq, dk, dv, bias, lamvec, g, *, tq, tk, lam_init):
    S = dq.shape[0]
    return pl.pallas_call(
        functools.partial(_diff_kernel, tq=tq, tk=tk, lam_init=lam_init),
        grid=(DIFF_HEADS, S // tq),
        in_specs=[pl.BlockSpec((tq, LANES), lambda h, i: (i, h)),
                  pl.BlockSpec((S, LANES), lambda h, i: (0, h)),
                  pl.BlockSpec((S, LANES), lambda h, i: (0, h)),
                  pl.BlockSpec((1, BIAS_KINDS, tk, tk), lambda h, i: (h, 0, 0, 0)),
                  pl.BlockSpec((8, LANES), lambda h, i: (0, 0)),
                  pl.BlockSpec((1, LANES), lambda h, i: (0, 0))],
        out_specs=pl.BlockSpec((tq, LANES), lambda h, i: (i, h)),
        out_shape=jax.ShapeDtypeStruct((S, DIFF_V_WIDTH), BF16),
        scratch_shapes=[pltpu.VMEM((2, tq, LANES), F32), pltpu.VMEM((2, tq, LANES), F32),
                        pltpu.VMEM((2, tq, LANES), F32)],
        compiler_params=_cparams(("arbitrary", "arbitrary")),
        name="diff_attn",
    )(dq, dk, dv, bias, lamvec, g)


def _merge_kernel(yf_ref, yd_ref, sa_ref, sb_ref, h_ref, wf_ref, wd_ref, wo_ref, g_ref, b_ref, o_ref, *, alpha):
    bf = jnp.dot(yf_ref[...], wf_ref[...], preferred_element_type=F32)
    bd = jnp.dot(yd_ref[...], wd_ref[...], preferred_element_type=F32)
    merged = sa_ref[...].astype(F32) * bf + sb_ref[...].astype(F32) * bd
    mix = jnp.dot(merged.astype(BF16), wo_ref[...], preferred_element_type=F32)
    o_ref[...] = _layer_norm(alpha * h_ref[...] + mix, g_ref[...], b_ref[...])


def _merge(yf, yd, sa, sb, h, wf, wd, wo, g, b, *, alpha, tm):
    S = h.shape[0]
    row = lambda n: pl.BlockSpec((tm, n), lambda i: (i, 0))
    full = lambda a: pl.BlockSpec(a.shape, lambda i: (0,) * a.ndim)
    return pl.pallas_call(
        functools.partial(_merge_kernel, alpha=alpha),
        grid=(S // tm,),
        in_specs=[row(512), row(512), row(D_MODEL), row(D_MODEL), row(D_MODEL),
                  full(wf), full(wd), full(wo), full(g), full(b)],
        out_specs=row(D_MODEL),
        out_shape=jax.ShapeDtypeStruct((S, D_MODEL), F32),
        compiler_params=_cparams(("arbitrary",)),
        name="merge_out",
    )(yf, yd, sa, sb, h, wf, wd, wo, g, b)


MOE_ROWS = 272
SCATTER_ROWS = 256
RANK_CHUNK = 256


def _router_kernel(h_ref, whi_ref, wlo_ref, cw_ref, rank_ref, rank_t_ref, cnt_ref):
    tm = h_ref.shape[0]
    h = h_ref[...]
    hhi = h.astype(BF16)
    hlo = (h - hhi.astype(F32)).astype(BF16)
    whi = whi_ref[...]
    dot = lambda a, b: jnp.dot(a, b, preferred_element_type=F32)
    logits = dot(hhi, whi) + dot(hhi, wlo_ref[...]) + dot(hlo, whi)
    lane = lax.broadcasted_iota(jnp.int32, logits.shape, 1).astype(F32)
    lg = jnp.where(lane < N_EXPERTS, logits, NEG)
    m1 = jnp.max(lg, axis=1, keepdims=True)
    i1 = jnp.min(jnp.where(lg == m1, lane, float(LANES)), axis=1, keepdims=True)
    lg2 = jnp.where(lane == i1, NEG, lg)
    m2 = jnp.max(lg2, axis=1, keepdims=True)
    i2 = jnp.min(jnp.where(lg2 == m2, lane, float(LANES)), axis=1, keepdims=True)
    e = jnp.exp(m2 - m1)
    g1 = 1.0 / (1.0 + e)
    chosen = jnp.logical_or(lane == i1, lane == i2)
    cw_ref[...] = jnp.where(lane == i1, g1, 0.0) + jnp.where(lane == i2, e * g1, 0.0)

    a = chosen.astype(BF16)
    r = lax.broadcasted_iota(jnp.int32, (RANK_CHUNK, RANK_CHUNK), 0)
    c = lax.broadcasted_iota(jnp.int32, (RANK_CHUNK, RANK_CHUNK), 1)
    strict_lower = (c < r).astype(BF16)
    ones_row = jnp.ones((8, RANK_CHUNK), BF16)
    carry = jnp.zeros((1, LANES), F32)
    for q in range(tm // RANK_CHUNK):
        rows = slice(q * RANK_CHUNK, (q + 1) * RANK_CHUNK)
        rank = jnp.where(chosen[rows], dot(strict_lower, a[rows]) + carry, -1.0)
        rank_ref[rows, :] = rank
        rank_t_ref[0, :, rows] = jnp.transpose(rank)[0:N_EXPERTS, :]
        carry = carry + dot(ones_row, a[rows])[0:1, :]
    cnt_ref[0] = jnp.broadcast_to(carry, (8, LANES)).astype(jnp.int32)


def _router(h, whi, wlo, *, tm):
    S = h.shape[0]
    nb = S // tm
    return pl.pallas_call(
        _router_kernel,
        grid=(nb,),
        in_specs=[pl.BlockSpec((tm, D_MODEL), lambda i: (i, 0)),
                  pl.BlockSpec(whi.shape, lambda i: (0, 0)), pl.BlockSpec(wlo.shape, lambda i: (0, 0))],
        out_specs=[pl.BlockSpec((tm, LANES), lambda i: (i, 0)),
                   pl.BlockSpec((tm, LANES), lambda i: (i, 0)),
                   pl.BlockSpec((1, N_EXPERTS, tm), lambda i: (i, 0, 0)),
                   pl.BlockSpec((1, 8, LANES), lambda i: (i, 0, 0))],
        out_shape=[jax.ShapeDtypeStruct((S, LANES), F32),
                   jax.ShapeDtypeStruct((S, LANES), F32),
                   jax.ShapeDtypeStruct((nb, N_EXPERTS, tm), F32),
                   jax.ShapeDtypeStruct((nb, 8, LANES), jnp.int32)],
        compiler_params=_cparams(("arbitrary",)),
        name="router",
    )(h, whi, wlo)


def _moe_kernel(cnt_ref, h_ref, cw_ref, rank_ref, rank_t_ref, wg_ref, wu_ref, wd_ref, g_ref, b_ref, o_ref,
                hb_sc, xs_sc, acc_sc, *, alpha):
    tm = h_ref.shape[0]
    i = pl.program_id(0)
    e = pl.program_id(1)
    c = pl.program_id(2)
    last_c = c == pl.num_programs(2) - 1
    n_rows = cnt_ref[i * N_EXPERTS + e]
    n_steps = (n_rows + MOE_ROWS - 1) // MOE_ROWS
    n_scatter = (n_rows + SCATTER_ROWS - 1) // SCATTER_ROWS

    @pl.when(jnp.logical_and(e == 0, c == 0))
    def _():
        hb_sc[...] = h_ref[...].astype(BF16)
        o_ref[...] = jnp.zeros_like(o_ref)

    @pl.when(c == 0)
    def _():
        rank_row = rank_t_ref[0, pl.ds(e, 1), :]

        def gather(t, carry):
            r0 = pl.multiple_of(t * MOE_ROWS, 16)
            slot = (lax.broadcasted_iota(jnp.int32, (MOE_ROWS, tm), 0) + r0).astype(F32)
            onehot = (rank_row == slot).astype(BF16)
            xs_sc[pl.ds(r0, MOE_ROWS), :] = jnp.dot(onehot, hb_sc[...], preferred_element_type=F32).astype(BF16)
            acc_sc[pl.ds(r0, MOE_ROWS), :] = jnp.zeros((MOE_ROWS, D_MODEL), F32)
            return carry

        lax.fori_loop(0, n_steps, gather, 0)

        @pl.when(n_steps * MOE_ROWS < n_scatter * SCATTER_ROWS)
        def _():
            acc_sc[pl.ds(pl.multiple_of(n_steps * MOE_ROWS, 16), SCATTER_ROWS), :] = jnp.zeros(
                (SCATTER_ROWS, D_MODEL), F32)

    def swiglu(t, carry):
        r0 = pl.multiple_of(t * MOE_ROWS, 16)
        x = xs_sc[pl.ds(r0, MOE_ROWS), :]
        gate = jnp.dot(x, wg_ref[0], preferred_element_type=F32)
        up = jnp.dot(x, wu_ref[0], preferred_element_type=F32)
        a = (gate * _sigmoid(gate) * up).astype(BF16)
        acc_sc[pl.ds(r0, MOE_ROWS), :] += jnp.dot(a, wd_ref[0], preferred_element_type=F32)
        return carry

    lax.fori_loop(0, n_steps, swiglu, 0)

    @pl.when(last_c)
    def _():
        lane = lax.broadcasted_iota(jnp.int32, (tm, LANES), 1)
        pick = lambda ref: jnp.broadcast_to(
            jnp.sum(jnp.where(lane == e, ref[...], 0.0), axis=1, keepdims=True), (tm, LANES))
        rank_col = pick(rank_ref)
        gate_col = jnp.concatenate([pick(cw_ref)] * (D_MODEL // LANES), axis=1)

        def scatter(t, carry):
            r0 = pl.multiple_of(t * SCATTER_ROWS, SCATTER_ROWS)
            slot = (lane + r0).astype(F32)
            onehot = jnp.concatenate([(rank_col == slot + float(a * LANES)) for a in range(SCATTER_ROWS // LANES)],
                                     axis=1).astype(BF16)
            y = acc_sc[pl.ds(r0, SCATTER_ROWS), :].astype(BF16)
            o_ref[...] += gate_col * jnp.dot(onehot, y, preferred_element_type=F32)
            return carry

        lax.fori_loop(0, n_scatter, scatter, 0)

    @pl.when(jnp.logical_and(e == pl.num_programs(1) - 1, last_c))
    def _():
        o_ref[...] = _layer_norm(alpha * h_ref[...] + o_ref[...], g_ref[...], b_ref[...])


def _moe(h, cw, rank, rank_t, counts, w_gu, w_dn, g, b, *, alpha, tm, fc):
    S = h.shape[0]
    E, _, F2 = w_gu.shape
    nc = F2 // 2 // fc
    once = pl.Buffered(1)
    rows_cap = pl.cdiv(tm, MOE_ROWS) * MOE_ROWS
    grid_spec = pltpu.PrefetchScalarGridSpec(
        num_scalar_prefetch=1,
        grid=(S // tm, E, nc),
        in_specs=[pl.BlockSpec((tm, D_MODEL), lambda i, e, c, cnt: (i, 0), pipeline_mode=once),
                  pl.BlockSpec((tm, LANES), lambda i, e, c, cnt: (i, 0), pipeline_mode=once),
                  pl.BlockSpec((tm, LANES), lambda i, e, c, cnt: (i, 0), pipeline_mode=once),
                  pl.BlockSpec((1, N_EXPERTS, tm), lambda i, e, c, cnt: (i, 0, 0), pipeline_mode=once),
                  pl.BlockSpec((1, D_MODEL, fc), lambda i, e, c, cnt: (e, 0, c)),
                  pl.BlockSpec((1, D_MODEL, fc), lambda i, e, c, cnt: (e, 0, nc + c)),
                  pl.BlockSpec((1, fc, D_MODEL), lambda i, e, c, cnt: (e, c, 0)),
                  pl.BlockSpec(g.shape, lambda i, e, c, cnt: (0, 0)),
                  pl.BlockSpec(b.shape, lambda i, e, c, cnt: (0, 0))],
        out_specs=pl.BlockSpec((tm, D_MODEL), lambda i, e, c, cnt: (i, 0), pipeline_mode=once),
        scratch_shapes=[pltpu.VMEM((tm, D_MODEL), BF16), pltpu.VMEM((rows_cap, D_MODEL), BF16),
                        pltpu.VMEM((rows_cap, D_MODEL), F32)],
    )
    return pl.pallas_call(
        functools.partial(_moe_kernel, alpha=alpha),
        grid_spec=grid_spec,
        out_shape=jax.ShapeDtypeStruct((S, D_MODEL), F32),
        compiler_params=_cparams(("arbitrary", "arbitrary", "arbitrary")),
        name="moe_ffn",
    )(counts, h, cw, rank, rank_t, w_gu, w_gu, w_dn, g, b)


def _ffn_kernel(h_ref, wg_ref, wu_ref, wd_ref, g_ref, b_ref, o_ref, hb_sc, acc_sc, *, alpha):
    c = pl.program_id(1)

    @pl.when(c == 0)
    def _():
        hb_sc[...] = h_ref[...].astype(BF16)
        acc_sc[...] = jnp.zeros_like(acc_sc)

    hb = hb_sc[...]
    gate = jnp.dot(hb, wg_ref[...], preferred_element_type=F32)
    up = jnp.dot(hb, wu_ref[...], preferred_element_type=F32)
    a = gate * _sigmoid(gate) * up
    acc_sc[...] += jnp.dot(a.astype(BF16), wd_ref[...], preferred_element_type=F32)

    @pl.when(c == pl.num_programs(1) - 1)
    def _():
        o_ref[...] = _layer_norm(alpha * h_ref[...] + acc_sc[...], g_ref[...], b_ref[...])


def _ffn(h, w_gu, w_dn, g, b, *, alpha, tm, fc):
    S = h.shape[0]
    nc = w_gu.shape[1] // 2 // fc
    return pl.pallas_call(
        functools.partial(_ffn_kernel, alpha=alpha),
        grid=(S // tm, nc),
        in_specs=[pl.BlockSpec((tm, D_MODEL), lambda i, c: (i, 0)),
                  pl.BlockSpec((D_MODEL, fc), lambda i, c: (0, c)),
                  pl.BlockSpec((D_MODEL, fc), lambda i, c: (0, nc + c)),
                  pl.BlockSpec((fc, D_MODEL), lambda i, c: (c, 0)),
                  pl.BlockSpec(g.shape, lambda i, c: (0, 0)), pl.BlockSpec(b.shape, lambda i, c: (0, 0))],
        out_specs=pl.BlockSpec((tm, D_MODEL), lambda i, c: (i, 0)),
        out_shape=jax.ShapeDtypeStruct((S, D_MODEL), F32),
        scratch_shapes=[pltpu.VMEM((tm, D_MODEL), BF16), pltpu.VMEM((tm, D_MODEL), F32)],
        compiler_params=_cparams(("arbitrary", "arbitrary")),
        name="dense_ffn",
    )(h, w_gu, w_gu, w_dn, g, b)


def _pad_lanes(a, n=LANES):
    return jnp.pad(a, [(0, 0)] * (a.ndim - 1) + [(0, n - a.shape[-1])])


def kernel(x, ln_in_g, ln_in_b, w_in, b_fgate, lam_q1, lam_k1, lam_q2, lam_k2, subln_g, w_branch_fox, w_branch_diff, w_out, ln_mix_g, ln_mix_b, rel_bias, w_ffn_gate_up, w_ffn_down, w_router, w_expert_gate_up, w_expert_down, ln_ffn_g, ln_ffn_b):
    B, S, _ = x.shape
    assert B == 1
    depth = w_in.shape[0]
    alpha = (2.0 * depth) ** 0.25
    tq = min(ATTN_TQ, S)
    tk = min(ATTN_TK, tq)
    tm = min(512, S)
    tf = min(1024, S)
    te = min(MOE_TM, S)
    assert tk >= REL_MAX_DISTANCE and tq % tk == 0 and S % tq == 0 and S % tf == 0 and S % tm == 0 and S % te == 0
    assert te % SCATTER_ROWS == 0 and te % RANK_CHUNK == 0 and MOE_ROWS % 16 == 0
    row = lambda v: v.reshape(1, -1).astype(F32)

    table = rel_bias.astype(F32)
    bias = _bias_tiles(table.T.reshape(-1), t=tk)

    h = x[0]
    for l in range(depth):
        lam_init = 0.8 - 0.6 * math.exp(-0.3 * l)
        fq, fk, fv, fg, dq, dk, dv, ga, gb = jnp.split(w_in[l], list(_cumsum(IN_SPLIT_SIZES))[:-1], axis=-1)
        w_cat = jnp.concatenate([fq, fk, fv, dq, dk, dv, ga, gb, _pad_lanes(fg)], axis=-1).astype(BF16)
        outs = _inproj(h, row(ln_in_g), row(ln_in_b), w_cat, _pad_lanes(row(b_fgate[l])), do_ln=(l == 0), tm=tk)
        qa, ka, va, stats, dq, dk, dv, sa, sb = outs[:9]
        if l == 0:
            h = outs[9]
        start = _plan(stats, nsub=tq // tk)[:S // tq, :FOX_HEADS].reshape(-1)
        y_fox = _fox_attention(start, qa, ka, va, tq=tq, tk=tk)
        lamvec = jnp.zeros((8, LANES), F32).at[0:4, 0:HEAD_DIM].set(
            jnp.stack([lam_q1[l], lam_k1[l], lam_q2[l], lam_k2[l]]).astype(F32))
        y_diff = _diff_attention(dq, dk, dv, bias, lamvec, row(subln_g[l]), tq=tq, tk=tk,
                                 lam_init=lam_init)
        h = _merge(y_fox, y_diff, sa, sb, h, w_branch_fox[l].astype(BF16), w_branch_diff[l].astype(BF16),
                   w_out[l].astype(BF16), row(ln_mix_g[l]), row(ln_mix_b[l]), alpha=alpha, tm=tm)
        g, b = row(ln_ffn_g[l]), row(ln_ffn_b[l])
        if l % 2 == 0:
            h = _ffn(h, w_ffn_gate_up[l // 2].astype(BF16), w_ffn_down[l // 2].astype(BF16), g, b,
                     alpha=alpha, tm=tf, fc=1408)
        else:
            wr = _pad_lanes(w_router[l // 2].astype(F32))
            whi = wr.astype(BF16)
            wlo = (wr - whi.astype(F32)).astype(BF16)
            cw, rank, rank_t, counts = _router(h, whi, wlo, tm=te)
            h = _moe(h, cw, rank, rank_t, counts[:, 0, :N_EXPERTS].reshape(-1),
                     w_expert_gate_up[l // 2].astype(BF16), w_expert_down[l // 2].astype(BF16),
                     g, b, alpha=alpha, tm=te, fc=896)
    return h[None]


def _cumsum(sizes):
    tot = 0
    for s in sizes:
        tot += s
        yield tot
```

```python
import functools
import math

import jax
import jax.numpy as jnp
from jax import lax
from jax.experimental import pallas as pl
from jax.experimental.pallas import tpu as pltpu

F32 = jnp.float32
BF16 = jnp.bfloat16

D_MODEL = 1024
HEAD_DIM = 64
FOX_HEADS = 8
FOX_WIDTH = FOX_HEADS * HEAD_DIM
DIFF_HEADS = 4
DIFF_QK_WIDTH = DIFF_HEADS * 2 * HEAD_DIM
DIFF_V_DIM = 2 * HEAD_DIM
DIFF_V_WIDTH = DIFF_HEADS * DIFF_V_DIM
IN_SPLIT_SIZES = (FOX_WIDTH, FOX_WIDTH, FOX_WIDTH, FOX_HEADS, DIFF_QK_WIDTH, DIFF_QK_WIDTH, DIFF_V_WIDTH,
                  D_MODEL, D_MODEL)
REL_BUCKETS = 32
REL_MAX_DISTANCE = 128
N_EXPERTS = 8
LN_EPS = 1e-5
SUBLN_EPS = 1e-5

LANES = 128
LOG2E = 1.4426950408889634
Q_SCALE = HEAD_DIM ** -0.5 * LOG2E
NEG = -1e30
VMEM_LIMIT = 56 * 1024 * 1024
ATTN_TQ = 2048
ATTN_TK = 512
MOE_TM = 2048


def _cparams(sem):
    return pltpu.CompilerParams(dimension_semantics=sem, vmem_limit_bytes=VMEM_LIMIT)


def _layer_norm(x, g, b):
    mu = jnp.mean(x, axis=-1, keepdims=True)
    xc = x - mu
    var = jnp.mean(xc * xc, axis=-1, keepdims=True)
    return xc * lax.rsqrt(var + LN_EPS) * g + b


def _sigmoid(x):
    return 1.0 / (1.0 + jnp.exp(-x))


def _split3(x):
    hi = x.astype(BF16)
    r1 = x - hi.astype(F32)
    mid = r1.astype(BF16)
    lo = (r1 - mid.astype(F32)).astype(BF16)
    return hi, mid, lo


_C_FQ, _C_FK, _C_FV, _C_DQ, _C_DK, _C_DV, _C_GA, _C_GB, _C_FG, _C_END = (
    0, 512, 1024, 1536, 2048, 2560, 3072, 4096, 5120, 5248)


def _inproj_kernel(x_ref, g_ref, b_ref, w_ref, bf_ref, qa_ref, ka_ref, va_ref, st_ref, dq, dk, dv, sa, sb,
                   *rest, do_ln):
    carry_ref = rest[-1]
    tb = x_ref.shape[0]

    @pl.when(pl.program_id(0) == 0)
    def _():
        carry_ref[...] = jnp.zeros_like(carry_ref)

    x = x_ref[...]
    if do_ln:
        x = _layer_norm(x, g_ref[...], b_ref[...])
        rest[0][...] = x
    xb = x.astype(BF16)

    def mm(a, b):
        return jnp.dot(xb, w_ref[:, a:b], preferred_element_type=F32)

    fq = (mm(_C_FQ, _C_FK) * Q_SCALE).astype(BF16)
    fk = mm(_C_FK, _C_FV).astype(BF16)
    fv = mm(_C_FV, _C_DQ).astype(BF16)

    z = mm(_C_FG, _C_END) + bf_ref[...]
    logf = jnp.minimum(z, 0.0) - jnp.log1p(jnp.exp(-jnp.abs(z)))
    r = lax.broadcasted_iota(jnp.int32, (tb, tb), 0)
    c = lax.broadcasted_iota(jnp.int32, (tb, tb), 1)
    tri = (c <= r).astype(BF16)
    hi, mid, lo = _split3(logf)
    dot = lambda a: jnp.dot(tri, a, preferred_element_type=F32)
    cum = (dot(hi) + dot(mid)) + dot(lo) + carry_ref[0:1, :]
    carry_ref[0:1, :] = cum[tb - 1:tb, :]

    c_hi, c_mid, c_lo = (t.astype(F32) for t in _split3(cum * LOG2E))
    lane = lax.broadcasted_iota(jnp.int32, (tb, LANES), 1)
    one = jnp.ones((tb, LANES), F32)
    zero = jnp.zeros((tb, LANES), F32)
    lane1 = lax.broadcasted_iota(jnp.int32, (1, LANES), 1)
    q_norm2 = jnp.zeros((1, LANES), F32)
    k_norm2 = jnp.zeros((1, LANES), F32)

    def max_norm2(slab):
        n2 = jnp.sum(jnp.where(lane < 64, slab * slab, 0.0), axis=1, keepdims=True)
        return jnp.max(n2, axis=0, keepdims=True)

    for h in range(FOX_HEADS):
        p, odd = divmod(h, 2)
        col = lambda t: jnp.broadcast_to(t[:, h:h + 1], (tb, LANES))
        chi, cmid, clo = col(c_hi), col(c_mid), col(c_lo)

        def head_slab(a):
            slab = a[:, p * LANES:(p + 1) * LANES].astype(F32)
            return pltpu.roll(slab, HEAD_DIM, 1) if odd else slab

        qslab = head_slab(fq)
        kslab = head_slab(fk)
        q_norm2 = jnp.where(lane1 == h, max_norm2(qslab), q_norm2)
        k_norm2 = jnp.where(lane1 == h, max_norm2(kslab), k_norm2)
        qa = jnp.where(lane < 64, qslab,
                       jnp.where(lane == 64, chi, jnp.where(lane == 65, cmid, jnp.where(lane == 66, clo,
                                 jnp.where(lane < 70, one, zero)))))
        ka = jnp.where(lane < 64, kslab,
                       jnp.where(lane < 67, one, jnp.where(lane == 67, -chi, jnp.where(lane == 68, -cmid,
                                 jnp.where(lane == 69, -clo, zero)))))
        qa_ref[h] = qa.astype(BF16)
        ka_ref[h] = ka.astype(BF16)
        vslab = fv[:, p * LANES:(p + 1) * LANES].astype(F32)
        if odd:
            va = jnp.where(lane >= 64, vslab, jnp.where(lane == 0, one, zero))
        else:
            va = jnp.where(lane < 64, vslab, jnp.where(lane == 64, one, zero))
        va_ref[h] = va.astype(BF16)

    dq[...] = (mm(_C_DQ, _C_DK) * Q_SCALE).astype(BF16)
    dk[...] = mm(_C_DK, _C_DV).astype(BF16)
    dv[...] = mm(_C_DV, _C_GA).astype(BF16)
    sa[...] = _sigmoid(mm(_C_GA, _C_GB)).astype(BF16)
    sb[...] = _sigmoid(mm(_C_GB, _C_FG)).astype(BF16)

    st_ref[0] = jnp.concatenate([q_norm2, k_norm2, cum[tb - 1:tb, :] * LOG2E, jnp.zeros((5, LANES), F32)], axis=0)


def _inproj(x, g, b, w, bf, *, do_ln, tm):
    S = x.shape[0]
    row = lambda n: pl.BlockSpec((tm, n), lambda i: (i, 0))
    full = lambda a: pl.BlockSpec(a.shape, lambda i: (0,) * a.ndim)
    hspec = pl.BlockSpec((FOX_HEADS, tm, LANES), lambda i: (0, i, 0))
    hshape = jax.ShapeDtypeStruct((FOX_HEADS, S, LANES), BF16)
    out_shape = [hshape] * 3 + [jax.ShapeDtypeStruct((S // tm, 8, LANES), F32)] + [
        jax.ShapeDtypeStruct((S, 512), BF16)] * 3 + [jax.ShapeDtypeStruct((S, D_MODEL), BF16)] * 2
    out_specs = [hspec] * 3 + [pl.BlockSpec((1, 8, LANES), lambda i: (i, 0, 0))] + [row(512)] * 3 + [row(D_MODEL)] * 2
    if do_ln:
        out_shape.append(jax.ShapeDtypeStruct((S, D_MODEL), F32))
        out_specs.append(row(D_MODEL))
    return pl.pallas_call(
        functools.partial(_inproj_kernel, do_ln=do_ln),
        grid=(S // tm,),
        in_specs=[row(D_MODEL), full(g), full(b), full(w), full(bf)],
        out_specs=out_specs,
        out_shape=out_shape,
        scratch_shapes=[pltpu.VMEM((8, LANES), F32)],
        compiler_params=_cparams(("arbitrary",)),
        name="inproj",
    )(x, g, b, w, bf)


SKIP_LOG2 = 150.0


def _plan_kernel(st_ref, start_ref, *, nt, nsub):
    qn = jnp.sqrt(st_ref[pl.ds(0, nt, stride=8), :])
    kn = jnp.sqrt(st_ref[pl.ds(1, nt, stride=8), :])
    cum_end = st_ref[pl.ds(2, nt, stride=8), :]
    tile = lax.broadcasted_iota(jnp.int32, (nt, LANES), 0)
    rows = []
    for i in range(nt // nsub):
        own = jnp.logical_and(tile >= i * nsub, tile < (i + 1) * nsub)
        q_max = jnp.max(jnp.where(own, qn, 0.0), axis=0, keepdims=True)
        k_own = jnp.max(jnp.where(own, kn, 0.0), axis=0, keepdims=True)
        cum_q = cum_end[i * nsub - 1:i * nsub, :] if i > 0 else jnp.zeros((1, LANES), F32)
        upper = q_max * kn + (cum_q - cum_end) + q_max * k_own
        slack = SKIP_LOG2 + 1e-4 * jnp.abs(cum_end)
        needed = jnp.logical_and(upper >= -slack, tile < i * nsub)
        rows.append(jnp.min(jnp.where(needed, tile, i * nsub), axis=0, keepdims=True))
    rows += [jnp.zeros((1, LANES), jnp.int32)] * (8 - len(rows))
    start_ref[...] = jnp.concatenate(rows, axis=0)


def _plan(stats, *, nsub):
    nt = stats.shape[0]
    assert nt // nsub <= 8
    return pl.pallas_call(
        functools.partial(_plan_kernel, nt=nt, nsub=nsub),
        out_shape=jax.ShapeDtypeStruct((8, LANES), jnp.int32),
        compiler_params=pltpu.CompilerParams(vmem_limit_bytes=VMEM_LIMIT),
        name="fox_plan",
    )(stats.reshape(nt * 8, LANES))


def _nt_dot(a, b):
    return lax.dot_general(a, b, (((1,), (1,)), ((), ())), preferred_element_type=F32)


def _fox_kernel(start_ref, q_ref, k_ref, v_ref, o_ref, m_sc, acc_sc, *, tq, tk):
    i = pl.program_id(1)
    pair = pl.program_id(0)
    m_sc[...] = jnp.full_like(m_sc, NEG)
    acc_sc[...] = jnp.zeros_like(acc_sc)
    nsub = tq // tk

    def tile(j, r0, masked):
        off = pl.multiple_of(j * tk, tk)
        rows = tq - r0
        for hh in range(2):
            s = _nt_dot(q_ref[hh, r0:, :], k_ref[hh, pl.ds(off, tk), :])
            if masked:
                r = lax.broadcasted_iota(jnp.int32, (rows, tk), 0)
                c = lax.broadcasted_iota(jnp.int32, (rows, tk), 1)
                s = jnp.where(c <= r, s, NEG)
            m_prev = m_sc[hh, r0:, :]
            m_next = jnp.maximum(m_prev, jnp.max(s, axis=1, keepdims=True))
            alpha = jnp.exp2(m_prev - m_next)
            p = jnp.exp2(s - jnp.concatenate([m_next] * (tk // LANES), axis=1)).astype(BF16)
            pv = jnp.dot(p, v_ref[hh, pl.ds(off, tk), :], preferred_element_type=F32)
            acc_sc[hh, r0:, :] = alpha * acc_sc[hh, r0:, :] + pv
            m_sc[hh, r0:, :] = m_next

    def body(j, carry):
        tile(j, 0, False)
        return carry

    first = jnp.minimum(start_ref[i * FOX_HEADS + 2 * pair], start_ref[i * FOX_HEADS + 2 * pair + 1])
    lax.fori_loop(first, i * nsub, body, 0)
    for u in range(nsub):
        tile(i * nsub + u, u * tk, True)

    a0 = acc_sc[0]
    a1 = acc_sc[1]
    lane = lax.broadcasted_iota(jnp.int32, (tq, LANES), 1)
    o_ref[...] = jnp.where(lane < 64, a0 / a0[:, 64:65], a1 / a1[:, 0:1]).astype(o_ref.dtype)


def _fox_attention(start, qa, ka, va, *, tq, tk):
    S = qa.shape[1]
    grid_spec = pltpu.PrefetchScalarGridSpec(
        num_scalar_prefetch=1,
        grid=(FOX_HEADS // 2, S // tq),
        in_specs=[pl.BlockSpec((2, tq, LANES), lambda p, i, st: (p, i, 0)),
                  pl.BlockSpec((2, S, LANES), lambda p, i, st: (p, 0, 0)),
                  pl.BlockSpec((2, S, LANES), lambda p, i, st: (p, 0, 0))],
        out_specs=pl.BlockSpec((tq, LANES), lambda p, i, st: (i, p)),
        scratch_shapes=[pltpu.VMEM((2, tq, LANES), F32), pltpu.VMEM((2, tq, LANES), F32)],
    )
    return pl.pallas_call(
        functools.partial(_fox_kernel, tq=tq, tk=tk),
        grid_spec=grid_spec,
        out_shape=jax.ShapeDtypeStruct((S, FOX_WIDTH), BF16),
        compiler_params=_cparams(("arbitrary", "arbitrary")),
        name="fox_attn",
    )(start, qa, ka, va)


BIAS_KINDS = 2


def _bias_tile_kernel(table_ref, o_ref, *, t):
    h = pl.program_id(0)
    kind = pl.program_id(1)
    r = lax.broadcasted_iota(jnp.int32, (t, t), 0)
    c = lax.broadcasted_iota(jnp.int32, (t, t), 1)
    dist = r - c + kind * t
    n = jnp.maximum(dist, 0)
    max_exact = REL_BUCKETS // 2
    nf = jnp.maximum(n, 1).astype(F32)
    log_part = jnp.log(nf / max_exact) / math.log(REL_MAX_DISTANCE / max_exact) * (REL_BUCKETS - max_exact)
    large = jnp.minimum(max_exact + log_part.astype(jnp.int32), REL_BUCKETS - 1)
    bucket = jnp.where(n < max_exact, n, large)
    val = jnp.zeros((t, t), F32)
    for b in range(REL_BUCKETS):
        val = jnp.where(bucket == b, table_ref[h * REL_BUCKETS + b], val)
    far = table_ref[h * REL_BUCKETS + REL_BUCKETS - 1]
    o_ref[0, 0] = jnp.where(dist >= 0, (val - far) * LOG2E, NEG)


def _bias_tiles(table_flat, *, t):
    return pl.pallas_call(
        functools.partial(_bias_tile_kernel, t=t),
        grid=(DIFF_HEADS, BIAS_KINDS),
        in_specs=[pl.BlockSpec(memory_space=pltpu.SMEM)],
        out_specs=pl.BlockSpec((1, 1, t, t), lambda h, k: (h, k, 0, 0)),
        out_shape=jax.ShapeDtypeStruct((DIFF_HEADS, BIAS_KINDS, t, t), F32),
        compiler_params=_cparams(("arbitrary", "arbitrary")),
        name="bias_tiles",
    )(table_flat)


def _diff_kernel(q_ref, k_ref, v_ref, bias_ref, lam_ref, g_ref, o_ref, m_sc, l_sc, acc_sc,
                 *, tq, tk, lam_init):
    i = pl.program_id(1)
    m_sc[...] = jnp.full_like(m_sc, NEG)
    l_sc[...] = jnp.zeros_like(l_sc)
    acc_sc[...] = jnp.zeros_like(acc_sc)
    nsub = tq // tk

    lane = lax.broadcasted_iota(jnp.int32, (tq, LANES), 1)
    q = q_ref[...]
    zero = jnp.zeros_like(q)
    qs = (jnp.where(lane < 64, q, zero), jnp.where(lane >= 64, q, zero))

    def tile(j, r0, kinds):
        off = pl.multiple_of(j * tk, tk)
        k = k_ref[pl.ds(off, tk), :]
        v = v_ref[pl.ds(off, tk), :]
        for c in range(2):
            s = _nt_dot(qs[c][r0:], k)
            m_prev = m_sc[c, r0:, :]
            if kinds is not None:
                s = jnp.concatenate([s[a * tk:(a + 1) * tk] + bias_ref[0, kd] if kd < 2 else s[a * tk:(a + 1) * tk]
                                     for a, kd in enumerate(kinds)], axis=0)
            m_next = jnp.maximum(m_prev, jnp.max(s, axis=1, keepdims=True))
            p = jnp.exp2(s - jnp.concatenate([m_next] * (tk // LANES), axis=1))
            alpha = jnp.exp2(m_prev - m_next)
            l_sc[c, r0:, :] = alpha * l_sc[c, r0:, :] + sum(p[:, a * LANES:(a + 1) * LANES]
                                                            for a in range(tk // LANES))
            pv = jnp.dot(p.astype(BF16), v, preferred_element_type=F32)
            acc_sc[c, r0:, :] = alpha * acc_sc[c, r0:, :] + pv
            m_sc[c, r0:, :] = m_next

    def body(j, carry):
        tile(j, 0, None)
        return carry

    lax.fori_loop(0, i * nsub - 1, body, 0)

    @pl.when(i > 0)
    def _():
        tile(i * nsub - 1, 0, (1,) + (2,) * (nsub - 1))

    for u in range(nsub):
        tile(i * nsub + u, u * tk, ((0, 1) + (2,) * nsub)[:nsub - u])

    lv = lam_ref[...]
    lam = (jnp.exp(jnp.sum(lv[0:1] * lv[1:2], axis=1, keepdims=True))
           - jnp.exp(jnp.sum(lv[2:3] * lv[3:4], axis=1, keepdims=True)) + lam_init)
    l0 = jnp.sum(l_sc[0], axis=1, keepdims=True)
    l1 = jnp.sum(l_sc[1], axis=1, keepdims=True)
    o = acc_sc[0] / l0 - lam * (acc_sc[1] / l1)
    o = o * lax.rsqrt(jnp.mean(o * o, axis=1, keepdims=True) + SUBLN_EPS) * g_ref[...] * (1.0 - lam_init)
    o_ref[...] = o.astype(o_ref.dtype)


def _diff_attention(dq, dk, dv, bias, lamvec, g, *, tq, tk, lam_init):
    S = dq.shape[0]
    return pl.pallas_call(
        functools.partial(_diff_kernel, tq=tq, tk=tk, lam_init=lam_init),
        grid=(DIFF_HEADS, S // tq),
        in_specs=[pl.BlockSpec((tq, LANES), lambda h, i: (i, h)),
                  pl.BlockSpec((S, LANES), lambda h, i: (0, h)),
                  pl.BlockSpec((S, LANES), lambda h, i: (0, h)),
                  pl.BlockSpec((1, BIAS_KINDS, tk, tk), lambda h, i: (h, 0, 0, 0)),
                  pl.BlockSpec((8, LANES), lambda h, i: (0, 0)),
                  pl.BlockSpec((1, LANES), lambda h, i: (0, 0))],
        out_specs=pl.BlockSpec((tq, LANES), lambda h, i: (i, h)),
        out_shape=jax.ShapeDtypeStruct((S, DIFF_V_WIDTH), BF16),
        scratch_shapes=[pltpu.VMEM((2, tq, LANES), F32), pltpu.VMEM((2, tq, LANES), F32),
                        pltpu.VMEM((2, tq, LANES), F32)],
        compiler_params=_cparams(("arbitrary", "arbitrary")),
        name="diff_attn",
    )(dq, dk, dv, bias, lamvec, g)


def _merge_kernel(yf_ref, yd_ref, sa_ref, sb_ref, h_ref, wf_ref, wd_ref, wo_ref, g_ref, b_ref, o_ref, *, alpha):
    bf = jnp.dot(yf_ref[...], wf_ref[...], preferred_element_type=F32)
    bd = jnp.dot(yd_ref[...], wd_ref[...], preferred_element_type=F32)
    merged = sa_ref[...].astype(F32) * bf + sb_ref[...].astype(F32) * bd
    mix = jnp.dot(merged.astype(BF16), wo_ref[...], preferred_element_type=F32)
    o_ref[...] = _layer_norm(alpha * h_ref[...] + mix, g_ref[...], b_ref[...])


def _merge(yf, yd, sa, sb, h, wf, wd, wo, g, b, *, alpha, tm):
    S = h.shape[0]
    row = lambda n: pl.BlockSpec((tm, n), lambda i: (i, 0))
    full = lambda a: pl.BlockSpec(a.shape, lambda i: (0,) * a.ndim)
    return pl.pallas_call(
        functools.partial(_merge_kernel, alpha=alpha),
        grid=(S // tm,),
        in_specs=[row(512), row(512), row(D_MODEL), row(D_MODEL), row(D_MODEL),
                  full(wf), full(wd), full(wo), full(g), full(b)],
        out_specs=row(D_MODEL),
        out_shape=jax.ShapeDtypeStruct((S, D_MODEL), F32),
        compiler_params=_cparams(("arbitrary",)),
        name="merge_out",
    )(yf, yd, sa, sb, h, wf, wd, wo, g, b)


MOE_ROWS = 272
SCATTER_ROWS = 256
RANK_CHUNK = 256


def _router_kernel(h_ref, whi_ref, wlo_ref, cw_ref, rank_ref, rank_t_ref, cnt_ref):
    tm = h_ref.shape[0]
    h = h_ref[...]
    hhi = h.astype(BF16)
    hlo = (h - hhi.astype(F32)).astype(BF16)
    whi = whi_ref[...]
    dot = lambda a, b: jnp.dot(a, b, preferred_element_type=F32)
    logits = dot(hhi, whi) + dot(hhi, wlo_ref[...]) + dot(hlo, whi)
    lane = lax.broadcasted_iota(jnp.int32, logits.shape, 1).astype(F32)
    lg = jnp.where(lane < N_EXPERTS, logits, NEG)
    m1 = jnp.max(lg, axis=1, keepdims=True)
    i1 = jnp.min(jnp.where(lg == m1, lane, float(LANES)), axis=1, keepdims=True)
    lg2 = jnp.where(lane == i1, NEG, lg)
    m2 = jnp.max(lg2, axis=1, keepdims=True)
    i2 = jnp.min(jnp.where(lg2 == m2, lane, float(LANES)), axis=1, keepdims=True)
    e = jnp.exp(m2 - m1)
    g1 = 1.0 / (1.0 + e)
    chosen = jnp.logical_or(lane == i1, lane == i2)
    cw_ref[...] = jnp.where(lane == i1, g1, 0.0) + jnp.where(lane == i2, e * g1, 0.0)

    a = chosen.astype(BF16)
    r = lax.broadcasted_iota(jnp.int32, (RANK_CHUNK, RANK_CHUNK), 0)
    c = lax.broadcasted_iota(jnp.int32, (RANK_CHUNK, RANK_CHUNK), 1)
    strict_lower = (c < r).astype(BF16)
    ones_row = jnp.ones((8, RANK_CHUNK), BF16)
    carry = jnp.zeros((1, LANES), F32)
    for q in range(tm // RANK_CHUNK):
        rows = slice(q * RANK_CHUNK, (q + 1) * RANK_CHUNK)
        rank = jnp.where(chosen[rows], dot(strict_lower, a[rows]) + carry, -1.0)
        rank_ref[rows, :] = rank
        rank_t_ref[0, :, rows] = jnp.transpose(rank)[0:N_EXPERTS, :]
        carry = carry + dot(ones_row, a[rows])[0:1, :]
    cnt_ref[0] = jnp.broadcast_to(carry, (8, LANES)).astype(jnp.int32)


def _router(h, whi, wlo, *, tm):
    S = h.shape[0]
    nb = S // tm
    return pl.pallas_call(
        _router_kernel,
        grid=(nb,),
        in_specs=[pl.BlockSpec((tm, D_MODEL), lambda i: (i, 0)),
                  pl.BlockSpec(whi.shape, lambda i: (0, 0)), pl.BlockSpec(wlo.shape, lambda i: (0, 0))],
        out_specs=[pl.BlockSpec((tm, LANES), lambda i: (i, 0)),
                   pl.BlockSpec((tm, LANES), lambda i: (i, 0)),
                   pl.BlockSpec((1, N_EXPERTS, tm), lambda i: (i, 0, 0)),
                   pl.BlockSpec((1, 8, LANES), lambda i: (i, 0, 0))],
        out_shape=[jax.ShapeDtypeStruct((S, LANES), F32),
                   jax.ShapeDtypeStruct((S, LANES), F32),
                   jax.ShapeDtypeStruct((nb, N_EXPERTS, tm), F32),
                   jax.ShapeDtypeStruct((nb, 8, LANES), jnp.int32)],
        compiler_params=_cparams(("arbitrary",)),
        name="router",
    )(h, whi, wlo)


def _moe_kernel(cnt_ref, h_ref, cw_ref, rank_ref, rank_t_ref, wg_ref, wu_ref, wd_ref, g_ref, b_ref, o_ref,
                hb_sc, xs_sc, acc_sc, *, alpha):
    tm = h_ref.shape[0]
    i = pl.program_id(0)
    e = pl.program_id(1)
    c = pl.program_id(2)
    last_c = c == pl.num_programs(2) - 1
    n_rows = cnt_ref[i * N_EXPERTS + e]
    n_steps = (n_rows + MOE_ROWS - 1) // MOE_ROWS
    n_scatter = (n_rows + SCATTER_ROWS - 1) // SCATTER_ROWS

    @pl.when(jnp.logical_and(e == 0, c == 0))
    def _():
        hb_sc[...] = h_ref[...].astype(BF16)
        o_ref[...] = jnp.zeros_like(o_ref)

    @pl.when(c == 0)
    def _():
        rank_row = rank_t_ref[0, pl.ds(e, 1), :]

        def gather(t, carry):
            r0 = pl.multiple_of(t * MOE_ROWS, 16)
            slot = (lax.broadcasted_iota(jnp.int32, (MOE_ROWS, tm), 0) + r0).astype(F32)
            onehot = (rank_row == slot).astype(BF16)
            xs_sc[pl.ds(r0, MOE_ROWS), :] = jnp.dot(onehot, hb_sc[...], preferred_element_type=F32).astype(BF16)
            acc_sc[pl.ds(r0, MOE_ROWS), :] = jnp.zeros((MOE_ROWS, D_MODEL), F32)
            return carry

        lax.fori_loop(0, n_steps, gather, 0)

        @pl.when(n_steps * MOE_ROWS < n_scatter * SCATTER_ROWS)
        def _():
            acc_sc[pl.ds(pl.multiple_of(n_steps * MOE_ROWS, 16), SCATTER_ROWS), :] = jnp.zeros(
                (SCATTER_ROWS, D_MODEL), F32)

    def swiglu(t, carry):
        r0 = pl.multiple_of(t * MOE_ROWS, 16)
        x = xs_sc[pl.ds(r0, MOE_ROWS), :]
        gate = jnp.dot(x, wg_ref[0], preferred_element_type=F32)
        up = jnp.dot(x, wu_ref[0], preferred_element_type=F32)
        a = (gate * _sigmoid(gate) * up).astype(BF16)
        acc_sc[pl.ds(r0, MOE_ROWS), :] += jnp.dot(a, wd_ref[0], preferred_element_type=F32)
        return carry

    lax.fori_loop(0, n_steps, swiglu, 0)

    @pl.when(last_c)
    def _():
        lane = lax.broadcasted_iota(jnp.int32, (tm, LANES), 1)
        pick = lambda ref: jnp.broadcast_to(
            jnp.sum(jnp.where(lane == e, ref[...], 0.0), axis=1, keepdims=True), (tm, LANES))
        rank_col = pick(rank_ref)
        gate_col = jnp.concatenate([pick(cw_ref)] * (D_MODEL // LANES), axis=1)

        def scatter(t, carry):
            r0 = pl.multiple_of(t * SCATTER_ROWS, SCATTER_ROWS)
            slot = (lane + r0).astype(F32)
            onehot = jnp.concatenate([(rank_col == slot + float(a * LANES)) for a in range(SCATTER_ROWS // LANES)],
                                     axis=1).astype(BF16)
            y = acc_sc[pl.ds(r0, SCATTER_ROWS), :].astype(BF16)
            o_ref[...] += gate_col * jnp.dot(onehot, y, preferred_element_type=F32)
            return carry

        lax.fori_loop(0, n_scatter, scatter, 0)

    @pl.when(jnp.logical_and(e == pl.num_programs(1) - 1, last_c))
    def _():
        o_ref[...] = _layer_norm(alpha * h_ref[...] + o_ref[...], g_ref[...], b_ref[...])


def _moe(h, cw, rank, rank_t, counts, w_gu, w_dn, g, b, *, alpha, tm, fc):
    S = h.shape[0]
    E, _, F2 = w_gu.shape
    nc = F2 // 2 // fc
    once = pl.Buffered(1)
    rows_cap = pl.cdiv(tm, MOE_ROWS) * MOE_ROWS
    grid_spec = pltpu.PrefetchScalarGridSpec(
        num_scalar_prefetch=1,
        grid=(S // tm, E, nc),
        in_specs=[pl.BlockSpec((tm, D_MODEL), lambda i, e, c, cnt: (i, 0), pipeline_mode=once),
                  pl.BlockSpec((tm, LANES), lambda i, e, c, cnt: (i, 0), pipeline_mode=once),
                  pl.BlockSpec((tm, LANES), lambda i, e, c, cnt: (i, 0), pipeline_mode=once),
                  pl.BlockSpec((1, N_EXPERTS, tm), lambda i, e, c, cnt: (i, 0, 0), pipeline_mode=once),
                  pl.BlockSpec((1, D_MODEL, fc), lambda i, e, c, cnt: (e, 0, c)),
                  pl.BlockSpec((1, D_MODEL, fc), lambda i, e, c, cnt: (e, 0, nc + c)),
                  pl.BlockSpec((1, fc, D_MODEL), lambda i, e, c, cnt: (e, c, 0)),
                  pl.BlockSpec(g.shape, lambda i, e, c, cnt: (0, 0)),
                  pl.BlockSpec(b.shape, lambda i, e, c, cnt: (0, 0))],
        out_specs=pl.BlockSpec((tm, D_MODEL), lambda i, e, c, cnt: (i, 0), pipeline_mode=once),
        scratch_shapes=[pltpu.VMEM((tm, D_MODEL), BF16), pltpu.VMEM((rows_cap, D_MODEL), BF16),
                        pltpu.VMEM((rows_cap, D_MODEL), F32)],
    )
    return pl.pallas_call(
        functools.partial(_moe_kernel, alpha=alpha),
        grid_spec=grid_spec,
        out_shape=jax.ShapeDtypeStruct((S, D_MODEL), F32),
        compiler_params=_cparams(("arbitrary", "arbitrary", "arbitrary")),
        name="moe_ffn",
    )(counts, h, cw, rank, rank_t, w_gu, w_gu, w_dn, g, b)


def _ffn_kernel(h_ref, wgu_ref, wd_ref, g_ref, b_ref, o_ref, *, alpha, fc):
    h = h_ref[...]
    hb = h.astype(BF16)
    d_ff = wd_ref.shape[0]
    acc = None
    for c in range(d_ff // fc):
        gate = jnp.dot(hb, wgu_ref[:, c * fc:(c + 1) * fc], preferred_element_type=F32)
        up = jnp.dot(hb, wgu_ref[:, d_ff + c * fc:d_ff + (c + 1) * fc], preferred_element_type=F32)
        a = (gate * _sigmoid(gate) * up).astype(BF16)
        y = jnp.dot(a, wd_ref[c * fc:(c + 1) * fc, :], preferred_element_type=F32)
        acc = y if acc is None else acc + y
    o_ref[...] = _layer_norm(alpha * h + acc, g_ref[...], b_ref[...])


def _ffn(h, w_gu, w_dn, g, b, *, alpha, tm, fc):
    S = h.shape[0]
    once = pl.Buffered(1)
    return pl.pallas_call(
        functools.partial(_ffn_kernel, alpha=alpha, fc=fc),
        grid=(S // tm,),
        in_specs=[pl.BlockSpec((tm, D_MODEL), lambda i: (i, 0)),
                  pl.BlockSpec(w_gu.shape, lambda i: (0, 0), pipeline_mode=once),
                  pl.BlockSpec(w_dn.shape, lambda i: (0, 0), pipeline_mode=once),
                  pl.BlockSpec(g.shape, lambda i: (0, 0)), pl.BlockSpec(b.shape, lambda i: (0, 0))],
        out_specs=pl.BlockSpec((tm, D_MODEL), lambda i: (i, 0)),
        out_shape=jax.ShapeDtypeStruct((S, D_MODEL), F32),
        compiler_params=_cparams(("arbitrary",)),
        name="dense_ffn",
    )(h, w_gu, w_dn, g, b)


def _pad_lanes(a, n=LANES):
    return jnp.pad(a, [(0, 0)] * (a.ndim - 1) + [(0, n - a.shape[-1])])


def kernel(x, ln_in_g, ln_in_b, w_in, b_fgate, lam_q1, lam_k1, lam_q2, lam_k2, subln_g, w_branch_fox, w_branch_diff, w_out, ln_mix_g, ln_mix_b, rel_bias, w_ffn_gate_up, w_ffn_down, w_router, w_expert_gate_up, w_expert_down, ln_ffn_g, ln_ffn_b):
    B, S, _ = x.shape
    assert B == 1
    depth = w_in.shape[0]
    alpha = (2.0 * depth) ** 0.25
    tq = min(ATTN_TQ, S)
    tk = min(ATTN_TK, tq)
    tm = min(512, S)
    tf = min(1024, S)
    te = min(MOE_TM, S)
    assert tk >= REL_MAX_DISTANCE and tq % tk == 0 and S % tq == 0 and S % tf == 0 and S % tm == 0 and S % te == 0
    assert te % SCATTER_ROWS == 0 and te % RANK_CHUNK == 0 and MOE_ROWS % 16 == 0
    row = lambda v: v.reshape(1, -1).astype(F32)

    table = rel_bias.astype(F32)
    bias = _bias_tiles(table.T.reshape(-1), t=tk)

    h = x[0]
    for l in range(depth):
        lam_init = 0.8 - 0.6 * math.exp(-0.3 * l)
        fq, fk, fv, fg, dq, dk, dv, ga, gb = jnp.split(w_in[l], list(_cumsum(IN_SPLIT_SIZES))[:-1], axis=-1)
        w_cat = jnp.concatenate([fq, fk, fv, dq, dk, dv, ga, gb, _pad_lanes(fg)], axis=-1).astype(BF16)
        outs = _inproj(h, row(ln_in_g), row(ln_in_b), w_cat, _pad_lanes(row(b_fgate[l])), do_ln=(l == 0), tm=tk)
        qa, ka, va, stats, dq, dk, dv, sa, sb = outs[:9]
        if l == 0:
            h = outs[9]
        start = _plan(stats, nsub=tq // tk)[:S // tq, :FOX_HEADS].reshape(-1)
        y_fox = _fox_attention(start, qa, ka, va, tq=tq, tk=tk)
        lamvec = jnp.zeros((8, LANES), F32).at[0:4, 0:HEAD_DIM].set(
            jnp.stack([lam_q1[l], lam_k1[l], lam_q2[l], lam_k2[l]]).astype(F32))
        y_diff = _diff_attention(dq, dk, dv, bias, lamvec, row(subln_g[l]), tq=tq, tk=tk,
                                 lam_init=lam_init)
        h = _merge(y_fox, y_diff, sa, sb, h, w_branch_fox[l].astype(BF16), w_branch_diff[l].astype(BF16),
                   w_out[l].astype(BF16), row(ln_mix_g[l]), row(ln_mix_b[l]), alpha=alpha, tm=tm)
        g, b = row(ln_ffn_g[l]), row(ln_ffn_b[l])
        if l % 2 == 0:
            h = _ffn(h, w_ffn_gate_up[l // 2].astype(BF16), w_ffn_down[l // 2].astype(BF16), g, b,
                     alpha=alpha, tm=tf, fc=1408)
        else:
            wr = _pad_lanes(w_router[l // 2].astype(F32))
            whi = wr.astype(BF16)
            wlo = (wr - whi.astype(F32)).astype(BF16)
            cw, rank, rank_t, counts = _router(h, whi, wlo, tm=te)
            h = _moe(h, cw, rank, rank_t, counts[:, 0, :N_EXPERTS].reshape(-1),
                     w_expert_gate_up[l // 2].astype(BF16), w_expert_down[l // 2].astype(BF16),
                     g, b, alpha=alpha, tm=te, fc=896)
    return h[None]


def _cumsum(sizes):
    tot = 0
    for s in sizes:
        tot += s
        yield tot
```

```python
import functools
import math

import jax
import jax.numpy as jnp
from jax import lax
from jax.experimental import pallas as pl
from jax.experimental.pallas import tpu as pltpu

F32 = jnp.float32
BF16 = jnp.bfloat16

D_MODEL = 1024
HEAD_DIM = 64
FOX_HEADS = 8
FOX_WIDTH = FOX_HEADS * HEAD_DIM
DIFF_HEADS = 4
DIFF_QK_WIDTH = DIFF_HEADS * 2 * HEAD_DIM
DIFF_V_DIM = 2 * HEAD_DIM
DIFF_V_WIDTH = DIFF_HEADS * DIFF_V_DIM
IN_SPLIT_SIZES = (FOX_WIDTH, FOX_WIDTH, FOX_WIDTH, FOX_HEADS, DIFF_QK_WIDTH, DIFF_QK_WIDTH, DIFF_V_WIDTH,
                  D_MODEL, D_MODEL)
REL_BUCKETS = 32
REL_MAX_DISTANCE = 128
N_EXPERTS = 8
LN_EPS = 1e-5
SUBLN_EPS = 1e-5

LANES = 128
LOG2E = 1.4426950408889634
Q_SCALE = HEAD_DIM ** -0.5 * LOG2E
NEG = -1e30
VMEM_LIMIT = 56 * 1024 * 1024
ATTN_TQ = 2048
FOX_TQ = 1024
ATTN_TK = 512
MOE_TM = 2048


def _cparams(sem):
    return pltpu.CompilerParams(dimension_semantics=sem, vmem_limit_bytes=VMEM_LIMIT)


def _layer_norm(x, g, b):
    mu = jnp.mean(x, axis=-1, keepdims=True)
    xc = x - mu
    var = jnp.mean(xc * xc, axis=-1, keepdims=True)
    return xc * lax.rsqrt(var + LN_EPS) * g + b


def _sigmoid(x):
    return 1.0 / (1.0 + jnp.exp(-x))


def _split3(x):
    hi = x.astype(BF16)
    r1 = x - hi.astype(F32)
    mid = r1.astype(BF16)
    lo = (r1 - mid.astype(F32)).astype(BF16)
    return hi, mid, lo


_C_FQ, _C_FK, _C_FV, _C_DQ, _C_DK, _C_DV, _C_GA, _C_GB, _C_FG, _C_END = (
    0, 512, 1024, 1536, 2048, 2560, 3072, 4096, 5120, 5248)


def _inproj_kernel(x_ref, g_ref, b_ref, w_ref, bf_ref, qa_ref, ka_ref, va_ref, st_ref, dq, dk, dv, sa, sb,
                   *rest, do_ln):
    carry_ref = rest[-1]
    tb = x_ref.shape[0]

    @pl.when(pl.program_id(0) == 0)
    def _():
        carry_ref[...] = jnp.zeros_like(carry_ref)

    x = x_ref[...]
    if do_ln:
        x = _layer_norm(x, g_ref[...], b_ref[...])
        rest[0][...] = x
    xb = x.astype(BF16)

    def mm(a, b):
        return jnp.dot(xb, w_ref[:, a:b], preferred_element_type=F32)

    fq = (mm(_C_FQ, _C_FK) * Q_SCALE).astype(BF16)
    fk = mm(_C_FK, _C_FV).astype(BF16)
    fv = mm(_C_FV, _C_DQ).astype(BF16)

    z = mm(_C_FG, _C_END) + bf_ref[...]
    logf = jnp.minimum(z, 0.0) - jnp.log1p(jnp.exp(-jnp.abs(z)))
    r = lax.broadcasted_iota(jnp.int32, (tb, tb), 0)
    c = lax.broadcasted_iota(jnp.int32, (tb, tb), 1)
    tri = (c <= r).astype(BF16)
    hi, mid, lo = _split3(logf)
    dot = lambda a: jnp.dot(tri, a, preferred_element_type=F32)
    cum = (dot(hi) + dot(mid)) + dot(lo) + carry_ref[0:1, :]
    carry_ref[0:1, :] = cum[tb - 1:tb, :]

    c_hi, c_mid, c_lo = (t.astype(F32) for t in _split3(cum * LOG2E))
    lane = lax.broadcasted_iota(jnp.int32, (tb, LANES), 1)
    one = jnp.ones((tb, LANES), F32)
    zero = jnp.zeros((tb, LANES), F32)
    lane1 = lax.broadcasted_iota(jnp.int32, (1, LANES), 1)
    q_norm2 = jnp.zeros((1, LANES), F32)
    k_norm2 = jnp.zeros((1, LANES), F32)

    def max_norm2(slab):
        n2 = jnp.sum(jnp.where(lane < 64, slab * slab, 0.0), axis=1, keepdims=True)
        return jnp.max(n2, axis=0, keepdims=True)

    for h in range(FOX_HEADS):
        p, odd = divmod(h, 2)
        col = lambda t: jnp.broadcast_to(t[:, h:h + 1], (tb, LANES))
        chi, cmid, clo = col(c_hi), col(c_mid), col(c_lo)

        def head_slab(a):
            slab = a[:, p * LANES:(p + 1) * LANES].astype(F32)
            return pltpu.roll(slab, HEAD_DIM, 1) if odd else slab

        qslab = head_slab(fq)
        kslab = head_slab(fk)
        q_norm2 = jnp.where(lane1 == h, max_norm2(qslab), q_norm2)
        k_norm2 = jnp.where(lane1 == h, max_norm2(kslab), k_norm2)
        qa = jnp.where(lane < 64, qslab,
                       jnp.where(lane == 64, chi, jnp.where(lane == 65, cmid, jnp.where(lane == 66, clo,
                                 jnp.where(lane < 70, one, zero)))))
        ka = jnp.where(lane < 64, kslab,
                       jnp.where(lane < 67, one, jnp.where(lane == 67, -chi, jnp.where(lane == 68, -cmid,
                                 jnp.where(lane == 69, -clo, zero)))))
        qa_ref[h] = qa.astype(BF16)
        ka_ref[h] = ka.astype(BF16)
        vslab = fv[:, p * LANES:(p + 1) * LANES].astype(F32)
        if odd:
            va = jnp.where(lane >= 64, vslab, jnp.where(lane == 0, one, zero))
        else:
            va = jnp.where(lane < 64, vslab, jnp.where(lane == 64, one, zero))
        va_ref[h] = va.astype(BF16)

    dq[...] = (mm(_C_DQ, _C_DK) * Q_SCALE).astype(BF16)
    dk[...] = mm(_C_DK, _C_DV).astype(BF16)
    dv[...] = mm(_C_DV, _C_GA).astype(BF16)
    sa[...] = _sigmoid(mm(_C_GA, _C_GB)).astype(BF16)
    sb[...] = _sigmoid(mm(_C_GB, _C_FG)).astype(BF16)

    st_ref[0] = jnp.concatenate([q_norm2, k_norm2, cum[tb - 1:tb, :] * LOG2E, jnp.zeros((5, LANES), F32)], axis=0)


def _inproj(x, g, b, w, bf, *, do_ln, tm):
    S = x.shape[0]
    row = lambda n: pl.BlockSpec((tm, n), lambda i: (i, 0))
    full = lambda a: pl.BlockSpec(a.shape, lambda i: (0,) * a.ndim)
    hspec = pl.BlockSpec((FOX_HEADS, tm, LANES), lambda i: (0, i, 0))
    hshape = jax.ShapeDtypeStruct((FOX_HEADS, S, LANES), BF16)
    out_shape = [hshape] * 3 + [jax.ShapeDtypeStruct((S // tm, 8, LANES), F32)] + [
        jax.ShapeDtypeStruct((S, 512), BF16)] * 3 + [jax.ShapeDtypeStruct((S, D_MODEL), BF16)] * 2
    out_specs = [hspec] * 3 + [pl.BlockSpec((1, 8, LANES), lambda i: (i, 0, 0))] + [row(512)] * 3 + [row(D_MODEL)] * 2
    if do_ln:
        out_shape.append(jax.ShapeDtypeStruct((S, D_MODEL), F32))
        out_specs.append(row(D_MODEL))
    return pl.pallas_call(
        functools.partial(_inproj_kernel, do_ln=do_ln),
        grid=(S // tm,),
        in_specs=[row(D_MODEL), full(g), full(b), full(w), full(bf)],
        out_specs=out_specs,
        out_shape=out_shape,
        scratch_shapes=[pltpu.VMEM((8, LANES), F32)],
        compiler_params=_cparams(("arbitrary",)),
        name="inproj",
    )(x, g, b, w, bf)


SKIP_LOG2 = 150.0


def _plan_kernel(st_ref, start_ref, *, nt, nsub):
    qn = jnp.sqrt(st_ref[pl.ds(0, nt, stride=8), :])
    kn = jnp.sqrt(st_ref[pl.ds(1, nt, stride=8), :])
    cum_end = st_ref[pl.ds(2, nt, stride=8), :]
    tile = lax.broadcasted_iota(jnp.int32, (nt, LANES), 0)
    rows = []
    for i in range(nt // nsub):
        own = jnp.logical_and(tile >= i * nsub, tile < (i + 1) * nsub)
        q_max = jnp.max(jnp.where(own, qn, 0.0), axis=0, keepdims=True)
        k_own = jnp.max(jnp.where(own, kn, 0.0), axis=0, keepdims=True)
        cum_q = cum_end[i * nsub - 1:i * nsub, :] if i > 0 else jnp.zeros((1, LANES), F32)
        upper = q_max * kn + (cum_q - cum_end) + q_max * k_own
        slack = SKIP_LOG2 + 1e-4 * jnp.abs(cum_end)
        needed = jnp.logical_and(upper >= -slack, tile < i * nsub)
        rows.append(jnp.min(jnp.where(needed, tile, i * nsub), axis=0, keepdims=True))
    rows += [jnp.zeros((1, LANES), jnp.int32)] * (start_ref.shape[0] - len(rows))
    start_ref[...] = jnp.concatenate(rows, axis=0)


def _plan(stats, *, nsub):
    nt = stats.shape[0]
    return pl.pallas_call(
        functools.partial(_plan_kernel, nt=nt, nsub=nsub),
        out_shape=jax.ShapeDtypeStruct((pl.cdiv(nt // nsub, 8) * 8, LANES), jnp.int32),
        compiler_params=pltpu.CompilerParams(vmem_limit_bytes=VMEM_LIMIT),
        name="fox_plan",
    )(stats.reshape(nt * 8, LANES))


def _nt_dot(a, b):
    return lax.dot_general(a, b, (((1,), (1,)), ((), ())), preferred_element_type=F32)


def _fox_kernel(start_ref, q_ref, k_ref, v_ref, o_ref, m_sc, acc_sc, *, tq, tk):
    i = pl.program_id(1)
    pair = pl.program_id(0)
    m_sc[...] = jnp.full_like(m_sc, NEG)
    acc_sc[...] = jnp.zeros_like(acc_sc)
    nsub = tq // tk

    def tile(j, r0, masked):
        off = pl.multiple_of(j * tk, tk)
        rows = tq - r0
        for hh in range(2):
            s = _nt_dot(q_ref[hh, r0:, :], k_ref[hh, pl.ds(off, tk), :])
            if masked:
                r = lax.broadcasted_iota(jnp.int32, (rows, tk), 0)
                c = lax.broadcasted_iota(jnp.int32, (rows, tk), 1)
                s = jnp.where(c <= r, s, NEG)
            m_prev = m_sc[hh, r0:, :]
            m_next = jnp.maximum(m_prev, jnp.max(s, axis=1, keepdims=True))
            alpha = jnp.exp2(m_prev - m_next)
            p = jnp.exp2(s - jnp.concatenate([m_next] * (tk // LANES), axis=1)).astype(BF16)
            pv = jnp.dot(p, v_ref[hh, pl.ds(off, tk), :], preferred_element_type=F32)
            acc_sc[hh, r0:, :] = alpha * acc_sc[hh, r0:, :] + pv
            m_sc[hh, r0:, :] = m_next

    def body(j, carry):
        tile(j, 0, False)
        return carry

    first = jnp.minimum(start_ref[i * FOX_HEADS + 2 * pair], start_ref[i * FOX_HEADS + 2 * pair + 1])
    lax.fori_loop(first, i * nsub, body, 0)
    for u in range(nsub):
        tile(i * nsub + u, u * tk, True)

    a0 = acc_sc[0]
    a1 = acc_sc[1]
    lane = lax.broadcasted_iota(jnp.int32, (tq, LANES), 1)
    o_ref[...] = jnp.where(lane < 64, a0 / a0[:, 64:65], a1 / a1[:, 0:1]).astype(o_ref.dtype)


def _fox_attention(start, qa, ka, va, *, tq, tk):
    S = qa.shape[1]
    grid_spec = pltpu.PrefetchScalarGridSpec(
        num_scalar_prefetch=1,
        grid=(FOX_HEADS // 2, S // tq),
        in_specs=[pl.BlockSpec((2, tq, LANES), lambda p, i, st: (p, i, 0)),
                  pl.BlockSpec((2, S, LANES), lambda p, i, st: (p, 0, 0)),
                  pl.BlockSpec((2, S, LANES), lambda p, i, st: (p, 0, 0))],
        out_specs=pl.BlockSpec((tq, LANES), lambda p, i, st: (i, p)),
        scratch_shapes=[pltpu.VMEM((2, tq, LANES), F32), pltpu.VMEM((2, tq, LANES), F32)],
    )
    return pl.pallas_call(
        functools.partial(_fox_kernel, tq=tq, tk=tk),
        grid_spec=grid_spec,
        out_shape=jax.ShapeDtypeStruct((S, FOX_WIDTH), BF16),
        compiler_params=_cparams(("arbitrary", "arbitrary")),
        name="fox_attn",
    )(start, qa, ka, va)


BIAS_KINDS = 2


def _bias_tile_kernel(table_ref, o_ref, *, t):
    h = pl.program_id(0)
    kind = pl.program_id(1)
    r = lax.broadcasted_iota(jnp.int32, (t, t), 0)
    c = lax.broadcasted_iota(jnp.int32, (t, t), 1)
    dist = r - c + kind * t
    n = jnp.maximum(dist, 0)
    max_exact = REL_BUCKETS // 2
    nf = jnp.maximum(n, 1).astype(F32)
    log_part = jnp.log(nf / max_exact) / math.log(REL_MAX_DISTANCE / max_exact) * (REL_BUCKETS - max_exact)
    large = jnp.minimum(max_exact + log_part.astype(jnp.int32), REL_BUCKETS - 1)
    bucket = jnp.where(n < max_exact, n, large)
    val = jnp.zeros((t, t), F32)
    for b in range(REL_BUCKETS):
        val = jnp.where(bucket == b, table_ref[h * REL_BUCKETS + b], val)
    far = table_ref[h * REL_BUCKETS + REL_BUCKETS - 1]
    o_ref[0, 0] = jnp.where(dist >= 0, (val - far) * LOG2E, NEG)


def _bias_tiles(table_flat, *, t):
    return pl.pallas_call(
        functools.partial(_bias_tile_kernel, t=t),
        grid=(DIFF_HEADS, BIAS_KINDS),
        in_specs=[pl.BlockSpec(memory_space=pltpu.SMEM)],
        out_specs=pl.BlockSpec((1, 1, t, t), lambda h, k: (h, k, 0, 0)),
        out_shape=jax.ShapeDtypeStruct((DIFF_HEADS, BIAS_KINDS, t, t), F32),
        compiler_params=_cparams(("arbitrary", "arbitrary")),
        name="bias_tiles",
    )(table_flat)


def _diff_kernel(q_ref, k_ref, v_ref, bias_ref, lam_ref, g_ref, o_ref, m_sc, l_sc, acc_sc,
                 *, tq, tk, lam_init):
    i = pl.program_id(1)
    m_sc[...] = jnp.full_like(m_sc, NEG)
    l_sc[...] = jnp.zeros_like(l_sc)
    acc_sc[...] = jnp.zeros_like(acc_sc)
    nsub = tq // tk

    lane = lax.broadcasted_iota(jnp.int32, (tq, LANES), 1)
    q = q_ref[...]
    zero = jnp.zeros_like(q)
    qs = (jnp.where(lane < 64, q, zero), jnp.where(lane >= 64, q, zero))

    def tile(j, r0, kinds):
        off = pl.multiple_of(j * tk, tk)
        k = k_ref[pl.ds(off, tk), :]
        v = v_ref[pl.ds(off, tk), :]
        for c in range(2):
            s = _nt_dot(qs[c][r0:], k)
            m_prev = m_sc[c, r0:, :]
            if kinds is not None:
                s = jnp.concatenate([s[a * tk:(a + 1) * tk] + bias_ref[0, kd] if kd < 2 else s[a * tk:(a + 1) * tk]
                                     for a, kd in enumerate(kinds)], axis=0)
            m_next = jnp.maximum(m_prev, jnp.max(s, axis=1, keepdims=True))
            p = jnp.exp2(s - jnp.concatenate([m_next] * (tk // LANES), axis=1))
            alpha = jnp.exp2(m_prev - m_next)
            l_sc[c, r0:, :] = alpha * l_sc[c, r0:, :] + sum(p[:, a * LANES:(a + 1) * LANES]
                                                            for a in range(tk // LANES))
            pv = jnp.dot(p.astype(BF16), v, preferred_element_type=F32)
            acc_sc[c, r0:, :] = alpha * acc_sc[c, r0:, :] + pv
            m_sc[c, r0:, :] = m_next

    def body(j, carry):
        tile(j, 0, None)
        return carry

    lax.fori_loop(0, i * nsub - 1, body, 0)

    @pl.when(i > 0)
    def _():
        tile(i * nsub - 1, 0, (1,) + (2,) * (nsub - 1))

    for u in range(nsub):
        tile(i * nsub + u, u * tk, ((0, 1) + (2,) * nsub)[:nsub - u])

    lv = lam_ref[...]
    lam = (jnp.exp(jnp.sum(lv[0:1] * lv[1:2], axis=1, keepdims=True))
           - jnp.exp(jnp.sum(lv[2:3] * lv[3:4], axis=1, keepdims=True)) + lam_init)
    l0 = jnp.sum(l_sc[0], axis=1, keepdims=True)
    l1 = jnp.sum(l_sc[1], axis=1, keepdims=True)
    o = acc_sc[0] / l0 - lam * (acc_sc[1] / l1)
    o = o * lax.rsqrt(jnp.mean(o * o, axis=1, keepdims=True) + SUBLN_EPS) * g_ref[...] * (1.0 - lam_init)
    o_ref[...] = o.astype(o_ref.dtype)


def _diff_attention(dq, dk, dv, bias, lamvec, g, *, tq, tk, lam_init):
    S = dq.shape[0]
    return pl.pallas_call(
        functools.partial(_diff_kernel, tq=tq, tk=tk, lam_init=lam_init),
        grid=(DIFF_HEADS, S // tq),
        in_specs=[pl.BlockSpec((tq, LANES), lambda h, i: (i, h)),
                  pl.BlockSpec((S, LANES), lambda h, i: (0, h)),
                  pl.BlockSpec((S, LANES), lambda h, i: (0, h)),
                  pl.BlockSpec((1, BIAS_KINDS, tk, tk), lambda h, i: (h, 0, 0, 0)),
                  pl.BlockSpec((8, LANES), lambda h, i: (0, 0)),
                  pl.BlockSpec((1, LANES), lambda h, i: (0, 0))],
        out_specs=pl.BlockSpec((tq, LANES), lambda h, i: (i, h)),
        out_shape=jax.ShapeDtypeStruct((S, DIFF_V_WIDTH), BF16),
        scratch_shapes=[pltpu.VMEM((2, tq, LANES), F32), pltpu.VMEM((2, tq, LANES), F32),
                        pltpu.VMEM((2, tq, LANES), F32)],
        compiler_params=_cparams(("arbitrary", "arbitrary")),
        name="diff_attn",
    )(dq, dk, dv, bias, lamvec, g)


def _merge_kernel(yf_ref, yd_ref, sa_ref, sb_ref, h_ref, wf_ref, wd_ref, wo_ref, g_ref, b_ref, o_ref, *, alpha):
    bf = jnp.dot(yf_ref[...], wf_ref[...], preferred_element_type=F32)
    bd = jnp.dot(yd_ref[...], wd_ref[...], preferred_element_type=F32)
    merged = sa_ref[...].astype(F32) * bf + sb_ref[...].astype(F32) * bd
    mix = jnp.dot(merged.astype(BF16), wo_ref[...], preferred_element_type=F32)
    o_ref[...] = _layer_norm(alpha * h_ref[...] + mix, g_ref[...], b_ref[...])


def _merge(yf, yd, sa, sb, h, wf, wd, wo, g, b, *, alpha, tm):
    S = h.shape[0]
    row = lambda n: pl.BlockSpec((tm, n), lambda i: (i, 0))
    full = lambda a: pl.BlockSpec(a.shape, lambda i: (0,) * a.ndim)
    return pl.pallas_call(
        functools.partial(_merge_kernel, alpha=alpha),
        grid=(S // tm,),
        in_specs=[row(512), row(512), row(D_MODEL), row(D_MODEL), row(D_MODEL),
                  full(wf), full(wd), full(wo), full(g), full(b)],
        out_specs=row(D_MODEL),
        out_shape=jax.ShapeDtypeStruct((S, D_MODEL), F32),
        compiler_params=_cparams(("arbitrary",)),
        name="merge_out",
    )(yf, yd, sa, sb, h, wf, wd, wo, g, b)


MOE_ROWS = 272
SCATTER_ROWS = 256
RANK_CHUNK = 256


def _router_kernel(h_ref, whi_ref, wlo_ref, cw_ref, rank_ref, rank_t_ref, cnt_ref):
    tm = h_ref.shape[0]
    h = h_ref[...]
    hhi = h.astype(BF16)
    hlo = (h - hhi.astype(F32)).astype(BF16)
    whi = whi_ref[...]
    dot = lambda a, b: jnp.dot(a, b, preferred_element_type=F32)
    logits = dot(hhi, whi) + dot(hhi, wlo_ref[...]) + dot(hlo, whi)
    lane = lax.broadcasted_iota(jnp.int32, logits.shape, 1).astype(F32)
    lg = jnp.where(lane < N_EXPERTS, logits, NEG)
    m1 = jnp.max(lg, axis=1, keepdims=True)
    i1 = jnp.min(jnp.where(lg == m1, lane, float(LANES)), axis=1, keepdims=True)
    lg2 = jnp.where(lane == i1, NEG, lg)
    m2 = jnp.max(lg2, axis=1, keepdims=True)
    i2 = jnp.min(jnp.where(lg2 == m2, lane, float(LANES)), axis=1, keepdims=True)
    e = jnp.exp(m2 - m1)
    g1 = 1.0 / (1.0 + e)
    chosen = jnp.logical_or(lane == i1, lane == i2)
    cw_ref[...] = jnp.where(lane == i1, g1, 0.0) + jnp.where(lane == i2, e * g1, 0.0)

    a = chosen.astype(BF16)
    r = lax.broadcasted_iota(jnp.int32, (RANK_CHUNK, RANK_CHUNK), 0)
    c = lax.broadcasted_iota(jnp.int32, (RANK_CHUNK, RANK_CHUNK), 1)
    strict_lower = (c < r).astype(BF16)
    ones_row = jnp.ones((8, RANK_CHUNK), BF16)
    carry = jnp.zeros((1, LANES), F32)
    for q in range(tm // RANK_CHUNK):
        rows = slice(q * RANK_CHUNK, (q + 1) * RANK_CHUNK)
        rank = jnp.where(chosen[rows], dot(strict_lower, a[rows]) + carry, -1.0)
        rank_ref[rows, :] = rank
        rank_t_ref[0, :, rows] = jnp.transpose(rank)[0:N_EXPERTS, :]
        carry = carry + dot(ones_row, a[rows])[0:1, :]
    cnt_ref[0] = jnp.broadcast_to(carry, (8, LANES)).astype(jnp.int32)


def _router(h, whi, wlo, *, tm):
    S = h.shape[0]
    nb = S // tm
    return pl.pallas_call(
        _router_kernel,
        grid=(nb,),
        in_specs=[pl.BlockSpec((tm, D_MODEL), lambda i: (i, 0)),
                  pl.BlockSpec(whi.shape, lambda i: (0, 0)), pl.BlockSpec(wlo.shape, lambda i: (0, 0))],
        out_specs=[pl.BlockSpec((tm, LANES), lambda i: (i, 0)),
                   pl.BlockSpec((tm, LANES), lambda i: (i, 0)),
                   pl.BlockSpec((1, N_EXPERTS, tm), lambda i: (i, 0, 0)),
                   pl.BlockSpec((1, 8, LANES), lambda i: (i, 0, 0))],
        out_shape=[jax.ShapeDtypeStruct((S, LANES), F32),
                   jax.ShapeDtypeStruct((S, LANES), F32),
                   jax.ShapeDtypeStruct((nb, N_EXPERTS, tm), F32),
                   jax.ShapeDtypeStruct((nb, 8, LANES), jnp.int32)],
        compiler_params=_cparams(("arbitrary",)),
        name="router",
    )(h, whi, wlo)


def _moe_kernel(cnt_ref, h_ref, cw_ref, rank_ref, rank_t_ref, wg_ref, wu_ref, wd_ref, g_ref, b_ref, o_ref,
                hb_sc, xs_sc, acc_sc, *, alpha):
    tm = h_ref.shape[0]
    i = pl.program_id(0)
    e = pl.program_id(1)
    c = pl.program_id(2)
    last_c = c == pl.num_programs(2) - 1
    n_rows = cnt_ref[i * N_EXPERTS + e]
    n_steps = (n_rows + MOE_ROWS - 1) // MOE_ROWS
    n_scatter = (n_rows + SCATTER_ROWS - 1) // SCATTER_ROWS

    @pl.when(jnp.logical_and(e == 0, c == 0))
    def _():
        hb_sc[...] = h_ref[...].astype(BF16)
        o_ref[...] = jnp.zeros_like(o_ref)

    @pl.when(c == 0)
    def _():
        rank_row = rank_t_ref[0, pl.ds(e, 1), :]

        def gather(t, carry):
            r0 = pl.multiple_of(t * MOE_ROWS, 16)
            slot = (lax.broadcasted_iota(jnp.int32, (MOE_ROWS, tm), 0) + r0).astype(F32)
            onehot = (rank_row == slot).astype(BF16)
            xs_sc[pl.ds(r0, MOE_ROWS), :] = jnp.dot(onehot, hb_sc[...], preferred_element_type=F32).astype(BF16)
            acc_sc[pl.ds(r0, MOE_ROWS), :] = jnp.zeros((MOE_ROWS, D_MODEL), F32)
            return carry

        lax.fori_loop(0, n_steps, gather, 0)

        @pl.when(n_steps * MOE_ROWS < n_scatter * SCATTER_ROWS)
        def _():
            acc_sc[pl.ds(pl.multiple_of(n_steps * MOE_ROWS, 16), SCATTER_ROWS), :] = jnp.zeros(
                (SCATTER_ROWS, D_MODEL), F32)

    def swiglu(t, carry):
        r0 = pl.multiple_of(t * MOE_ROWS, 16)
        x = xs_sc[pl.ds(r0, MOE_ROWS), :]
        gate = jnp.dot(x, wg_ref[0], preferred_element_type=F32)
        up = jnp.dot(x, wu_ref[0], preferred_element_type=F32)
        a = (gate * _sigmoid(gate) * up).astype(BF16)
        acc_sc[pl.ds(r0, MOE_ROWS), :] += jnp.dot(a, wd_ref[0], preferred_element_type=F32)
        return carry

    lax.fori_loop(0, n_steps, swiglu, 0)

    @pl.when(last_c)
    def _():
        lane = lax.broadcasted_iota(jnp.int32, (tm, LANES), 1)
        pick = lambda ref: jnp.broadcast_to(
            jnp.sum(jnp.where(lane == e, ref[...], 0.0), axis=1, keepdims=True), (tm, LANES))
        rank_col = pick(rank_ref)
        gate_col = jnp.concatenate([pick(cw_ref)] * (D_MODEL // LANES), axis=1)

        def scatter(t, carry):
            r0 = pl.multiple_of(t * SCATTER_ROWS, SCATTER_ROWS)
            slot = (lane + r0).astype(F32)
            onehot = jnp.concatenate([(rank_col == slot + float(a * LANES)) for a in range(SCATTER_ROWS // LANES)],
                                     axis=1).astype(BF16)
            y = acc_sc[pl.ds(r0, SCATTER_ROWS), :].astype(BF16)
            o_ref[...] += gate_col * jnp.dot(onehot, y, preferred_element_type=F32)
            return carry

        lax.fori_loop(0, n_scatter, scatter, 0)

    @pl.when(jnp.logical_and(e == pl.num_programs(1) - 1, last_c))
    def _():
        o_ref[...] = _layer_norm(alpha * h_ref[...] + o_ref[...], g_ref[...], b_ref[...])


def _moe(h, cw, rank, rank_t, counts, w_gu, w_dn, g, b, *, alpha, tm, fc):
    S = h.shape[0]
    E, _, F2 = w_gu.shape
    nc = F2 // 2 // fc
    once = pl.Buffered(1)
    rows_cap = pl.cdiv(tm, MOE_ROWS) * MOE_ROWS
    grid_spec = pltpu.PrefetchScalarGridSpec(
        num_scalar_prefetch=1,
        grid=(S // tm, E, nc),
        in_specs=[pl.BlockSpec((tm, D_MODEL), lambda i, e, c, cnt: (i, 0), pipeline_mode=once),
                  pl.BlockSpec((tm, LANES), lambda i, e, c, cnt: (i, 0), pipeline_mode=once),
                  pl.BlockSpec((tm, LANES), lambda i, e, c, cnt: (i, 0), pipeline_mode=once),
                  pl.BlockSpec((1, N_EXPERTS, tm), lambda i, e, c, cnt: (i, 0, 0), pipeline_mode=once),
                  pl.BlockSpec((1, D_MODEL, fc), lambda i, e, c, cnt: (e, 0, c)),
                  pl.BlockSpec((1, D_MODEL, fc), lambda i, e, c, cnt: (e, 0, nc + c)),
                  pl.BlockSpec((1, fc, D_MODEL), lambda i, e, c, cnt: (e, c, 0)),
                  pl.BlockSpec(g.shape, lambda i, e, c, cnt: (0, 0)),
                  pl.BlockSpec(b.shape, lambda i, e, c, cnt: (0, 0))],
        out_specs=pl.BlockSpec((tm, D_MODEL), lambda i, e, c, cnt: (i, 0), pipeline_mode=once),
        scratch_shapes=[pltpu.VMEM((tm, D_MODEL), BF16), pltpu.VMEM((rows_cap, D_MODEL), BF16),
                        pltpu.VMEM((rows_cap, D_MODEL), F32)],
    )
    return pl.pallas_call(
        functools.partial(_moe_kernel, alpha=alpha),
        grid_spec=grid_spec,
        out_shape=jax.ShapeDtypeStruct((S, D_MODEL), F32),
        compiler_params=_cparams(("arbitrary", "arbitrary", "arbitrary")),
        name="moe_ffn",
    )(counts, h, cw, rank, rank_t, w_gu, w_gu, w_dn, g, b)


def _ffn_kernel(h_ref, wgu_ref, wd_ref, g_ref, b_ref, o_ref, *, alpha, fc):
    h = h_ref[...]
    hb = h.astype(BF16)
    d_ff = wd_ref.shape[0]
    acc = None
    for c in range(d_ff // fc):
        gate = jnp.dot(hb, wgu_ref[:, c * fc:(c + 1) * fc], preferred_element_type=F32)
        up = jnp.dot(hb, wgu_ref[:, d_ff + c * fc:d_ff + (c + 1) * fc], preferred_element_type=F32)
        a = (gate * _sigmoid(gate) * up).astype(BF16)
        y = jnp.dot(a, wd_ref[c * fc:(c + 1) * fc, :], preferred_element_type=F32)
        acc = y if acc is None else acc + y
    o_ref[...] = _layer_norm(alpha * h + acc, g_ref[...], b_ref[...])


def _ffn(h, w_gu, w_dn, g, b, *, alpha, tm, fc):
    S = h.shape[0]
    once = pl.Buffered(1)
    return pl.pallas_call(
        functools.partial(_ffn_kernel, alpha=alpha, fc=fc),
        grid=(S // tm,),
        in_specs=[pl.BlockSpec((tm, D_MODEL), lambda i: (i, 0)),
                  pl.BlockSpec(w_gu.shape, lambda i: (0, 0), pipeline_mode=once),
                  pl.BlockSpec(w_dn.shape, lambda i: (0, 0), pipeline_mode=once),
                  pl.BlockSpec(g.shape, lambda i: (0, 0)), pl.BlockSpec(b.shape, lambda i: (0, 0))],
        out_specs=pl.BlockSpec((tm, D_MODEL), lambda i: (i, 0)),
        out_shape=jax.ShapeDtypeStruct((S, D_MODEL), F32),
        compiler_params=_cparams(("arbitrary",)),
        name="dense_ffn",
    )(h, w_gu, w_dn, g, b)


def _pad_lanes(a, n=LANES):
    return jnp.pad(a, [(0, 0)] * (a.ndim - 1) + [(0, n - a.shape[-1])])


def kernel(x, ln_in_g, ln_in_b, w_in, b_fgate, lam_q1, lam_k1, lam_q2, lam_k2, subln_g, w_branch_fox, w_branch_diff, w_out, ln_mix_g, ln_mix_b, rel_bias, w_ffn_gate_up, w_ffn_down, w_router, w_expert_gate_up, w_expert_down, ln_ffn_g, ln_ffn_b):
    B, S, _ = x.shape
    assert B == 1
    depth = w_in.shape[0]
    alpha = (2.0 * depth) ** 0.25
    tq = min(ATTN_TQ, S)
    tqf = min(FOX_TQ, S)
    tk = min(ATTN_TK, tq)
    tm = min(512, S)
    tf = min(1024, S)
    te = min(MOE_TM, S)
    assert tqf % tk == 0 and S % tqf == 0
    assert tk >= REL_MAX_DISTANCE and tq % tk == 0 and S % tq == 0 and S % tf == 0 and S % tm == 0 and S % te == 0
    assert te % SCATTER_ROWS == 0 and te % RANK_CHUNK == 0 and MOE_ROWS % 16 == 0
    row = lambda v: v.reshape(1, -1).astype(F32)

    table = rel_bias.astype(F32)
    bias = _bias_tiles(table.T.reshape(-1), t=tk)

    h = x[0]
    for l in range(depth):
        lam_init = 0.8 - 0.6 * math.exp(-0.3 * l)
        fq, fk, fv, fg, dq, dk, dv, ga, gb = jnp.split(w_in[l], list(_cumsum(IN_SPLIT_SIZES))[:-1], axis=-1)
        w_cat = jnp.concatenate([fq, fk, fv, dq, dk, dv, ga, gb, _pad_lanes(fg)], axis=-1).astype(BF16)
        outs = _inproj(h, row(ln_in_g), row(ln_in_b), w_cat, _pad_lanes(row(b_fgate[l])), do_ln=(l == 0), tm=tk)
        qa, ka, va, stats, dq, dk, dv, sa, sb = outs[:9]
        if l == 0:
            h = outs[9]
        start = _plan(stats, nsub=tqf // tk)[:S // tqf, :FOX_HEADS].reshape(-1)
        y_fox = _fox_attention(start, qa, ka, va, tq=tqf, tk=tk)
        lamvec = jnp.zeros((8, LANES), F32).at[0:4, 0:HEAD_DIM].set(
            jnp.stack([lam_q1[l], lam_k1[l], lam_q2[l], lam_k2[l]]).astype(F32))
        y_diff = _diff_attention(dq, dk, dv, bias, lamvec, row(subln_g[l]), tq=tq, tk=tk,
                                 lam_init=lam_init)
        h = _merge(y_fox, y_diff, sa, sb, h, w_branch_fox[l].astype(BF16), w_branch_diff[l].astype(BF16),
                   w_out[l].astype(BF16), row(ln_mix_g[l]), row(ln_mix_b[l]), alpha=alpha, tm=tm)
        g, b = row(ln_ffn_g[l]), row(ln_ffn_b[l])
        if l % 2 == 0:
            h = _ffn(h, w_ffn_gate_up[l // 2].astype(BF16), w_ffn_down[l // 2].astype(BF16), g, b,
                     alpha=alpha, tm=tf, fc=1408)
        else:
            wr = _pad_lanes(w_router[l // 2].astype(F32))
            whi = wr.astype(BF16)
            wlo = (wr - whi.astype(F32)).astype(BF16)
            cw, rank, rank_t, counts = _router(h, whi, wlo, tm=te)
            h = _moe(h, cw, rank, rank_t, counts[:, 0, :N_EXPERTS].reshape(-1),
                     w_expert_gate_up[l // 2].astype(BF16), w_expert_down[l // 2].astype(BF16),
                     g, b, alpha=alpha, tm=te, fc=896)
    return h[None]


def _cumsum(sizes):
    tot = 0
    for s in sizes:
        tot += s
        yield tot
```

```python
import functools
import math

import jax
import jax.numpy as jnp
from jax import lax
from jax.experimental import pallas as pl
from jax.experimental.pallas import tpu as pltpu

F32 = jnp.float32
BF16 = jnp.bfloat16

D_MODEL = 1024
HEAD_DIM = 64
FOX_HEADS = 8
FOX_WIDTH = FOX_HEADS * HEAD_DIM
DIFF_HEADS = 4
DIFF_QK_WIDTH = DIFF_HEADS * 2 * HEAD_DIM
DIFF_V_DIM = 2 * HEAD_DIM
DIFF_V_WIDTH = DIFF_HEADS * DIFF_V_DIM
IN_SPLIT_SIZES = (FOX_WIDTH, FOX_WIDTH, FOX_WIDTH, FOX_HEADS, DIFF_QK_WIDTH, DIFF_QK_WIDTH, DIFF_V_WIDTH,
                  D_MODEL, D_MODEL)
REL_BUCKETS = 32
REL_MAX_DISTANCE = 128
N_EXPERTS = 8
LN_EPS = 1e-5
SUBLN_EPS = 1e-5

LANES = 128
LOG2E = 1.4426950408889634
Q_SCALE = HEAD_DIM ** -0.5 * LOG2E
NEG = -1e30
VMEM_LIMIT = 56 * 1024 * 1024
ATTN_TQ = 2048
ATTN_TK = 512
MOE_TM = 2048


def _cparams(sem):
    return pltpu.CompilerParams(dimension_semantics=sem, vmem_limit_bytes=VMEM_LIMIT)


def _layer_norm(x, g, b):
    mu = jnp.mean(x, axis=-1, keepdims=True)
    xc = x - mu
    var = jnp.mean(xc * xc, axis=-1, keepdims=True)
    return xc * lax.rsqrt(var + LN_EPS) * g + b


def _sigmoid(x):
    return 1.0 / (1.0 + jnp.exp(-x))


def _split3(x):
    hi = x.astype(BF16)
    r1 = x - hi.astype(F32)
    mid = r1.astype(BF16)
    lo = (r1 - mid.astype(F32)).astype(BF16)
    return hi, mid, lo


_C_FQ, _C_FK, _C_FV, _C_DQ, _C_DK, _C_DV, _C_GA, _C_GB, _C_FG, _C_END = (
    0, 512, 1024, 1536, 2048, 2560, 3072, 4096, 5120, 5248)


def _inproj_kernel(x_ref, g_ref, b_ref, w_ref, bf_ref, qa_ref, ka_ref, va_ref, st_ref, dq, dk, dv, sa, sb, dkn,
                   *rest, do_ln):
    carry_ref = rest[-1]
    tb = x_ref.shape[0]

    @pl.when(pl.program_id(0) == 0)
    def _():
        carry_ref[...] = jnp.zeros_like(carry_ref)

    x = x_ref[...]
    if do_ln:
        x = _layer_norm(x, g_ref[...], b_ref[...])
        rest[0][...] = x
    xb = x.astype(BF16)

    def mm(a, b):
        return jnp.dot(xb, w_ref[:, a:b], preferred_element_type=F32)

    fq = (mm(_C_FQ, _C_FK) * Q_SCALE).astype(BF16)
    fk = mm(_C_FK, _C_FV).astype(BF16)
    fv = mm(_C_FV, _C_DQ).astype(BF16)

    z = mm(_C_FG, _C_END) + bf_ref[...]
    logf = jnp.minimum(z, 0.0) - jnp.log1p(jnp.exp(-jnp.abs(z)))
    r = lax.broadcasted_iota(jnp.int32, (tb, tb), 0)
    c = lax.broadcasted_iota(jnp.int32, (tb, tb), 1)
    tri = (c <= r).astype(BF16)
    hi, mid, lo = _split3(logf)
    dot = lambda a: jnp.dot(tri, a, preferred_element_type=F32)
    cum = (dot(hi) + dot(mid)) + dot(lo) + carry_ref[0:1, :]
    carry_ref[0:1, :] = cum[tb - 1:tb, :]

    c_hi, c_mid, c_lo = (t.astype(F32) for t in _split3(cum * LOG2E))
    lane = lax.broadcasted_iota(jnp.int32, (tb, LANES), 1)
    one = jnp.ones((tb, LANES), F32)
    zero = jnp.zeros((tb, LANES), F32)
    lane1 = lax.broadcasted_iota(jnp.int32, (1, LANES), 1)
    q_norm2 = jnp.zeros((1, LANES), F32)
    k_norm2 = jnp.zeros((1, LANES), F32)

    def max_norm2(slab):
        n2 = jnp.sum(jnp.where(lane < 64, slab * slab, 0.0), axis=1, keepdims=True)
        return jnp.max(n2, axis=0, keepdims=True)

    for h in range(FOX_HEADS):
        p, odd = divmod(h, 2)
        col = lambda t: jnp.broadcast_to(t[:, h:h + 1], (tb, LANES))
        chi, cmid, clo = col(c_hi), col(c_mid), col(c_lo)

        def head_slab(a):
            slab = a[:, p * LANES:(p + 1) * LANES].astype(F32)
            return pltpu.roll(slab, HEAD_DIM, 1) if odd else slab

        qslab = head_slab(fq)
        kslab = head_slab(fk)
        q_norm2 = jnp.where(lane1 == h, max_norm2(qslab), q_norm2)
        k_norm2 = jnp.where(lane1 == h, max_norm2(kslab), k_norm2)
        qa = jnp.where(lane < 64, qslab,
                       jnp.where(lane == 64, chi, jnp.where(lane == 65, cmid, jnp.where(lane == 66, clo,
                                 jnp.where(lane < 70, one, zero)))))
        ka = jnp.where(lane < 64, kslab,
                       jnp.where(lane < 67, one, jnp.where(lane == 67, -chi, jnp.where(lane == 68, -cmid,
                                 jnp.where(lane == 69, -clo, zero)))))
        qa_ref[h] = qa.astype(BF16)
        ka_ref[h] = ka.astype(BF16)
        vslab = fv[:, p * LANES:(p + 1) * LANES].astype(F32)
        if odd:
            va = jnp.where(lane >= 64, vslab, jnp.where(lane == 0, one, zero))
        else:
            va = jnp.where(lane < 64, vslab, jnp.where(lane == 64, one, zero))
        va_ref[h] = va.astype(BF16)

    dq[...] = (mm(_C_DQ, _C_DK) * Q_SCALE).astype(BF16)
    dk16 = mm(_C_DK, _C_DV).astype(BF16)
    dk[...] = dk16
    dv[...] = mm(_C_DV, _C_GA).astype(BF16)
    sa[...] = _sigmoid(mm(_C_GA, _C_GB)).astype(BF16)
    sb[...] = _sigmoid(mm(_C_GB, _C_FG)).astype(BF16)

    dk_norm2 = carry_ref[1:2, :]
    for h in range(DIFF_HEADS):
        slab = dk16[:, h * LANES:(h + 1) * LANES].astype(F32)
        sq = slab * slab
        for c in range(2):
            n2 = jnp.sum(jnp.where((lane < 64) == (c == 0), sq, 0.0), axis=1, keepdims=True)
            dk_norm2 = jnp.where(lane1 == 2 * h + c, jnp.maximum(dk_norm2, jnp.max(n2, axis=0, keepdims=True)),
                                 dk_norm2)
    carry_ref[1:2, :] = dk_norm2
    dkn[...] = jnp.broadcast_to(dk_norm2, dkn.shape)

    st_ref[0] = jnp.concatenate([q_norm2, k_norm2, cum[tb - 1:tb, :] * LOG2E, jnp.zeros((5, LANES), F32)], axis=0)


def _inproj(x, g, b, w, bf, *, do_ln, tm):
    S = x.shape[0]
    row = lambda n: pl.BlockSpec((tm, n), lambda i: (i, 0))
    full = lambda a: pl.BlockSpec(a.shape, lambda i: (0,) * a.ndim)
    hspec = pl.BlockSpec((FOX_HEADS, tm, LANES), lambda i: (0, i, 0))
    hshape = jax.ShapeDtypeStruct((FOX_HEADS, S, LANES), BF16)
    out_shape = [hshape] * 3 + [jax.ShapeDtypeStruct((S // tm, 8, LANES), F32)] + [
        jax.ShapeDtypeStruct((S, 512), BF16)] * 3 + [jax.ShapeDtypeStruct((S, D_MODEL), BF16)] * 2 + [
        jax.ShapeDtypeStruct((8, LANES), F32)]
    out_specs = [hspec] * 3 + [pl.BlockSpec((1, 8, LANES), lambda i: (i, 0, 0))] + [row(512)] * 3 + [
        row(D_MODEL)] * 2 + [pl.BlockSpec((8, LANES), lambda i: (0, 0))]
    if do_ln:
        out_shape.append(jax.ShapeDtypeStruct((S, D_MODEL), F32))
        out_specs.append(row(D_MODEL))
    return pl.pallas_call(
        functools.partial(_inproj_kernel, do_ln=do_ln),
        grid=(S // tm,),
        in_specs=[row(D_MODEL), full(g), full(b), full(w), full(bf)],
        out_specs=out_specs,
        out_shape=out_shape,
        scratch_shapes=[pltpu.VMEM((8, LANES), F32)],
        compiler_params=_cparams(("arbitrary",)),
        name="inproj",
    )(x, g, b, w, bf)


SKIP_LOG2 = 150.0


def _plan_kernel(st_ref, start_ref, *, nt, nsub):
    qn = jnp.sqrt(st_ref[pl.ds(0, nt, stride=8), :])
    kn = jnp.sqrt(st_ref[pl.ds(1, nt, stride=8), :])
    cum_end = st_ref[pl.ds(2, nt, stride=8), :]
    tile = lax.broadcasted_iota(jnp.int32, (nt, LANES), 0)
    rows = []
    for i in range(nt // nsub):
        own = jnp.logical_and(tile >= i * nsub, tile < (i + 1) * nsub)
        q_max = jnp.max(jnp.where(own, qn, 0.0), axis=0, keepdims=True)
        k_own = jnp.max(jnp.where(own, kn, 0.0), axis=0, keepdims=True)
        cum_q = cum_end[i * nsub - 1:i * nsub, :] if i > 0 else jnp.zeros((1, LANES), F32)
        upper = q_max * kn + (cum_q - cum_end) + q_max * k_own
        slack = SKIP_LOG2 + 1e-4 * jnp.abs(cum_end)
        needed = jnp.logical_and(upper >= -slack, tile < i * nsub)
        rows.append(jnp.min(jnp.where(needed, tile, i * nsub), axis=0, keepdims=True))
    rows += [jnp.zeros((1, LANES), jnp.int32)] * (8 - len(rows))
    start_ref[...] = jnp.concatenate(rows, axis=0)


def _plan(stats, *, nsub):
    nt = stats.shape[0]
    assert nt // nsub <= 8
    return pl.pallas_call(
        functools.partial(_plan_kernel, nt=nt, nsub=nsub),
        out_shape=jax.ShapeDtypeStruct((8, LANES), jnp.int32),
        compiler_params=pltpu.CompilerParams(vmem_limit_bytes=VMEM_LIMIT),
        name="fox_plan",
    )(stats.reshape(nt * 8, LANES))


def _nt_dot(a, b):
    return lax.dot_general(a, b, (((1,), (1,)), ((), ())), preferred_element_type=F32)


def _fox_kernel(start_ref, q_ref, k_ref, v_ref, o_ref, m_sc, acc_sc, *, tq, tk):
    i = pl.program_id(1)
    pair = pl.program_id(0)
    m_sc[...] = jnp.full_like(m_sc, NEG)
    acc_sc[...] = jnp.zeros_like(acc_sc)
    nsub = tq // tk

    def tile(j, r0, masked):
        off = pl.multiple_of(j * tk, tk)
        rows = tq - r0
        for hh in range(2):
            s = _nt_dot(q_ref[hh, r0:, :], k_ref[hh, pl.ds(off, tk), :])
            if masked:
                r = lax.broadcasted_iota(jnp.int32, (rows, tk), 0)
                c = lax.broadcasted_iota(jnp.int32, (rows, tk), 1)
                s = jnp.where(c <= r, s, NEG)
            m_prev = m_sc[hh, r0:, :]
            m_next = jnp.maximum(m_prev, jnp.max(s, axis=1, keepdims=True))
            alpha = jnp.exp2(m_prev - m_next)
            p = jnp.exp2(s - jnp.concatenate([m_next] * (tk // LANES), axis=1)).astype(BF16)
            pv = jnp.dot(p, v_ref[hh, pl.ds(off, tk), :], preferred_element_type=F32)
            acc_sc[hh, r0:, :] = alpha * acc_sc[hh, r0:, :] + pv
            m_sc[hh, r0:, :] = m_next

    def body(j, carry):
        tile(j, 0, False)
        return carry

    first = jnp.minimum(start_ref[i * FOX_HEADS + 2 * pair], start_ref[i * FOX_HEADS + 2 * pair + 1])
    lax.fori_loop(first, i * nsub, body, 0)
    for u in range(nsub):
        tile(i * nsub + u, u * tk, True)

    a0 = acc_sc[0]
    a1 = acc_sc[1]
    lane = lax.broadcasted_iota(jnp.int32, (tq, LANES), 1)
    o_ref[...] = jnp.where(lane < 64, a0 / a0[:, 64:65], a1 / a1[:, 0:1]).astype(o_ref.dtype)


def _fox_attention(start, qa, ka, va, *, tq, tk):
    S = qa.shape[1]
    grid_spec = pltpu.PrefetchScalarGridSpec(
        num_scalar_prefetch=1,
        grid=(FOX_HEADS // 2, S // tq),
        in_specs=[pl.BlockSpec((2, tq, LANES), lambda p, i, st: (p, i, 0)),
                  pl.BlockSpec((2, S, LANES), lambda p, i, st: (p, 0, 0)),
                  pl.BlockSpec((2, S, LANES), lambda p, i, st: (p, 0, 0))],
        out_specs=pl.BlockSpec((tq, LANES), lambda p, i, st: (i, p)),
        scratch_shapes=[pltpu.VMEM((2, tq, LANES), F32), pltpu.VMEM((2, tq, LANES), F32)],
    )
    return pl.pallas_call(
        functools.partial(_fox_kernel, tq=tq, tk=tk),
        grid_spec=grid_spec,
        out_shape=jax.ShapeDtypeStruct((S, FOX_WIDTH), BF16),
        compiler_params=_cparams(("arbitrary", "arbitrary")),
        name="fox_attn",
    )(start, qa, ka, va)


BIAS_KINDS = 2


def _bias_tile_kernel(table_ref, o_ref, *, t):
    h = pl.program_id(0)
    kind = pl.program_id(1)
    r = lax.broadcasted_iota(jnp.int32, (t, t), 0)
    c = lax.broadcasted_iota(jnp.int32, (t, t), 1)
    dist = r - c + kind * t
    n = jnp.maximum(dist, 0)
    max_exact = REL_BUCKETS // 2
    nf = jnp.maximum(n, 1).astype(F32)
    log_part = jnp.log(nf / max_exact) / math.log(REL_MAX_DISTANCE / max_exact) * (REL_BUCKETS - max_exact)
    large = jnp.minimum(max_exact + log_part.astype(jnp.int32), REL_BUCKETS - 1)
    bucket = jnp.where(n < max_exact, n, large)
    val = jnp.zeros((t, t), F32)
    for b in range(REL_BUCKETS):
        val = jnp.where(bucket == b, table_ref[h * REL_BUCKETS + b], val)
    far = table_ref[h * REL_BUCKETS + REL_BUCKETS - 1]
    o_ref[0, 0] = jnp.where(dist >= 0, (val - far) * LOG2E, NEG)


def _bias_tiles(table_flat, *, t):
    return pl.pallas_call(
        functools.partial(_bias_tile_kernel, t=t),
        grid=(DIFF_HEADS, BIAS_KINDS),
        in_specs=[pl.BlockSpec(memory_space=pltpu.SMEM)],
        out_specs=pl.BlockSpec((1, 1, t, t), lambda h, k: (h, k, 0, 0)),
        out_shape=jax.ShapeDtypeStruct((DIFF_HEADS, BIAS_KINDS, t, t), F32),
        compiler_params=_cparams(("arbitrary", "arbitrary")),
        name="bias_tiles",
    )(table_flat)


FIXED_REF_MAX_SHIFT = 100.0


def _diff_kernel(tab_ref, q_ref, k_ref, v_ref, bias_ref, kn_ref, lam_ref, g_ref, o_ref, m_sc, l_sc, acc_sc,
                 *, tq, tk, lam_init):
    h = pl.program_id(0)
    i = pl.program_id(1)
    m_sc[...] = jnp.full_like(m_sc, NEG)
    l_sc[...] = jnp.zeros_like(l_sc)
    acc_sc[...] = jnp.zeros_like(acc_sc)
    nsub = tq // tk

    lane = lax.broadcasted_iota(jnp.int32, (tq, LANES), 1)
    q = q_ref[...]
    zero = jnp.zeros_like(q)
    qs = (jnp.where(lane < 64, q, zero), jnp.where(lane >= 64, q, zero))

    def tile(j, r0, kinds):
        off = pl.multiple_of(j * tk, tk)
        k = k_ref[pl.ds(off, tk), :]
        v = v_ref[pl.ds(off, tk), :]
        for c in range(2):
            s = _nt_dot(qs[c][r0:], k)
            m_prev = m_sc[c, r0:, :]
            if kinds is not None:
                s = jnp.concatenate([s[a * tk:(a + 1) * tk] + bias_ref[0, kd] if kd < 2 else s[a * tk:(a + 1) * tk]
                                     for a, kd in enumerate(kinds)], axis=0)
            m_next = jnp.maximum(m_prev, jnp.max(s, axis=1, keepdims=True))
            p = jnp.exp2(s - jnp.concatenate([m_next] * (tk // LANES), axis=1))
            alpha = jnp.exp2(m_prev - m_next)
            l_sc[c, r0:, :] = alpha * l_sc[c, r0:, :] + sum(p[:, a * LANES:(a + 1) * LANES]
                                                            for a in range(tk // LANES))
            pv = jnp.dot(p.astype(BF16), v, preferred_element_type=F32)
            acc_sc[c, r0:, :] = alpha * acc_sc[c, r0:, :] + pv
            m_sc[c, r0:, :] = m_next

    def fixed_tile(j, carry):
        off = pl.multiple_of(j * tk, tk)
        k = k_ref[pl.ds(off, tk), :]
        v = v_ref[pl.ds(off, tk), :]
        for c in range(2):
            p = jnp.exp2(_nt_dot(qs[c], k) - jnp.concatenate([m_sc[c]] * (tk // LANES), axis=1))
            l_sc[c] += sum(p[:, a * LANES:(a + 1) * LANES] for a in range(tk // LANES))
            acc_sc[c] += jnp.dot(p.astype(BF16), v, preferred_element_type=F32)
        return carry

    def running_tile(j, carry):
        tile(j, 0, None)
        return carry

    @pl.when(i > 0)
    def _():
        tile(i * nsub - 1, 0, (1,) + (2,) * (nsub - 1))

    for u in range(nsub):
        tile(i * nsub + u, u * tk, ((0, 1) + (2,) * nsub)[:nsub - u])

    lane1 = lax.broadcasted_iota(jnp.int32, (1, LANES), 1)
    b0 = (tab_ref[h * REL_BUCKETS] - tab_ref[h * REL_BUCKETS + REL_BUCKETS - 1]) * LOG2E
    shifts = []
    for c in range(2):
        qf = qs[c].astype(F32)
        qn2 = jnp.max(jnp.sum(qf * qf, axis=1, keepdims=True), axis=0, keepdims=True)
        kn2 = jnp.sum(jnp.where(lane1 == 2 * h + c, kn_ref[0:1, :], 0.0), axis=1, keepdims=True)
        shifts.append(jnp.ceil(2.0 * jnp.sqrt(qn2 * kn2) - b0 + 1.0))
    fits = jnp.max(jnp.maximum(shifts[0], shifts[1])) <= FIXED_REF_MAX_SHIFT
    n_far = i * nsub - 1

    @pl.when(fits)
    def _():
        for c in range(2):
            down = jnp.exp2(-shifts[c])
            m_sc[c] = m_sc[c] + shifts[c]
            l_sc[c] = l_sc[c] * down
            acc_sc[c] = acc_sc[c] * down
        lax.fori_loop(0, n_far, fixed_tile, 0)

    @pl.when(jnp.logical_not(fits))
    def _():
        lax.fori_loop(0, n_far, running_tile, 0)

    lv = lam_ref[...]
    lam = (jnp.exp(jnp.sum(lv[0:1] * lv[1:2], axis=1, keepdims=True))
           - jnp.exp(jnp.sum(lv[2:3] * lv[3:4], axis=1, keepdims=True)) + lam_init)
    l0 = jnp.sum(l_sc[0], axis=1, keepdims=True)
    l1 = jnp.sum(l_sc[1], axis=1, keepdims=True)
    o = acc_sc[0] / l0 - lam * (acc_sc[1] / l1)
    o = o * lax.rsqrt(jnp.mean(o * o, axis=1, keepdims=True) + SUBLN_EPS) * g_ref[...] * (1.0 - lam_init)
    o_ref[...] = o.astype(o_ref.dtype)


def _diff_attention(table_flat, dq, dk, dv, bias, dkn, lamvec, g, *, tq, tk, lam_init):
    S = dq.shape[0]
    grid_spec = pltpu.PrefetchScalarGridSpec(
        num_scalar_prefetch=1,
        grid=(DIFF_HEADS, S // tq),
        in_specs=[pl.BlockSpec((tq, LANES), lambda h, i, tab: (i, h)),
                  pl.BlockSpec((S, LANES), lambda h, i, tab: (0, h)),
                  pl.BlockSpec((S, LANES), lambda h, i, tab: (0, h)),
                  pl.BlockSpec((1, BIAS_KINDS, tk, tk), lambda h, i, tab: (h, 0, 0, 0)),
                  pl.BlockSpec((8, LANES), lambda h, i, tab: (0, 0)),
                  pl.BlockSpec((8, LANES), lambda h, i, tab: (0, 0)),
                  pl.BlockSpec((1, LANES), lambda h, i, tab: (0, 0))],
        out_specs=pl.BlockSpec((tq, LANES), lambda h, i, tab: (i, h)),
        scratch_shapes=[pltpu.VMEM((2, tq, LANES), F32), pltpu.VMEM((2, tq, LANES), F32),
                        pltpu.VMEM((2, tq, LANES), F32)],
    )
    return pl.pallas_call(
        functools.partial(_diff_kernel, tq=tq, tk=tk, lam_init=lam_init),
        grid_spec=grid_spec,
        out_shape=jax.ShapeDtypeStruct((S, DIFF_V_WIDTH), BF16),
        compiler_params=_cparams(("arbitrary", "arbitrary")),
        name="diff_attn",
    )(table_flat, dq, dk, dv, bias, dkn, lamvec, g)


def _merge_kernel(yf_ref, yd_ref, sa_ref, sb_ref, h_ref, wf_ref, wd_ref, wo_ref, g_ref, b_ref, o_ref, *, alpha):
    bf = jnp.dot(yf_ref[...], wf_ref[...], preferred_element_type=F32)
    bd = jnp.dot(yd_ref[...], wd_ref[...], preferred_element_type=F32)
    merged = sa_ref[...].astype(F32) * bf + sb_ref[...].astype(F32) * bd
    mix = jnp.dot(merged.astype(BF16), wo_ref[...], preferred_element_type=F32)
    o_ref[...] = _layer_norm(alpha * h_ref[...] + mix, g_ref[...], b_ref[...])


def _merge(yf, yd, sa, sb, h, wf, wd, wo, g, b, *, alpha, tm):
    S = h.shape[0]
    row = lambda n: pl.BlockSpec((tm, n), lambda i: (i, 0))
    full = lambda a: pl.BlockSpec(a.shape, lambda i: (0,) * a.ndim)
    return pl.pallas_call(
        functools.partial(_merge_kernel, alpha=alpha),
        grid=(S // tm,),
        in_specs=[row(512), row(512), row(D_MODEL), row(D_MODEL), row(D_MODEL),
                  full(wf), full(wd), full(wo), full(g), full(b)],
        out_specs=row(D_MODEL),
        out_shape=jax.ShapeDtypeStruct((S, D_MODEL), F32),
        compiler_params=_cparams(("arbitrary",)),
        name="merge_out",
    )(yf, yd, sa, sb, h, wf, wd, wo, g, b)


MOE_ROWS = 272
SCATTER_ROWS = 256
RANK_CHUNK = 256


def _router_kernel(h_ref, whi_ref, wlo_ref, cw_ref, rank_ref, rank_t_ref, cnt_ref):
    tm = h_ref.shape[0]
    h = h_ref[...]
    hhi = h.astype(BF16)
    hlo = (h - hhi.astype(F32)).astype(BF16)
    whi = whi_ref[...]
    dot = lambda a, b: jnp.dot(a, b, preferred_element_type=F32)
    logits = dot(hhi, whi) + dot(hhi, wlo_ref[...]) + dot(hlo, whi)
    lane = lax.broadcasted_iota(jnp.int32, logits.shape, 1).astype(F32)
    lg = jnp.where(lane < N_EXPERTS, logits, NEG)
    m1 = jnp.max(lg, axis=1, keepdims=True)
    i1 = jnp.min(jnp.where(lg == m1, lane, float(LANES)), axis=1, keepdims=True)
    lg2 = jnp.where(lane == i1, NEG, lg)
    m2 = jnp.max(lg2, axis=1, keepdims=True)
    i2 = jnp.min(jnp.where(lg2 == m2, lane, float(LANES)), axis=1, keepdims=True)
    e = jnp.exp(m2 - m1)
    g1 = 1.0 / (1.0 + e)
    chosen = jnp.logical_or(lane == i1, lane == i2)
    cw_ref[...] = jnp.where(lane == i1, g1, 0.0) + jnp.where(lane == i2, e * g1, 0.0)

    a = chosen.astype(BF16)
    r = lax.broadcasted_iota(jnp.int32, (RANK_CHUNK, RANK_CHUNK), 0)
    c = lax.broadcasted_iota(jnp.int32, (RANK_CHUNK, RANK_CHUNK), 1)
    strict_lower = (c < r).astype(BF16)
    ones_row = jnp.ones((8, RANK_CHUNK), BF16)
    carry = jnp.zeros((1, LANES), F32)
    for q in range(tm // RANK_CHUNK):
        rows = slice(q * RANK_CHUNK, (q + 1) * RANK_CHUNK)
        rank = jnp.where(chosen[rows], dot(strict_lower, a[rows]) + carry, -1.0)
        rank_ref[rows, :] = rank
        rank_t_ref[0, :, rows] = jnp.transpose(rank)[0:N_EXPERTS, :]
        carry = carry + dot(ones_row, a[rows])[0:1, :]
    cnt_ref[0] = jnp.broadcast_to(carry, (8, LANES)).astype(jnp.int32)


def _router(h, whi, wlo, *, tm):
    S = h.shape[0]
    nb = S // tm
    return pl.pallas_call(
        _router_kernel,
        grid=(nb,),
        in_specs=[pl.BlockSpec((tm, D_MODEL), lambda i: (i, 0)),
                  pl.BlockSpec(whi.shape, lambda i: (0, 0)), pl.BlockSpec(wlo.shape, lambda i: (0, 0))],
        out_specs=[pl.BlockSpec((tm, LANES), lambda i: (i, 0)),
                   pl.BlockSpec((tm, LANES), lambda i: (i, 0)),
                   pl.BlockSpec((1, N_EXPERTS, tm), lambda i: (i, 0, 0)),
                   pl.BlockSpec((1, 8, LANES), lambda i: (i, 0, 0))],
        out_shape=[jax.ShapeDtypeStruct((S, LANES), F32),
                   jax.ShapeDtypeStruct((S, LANES), F32),
                   jax.ShapeDtypeStruct((nb, N_EXPERTS, tm), F32),
                   jax.ShapeDtypeStruct((nb, 8, LANES), jnp.int32)],
        compiler_params=_cparams(("arbitrary",)),
        name="router",
    )(h, whi, wlo)


def _moe_kernel(cnt_ref, h_ref, cw_ref, rank_ref, rank_t_ref, wg_ref, wu_ref, wd_ref, g_ref, b_ref, o_ref,
                hb_sc, xs_sc, acc_sc, *, alpha):
    tm = h_ref.shape[0]
    i = pl.program_id(0)
    e = pl.program_id(1)
    c = pl.program_id(2)
    last_c = c == pl.num_programs(2) - 1
    n_rows = cnt_ref[i * N_EXPERTS + e]
    n_steps = (n_rows + MOE_ROWS - 1) // MOE_ROWS
    n_scatter = (n_rows + SCATTER_ROWS - 1) // SCATTER_ROWS

    @pl.when(jnp.logical_and(e == 0, c == 0))
    def _():
        hb_sc[...] = h_ref[...].astype(BF16)
        o_ref[...] = jnp.zeros_like(o_ref)

    @pl.when(c == 0)
    def _():
        rank_row = rank_t_ref[0, pl.ds(e, 1), :]

        def gather(t, carry):
            r0 = pl.multiple_of(t * MOE_ROWS, 16)
            slot = (lax.broadcasted_iota(jnp.int32, (MOE_ROWS, tm), 0) + r0).astype(F32)
            onehot = (rank_row == slot).astype(BF16)
            xs_sc[pl.ds(r0, MOE_ROWS), :] = jnp.dot(onehot, hb_sc[...], preferred_element_type=F32).astype(BF16)
            acc_sc[pl.ds(r0, MOE_ROWS), :] = jnp.zeros((MOE_ROWS, D_MODEL), F32)
            return carry

        lax.fori_loop(0, n_steps, gather, 0)

        @pl.when(n_steps * MOE_ROWS < n_scatter * SCATTER_ROWS)
        def _():
            acc_sc[pl.ds(pl.multiple_of(n_steps * MOE_ROWS, 16), SCATTER_ROWS), :] = jnp.zeros(
                (SCATTER_ROWS, D_MODEL), F32)

    def swiglu(t, carry):
        r0 = pl.multiple_of(t * MOE_ROWS, 16)
        x = xs_sc[pl.ds(r0, MOE_ROWS), :]
        gate = jnp.dot(x, wg_ref[0], preferred_element_type=F32)
        up = jnp.dot(x, wu_ref[0], preferred_element_type=F32)
        a = (gate * _sigmoid(gate) * up).astype(BF16)
        acc_sc[pl.ds(r0, MOE_ROWS), :] += jnp.dot(a, wd_ref[0], preferred_element_type=F32)
        return carry

    lax.fori_loop(0, n_steps, swiglu, 0)

    @pl.when(last_c)
    def _():
        lane = lax.broadcasted_iota(jnp.int32, (tm, LANES), 1)
        pick = lambda ref: jnp.broadcast_to(
            jnp.sum(jnp.where(lane == e, ref[...], 0.0), axis=1, keepdims=True), (tm, LANES))
        rank_col = pick(rank_ref)
        gate_col = jnp.concatenate([pick(cw_ref)] * (D_MODEL // LANES), axis=1)

        def scatter(t, carry):
            r0 = pl.multiple_of(t * SCATTER_ROWS, SCATTER_ROWS)
            slot = (lane + r0).astype(F32)
            onehot = jnp.concatenate([(rank_col == slot + float(a * LANES)) for a in range(SCATTER_ROWS // LANES)],
                                     axis=1).astype(BF16)
            y = acc_sc[pl.ds(r0, SCATTER_ROWS), :].astype(BF16)
            o_ref[...] += gate_col * jnp.dot(onehot, y, preferred_element_type=F32)
            return carry

        lax.fori_loop(0, n_scatter, scatter, 0)

    @pl.when(jnp.logical_and(e == pl.num_programs(1) - 1, last_c))
    def _():
        o_ref[...] = _layer_norm(alpha * h_ref[...] + o_ref[...], g_ref[...], b_ref[...])


def _moe(h, cw, rank, rank_t, counts, w_gu, w_dn, g, b, *, alpha, tm, fc):
    S = h.shape[0]
    E, _, F2 = w_gu.shape
    nc = F2 // 2 // fc
    once = pl.Buffered(1)
    rows_cap = pl.cdiv(tm, MOE_ROWS) * MOE_ROWS
    grid_spec = pltpu.PrefetchScalarGridSpec(
        num_scalar_prefetch=1,
        grid=(S // tm, E, nc),
        in_specs=[pl.BlockSpec((tm, D_MODEL), lambda i, e, c, cnt: (i, 0), pipeline_mode=once),
                  pl.BlockSpec((tm, LANES), lambda i, e, c, cnt: (i, 0), pipeline_mode=once),
                  pl.BlockSpec((tm, LANES), lambda i, e, c, cnt: (i, 0), pipeline_mode=once),
                  pl.BlockSpec((1, N_EXPERTS, tm), lambda i, e, c, cnt: (i, 0, 0), pipeline_mode=once),
                  pl.BlockSpec((1, D_MODEL, fc), lambda i, e, c, cnt: (e, 0, c)),
                  pl.BlockSpec((1, D_MODEL, fc), lambda i, e, c, cnt: (e, 0, nc + c)),
                  pl.BlockSpec((1, fc, D_MODEL), lambda i, e, c, cnt: (e, c, 0)),
                  pl.BlockSpec(g.shape, lambda i, e, c, cnt: (0, 0)),
                  pl.BlockSpec(b.shape, lambda i, e, c, cnt: (0, 0))],
        out_specs=pl.BlockSpec((tm, D_MODEL), lambda i, e, c, cnt: (i, 0), pipeline_mode=once),
        scratch_shapes=[pltpu.VMEM((tm, D_MODEL), BF16), pltpu.VMEM((rows_cap, D_MODEL), BF16),
                        pltpu.VMEM((rows_cap, D_MODEL), F32)],
    )
    return pl.pallas_call(
        functools.partial(_moe_kernel, alpha=alpha),
        grid_spec=grid_spec,
        out_shape=jax.ShapeDtypeStruct((S, D_MODEL), F32),
        compiler_params=_cparams(("arbitrary", "arbitrary", "arbitrary")),
        name="moe_ffn",
    )(counts, h, cw, rank, rank_t, w_gu, w_gu, w_dn, g, b)


def _ffn_kernel(h_ref, wgu_ref, wd_ref, g_ref, b_ref, o_ref, *, alpha, fc):
    h = h_ref[...]
    hb = h.astype(BF16)
    d_ff = wd_ref.shape[0]
    acc = None
    for c in range(d_ff // fc):
        gate = jnp.dot(hb, wgu_ref[:, c * fc:(c + 1) * fc], preferred_element_type=F32)
        up = jnp.dot(hb, wgu_ref[:, d_ff + c * fc:d_ff + (c + 1) * fc], preferred_element_type=F32)
        a = (gate * _sigmoid(gate) * up).astype(BF16)
        y = jnp.dot(a, wd_ref[c * fc:(c + 1) * fc, :], preferred_element_type=F32)
        acc = y if acc is None else acc + y
    o_ref[...] = _layer_norm(alpha * h + acc, g_ref[...], b_ref[...])


def _ffn(h, w_gu, w_dn, g, b, *, alpha, tm, fc):
    S = h.shape[0]
    once = pl.Buffered(1)
    return pl.pallas_call(
        functools.partial(_ffn_kernel, alpha=alpha, fc=fc),
        grid=(S // tm,),
        in_specs=[pl.BlockSpec((tm, D_MODEL), lambda i: (i, 0)),
                  pl.BlockSpec(w_gu.shape, lambda i: (0, 0), pipeline_mode=once),
                  pl.BlockSpec(w_dn.shape, lambda i: (0, 0), pipeline_mode=once),
                  pl.BlockSpec(g.shape, lambda i: (0, 0)), pl.BlockSpec(b.shape, lambda i: (0, 0))],
        out_specs=pl.BlockSpec((tm, D_MODEL), lambda i: (i, 0)),
        out_shape=jax.ShapeDtypeStruct((S, D_MODEL), F32),
        compiler_params=_cparams(("arbitrary",)),
        name="dense_ffn",
    )(h, w_gu, w_dn, g, b)


def _pad_lanes(a, n=LANES):
    return jnp.pad(a, [(0, 0)] * (a.ndim - 1) + [(0, n - a.shape[-1])])


def kernel(x, ln_in_g, ln_in_b, w_in, b_fgate, lam_q1, lam_k1, lam_q2, lam_k2, subln_g, w_branch_fox, w_branch_diff, w_out, ln_mix_g, ln_mix_b, rel_bias, w_ffn_gate_up, w_ffn_down, w_router, w_expert_gate_up, w_expert_down, ln_ffn_g, ln_ffn_b):
    B, S, _ = x.shape
    assert B == 1
    depth = w_in.shape[0]
    alpha = (2.0 * depth) ** 0.25
    tq = min(ATTN_TQ, S)
    tk = min(ATTN_TK, tq)
    tm = min(512, S)
    tf = min(1024, S)
    te = min(MOE_TM, S)
    assert tk >= REL_MAX_DISTANCE and tq % tk == 0 and S % tq == 0 and S % tf == 0 and S % tm == 0 and S % te == 0
    assert te % SCATTER_ROWS == 0 and te % RANK_CHUNK == 0 and MOE_ROWS % 16 == 0
    row = lambda v: v.reshape(1, -1).astype(F32)

    table_flat = rel_bias.astype(F32).T.reshape(-1)
    bias = _bias_tiles(table_flat, t=tk)

    h = x[0]
    for l in range(depth):
        lam_init = 0.8 - 0.6 * math.exp(-0.3 * l)
        fq, fk, fv, fg, dq, dk, dv, ga, gb = jnp.split(w_in[l], list(_cumsum(IN_SPLIT_SIZES))[:-1], axis=-1)
        w_cat = jnp.concatenate([fq, fk, fv, dq, dk, dv, ga, gb, _pad_lanes(fg)], axis=-1).astype(BF16)
        outs = _inproj(h, row(ln_in_g), row(ln_in_b), w_cat, _pad_lanes(row(b_fgate[l])), do_ln=(l == 0), tm=tk)
        qa, ka, va, stats, dq, dk, dv, sa, sb, dkn = outs[:10]
        if l == 0:
            h = outs[10]
        start = _plan(stats, nsub=tq // tk)[:S // tq, :FOX_HEADS].reshape(-1)
        y_fox = _fox_attention(start, qa, ka, va, tq=tq, tk=tk)
        lamvec = jnp.zeros((8, LANES), F32).at[0:4, 0:HEAD_DIM].set(
            jnp.stack([lam_q1[l], lam_k1[l], lam_q2[l], lam_k2[l]]).astype(F32))
        y_diff = _diff_attention(table_flat, dq, dk, dv, bias, dkn, lamvec, row(subln_g[l]), tq=tq, tk=tk,
                                 lam_init=lam_init)
        h = _merge(y_fox, y_diff, sa, sb, h, w_branch_fox[l].astype(BF16), w_branch_diff[l].astype(BF16),
                   w_out[l].astype(BF16), row(ln_mix_g[l]), row(ln_mix_b[l]), alpha=alpha, tm=tm)
        g, b = row(ln_ffn_g[l]), row(ln_ffn_b[l])
        if l % 2 == 0:
            h = _ffn(h, w_ffn_gate_up[l // 2].astype(BF16), w_ffn_down[l // 2].astype(BF16), g, b,
                     alpha=alpha, tm=tf, fc=1408)
        else:
            wr = _pad_lanes(w_router[l // 2].astype(F32))
            whi = wr.astype(BF16)
            wlo = (wr - whi.astype(F32)).astype(BF16)
            cw, rank, rank_t, counts = _router(h, whi, wlo, tm=te)
            h = _moe(h, cw, rank, rank_t, counts[:, 0, :N_EXPERTS].reshape(-1),
                     w_expert_gate_up[l // 2].astype(BF16), w_expert_down[l // 2].astype(BF16),
                     g, b, alpha=alpha, tm=te, fc=896)
    return h[None]


def _cumsum(sizes):
    tot = 0
    for s in sizes:
        tot += s
        yield tot
```

```python
import functools
import math

import jax
import jax.numpy as jnp
from jax import lax
from jax.experimental import pallas as pl
from jax.experimental.pallas import tpu as pltpu

F32 = jnp.float32
BF16 = jnp.bfloat16

D_MODEL = 1024
HEAD_DIM = 64
FOX_HEADS = 8
FOX_WIDTH = FOX_HEADS * HEAD_DIM
DIFF_HEADS = 4
DIFF_QK_WIDTH = DIFF_HEADS * 2 * HEAD_DIM
DIFF_V_DIM = 2 * HEAD_DIM
DIFF_V_WIDTH = DIFF_HEADS * DIFF_V_DIM
IN_SPLIT_SIZES = (FOX_WIDTH, FOX_WIDTH, FOX_WIDTH, FOX_HEADS, DIFF_QK_WIDTH, DIFF_QK_WIDTH, DIFF_V_WIDTH,
                  D_MODEL, D_MODEL)
REL_BUCKETS = 32
REL_MAX_DISTANCE = 128
N_EXPERTS = 8
LN_EPS = 1e-5
SUBLN_EPS = 1e-5

LANES = 128
LOG2E = 1.4426950408889634
Q_SCALE = HEAD_DIM ** -0.5 * LOG2E
NEG = -1e30
VMEM_LIMIT = 56 * 1024 * 1024
ATTN_TQ = 2048
ATTN_TK = 512
MOE_TM = 2048


def _cparams(sem):
    return pltpu.CompilerParams(dimension_semantics=sem, vmem_limit_bytes=VMEM_LIMIT)


def _layer_norm(x, g, b):
    mu = jnp.mean(x, axis=-1, keepdims=True)
    xc = x - mu
    var = jnp.mean(xc * xc, axis=-1, keepdims=True)
    return xc * lax.rsqrt(var + LN_EPS) * g + b


def _sigmoid(x):
    return 1.0 / (1.0 + jnp.exp(-x))


def _split3(x):
    hi = x.astype(BF16)
    r1 = x - hi.astype(F32)
    mid = r1.astype(BF16)
    lo = (r1 - mid.astype(F32)).astype(BF16)
    return hi, mid, lo


_C_FQ, _C_FK, _C_FV, _C_DQ, _C_DK, _C_DV, _C_GA, _C_GB, _C_FG, _C_END = (
    0, 512, 1024, 1536, 2048, 2560, 3072, 4096, 5120, 5248)


def _inproj_kernel(x_ref, g_ref, b_ref, w_ref, bf_ref, qa_ref, ka_ref, va_ref, st_ref, dq, dk, dv, sa, sb, dkn,
                   *rest, do_ln):
    carry_ref = rest[-1]
    tb = x_ref.shape[0]

    @pl.when(pl.program_id(0) == 0)
    def _():
        carry_ref[...] = jnp.zeros_like(carry_ref)

    x = x_ref[...]
    if do_ln:
        x = _layer_norm(x, g_ref[...], b_ref[...])
        rest[0][...] = x
    xb = x.astype(BF16)

    def mm(a, b):
        return jnp.dot(xb, w_ref[:, a:b], preferred_element_type=F32)

    fq = (mm(_C_FQ, _C_FK) * Q_SCALE).astype(BF16)
    fk = mm(_C_FK, _C_FV).astype(BF16)
    fv = mm(_C_FV, _C_DQ).astype(BF16)

    z = mm(_C_FG, _C_END) + bf_ref[...]
    logf = jnp.minimum(z, 0.0) - jnp.log1p(jnp.exp(-jnp.abs(z)))
    r = lax.broadcasted_iota(jnp.int32, (tb, tb), 0)
    c = lax.broadcasted_iota(jnp.int32, (tb, tb), 1)
    tri = (c <= r).astype(BF16)
    hi, mid, lo = _split3(logf)
    dot = lambda a: jnp.dot(tri, a, preferred_element_type=F32)
    cum = (dot(hi) + dot(mid)) + dot(lo) + carry_ref[0:1, :]
    carry_ref[0:1, :] = cum[tb - 1:tb, :]

    c_hi, c_mid, c_lo = (t.astype(F32) for t in _split3(cum * LOG2E))
    lane = lax.broadcasted_iota(jnp.int32, (tb, LANES), 1)
    one = jnp.ones((tb, LANES), F32)
    zero = jnp.zeros((tb, LANES), F32)
    lane1 = lax.broadcasted_iota(jnp.int32, (1, LANES), 1)
    q_norm2 = jnp.zeros((1, LANES), F32)
    k_norm2 = jnp.zeros((1, LANES), F32)

    def max_norm2(slab):
        n2 = jnp.sum(jnp.where(lane < 64, slab * slab, 0.0), axis=1, keepdims=True)
        return jnp.max(n2, axis=0, keepdims=True)

    for h in range(FOX_HEADS):
        p, odd = divmod(h, 2)
        col = lambda t: jnp.broadcast_to(t[:, h:h + 1], (tb, LANES))
        chi, cmid, clo = col(c_hi), col(c_mid), col(c_lo)

        def head_slab(a):
            slab = a[:, p * LANES:(p + 1) * LANES].astype(F32)
            return pltpu.roll(slab, HEAD_DIM, 1) if odd else slab

        qslab = head_slab(fq)
        kslab = head_slab(fk)
        q_norm2 = jnp.where(lane1 == h, max_norm2(qslab), q_norm2)
        k_norm2 = jnp.where(lane1 == h, max_norm2(kslab), k_norm2)
        qa = jnp.where(lane < 64, qslab,
                       jnp.where(lane == 64, chi, jnp.where(lane == 65, cmid, jnp.where(lane == 66, clo,
                                 jnp.where(lane < 70, one, zero)))))
        ka = jnp.where(lane < 64, kslab,
                       jnp.where(lane < 67, one, jnp.where(lane == 67, -chi, jnp.where(lane == 68, -cmid,
                                 jnp.where(lane == 69, -clo, zero)))))
        qa_ref[h] = qa.astype(BF16)
        ka_ref[h] = ka.astype(BF16)
        vslab = fv[:, p * LANES:(p + 1) * LANES].astype(F32)
        if odd:
            va = jnp.where(lane >= 64, vslab, jnp.where(lane == 0, one, zero))
        else:
            va = jnp.where(lane < 64, vslab, jnp.where(lane == 64, one, zero))
        va_ref[h] = va.astype(BF16)

    dq[...] = (mm(_C_DQ, _C_DK) * Q_SCALE).astype(BF16)
    dk16 = mm(_C_DK, _C_DV).astype(BF16)
    dk[...] = dk16
    dv[...] = mm(_C_DV, _C_GA).astype(BF16)
    sa[...] = _sigmoid(mm(_C_GA, _C_GB)).astype(BF16)
    sb[...] = _sigmoid(mm(_C_GB, _C_FG)).astype(BF16)

    dk_norm2 = carry_ref[1:2, :]
    for h in range(DIFF_HEADS):
        slab = dk16[:, h * LANES:(h + 1) * LANES].astype(F32)
        sq = slab * slab
        for c in range(2):
            n2 = jnp.sum(jnp.where((lane < 64) == (c == 0), sq, 0.0), axis=1, keepdims=True)
            dk_norm2 = jnp.where(lane1 == 2 * h + c, jnp.maximum(dk_norm2, jnp.max(n2, axis=0, keepdims=True)),
                                 dk_norm2)
    carry_ref[1:2, :] = dk_norm2
    dkn[...] = jnp.broadcast_to(dk_norm2, dkn.shape)

    st_ref[0] = jnp.concatenate([q_norm2, k_norm2, cum[tb - 1:tb, :] * LOG2E, jnp.zeros((5, LANES), F32)], axis=0)


def _inproj(x, g, b, w, bf, *, do_ln, tm):
    S = x.shape[0]
    row = lambda n: pl.BlockSpec((tm, n), lambda i: (i, 0))
    full = lambda a: pl.BlockSpec(a.shape, lambda i: (0,) * a.ndim)
    hspec = pl.BlockSpec((FOX_HEADS, tm, LANES), lambda i: (0, i, 0))
    hshape = jax.ShapeDtypeStruct((FOX_HEADS, S, LANES), BF16)
    out_shape = [hshape] * 3 + [jax.ShapeDtypeStruct((S // tm, 8, LANES), F32)] + [
        jax.ShapeDtypeStruct((S, 512), BF16)] * 3 + [jax.ShapeDtypeStruct((S, D_MODEL), BF16)] * 2 + [
        jax.ShapeDtypeStruct((8, LANES), F32)]
    out_specs = [hspec] * 3 + [pl.BlockSpec((1, 8, LANES), lambda i: (i, 0, 0))] + [row(512)] * 3 + [
        row(D_MODEL)] * 2 + [pl.BlockSpec((8, LANES), lambda i: (0, 0))]
    if do_ln:
        out_shape.append(jax.ShapeDtypeStruct((S, D_MODEL), F32))
        out_specs.append(row(D_MODEL))
    return pl.pallas_call(
        functools.partial(_inproj_kernel, do_ln=do_ln),
        grid=(S // tm,),
        in_specs=[row(D_MODEL), full(g), full(b), full(w), full(bf)],
        out_specs=out_specs,
        out_shape=out_shape,
        scratch_shapes=[pltpu.VMEM((8, LANES), F32)],
        compiler_params=_cparams(("arbitrary",)),
        name="inproj",
    )(x, g, b, w, bf)


SKIP_LOG2 = 150.0


def _plan_kernel(st_ref, start_ref, *, nt, nsub):
    qn = jnp.sqrt(st_ref[pl.ds(0, nt, stride=8), :])
    kn = jnp.sqrt(st_ref[pl.ds(1, nt, stride=8), :])
    cum_end = st_ref[pl.ds(2, nt, stride=8), :]
    tile = lax.broadcasted_iota(jnp.int32, (nt, LANES), 0)
    rows = []
    for i in range(nt // nsub):
        own = jnp.logical_and(tile >= i * nsub, tile < (i + 1) * nsub)
        q_max = jnp.max(jnp.where(own, qn, 0.0), axis=0, keepdims=True)
        k_own = jnp.max(jnp.where(own, kn, 0.0), axis=0, keepdims=True)
        cum_q = cum_end[i * nsub - 1:i * nsub, :] if i > 0 else jnp.zeros((1, LANES), F32)
        upper = q_max * kn + (cum_q - cum_end) + q_max * k_own
        slack = SKIP_LOG2 + 1e-4 * jnp.abs(cum_end)
        needed = jnp.logical_and(upper >= -slack, tile < i * nsub)
        rows.append(jnp.min(jnp.where(needed, tile, i * nsub), axis=0, keepdims=True))
    rows += [jnp.zeros((1, LANES), jnp.int32)] * (8 - len(rows))
    start_ref[...] = jnp.concatenate(rows, axis=0)


def _plan(stats, *, nsub):
    nt = stats.shape[0]
    assert nt // nsub <= 8
    return pl.pallas_call(
        functools.partial(_plan_kernel, nt=nt, nsub=nsub),
        out_shape=jax.ShapeDtypeStruct((8, LANES), jnp.int32),
        compiler_params=pltpu.CompilerParams(vmem_limit_bytes=VMEM_LIMIT),
        name="fox_plan",
    )(stats.reshape(nt * 8, LANES))


def _nt_dot(a, b):
    return lax.dot_general(a, b, (((1,), (1,)), ((), ())), preferred_element_type=F32)


def _fox_kernel(start_ref, q_ref, k_ref, v_ref, o_ref, m_sc, acc_sc, *, tq, tk):
    i = pl.program_id(1)
    pair = pl.program_id(0)
    m_sc[...] = jnp.full_like(m_sc, NEG)
    acc_sc[...] = jnp.zeros_like(acc_sc)
    nsub = tq // tk

    def tile(j, r0, masked):
        off = pl.multiple_of(j * tk, tk)
        rows = tq - r0
        for hh in range(2):
            s = _nt_dot(q_ref[hh, r0:, :], k_ref[hh, pl.ds(off, tk), :])
            if masked:
                r = lax.broadcasted_iota(jnp.int32, (rows, tk), 0)
                c = lax.broadcasted_iota(jnp.int32, (rows, tk), 1)
                s = jnp.where(c <= r, s, NEG)
            m_prev = m_sc[hh, r0:, :]
            m_next = jnp.maximum(m_prev, jnp.max(s, axis=1, keepdims=True))
            alpha = jnp.exp2(m_prev - m_next)
            p = jnp.exp2(s - jnp.concatenate([m_next] * (tk // LANES), axis=1)).astype(BF16)
            pv = jnp.dot(p, v_ref[hh, pl.ds(off, tk), :], preferred_element_type=F32)
            acc_sc[hh, r0:, :] = alpha * acc_sc[hh, r0:, :] + pv
            m_sc[hh, r0:, :] = m_next

    def body(j, carry):
        tile(j, 0, False)
        return carry

    first = jnp.minimum(start_ref[i * FOX_HEADS + 2 * pair], start_ref[i * FOX_HEADS + 2 * pair + 1])
    lax.fori_loop(first, i * nsub, body, 0)
    for u in range(nsub):
        tile(i * nsub + u, u * tk, True)

    a0 = acc_sc[0]
    a1 = acc_sc[1]
    lane = lax.broadcasted_iota(jnp.int32, (tq, LANES), 1)
    o_ref[...] = jnp.where(lane < 64, a0 / a0[:, 64:65], a1 / a1[:, 0:1]).astype(o_ref.dtype)


def _fox_attention(start, qa, ka, va, *, tq, tk):
    S = qa.shape[1]
    grid_spec = pltpu.PrefetchScalarGridSpec(
        num_scalar_prefetch=1,
        grid=(FOX_HEADS // 2, S // tq),
        in_specs=[pl.BlockSpec((2, tq, LANES), lambda p, i, st: (p, i, 0)),
                  pl.BlockSpec((2, S, LANES), lambda p, i, st: (p, 0, 0)),
                  pl.BlockSpec((2, S, LANES), lambda p, i, st: (p, 0, 0))],
        out_specs=pl.BlockSpec((tq, LANES), lambda p, i, st: (i, p)),
        scratch_shapes=[pltpu.VMEM((2, tq, LANES), F32), pltpu.VMEM((2, tq, LANES), F32)],
    )
    return pl.pallas_call(
        functools.partial(_fox_kernel, tq=tq, tk=tk),
        grid_spec=grid_spec,
        out_shape=jax.ShapeDtypeStruct((S, FOX_WIDTH), BF16),
        compiler_params=_cparams(("arbitrary", "arbitrary")),
        name="fox_attn",
    )(start, qa, ka, va)


BIAS_KINDS = 2


def _bias_tile_kernel(table_ref, o_ref, *, t):
    h = pl.program_id(0)
    kind = pl.program_id(1)
    r = lax.broadcasted_iota(jnp.int32, (t, t), 0)
    c = lax.broadcasted_iota(jnp.int32, (t, t), 1)
    dist = r - c + kind * t
    n = jnp.maximum(dist, 0)
    max_exact = REL_BUCKETS // 2
    nf = jnp.maximum(n, 1).astype(F32)
    log_part = jnp.log(nf / max_exact) / math.log(REL_MAX_DISTANCE / max_exact) * (REL_BUCKETS - max_exact)
    large = jnp.minimum(max_exact + log_part.astype(jnp.int32), REL_BUCKETS - 1)
    bucket = jnp.where(n < max_exact, n, large)
    val = jnp.zeros((t, t), F32)
    for b in range(REL_BUCKETS):
        val = jnp.where(bucket == b, table_ref[h * REL_BUCKETS + b], val)
    far = table_ref[h * REL_BUCKETS + REL_BUCKETS - 1]
    o_ref[0, 0] = jnp.where(dist >= 0, (val - far) * LOG2E, NEG)


def _bias_tiles(table_flat, *, t):
    return pl.pallas_call(
        functools.partial(_bias_tile_kernel, t=t),
        grid=(DIFF_HEADS, BIAS_KINDS),
        in_specs=[pl.BlockSpec(memory_space=pltpu.SMEM)],
        out_specs=pl.BlockSpec((1, 1, t, t), lambda h, k: (h, k, 0, 0)),
        out_shape=jax.ShapeDtypeStruct((DIFF_HEADS, BIAS_KINDS, t, t), F32),
        compiler_params=_cparams(("arbitrary", "arbitrary")),
        name="bias_tiles",
    )(table_flat)


FIXED_REF_MAX_SHIFT = 64.0


def _diff_kernel(tab_ref, q_ref, k_ref, v_ref, bias_ref, kn_ref, lam_ref, g_ref, o_ref, m_sc, l_sc, acc_sc,
                 *, tq, tk, lam_init):
    h = pl.program_id(0)
    i = pl.program_id(1)
    m_sc[...] = jnp.full_like(m_sc, NEG)
    l_sc[...] = jnp.zeros_like(l_sc)
    acc_sc[...] = jnp.zeros_like(acc_sc)
    nsub = tq // tk

    lane = lax.broadcasted_iota(jnp.int32, (tq, LANES), 1)
    q = q_ref[...]
    zero = jnp.zeros_like(q)
    qs = (jnp.where(lane < 64, q, zero), jnp.where(lane >= 64, q, zero))

    def tile(j, r0, kinds):
        off = pl.multiple_of(j * tk, tk)
        k = k_ref[pl.ds(off, tk), :]
        v = v_ref[pl.ds(off, tk), :]
        for c in range(2):
            s = _nt_dot(qs[c][r0:], k)
            m_prev = m_sc[c, r0:, :]
            if kinds is not None:
                s = jnp.concatenate([s[a * tk:(a + 1) * tk] + bias_ref[0, kd] if kd < 2 else s[a * tk:(a + 1) * tk]
                                     for a, kd in enumerate(kinds)], axis=0)
            m_next = jnp.maximum(m_prev, jnp.max(s, axis=1, keepdims=True))
            p = jnp.exp2(s - jnp.concatenate([m_next] * (tk // LANES), axis=1))
            alpha = jnp.exp2(m_prev - m_next)
            l_sc[c, r0:, :] = alpha * l_sc[c, r0:, :] + sum(p[:, a * LANES:(a + 1) * LANES]
                                                            for a in range(tk // LANES))
            pv = jnp.dot(p.astype(BF16), v, preferred_element_type=F32)
            acc_sc[c, r0:, :] = alpha * acc_sc[c, r0:, :] + pv
            m_sc[c, r0:, :] = m_next

    def fixed_tile(j, carry):
        off = pl.multiple_of(j * tk, tk)
        k = k_ref[pl.ds(off, tk), :]
        v = v_ref[pl.ds(off, tk), :]
        for c in range(2):
            p = jnp.exp2(_nt_dot(qs[c], k) - jnp.concatenate([m_sc[c]] * (tk // LANES), axis=1))
            l_sc[c] += sum(p[:, a * LANES:(a + 1) * LANES] for a in range(tk // LANES))
            acc_sc[c] += jnp.dot(p.astype(BF16), v, preferred_element_type=F32)
        return carry

    def running_tile(j, carry):
        tile(j, 0, None)
        return carry

    @pl.when(i > 0)
    def _():
        tile(i * nsub - 1, 0, (1,) + (2,) * (nsub - 1))

    for u in range(nsub):
        tile(i * nsub + u, u * tk, ((0, 1) + (2,) * nsub)[:nsub - u])

    lane1 = lax.broadcasted_iota(jnp.int32, (1, LANES), 1)
    b0 = (tab_ref[h * REL_BUCKETS] - tab_ref[h * REL_BUCKETS + REL_BUCKETS - 1]) * LOG2E
    shifts = []
    for c in range(2):
        qf = qs[c].astype(F32)
        qn2 = jnp.max(jnp.sum(qf * qf, axis=1, keepdims=True), axis=0, keepdims=True)
        kn2 = jnp.sum(jnp.where(lane1 == 2 * h + c, kn_ref[0:1, :], 0.0), axis=1, keepdims=True)
        shifts.append(jnp.ceil(2.0 * jnp.sqrt(qn2 * kn2) - b0 + 1.0))
    fits = jnp.max(jnp.maximum(shifts[0], shifts[1])) <= FIXED_REF_MAX_SHIFT
    n_far = i * nsub - 1

    @pl.when(fits)
    def _():
        for c in range(2):
            down = jnp.exp2(-shifts[c])
            m_sc[c] = m_sc[c] + shifts[c]
            l_sc[c] = l_sc[c] * down
            acc_sc[c] = acc_sc[c] * down
        lax.fori_loop(0, n_far, fixed_tile, 0)

    @pl.when(jnp.logical_not(fits))
    def _():
        lax.fori_loop(0, n_far, running_tile, 0)

    lv = lam_ref[...]
    lam = (jnp.exp(jnp.sum(lv[0:1] * lv[1:2], axis=1, keepdims=True))
           - jnp.exp(jnp.sum(lv[2:3] * lv[3:4], axis=1, keepdims=True)) + lam_init)
    l0 = jnp.sum(l_sc[0], axis=1, keepdims=True)
    l1 = jnp.sum(l_sc[1], axis=1, keepdims=True)
    o = acc_sc[0] / l0 - lam * (acc_sc[1] / l1)
    o = o * lax.rsqrt(jnp.mean(o * o, axis=1, keepdims=True) + SUBLN_EPS) * g_ref[...] * (1.0 - lam_init)
    o_ref[...] = o.astype(o_ref.dtype)


def _diff_attention(table_flat, dq, dk, dv, bias, dkn, lamvec, g, *, tq, tk, lam_init):
    S = dq.shape[0]
    grid_spec = pltpu.PrefetchScalarGridSpec(
        num_scalar_prefetch=1,
        grid=(DIFF_HEADS, S // tq),
        in_specs=[pl.BlockSpec((tq, LANES), lambda h, i, tab: (i, h)),
                  pl.BlockSpec((S, LANES), lambda h, i, tab: (0, h)),
                  pl.BlockSpec((S, LANES), lambda h, i, tab: (0, h)),
                  pl.BlockSpec((1, BIAS_KINDS, tk, tk), lambda h, i, tab: (h, 0, 0, 0)),
                  pl.BlockSpec((8, LANES), lambda h, i, tab: (0, 0)),
                  pl.BlockSpec((8, LANES), lambda h, i, tab: (0, 0)),
                  pl.BlockSpec((1, LANES), lambda h, i, tab: (0, 0))],
        out_specs=pl.BlockSpec((tq, LANES), lambda h, i, tab: (i, h)),
        scratch_shapes=[pltpu.VMEM((2, tq, LANES), F32), pltpu.VMEM((2, tq, LANES), F32),
                        pltpu.VMEM((2, tq, LANES), F32)],
    )
    return pl.pallas_call(
        functools.partial(_diff_kernel, tq=tq, tk=tk, lam_init=lam_init),
        grid_spec=grid_spec,
        out_shape=jax.ShapeDtypeStruct((S, DIFF_V_WIDTH), BF16),
        compiler_params=_cparams(("arbitrary", "arbitrary")),
        name="diff_attn",
    )(table_flat, dq, dk, dv, bias, dkn, lamvec, g)


def _merge_kernel(yf_ref, yd_ref, sa_ref, sb_ref, h_ref, wf_ref, wd_ref, wo_ref, g_ref, b_ref, o_ref, *, alpha):
    bf = jnp.dot(yf_ref[...], wf_ref[...], preferred_element_type=F32)
    bd = jnp.dot(yd_ref[...], wd_ref[...], preferred_element_type=F32)
    merged = sa_ref[...].astype(F32) * bf + sb_ref[...].astype(F32) * bd
    mix = jnp.dot(merged.astype(BF16), wo_ref[...], preferred_element_type=F32)
    o_ref[...] = _layer_norm(alpha * h_ref[...] + mix, g_ref[...], b_ref[...])


def _merge(yf, yd, sa, sb, h, wf, wd, wo, g, b, *, alpha, tm):
    S = h.shape[0]
    row = lambda n: pl.BlockSpec((tm, n), lambda i: (i, 0))
    full = lambda a: pl.BlockSpec(a.shape, lambda i: (0,) * a.ndim)
    return pl.pallas_call(
        functools.partial(_merge_kernel, alpha=alpha),
        grid=(S // tm,),
        in_specs=[row(512), row(512), row(D_MODEL), row(D_MODEL), row(D_MODEL),
                  full(wf), full(wd), full(wo), full(g), full(b)],
        out_specs=row(D_MODEL),
        out_shape=jax.ShapeDtypeStruct((S, D_MODEL), F32),
        compiler_params=_cparams(("arbitrary",)),
        name="merge_out",
    )(yf, yd, sa, sb, h, wf, wd, wo, g, b)


MOE_ROWS = 272
SCATTER_ROWS = 256
RANK_CHUNK = 256


def _router_kernel(h_ref, whi_ref, wlo_ref, cw_ref, rank_ref, rank_t_ref, cnt_ref):
    tm = h_ref.shape[0]
    h = h_ref[...]
    hhi = h.astype(BF16)
    hlo = (h - hhi.astype(F32)).astype(BF16)
    whi = whi_ref[...]
    dot = lambda a, b: jnp.dot(a, b, preferred_element_type=F32)
    logits = dot(hhi, whi) + dot(hhi, wlo_ref[...]) + dot(hlo, whi)
    lane = lax.broadcasted_iota(jnp.int32, logits.shape, 1).astype(F32)
    lg = jnp.where(lane < N_EXPERTS, logits, NEG)
    m1 = jnp.max(lg, axis=1, keepdims=True)
    i1 = jnp.min(jnp.where(lg == m1, lane, float(LANES)), axis=1, keepdims=True)
    lg2 = jnp.where(lane == i1, NEG, lg)
    m2 = jnp.max(lg2, axis=1, keepdims=True)
    i2 = jnp.min(jnp.where(lg2 == m2, lane, float(LANES)), axis=1, keepdims=True)
    e = jnp.exp(m2 - m1)
    g1 = 1.0 / (1.0 + e)
    chosen = jnp.logical_or(lane == i1, lane == i2)
    cw_ref[...] = jnp.where(lane == i1, g1, 0.0) + jnp.where(lane == i2, e * g1, 0.0)

    a = chosen.astype(BF16)
    r = lax.broadcasted_iota(jnp.int32, (RANK_CHUNK, RANK_CHUNK), 0)
    c = lax.broadcasted_iota(jnp.int32, (RANK_CHUNK, RANK_CHUNK), 1)
    strict_lower = (c < r).astype(BF16)
    ones_row = jnp.ones((8, RANK_CHUNK), BF16)
    carry = jnp.zeros((1, LANES), F32)
    for q in range(tm // RANK_CHUNK):
        rows = slice(q * RANK_CHUNK, (q + 1) * RANK_CHUNK)
        rank = jnp.where(chosen[rows], dot(strict_lower, a[rows]) + carry, -1.0)
        rank_ref[rows, :] = rank
        rank_t_ref[0, :, rows] = jnp.transpose(rank)[0:N_EXPERTS, :]
        carry = carry + dot(ones_row, a[rows])[0:1, :]
    cnt_ref[0] = jnp.broadcast_to(carry, (8, LANES)).astype(jnp.int32)


def _router(h, whi, wlo, *, tm):
    S = h.shape[0]
    nb = S // tm
    return pl.pallas_call(
        _router_kernel,
        grid=(nb,),
        in_specs=[pl.BlockSpec((tm, D_MODEL), lambda i: (i, 0)),
                  pl.BlockSpec(whi.shape, lambda i: (0, 0)), pl.BlockSpec(wlo.shape, lambda i: (0, 0))],
        out_specs=[pl.BlockSpec((tm, LANES), lambda i: (i, 0)),
                   pl.BlockSpec((tm, LANES), lambda i: (i, 0)),
                   pl.BlockSpec((1, N_EXPERTS, tm), lambda i: (i, 0, 0)),
                   pl.BlockSpec((1, 8, LANES), lambda i: (i, 0, 0))],
        out_shape=[jax.ShapeDtypeStruct((S, LANES), F32),
                   jax.ShapeDtypeStruct((S, LANES), F32),
                   jax.ShapeDtypeStruct((nb, N_EXPERTS, tm), F32),
                   jax.ShapeDtypeStruct((nb, 8, LANES), jnp.int32)],
        compiler_params=_cparams(("arbitrary",)),
        name="router",
    )(h, whi, wlo)


def _moe_kernel(cnt_ref, h_ref, cw_ref, rank_ref, rank_t_ref, wg_ref, wu_ref, wd_ref, g_ref, b_ref, o_ref,
                hb_sc, xs_sc, acc_sc, *, alpha):
    tm = h_ref.shape[0]
    i = pl.program_id(0)
    e = pl.program_id(1)
    c = pl.program_id(2)
    last_c = c == pl.num_programs(2) - 1
    n_rows = cnt_ref[i * N_EXPERTS + e]
    n_steps = (n_rows + MOE_ROWS - 1) // MOE_ROWS
    n_scatter = (n_rows + SCATTER_ROWS - 1) // SCATTER_ROWS

    @pl.when(jnp.logical_and(e == 0, c == 0))
    def _():
        hb_sc[...] = h_ref[...].astype(BF16)
        o_ref[...] = jnp.zeros_like(o_ref)

    @pl.when(c == 0)
    def _():
        rank_row = rank_t_ref[0, pl.ds(e, 1), :]

        def gather(t, carry):
            r0 = pl.multiple_of(t * MOE_ROWS, 16)
            slot = (lax.broadcasted_iota(jnp.int32, (MOE_ROWS, tm), 0) + r0).astype(F32)
            onehot = (rank_row == slot).astype(BF16)
            xs_sc[pl.ds(r0, MOE_ROWS), :] = jnp.dot(onehot, hb_sc[...], preferred_element_type=F32).astype(BF16)
            acc_sc[pl.ds(r0, MOE_ROWS), :] = jnp.zeros((MOE_ROWS, D_MODEL), F32)
            return carry

        lax.fori_loop(0, n_steps, gather, 0)

        @pl.when(n_steps * MOE_ROWS < n_scatter * SCATTER_ROWS)
        def _():
            acc_sc[pl.ds(pl.multiple_of(n_steps * MOE_ROWS, 16), SCATTER_ROWS), :] = jnp.zeros(
                (SCATTER_ROWS, D_MODEL), F32)

    def swiglu(t, carry):
        r0 = pl.multiple_of(t * MOE_ROWS, 16)
        x = xs_sc[pl.ds(r0, MOE_ROWS), :]
        gate = jnp.dot(x, wg_ref[0], preferred_element_type=F32)
        up = jnp.dot(x, wu_ref[0], preferred_element_type=F32)
        a = (gate * _sigmoid(gate) * up).astype(BF16)
        acc_sc[pl.ds(r0, MOE_ROWS), :] += jnp.dot(a, wd_ref[0], preferred_element_type=F32)
        return carry

    lax.fori_loop(0, n_steps, swiglu, 0)

    @pl.when(last_c)
    def _():
        lane = lax.broadcasted_iota(jnp.int32, (tm, LANES), 1)
        pick = lambda ref: jnp.broadcast_to(
            jnp.sum(jnp.where(lane == e, ref[...], 0.0), axis=1, keepdims=True), (tm, LANES))
        rank_col = pick(rank_ref)
        gate_col = jnp.concatenate([pick(cw_ref)] * (D_MODEL // LANES), axis=1)

        def scatter(t, carry):
            r0 = pl.multiple_of(t * SCATTER_ROWS, SCATTER_ROWS)
            slot = (lane + r0).astype(F32)
            onehot = jnp.concatenate([(rank_col == slot + float(a * LANES)) for a in range(SCATTER_ROWS // LANES)],
                                     axis=1).astype(BF16)
            y = acc_sc[pl.ds(r0, SCATTER_ROWS), :].astype(BF16)
            o_ref[...] += gate_col * jnp.dot(onehot, y, preferred_element_type=F32)
            return carry

        lax.fori_loop(0, n_scatter, scatter, 0)

    @pl.when(jnp.logical_and(e == pl.num_programs(1) - 1, last_c))
    def _():
        o_ref[...] = _layer_norm(alpha * h_ref[...] + o_ref[...], g_ref[...], b_ref[...])


def _moe(h, cw, rank, rank_t, counts, w_gu, w_dn, g, b, *, alpha, tm, fc):
    S = h.shape[0]
    E, _, F2 = w_gu.shape
    nc = F2 // 2 // fc
    once = pl.Buffered(1)
    rows_cap = pl.cdiv(tm, MOE_ROWS) * MOE_ROWS
    grid_spec = pltpu.PrefetchScalarGridSpec(
        num_scalar_prefetch=1,
        grid=(S // tm, E, nc),
        in_specs=[pl.BlockSpec((tm, D_MODEL), lambda i, e, c, cnt: (i, 0), pipeline_mode=once),
                  pl.BlockSpec((tm, LANES), lambda i, e, c, cnt: (i, 0), pipeline_mode=once),
                  pl.BlockSpec((tm, LANES), lambda i, e, c, cnt: (i, 0), pipeline_mode=once),
                  pl.BlockSpec((1, N_EXPERTS, tm), lambda i, e, c, cnt: (i, 0, 0), pipeline_mode=once),
                  pl.BlockSpec((1, D_MODEL, fc), lambda i, e, c, cnt: (e, 0, c)),
                  pl.BlockSpec((1, D_MODEL, fc), lambda i, e, c, cnt: (e, 0, nc + c)),
                  pl.BlockSpec((1, fc, D_MODEL), lambda i, e, c, cnt: (e, c, 0)),
                  pl.BlockSpec(g.shape, lambda i, e, c, cnt: (0, 0)),
                  pl.BlockSpec(b.shape, lambda i, e, c, cnt: (0, 0))],
        out_specs=pl.BlockSpec((tm, D_MODEL), lambda i, e, c, cnt: (i, 0), pipeline_mode=once),
        scratch_shapes=[pltpu.VMEM((tm, D_MODEL), BF16), pltpu.VMEM((rows_cap, D_MODEL), BF16),
                        pltpu.VMEM((rows_cap, D_MODEL), F32)],
    )
    return pl.pallas_call(
        functools.partial(_moe_kernel, alpha=alpha),
        grid_spec=grid_spec,
        out_shape=jax.ShapeDtypeStruct((S, D_MODEL), F32),
        compiler_params=_cparams(("arbitrary", "arbitrary", "arbitrary")),
        name="moe_ffn",
    )(counts, h, cw, rank, rank_t, w_gu, w_gu, w_dn, g, b)


def _ffn_kernel(h_ref, wgu_ref, wd_ref, g_ref, b_ref, o_ref, *, alpha, fc):
    h = h_ref[...]
    hb = h.astype(BF16)
    d_ff = wd_ref.shape[0]
    acc = None
    for c in range(d_ff // fc):
        gate = jnp.dot(hb, wgu_ref[:, c * fc:(c + 1) * fc], preferred_element_type=F32)
        up = jnp.dot(hb, wgu_ref[:, d_ff + c * fc:d_ff + (c + 1) * fc], preferred_element_type=F32)
        a = (gate * _sigmoid(gate) * up).astype(BF16)
        y = jnp.dot(a, wd_ref[c * fc:(c + 1) * fc, :], preferred_element_type=F32)
        acc = y if acc is None else acc + y
    o_ref[...] = _layer_norm(alpha * h + acc, g_ref[...], b_ref[...])


def _ffn(h, w_gu, w_dn, g, b, *, alpha, tm, fc):
    S = h.shape[0]
    once = pl.Buffered(1)
    return pl.pallas_call(
        functools.partial(_ffn_kernel, alpha=alpha, fc=fc),
        grid=(S // tm,),
        in_specs=[pl.BlockSpec((tm, D_MODEL), lambda i: (i, 0)),
                  pl.BlockSpec(w_gu.shape, lambda i: (0, 0), pipeline_mode=once),
                  pl.BlockSpec(w_dn.shape, lambda i: (0, 0), pipeline_mode=once),
                  pl.BlockSpec(g.shape, lambda i: (0, 0)), pl.BlockSpec(b.shape, lambda i: (0, 0))],
        out_specs=pl.BlockSpec((tm, D_MODEL), lambda i: (i, 0)),
        out_shape=jax.ShapeDtypeStruct((S, D_MODEL), F32),
        compiler_params=_cparams(("arbitrary",)),
        name="dense_ffn",
    )(h, w_gu, w_dn, g, b)


def _pad_lanes(a, n=LANES):
    return jnp.pad(a, [(0, 0)] * (a.ndim - 1) + [(0, n - a.shape[-1])])


def kernel(x, ln_in_g, ln_in_b, w_in, b_fgate, lam_q1, lam_k1, lam_q2, lam_k2, subln_g, w_branch_fox, w_branch_diff, w_out, ln_mix_g, ln_mix_b, rel_bias, w_ffn_gate_up, w_ffn_down, w_router, w_expert_gate_up, w_expert_down, ln_ffn_g, ln_ffn_b):
    B, S, _ = x.shape
    assert B == 1
    depth = w_in.shape[0]
    alpha = (2.0 * depth) ** 0.25
    tq = min(ATTN_TQ, S)
    tk = min(ATTN_TK, tq)
    tm = min(512, S)
    tf = min(1024, S)
    te = min(MOE_TM, S)
    assert tk >= REL_MAX_DISTANCE and tq % tk == 0 and S % tq == 0 and S % tf == 0 and S % tm == 0 and S % te == 0
    assert te % SCATTER_ROWS == 0 and te % RANK_CHUNK == 0 and MOE_ROWS % 16 == 0
    row = lambda v: v.reshape(1, -1).astype(F32)

    table_flat = rel_bias.astype(F32).T.reshape(-1)
    bias = _bias_tiles(table_flat, t=tk)

    h = x[0]
    for l in range(depth):
        lam_init = 0.8 - 0.6 * math.exp(-0.3 * l)
        fq, fk, fv, fg, dq, dk, dv, ga, gb = jnp.split(w_in[l], list(_cumsum(IN_SPLIT_SIZES))[:-1], axis=-1)
        w_cat = jnp.concatenate([fq, fk, fv, dq, dk, dv, ga, gb, _pad_lanes(fg)], axis=-1).astype(BF16)
        outs = _inproj(h, row(ln_in_g), row(ln_in_b), w_cat, _pad_lanes(row(b_fgate[l])), do_ln=(l == 0), tm=tk)
        qa, ka, va, stats, dq, dk, dv, sa, sb, dkn = outs[:10]
        if l == 0:
            h = outs[10]
        start = _plan(stats, nsub=tq // tk)[:S // tq, :FOX_HEADS].reshape(-1)
        y_fox = _fox_attention(start, qa, ka, va, tq=tq, tk=tk)
        lamvec = jnp.zeros((8, LANES), F32).at[0:4, 0:HEAD_DIM].set(
            jnp.stack([lam_q1[l], lam_k1[l], lam_q2[l], lam_k2[l]]).astype(F32))
        y_diff = _diff_attention(table_flat, dq, dk, dv, bias, dkn, lamvec, row(subln_g[l]), tq=tq, tk=tk,
                                 lam_init=lam_init)
        h = _merge(y_fox, y_diff, sa, sb, h, w_branch_fox[l].astype(BF16), w_branch_diff[l].astype(BF16),
                   w_out[l].astype(BF16), row(ln_mix_g[l]), row(ln_mix_b[l]), alpha=alpha, tm=tm)
        g, b = row(ln_ffn_g[l]), row(ln_ffn_b[l])
        if l % 2 == 0:
            h = _ffn(h, w_ffn_gate_up[l // 2].astype(BF16), w_ffn_down[l // 2].astype(BF16), g, b,
                     alpha=alpha, tm=tf, fc=1408)
        else:
            wr = _pad_lanes(w_router[l // 2].astype(F32))
            whi = wr.astype(BF16)
            wlo = (wr - whi.astype(F32)).astype(BF16)
            cw, rank, rank_t, counts = _router(h, whi, wlo, tm=te)
            h = _moe(h, cw, rank, rank_t, counts[:, 0, :N_EXPERTS].reshape(-1),
                     w_expert_gate_up[l // 2].astype(BF16), w_expert_down[l // 2].astype(BF16),
                     g, b, alpha=alpha, tm=te, fc=896)
    return h[None]


def _cumsum(sizes):
    tot = 0
    for s in sizes:
        tot += s
        yield tot
```

```python
import functools
import math

import jax
import jax.numpy as jnp
from jax import lax
from jax.experimental import pallas as pl
from jax.experimental.pallas import tpu as pltpu

F32 = jnp.float32
BF16 = jnp.bfloat16

D_MODEL = 1024
HEAD_DIM = 64
FOX_HEADS = 8
FOX_WIDTH = FOX_HEADS * HEAD_DIM
DIFF_HEADS = 4
DIFF_QK_WIDTH = DIFF_HEADS * 2 * HEAD_DIM
DIFF_V_DIM = 2 * HEAD_DIM
DIFF_V_WIDTH = DIFF_HEADS * DIFF_V_DIM
IN_SPLIT_SIZES = (FOX_WIDTH, FOX_WIDTH, FOX_WIDTH, FOX_HEADS, DIFF_QK_WIDTH, DIFF_QK_WIDTH, DIFF_V_WIDTH,
                  D_MODEL, D_MODEL)
REL_BUCKETS = 32
REL_MAX_DISTANCE = 128
N_EXPERTS = 8
LN_EPS = 1e-5
SUBLN_EPS = 1e-5

LANES = 128
LOG2E = 1.4426950408889634
Q_SCALE = HEAD_DIM ** -0.5 * LOG2E
NEG = -1e30
VMEM_LIMIT = 56 * 1024 * 1024
ATTN_TQ = 2048
ATTN_TK = 512
MOE_TM = 2048


def _cparams(sem):
    return pltpu.CompilerParams(dimension_semantics=sem, vmem_limit_bytes=VMEM_LIMIT)


def _layer_norm(x, g, b):
    mu = jnp.mean(x, axis=-1, keepdims=True)
    xc = x - mu
    var = jnp.mean(xc * xc, axis=-1, keepdims=True)
    return xc * lax.rsqrt(var + LN_EPS) * g + b


def _sigmoid(x):
    return 1.0 / (1.0 + jnp.exp(-x))


def _split3(x):
    hi = x.astype(BF16)
    r1 = x - hi.astype(F32)
    mid = r1.astype(BF16)
    lo = (r1 - mid.astype(F32)).astype(BF16)
    return hi, mid, lo


_C_FQ, _C_FK, _C_FV, _C_DQ, _C_DK, _C_DV, _C_GA, _C_GB, _C_FG, _C_END = (
    0, 512, 1024, 1536, 2048, 2560, 3072, 4096, 5120, 5248)


def _inproj_kernel(x_ref, g_ref, b_ref, w_ref, bf_ref, qa_ref, ka_ref, va_ref, st_ref, dq, dk, dv, sa, sb, dkn,
                   *rest, do_ln):
    carry_ref = rest[-1]
    tb = x_ref.shape[0]

    @pl.when(pl.program_id(0) == 0)
    def _():
        carry_ref[...] = jnp.zeros_like(carry_ref)

    x = x_ref[...]
    if do_ln:
        x = _layer_norm(x, g_ref[...], b_ref[...])
        rest[0][...] = x
    xb = x.astype(BF16)

    def mm(a, b):
        return jnp.dot(xb, w_ref[:, a:b], preferred_element_type=F32)

    fq = (mm(_C_FQ, _C_FK) * Q_SCALE).astype(BF16)
    fk = mm(_C_FK, _C_FV).astype(BF16)
    fv = mm(_C_FV, _C_DQ).astype(BF16)

    z = mm(_C_FG, _C_END) + bf_ref[...]
    logf = jnp.minimum(z, 0.0) - jnp.log1p(jnp.exp(-jnp.abs(z)))
    r = lax.broadcasted_iota(jnp.int32, (tb, tb), 0)
    c = lax.broadcasted_iota(jnp.int32, (tb, tb), 1)
    tri = (c <= r).astype(BF16)
    hi, mid, lo = _split3(logf)
    dot = lambda a: jnp.dot(tri, a, preferred_element_type=F32)
    cum = (dot(hi) + dot(mid)) + dot(lo) + carry_ref[0:1, :]
    carry_ref[0:1, :] = cum[tb - 1:tb, :]

    c_hi, c_mid, c_lo = (t.astype(F32) for t in _split3(cum * LOG2E))
    lane = lax.broadcasted_iota(jnp.int32, (tb, LANES), 1)
    one = jnp.ones((tb, LANES), F32)
    zero = jnp.zeros((tb, LANES), F32)
    lane1 = lax.broadcasted_iota(jnp.int32, (1, LANES), 1)
    q_norm2 = jnp.zeros((1, LANES), F32)
    k_norm2 = jnp.zeros((1, LANES), F32)

    def max_norm2(slab):
        n2 = jnp.sum(jnp.where(lane < 64, slab * slab, 0.0), axis=1, keepdims=True)
        return jnp.max(n2, axis=0, keepdims=True)

    for h in range(FOX_HEADS):
        p, odd = divmod(h, 2)
        col = lambda t: jnp.broadcast_to(t[:, h:h + 1], (tb, LANES))
        chi, cmid, clo = col(c_hi), col(c_mid), col(c_lo)

        def head_slab(a):
            slab = a[:, p * LANES:(p + 1) * LANES].astype(F32)
            return pltpu.roll(slab, HEAD_DIM, 1) if odd else slab

        qslab = head_slab(fq)
        kslab = head_slab(fk)
        q_norm2 = jnp.where(lane1 == h, max_norm2(qslab), q_norm2)
        k_norm2 = jnp.where(lane1 == h, max_norm2(kslab), k_norm2)
        qa = jnp.where(lane < 64, qslab,
                       jnp.where(lane == 64, chi, jnp.where(lane == 65, cmid, jnp.where(lane == 66, clo,
                                 jnp.where(lane < 70, one, zero)))))
        ka = jnp.where(lane < 64, kslab,
                       jnp.where(lane < 67, one, jnp.where(lane == 67, -chi, jnp.where(lane == 68, -cmid,
                                 jnp.where(lane == 69, -clo, zero)))))
        qa_ref[h] = qa.astype(BF16)
        ka_ref[h] = ka.astype(BF16)
        vslab = fv[:, p * LANES:(p + 1) * LANES].astype(F32)
        if odd:
            va = jnp.where(lane >= 64, vslab, jnp.where(lane == 0, one, zero))
        else:
            va = jnp.where(lane < 64, vslab, jnp.where(lane == 64, one, zero))
        va_ref[h] = va.astype(BF16)

    dq[...] = (mm(_C_DQ, _C_DK) * Q_SCALE).astype(BF16)
    dk16 = mm(_C_DK, _C_DV).astype(BF16)
    dk[...] = dk16
    dv[...] = mm(_C_DV, _C_GA).astype(BF16)
    sa[...] = _sigmoid(mm(_C_GA, _C_GB)).astype(BF16)
    sb[...] = _sigmoid(mm(_C_GB, _C_FG)).astype(BF16)

    dk_norm2 = carry_ref[1:2, :]
    for h in range(DIFF_HEADS):
        slab = dk16[:, h * LANES:(h + 1) * LANES].astype(F32)
        sq = slab * slab
        for c in range(2):
            n2 = jnp.sum(jnp.where((lane < 64) == (c == 0), sq, 0.0), axis=1, keepdims=True)
            dk_norm2 = jnp.where(lane1 == 2 * h + c, jnp.maximum(dk_norm2, jnp.max(n2, axis=0, keepdims=True)),
                                 dk_norm2)
    carry_ref[1:2, :] = dk_norm2
    dkn[...] = jnp.broadcast_to(dk_norm2, dkn.shape)

    st_ref[0] = jnp.concatenate([q_norm2, k_norm2, cum[tb - 1:tb, :] * LOG2E, jnp.zeros((5, LANES), F32)], axis=0)


def _inproj(x, g, b, w, bf, *, do_ln, tm):
    S = x.shape[0]
    row = lambda n: pl.BlockSpec((tm, n), lambda i: (i, 0))
    full = lambda a: pl.BlockSpec(a.shape, lambda i: (0,) * a.ndim)
    hspec = pl.BlockSpec((FOX_HEADS, tm, LANES), lambda i: (0, i, 0))
    hshape = jax.ShapeDtypeStruct((FOX_HEADS, S, LANES), BF16)
    out_shape = [hshape] * 3 + [jax.ShapeDtypeStruct((S // tm, 8, LANES), F32)] + [
        jax.ShapeDtypeStruct((S, 512), BF16)] * 3 + [jax.ShapeDtypeStruct((S, D_MODEL), BF16)] * 2 + [
        jax.ShapeDtypeStruct((8, LANES), F32)]
    out_specs = [hspec] * 3 + [pl.BlockSpec((1, 8, LANES), lambda i: (i, 0, 0))] + [row(512)] * 3 + [
        row(D_MODEL)] * 2 + [pl.BlockSpec((8, LANES), lambda i: (0, 0))]
    if do_ln:
        out_shape.append(jax.ShapeDtypeStruct((S, D_MODEL), F32))
        out_specs.append(row(D_MODEL))
    return pl.pallas_call(
        functools.partial(_inproj_kernel, do_ln=do_ln),
        grid=(S // tm,),
        in_specs=[row(D_MODEL), full(g), full(b), full(w), full(bf)],
        out_specs=out_specs,
        out_shape=out_shape,
        scratch_shapes=[pltpu.VMEM((8, LANES), F32)],
        compiler_params=_cparams(("arbitrary",)),
        name="inproj",
    )(x, g, b, w, bf)


SKIP_LOG2 = 150.0


def _plan_kernel(st_ref, start_ref, *, nt, nsub):
    qn = jnp.sqrt(st_ref[pl.ds(0, nt, stride=8), :])
    kn = jnp.sqrt(st_ref[pl.ds(1, nt, stride=8), :])
    cum_end = st_ref[pl.ds(2, nt, stride=8), :]
    tile = lax.broadcasted_iota(jnp.int32, (nt, LANES), 0)
    rows = []
    for i in range(nt // nsub):
        own = jnp.logical_and(tile >= i * nsub, tile < (i + 1) * nsub)
        q_max = jnp.max(jnp.where(own, qn, 0.0), axis=0, keepdims=True)
        k_own = jnp.max(jnp.where(own, kn, 0.0), axis=0, keepdims=True)
        cum_q = cum_end[i * nsub - 1:i * nsub, :] if i > 0 else jnp.zeros((1, LANES), F32)
        upper = q_max * kn + (cum_q - cum_end) + q_max * k_own
        slack = SKIP_LOG2 + 1e-4 * jnp.abs(cum_end)
        needed = jnp.logical_and(upper >= -slack, tile < i * nsub)
        rows.append(jnp.min(jnp.where(needed, tile, i * nsub), axis=0, keepdims=True))
    rows += [jnp.zeros((1, LANES), jnp.int32)] * (8 - len(rows))
    start_ref[...] = jnp.concatenate(rows, axis=0)


def _plan(stats, *, nsub):
    nt = stats.shape[0]
    assert nt // nsub <= 8
    return pl.pallas_call(
        functools.partial(_plan_kernel, nt=nt, nsub=nsub),
        out_shape=jax.ShapeDtypeStruct((8, LANES), jnp.int32),
        compiler_params=pltpu.CompilerParams(vmem_limit_bytes=VMEM_LIMIT),
        name="fox_plan",
    )(stats.reshape(nt * 8, LANES))


def _nt_dot(a, b):
    return lax.dot_general(a, b, (((1,), (1,)), ((), ())), preferred_element_type=F32)


def _fox_kernel(start_ref, q_ref, k_ref, v_ref, o_ref, m_sc, acc_sc, *, tq, tk):
    i = pl.program_id(1)
    pair = pl.program_id(0)
    m_sc[...] = jnp.full_like(m_sc, NEG)
    acc_sc[...] = jnp.zeros_like(acc_sc)
    nsub = tq // tk

    def tile(j, r0, masked):
        off = pl.multiple_of(j * tk, tk)
        rows = tq - r0
        for hh in range(2):
            s = _nt_dot(q_ref[hh, r0:, :], k_ref[hh, pl.ds(off, tk), :])
            if masked:
                r = lax.broadcasted_iota(jnp.int32, (rows, tk), 0)
                c = lax.broadcasted_iota(jnp.int32, (rows, tk), 1)
                s = jnp.where(c <= r, s, NEG)
            m_prev = m_sc[hh, r0:, :]
            m_next = jnp.maximum(m_prev, jnp.max(s, axis=1, keepdims=True))
            alpha = jnp.exp2(m_prev - m_next)
            p = jnp.exp2(s - jnp.concatenate([m_next] * (tk // LANES), axis=1)).astype(BF16)
            pv = jnp.dot(p, v_ref[hh, pl.ds(off, tk), :], preferred_element_type=F32)
            acc_sc[hh, r0:, :] = alpha * acc_sc[hh, r0:, :] + pv
            m_sc[hh, r0:, :] = m_next

    def body(j, carry):
        tile(j, 0, False)
        return carry

    first = jnp.minimum(start_ref[i * FOX_HEADS + 2 * pair], start_ref[i * FOX_HEADS + 2 * pair + 1])
    lax.fori_loop(first, i * nsub, body, 0)
    for u in range(nsub):
        tile(i * nsub + u, u * tk, True)

    a0 = acc_sc[0]
    a1 = acc_sc[1]
    lane = lax.broadcasted_iota(jnp.int32, (tq, LANES), 1)
    o_ref[...] = jnp.where(lane < 64, a0 / a0[:, 64:65], a1 / a1[:, 0:1]).astype(o_ref.dtype)


def _fox_attention(start, qa, ka, va, *, tq, tk):
    S = qa.shape[1]
    grid_spec = pltpu.PrefetchScalarGridSpec(
        num_scalar_prefetch=1,
        grid=(FOX_HEADS // 2, S // tq),
        in_specs=[pl.BlockSpec((2, tq, LANES), lambda p, i, st: (p, i, 0)),
                  pl.BlockSpec((2, S, LANES), lambda p, i, st: (p, 0, 0)),
                  pl.BlockSpec((2, S, LANES), lambda p, i, st: (p, 0, 0))],
        out_specs=pl.BlockSpec((tq, LANES), lambda p, i, st: (i, p)),
        scratch_shapes=[pltpu.VMEM((2, tq, LANES), F32), pltpu.VMEM((2, tq, LANES), F32)],
    )
    return pl.pallas_call(
        functools.partial(_fox_kernel, tq=tq, tk=tk),
        grid_spec=grid_spec,
        out_shape=jax.ShapeDtypeStruct((S, FOX_WIDTH), BF16),
        compiler_params=_cparams(("arbitrary", "arbitrary")),
        name="fox_attn",
    )(start, qa, ka, va)


BIAS_KINDS = 2


def _bias_tile_kernel(table_ref, o_ref, *, t):
    h = pl.program_id(0)
    kind = pl.program_id(1)
    r = lax.broadcasted_iota(jnp.int32, (t, t), 0)
    c = lax.broadcasted_iota(jnp.int32, (t, t), 1)
    dist = r - c + kind * t
    n = jnp.maximum(dist, 0)
    max_exact = REL_BUCKETS // 2
    nf = jnp.maximum(n, 1).astype(F32)
    log_part = jnp.log(nf / max_exact) / math.log(REL_MAX_DISTANCE / max_exact) * (REL_BUCKETS - max_exact)
    large = jnp.minimum(max_exact + log_part.astype(jnp.int32), REL_BUCKETS - 1)
    bucket = jnp.where(n < max_exact, n, large)
    val = jnp.zeros((t, t), F32)
    for b in range(REL_BUCKETS):
        val = jnp.where(bucket == b, table_ref[h * REL_BUCKETS + b], val)
    far = table_ref[h * REL_BUCKETS + REL_BUCKETS - 1]
    o_ref[0, 0] = jnp.where(dist >= 0, (val - far) * LOG2E, NEG)


def _bias_tiles(table_flat, *, t):
    return pl.pallas_call(
        functools.partial(_bias_tile_kernel, t=t),
        grid=(DIFF_HEADS, BIAS_KINDS),
        in_specs=[pl.BlockSpec(memory_space=pltpu.SMEM)],
        out_specs=pl.BlockSpec((1, 1, t, t), lambda h, k: (h, k, 0, 0)),
        out_shape=jax.ShapeDtypeStruct((DIFF_HEADS, BIAS_KINDS, t, t), F32),
        compiler_params=_cparams(("arbitrary", "arbitrary")),
        name="bias_tiles",
    )(table_flat)


FIXED_REF_MAX_SHIFT = 64.0


def _diff_kernel(tab_ref, q_ref, k_ref, v_ref, bias_ref, kn_ref, lam_ref, g_ref, o_ref, m_sc, l_sc, acc_sc,
                 *, tq, tk, lam_init):
    h = pl.program_id(0)
    i = pl.program_id(1)
    l_sc[...] = jnp.zeros_like(l_sc)
    acc_sc[...] = jnp.zeros_like(acc_sc)
    nsub = tq // tk

    lane = lax.broadcasted_iota(jnp.int32, (tq, LANES), 1)
    q = q_ref[...]
    zero = jnp.zeros_like(q)
    qs = (jnp.where(lane < 64, q, zero), jnp.where(lane >= 64, q, zero))

    def run(fixed):
        def tile(j, r0, kinds):
            off = pl.multiple_of(j * tk, tk)
            k = k_ref[pl.ds(off, tk), :]
            v = v_ref[pl.ds(off, tk), :]
            for c in range(2):
                s = _nt_dot(qs[c][r0:], k)
                if kinds is not None:
                    s = jnp.concatenate([s[a * tk:(a + 1) * tk] + bias_ref[0, kd] if kd < 2
                                         else s[a * tk:(a + 1) * tk] for a, kd in enumerate(kinds)], axis=0)
                m_prev = m_sc[c, r0:, :]
                if fixed:
                    p = jnp.exp2(s - jnp.concatenate([m_prev] * (tk // LANES), axis=1))
                    l_sc[c, r0:, :] += sum(p[:, a * LANES:(a + 1) * LANES] for a in range(tk // LANES))
                    acc_sc[c, r0:, :] += jnp.dot(p.astype(BF16), v, preferred_element_type=F32)
                else:
                    m_next = jnp.maximum(m_prev, jnp.max(s, axis=1, keepdims=True))
                    p = jnp.exp2(s - jnp.concatenate([m_next] * (tk // LANES), axis=1))
                    alpha = jnp.exp2(m_prev - m_next)
                    l_sc[c, r0:, :] = alpha * l_sc[c, r0:, :] + sum(p[:, a * LANES:(a + 1) * LANES]
                                                                    for a in range(tk // LANES))
                    pv = jnp.dot(p.astype(BF16), v, preferred_element_type=F32)
                    acc_sc[c, r0:, :] = alpha * acc_sc[c, r0:, :] + pv
                    m_sc[c, r0:, :] = m_next

        def body(j, carry):
            tile(j, 0, None)
            return carry

        lax.fori_loop(0, i * nsub - 1, body, 0)

        @pl.when(i > 0)
        def _():
            tile(i * nsub - 1, 0, (1,) + (2,) * (nsub - 1))

        for u in range(nsub):
            tile(i * nsub + u, u * tk, ((0, 1) + (2,) * nsub)[:nsub - u])

    lane1 = lax.broadcasted_iota(jnp.int32, (1, LANES), 1)
    far = tab_ref[h * REL_BUCKETS + REL_BUCKETS - 1]
    b0 = (tab_ref[h * REL_BUCKETS] - far) * LOG2E
    bmax = b0
    for b in range(1, REL_BUCKETS):
        bmax = jnp.maximum(bmax, (tab_ref[h * REL_BUCKETS + b] - far) * LOG2E)
    bounds = []
    for c in range(2):
        qf = qs[c].astype(F32)
        qn = jnp.sqrt(jnp.sum(qf * qf, axis=1, keepdims=True))
        kmax = jnp.sqrt(jnp.sum(jnp.where(lane1 == 2 * h + c, kn_ref[0:1, :], 0.0), axis=1, keepdims=True))
        bounds.append(qn * kmax + 1.0)
    gap = 2.0 * jnp.max(jnp.maximum(bounds[0], bounds[1])) + (bmax - b0)
    fits = gap <= FIXED_REF_MAX_SHIFT

    @pl.when(fits)
    def _():
        for c in range(2):
            m_sc[c] = jnp.broadcast_to(bounds[c] + bmax, (tq, LANES))
        run(True)

    @pl.when(jnp.logical_not(fits))
    def _():
        m_sc[...] = jnp.full_like(m_sc, NEG)
        run(False)

    lv = lam_ref[...]
    lam = (jnp.exp(jnp.sum(lv[0:1] * lv[1:2], axis=1, keepdims=True))
           - jnp.exp(jnp.sum(lv[2:3] * lv[3:4], axis=1, keepdims=True)) + lam_init)
    l0 = jnp.sum(l_sc[0], axis=1, keepdims=True)
    l1 = jnp.sum(l_sc[1], axis=1, keepdims=True)
    o = acc_sc[0] / l0 - lam * (acc_sc[1] / l1)
    o = o * lax.rsqrt(jnp.mean(o * o, axis=1, keepdims=True) + SUBLN_EPS) * g_ref[...] * (1.0 - lam_init)
    o_ref[...] = o.astype(o_ref.dtype)


def _diff_attention(table_flat, dq, dk, dv, bias, dkn, lamvec, g, *, tq, tk, lam_init):
    S = dq.shape[0]
    grid_spec = pltpu.PrefetchScalarGridSpec(
        num_scalar_prefetch=1,
        grid=(DIFF_HEADS, S // tq),
        in_specs=[pl.BlockSpec((tq, LANES), lambda h, i, tab: (i, h)),
                  pl.BlockSpec((S, LANES), lambda h, i, tab: (0, h)),
                  pl.BlockSpec((S, LANES), lambda h, i, tab: (0, h)),
                  pl.BlockSpec((1, BIAS_KINDS, tk, tk), lambda h, i, tab: (h, 0, 0, 0)),
                  pl.BlockSpec((8, LANES), lambda h, i, tab: (0, 0)),
                  pl.BlockSpec((8, LANES), lambda h, i, tab: (0, 0)),
                  pl.BlockSpec((1, LANES), lambda h, i, tab: (0, 0))],
        out_specs=pl.BlockSpec((tq, LANES), lambda h, i, tab: (i, h)),
        scratch_shapes=[pltpu.VMEM((2, tq, LANES), F32), pltpu.VMEM((2, tq, LANES), F32),
                        pltpu.VMEM((2, tq, LANES), F32)],
    )
    return pl.pallas_call(
        functools.partial(_diff_kernel, tq=tq, tk=tk, lam_init=lam_init),
        grid_spec=grid_spec,
        out_shape=jax.ShapeDtypeStruct((S, DIFF_V_WIDTH), BF16),
        compiler_params=_cparams(("arbitrary", "arbitrary")),
        name="diff_attn",
    )(table_flat, dq, dk, dv, bias, dkn, lamvec, g)


def _merge_kernel(yf_ref, yd_ref, sa_ref, sb_ref, h_ref, wf_ref, wd_ref, wo_ref, g_ref, b_ref, o_ref, *, alpha):
    bf = jnp.dot(yf_ref[...], wf_ref[...], preferred_element_type=F32)
    bd = jnp.dot(yd_ref[...], wd_ref[...], preferred_element_type=F32)
    merged = sa_ref[...].astype(F32) * bf + sb_ref[...].astype(F32) * bd
    mix = jnp.dot(merged.astype(BF16), wo_ref[...], preferred_element_type=F32)
    o_ref[...] = _layer_norm(alpha * h_ref[...] + mix, g_ref[...], b_ref[...])


def _merge(yf, yd, sa, sb, h, wf, wd, wo, g, b, *, alpha, tm):
    S = h.shape[0]
    row = lambda n: pl.BlockSpec((tm, n), lambda i: (i, 0))
    full = lambda a: pl.BlockSpec(a.shape, lambda i: (0,) * a.ndim)
    return pl.pallas_call(
        functools.partial(_merge_kernel, alpha=alpha),
        grid=(S // tm,),
        in_specs=[row(512), row(512), row(D_MODEL), row(D_MODEL), row(D_MODEL),
                  full(wf), full(wd), full(wo), full(g), full(b)],
        out_specs=row(D_MODEL),
        out_shape=jax.ShapeDtypeStruct((S, D_MODEL), F32),
        compiler_params=_cparams(("arbitrary",)),
        name="merge_out",
    )(yf, yd, sa, sb, h, wf, wd, wo, g, b)


MOE_ROWS = 272
SCATTER_ROWS = 256
RANK_CHUNK = 256


def _router_kernel(h_ref, whi_ref, wlo_ref, cw_ref, rank_ref, rank_t_ref, cnt_ref):
    tm = h_ref.shape[0]
    h = h_ref[...]
    hhi = h.astype(BF16)
    hlo = (h - hhi.astype(F32)).astype(BF16)
    whi = whi_ref[...]
    dot = lambda a, b: jnp.dot(a, b, preferred_element_type=F32)
    logits = dot(hhi, whi) + dot(hhi, wlo_ref[...]) + dot(hlo, whi)
    lane = lax.broadcasted_iota(jnp.int32, logits.shape, 1).astype(F32)
    lg = jnp.where(lane < N_EXPERTS, logits, NEG)
    m1 = jnp.max(lg, axis=1, keepdims=True)
    i1 = jnp.min(jnp.where(lg == m1, lane, float(LANES)), axis=1, keepdims=True)
    lg2 = jnp.where(lane == i1, NEG, lg)
    m2 = jnp.max(lg2, axis=1, keepdims=True)
    i2 = jnp.min(jnp.where(lg2 == m2, lane, float(LANES)), axis=1, keepdims=True)
    e = jnp.exp(m2 - m1)
    g1 = 1.0 / (1.0 + e)
    chosen = jnp.logical_or(lane == i1, lane == i2)
    cw_ref[...] = jnp.where(lane == i1, g1, 0.0) + jnp.where(lane == i2, e * g1, 0.0)

    a = chosen.astype(BF16)
    r = lax.broadcasted_iota(jnp.int32, (RANK_CHUNK, RANK_CHUNK), 0)
    c = lax.broadcasted_iota(jnp.int32, (RANK_CHUNK, RANK_CHUNK), 1)
    strict_lower = (c < r).astype(BF16)
    ones_row = jnp.ones((8, RANK_CHUNK), BF16)
    carry = jnp.zeros((1, LANES), F32)
    for q in range(tm // RANK_CHUNK):
        rows = slice(q * RANK_CHUNK, (q + 1) * RANK_CHUNK)
        rank = jnp.where(chosen[rows], dot(strict_lower, a[rows]) + carry, -1.0)
        rank_ref[rows, :] = rank
        rank_t_ref[0, :, rows] = jnp.transpose(rank)[0:N_EXPERTS, :]
        carry = carry + dot(ones_row, a[rows])[0:1, :]
    cnt_ref[0] = jnp.broadcast_to(carry, (8, LANES)).astype(jnp.int32)


def _router(h, whi, wlo, *, tm):
    S = h.shape[0]
    nb = S // tm
    return pl.pallas_call(
        _router_kernel,
        grid=(nb,),
        in_specs=[pl.BlockSpec((tm, D_MODEL), lambda i: (i, 0)),
                  pl.BlockSpec(whi.shape, lambda i: (0, 0)), pl.BlockSpec(wlo.shape, lambda i: (0, 0))],
        out_specs=[pl.BlockSpec((tm, LANES), lambda i: (i, 0)),
                   pl.BlockSpec((tm, LANES), lambda i: (i, 0)),
                   pl.BlockSpec((1, N_EXPERTS, tm), lambda i: (i, 0, 0)),
                   pl.BlockSpec((1, 8, LANES), lambda i: (i, 0, 0))],
        out_shape=[jax.ShapeDtypeStruct((S, LANES), F32),
                   jax.ShapeDtypeStruct((S, LANES), F32),
                   jax.ShapeDtypeStruct((nb, N_EXPERTS, tm), F32),
                   jax.ShapeDtypeStruct((nb, 8, LANES), jnp.int32)],
        compiler_params=_cparams(("arbitrary",)),
        name="router",
    )(h, whi, wlo)


def _moe_kernel(cnt_ref, h_ref, cw_ref, rank_ref, rank_t_ref, wg_ref, wu_ref, wd_ref, g_ref, b_ref, o_ref,
                hb_sc, xs_sc, acc_sc, *, alpha):
    tm = h_ref.shape[0]
    i = pl.program_id(0)
    e = pl.program_id(1)
    c = pl.program_id(2)
    last_c = c == pl.num_programs(2) - 1
    n_rows = cnt_ref[i * N_EXPERTS + e]
    n_steps = (n_rows + MOE_ROWS - 1) // MOE_ROWS
    n_scatter = (n_rows + SCATTER_ROWS - 1) // SCATTER_ROWS

    @pl.when(jnp.logical_and(e == 0, c == 0))
    def _():
        hb_sc[...] = h_ref[...].astype(BF16)
        o_ref[...] = jnp.zeros_like(o_ref)

    @pl.when(c == 0)
    def _():
        rank_row = rank_t_ref[0, pl.ds(e, 1), :]

        def gather(t, carry):
            r0 = pl.multiple_of(t * MOE_ROWS, 16)
            slot = (lax.broadcasted_iota(jnp.int32, (MOE_ROWS, tm), 0) + r0).astype(F32)
            onehot = (rank_row == slot).astype(BF16)
            xs_sc[pl.ds(r0, MOE_ROWS), :] = jnp.dot(onehot, hb_sc[...], preferred_element_type=F32).astype(BF16)
            acc_sc[pl.ds(r0, MOE_ROWS), :] = jnp.zeros((MOE_ROWS, D_MODEL), F32)
            return carry

        lax.fori_loop(0, n_steps, gather, 0)

        @pl.when(n_steps * MOE_ROWS < n_scatter * SCATTER_ROWS)
        def _():
            acc_sc[pl.ds(pl.multiple_of(n_steps * MOE_ROWS, 16), SCATTER_ROWS), :] = jnp.zeros(
                (SCATTER_ROWS, D_MODEL), F32)

    def swiglu(t, carry):
        r0 = pl.multiple_of(t * MOE_ROWS, 16)
        x = xs_sc[pl.ds(r0, MOE_ROWS), :]
        gate = jnp.dot(x, wg_ref[0], preferred_element_type=F32)
        up = jnp.dot(x, wu_ref[0], preferred_element_type=F32)
        a = (gate * _sigmoid(gate) * up).astype(BF16)
        acc_sc[pl.ds(r0, MOE_ROWS), :] += jnp.dot(a, wd_ref[0], preferred_element_type=F32)
        return carry

    lax.fori_loop(0, n_steps, swiglu, 0)

    @pl.when(last_c)
    def _():
        lane = lax.broadcasted_iota(jnp.int32, (tm, LANES), 1)
        pick = lambda ref: jnp.broadcast_to(
            jnp.sum(jnp.where(lane == e, ref[...], 0.0), axis=1, keepdims=True), (tm, LANES))
        rank_col = pick(rank_ref)
        gate_col = jnp.concatenate([pick(cw_ref)] * (D_MODEL // LANES), axis=1)

        def scatter(t, carry):
            r0 = pl.multiple_of(t * SCATTER_ROWS, SCATTER_ROWS)
            slot = (lane + r0).astype(F32)
            onehot = jnp.concatenate([(rank_col == slot + float(a * LANES)) for a in range(SCATTER_ROWS // LANES)],
                                     axis=1).astype(BF16)
            y = acc_sc[pl.ds(r0, SCATTER_ROWS), :].astype(BF16)
            o_ref[...] += gate_col * jnp.dot(onehot, y, preferred_element_type=F32)
            return carry

        lax.fori_loop(0, n_scatter, scatter, 0)

    @pl.when(jnp.logical_and(e == pl.num_programs(1) - 1, last_c))
    def _():
        o_ref[...] = _layer_norm(alpha * h_ref[...] + o_ref[...], g_ref[...], b_ref[...])


def _moe(h, cw, rank, rank_t, counts, w_gu, w_dn, g, b, *, alpha, tm, fc):
    S = h.shape[0]
    E, _, F2 = w_gu.shape
    nc = F2 // 2 // fc
    once = pl.Buffered(1)
    rows_cap = pl.cdiv(tm, MOE_ROWS) * MOE_ROWS
    grid_spec = pltpu.PrefetchScalarGridSpec(
        num_scalar_prefetch=1,
        grid=(S // tm, E, nc),
        in_specs=[pl.BlockSpec((tm, D_MODEL), lambda i, e, c, cnt: (i, 0), pipeline_mode=once),
                  pl.BlockSpec((tm, LANES), lambda i, e, c, cnt: (i, 0), pipeline_mode=once),
                  pl.BlockSpec((tm, LANES), lambda i, e, c, cnt: (i, 0), pipeline_mode=once),
                  pl.BlockSpec((1, N_EXPERTS, tm), lambda i, e, c, cnt: (i, 0, 0), pipeline_mode=once),
                  pl.BlockSpec((1, D_MODEL, fc), lambda i, e, c, cnt: (e, 0, c)),
                  pl.BlockSpec((1, D_MODEL, fc), lambda i, e, c, cnt: (e, 0, nc + c)),
                  pl.BlockSpec((1, fc, D_MODEL), lambda i, e, c, cnt: (e, c, 0)),
                  pl.BlockSpec(g.shape, lambda i, e, c, cnt: (0, 0)),
                  pl.BlockSpec(b.shape, lambda i, e, c, cnt: (0, 0))],
        out_specs=pl.BlockSpec((tm, D_MODEL), lambda i, e, c, cnt: (i, 0), pipeline_mode=once),
        scratch_shapes=[pltpu.VMEM((tm, D_MODEL), BF16), pltpu.VMEM((rows_cap, D_MODEL), BF16),
                        pltpu.VMEM((rows_cap, D_MODEL), F32)],
    )
    return pl.pallas_call(
        functools.partial(_moe_kernel, alpha=alpha),
        grid_spec=grid_spec,
        out_shape=jax.ShapeDtypeStruct((S, D_MODEL), F32),
        compiler_params=_cparams(("arbitrary", "arbitrary", "arbitrary")),
        name="moe_ffn",
    )(counts, h, cw, rank, rank_t, w_gu, w_gu, w_dn, g, b)


def _ffn_kernel(h_ref, wgu_ref, wd_ref, g_ref, b_ref, o_ref, *, alpha, fc):
    h = h_ref[...]
    hb = h.astype(BF16)
    d_ff = wd_ref.shape[0]
    acc = None
    for c in range(d_ff // fc):
        gate = jnp.dot(hb, wgu_ref[:, c * fc:(c + 1) * fc], preferred_element_type=F32)
        up = jnp.dot(hb, wgu_ref[:, d_ff + c * fc:d_ff + (c + 1) * fc], preferred_element_type=F32)
        a = (gate * _sigmoid(gate) * up).astype(BF16)
        y = jnp.dot(a, wd_ref[c * fc:(c + 1) * fc, :], preferred_element_type=F32)
        acc = y if acc is None else acc + y
    o_ref[...] = _layer_norm(alpha * h + acc, g_ref[...], b_ref[...])


def _ffn(h, w_gu, w_dn, g, b, *, alpha, tm, fc):
    S = h.shape[0]
    once = pl.Buffered(1)
    return pl.pallas_call(
        functools.partial(_ffn_kernel, alpha=alpha, fc=fc),
        grid=(S // tm,),
        in_specs=[pl.BlockSpec((tm, D_MODEL), lambda i: (i, 0)),
                  pl.BlockSpec(w_gu.shape, lambda i: (0, 0), pipeline_mode=once),
                  pl.BlockSpec(w_dn.shape, lambda i: (0, 0), pipeline_mode=once),
                  pl.BlockSpec(g.shape, lambda i: (0, 0)), pl.BlockSpec(b.shape, lambda i: (0, 0))],
        out_specs=pl.BlockSpec((tm, D_MODEL), lambda i: (i, 0)),
        out_shape=jax.ShapeDtypeStruct((S, D_MODEL), F32),
        compiler_params=_cparams(("arbitrary",)),
        name="dense_ffn",
    )(h, w_gu, w_dn, g, b)


def _pad_lanes(a, n=LANES):
    return jnp.pad(a, [(0, 0)] * (a.ndim - 1) + [(0, n - a.shape[-1])])


def kernel(x, ln_in_g, ln_in_b, w_in, b_fgate, lam_q1, lam_k1, lam_q2, lam_k2, subln_g, w_branch_fox, w_branch_diff, w_out, ln_mix_g, ln_mix_b, rel_bias, w_ffn_gate_up, w_ffn_down, w_router, w_expert_gate_up, w_expert_down, ln_ffn_g, ln_ffn_b):
    B, S, _ = x.shape
    assert B == 1
    depth = w_in.shape[0]
    alpha = (2.0 * depth) ** 0.25
    tq = min(ATTN_TQ, S)
    tk = min(ATTN_TK, tq)
    tm = min(512, S)
    tf = min(1024, S)
    te = min(MOE_TM, S)
    assert tk >= REL_MAX_DISTANCE and tq % tk == 0 and S % tq == 0 and S % tf == 0 and S % tm == 0 and S % te == 0
    assert te % SCATTER_ROWS == 0 and te % RANK_CHUNK == 0 and MOE_ROWS % 16 == 0
    row = lambda v: v.reshape(1, -1).astype(F32)

    table_flat = rel_bias.astype(F32).T.reshape(-1)
    bias = _bias_tiles(table_flat, t=tk)

    h = x[0]
    for l in range(depth):
        lam_init = 0.8 - 0.6 * math.exp(-0.3 * l)
        fq, fk, fv, fg, dq, dk, dv, ga, gb = jnp.split(w_in[l], list(_cumsum(IN_SPLIT_SIZES))[:-1], axis=-1)
        w_cat = jnp.concatenate([fq, fk, fv, dq, dk, dv, ga, gb, _pad_lanes(fg)], axis=-1).astype(BF16)
        outs = _inproj(h, row(ln_in_g), row(ln_in_b), w_cat, _pad_lanes(row(b_fgate[l])), do_ln=(l == 0), tm=tk)
        qa, ka, va, stats, dq, dk, dv, sa, sb, dkn = outs[:10]
        if l == 0:
            h = outs[10]
        start = _plan(stats, nsub=tq // tk)[:S // tq, :FOX_HEADS].reshape(-1)
        y_fox = _fox_attention(start, qa, ka, va, tq=tq, tk=tk)
        lamvec = jnp.zeros((8, LANES), F32).at[0:4, 0:HEAD_DIM].set(
            jnp.stack([lam_q1[l], lam_k1[l], lam_q2[l], lam_k2[l]]).astype(F32))
        y_diff = _diff_attention(table_flat, dq, dk, dv, bias, dkn, lamvec, row(subln_g[l]), tq=tq, tk=tk,
                                 lam_init=lam_init)
        h = _merge(y_fox, y_diff, sa, sb, h, w_branch_fox[l].astype(BF16), w_branch_diff[l].astype(BF16),
                   w_out[l].astype(BF16), row(ln_mix_g[l]), row(ln_mix_b[l]), alpha=alpha, tm=tm)
        g, b = row(ln_ffn_g[l]), row(ln_ffn_b[l])
        if l % 2 == 0:
            h = _ffn(h, w_ffn_gate_up[l // 2].astype(BF16), w_ffn_down[l // 2].astype(BF16), g, b,
                     alpha=alpha, tm=tf, fc=1408)
        else:
            wr = _pad_lanes(w_router[l // 2].astype(F32))
            whi = wr.astype(BF16)
            wlo = (wr - whi.astype(F32)).astype(BF16)
            cw, rank, rank_t, counts = _router(h, whi, wlo, tm=te)
            h = _moe(h, cw, rank, rank_t, counts[:, 0, :N_EXPERTS].reshape(-1),
                     w_expert_gate_up[l // 2].astype(BF16), w_expert_down[l // 2].astype(BF16),
                     g, b, alpha=alpha, tm=te, fc=896)
    return h[None]


def _cumsum(sizes):
    tot = 0
    for s in sizes:
        tot += s
        yield tot
```

```python
import functools
import math

import jax
import jax.numpy as jnp
from jax import lax
from jax.experimental import pallas as pl
from jax.experimental.pallas import tpu as pltpu

F32 = jnp.float32
BF16 = jnp.bfloat16

D_MODEL = 1024
HEAD_DIM = 64
FOX_HEADS = 8
FOX_WIDTH = FOX_HEADS * HEAD_DIM
DIFF_HEADS = 4
DIFF_QK_WIDTH = DIFF_HEADS * 2 * HEAD_DIM
DIFF_V_DIM = 2 * HEAD_DIM
DIFF_V_WIDTH = DIFF_HEADS * DIFF_V_DIM
IN_SPLIT_SIZES = (FOX_WIDTH, FOX_WIDTH, FOX_WIDTH, FOX_HEADS, DIFF_QK_WIDTH, DIFF_QK_WIDTH, DIFF_V_WIDTH,
                  D_MODEL, D_MODEL)
REL_BUCKETS = 32
REL_MAX_DISTANCE = 128
N_EXPERTS = 8
LN_EPS = 1e-5
SUBLN_EPS = 1e-5

LANES = 128
LOG2E = 1.4426950408889634
Q_SCALE = HEAD_DIM ** -0.5 * LOG2E
NEG = -1e30
VMEM_LIMIT = 56 * 1024 * 1024
ATTN_TQ = 2048
ATTN_TK = 512
MOE_TM = 2048


def _cparams(sem):
    return pltpu.CompilerParams(dimension_semantics=sem, vmem_limit_bytes=VMEM_LIMIT)


def _layer_norm(x, g, b):
    mu = jnp.mean(x, axis=-1, keepdims=True)
    xc = x - mu
    var = jnp.mean(xc * xc, axis=-1, keepdims=True)
    return xc * lax.rsqrt(var + LN_EPS) * g + b


def _sigmoid(x):
    return 1.0 / (1.0 + jnp.exp(-x))


def _split3(x):
    hi = x.astype(BF16)
    r1 = x - hi.astype(F32)
    mid = r1.astype(BF16)
    lo = (r1 - mid.astype(F32)).astype(BF16)
    return hi, mid, lo


_C_FQ, _C_FK, _C_FV, _C_DQ, _C_DK, _C_DV, _C_GA, _C_GB, _C_FG, _C_END = (
    0, 512, 1024, 1536, 2048, 2560, 3072, 4096, 5120, 5248)


def _inproj_kernel(x_ref, g_ref, b_ref, w_ref, bf_ref, qa_ref, ka_ref, va_ref, st_ref, dq, dk, dv, sa, sb, dkn,
                   *rest, do_ln):
    carry_ref = rest[-1]
    tb = x_ref.shape[0]

    @pl.when(pl.program_id(0) == 0)
    def _():
        carry_ref[...] = jnp.zeros_like(carry_ref)

    x = x_ref[...]
    if do_ln:
        x = _layer_norm(x, g_ref[...], b_ref[...])
        rest[0][...] = x
    xb = x.astype(BF16)

    def mm(a, b):
        return jnp.dot(xb, w_ref[:, a:b], preferred_element_type=F32)

    fq = (mm(_C_FQ, _C_FK) * Q_SCALE).astype(BF16)
    fk = mm(_C_FK, _C_FV).astype(BF16)
    fv = mm(_C_FV, _C_DQ).astype(BF16)

    z = mm(_C_FG, _C_END) + bf_ref[...]
    logf = jnp.minimum(z, 0.0) - jnp.log1p(jnp.exp(-jnp.abs(z)))
    r = lax.broadcasted_iota(jnp.int32, (tb, tb), 0)
    c = lax.broadcasted_iota(jnp.int32, (tb, tb), 1)
    tri = (c <= r).astype(BF16)
    hi, mid, lo = _split3(logf)
    dot = lambda a: jnp.dot(tri, a, preferred_element_type=F32)
    cum = (dot(hi) + dot(mid)) + dot(lo) + carry_ref[0:1, :]
    carry_ref[0:1, :] = cum[tb - 1:tb, :]

    c_hi, c_mid, c_lo = (t.astype(F32) for t in _split3(cum * LOG2E))
    lane = lax.broadcasted_iota(jnp.int32, (tb, LANES), 1)
    one = jnp.ones((tb, LANES), F32)
    zero = jnp.zeros((tb, LANES), F32)
    lane1 = lax.broadcasted_iota(jnp.int32, (1, LANES), 1)
    q_norm2 = jnp.zeros((1, LANES), F32)
    k_norm2 = jnp.zeros((1, LANES), F32)

    def max_norm2(slab):
        n2 = jnp.sum(jnp.where(lane < 64, slab * slab, 0.0), axis=1, keepdims=True)
        return jnp.max(n2, axis=0, keepdims=True)

    for h in range(FOX_HEADS):
        p, odd = divmod(h, 2)
        col = lambda t: jnp.broadcast_to(t[:, h:h + 1], (tb, LANES))
        chi, cmid, clo = col(c_hi), col(c_mid), col(c_lo)

        def head_slab(a):
            slab = a[:, p * LANES:(p + 1) * LANES].astype(F32)
            return pltpu.roll(slab, HEAD_DIM, 1) if odd else slab

        qslab = head_slab(fq)
        kslab = head_slab(fk)
        q_norm2 = jnp.where(lane1 == h, max_norm2(qslab), q_norm2)
        k_norm2 = jnp.where(lane1 == h, max_norm2(kslab), k_norm2)
        qa = jnp.where(lane < 64, qslab,
                       jnp.where(lane == 64, chi, jnp.where(lane == 65, cmid, jnp.where(lane == 66, clo,
                                 jnp.where(lane < 70, one, zero)))))
        ka = jnp.where(lane < 64, kslab,
                       jnp.where(lane < 67, one, jnp.where(lane == 67, -chi, jnp.where(lane == 68, -cmid,
                                 jnp.where(lane == 69, -clo, zero)))))
        qa_ref[h] = qa.astype(BF16)
        ka_ref[h] = ka.astype(BF16)
        vslab = fv[:, p * LANES:(p + 1) * LANES].astype(F32)
        if odd:
            va = jnp.where(lane >= 64, vslab, jnp.where(lane == 0, one, zero))
        else:
            va = jnp.where(lane < 64, vslab, jnp.where(lane == 64, one, zero))
        va_ref[h] = va.astype(BF16)

    dq[...] = (mm(_C_DQ, _C_DK) * Q_SCALE).astype(BF16)
    dk16 = mm(_C_DK, _C_DV).astype(BF16)
    dk[...] = dk16
    dv[...] = mm(_C_DV, _C_GA).astype(BF16)
    sa[...] = _sigmoid(mm(_C_GA, _C_GB)).astype(BF16)
    sb[...] = _sigmoid(mm(_C_GB, _C_FG)).astype(BF16)

    dk_norm2 = carry_ref[1:2, :]
    for h in range(DIFF_HEADS):
        slab = dk16[:, h * LANES:(h + 1) * LANES].astype(F32)
        sq = slab * slab
        for c in range(2):
            n2 = jnp.sum(jnp.where((lane < 64) == (c == 0), sq, 0.0), axis=1, keepdims=True)
            dk_norm2 = jnp.where(lane1 == 2 * h + c, jnp.maximum(dk_norm2, jnp.max(n2, axis=0, keepdims=True)),
                                 dk_norm2)
    carry_ref[1:2, :] = dk_norm2
    dkn[...] = jnp.broadcast_to(dk_norm2, dkn.shape)

    st_ref[0] = jnp.concatenate([q_norm2, k_norm2, cum[tb - 1:tb, :] * LOG2E, jnp.zeros((5, LANES), F32)], axis=0)


def _inproj(x, g, b, w, bf, *, do_ln, tm):
    S = x.shape[0]
    row = lambda n: pl.BlockSpec((tm, n), lambda i: (i, 0))
    full = lambda a: pl.BlockSpec(a.shape, lambda i: (0,) * a.ndim)
    hspec = pl.BlockSpec((FOX_HEADS, tm, LANES), lambda i: (0, i, 0))
    hshape = jax.ShapeDtypeStruct((FOX_HEADS, S, LANES), BF16)
    out_shape = [hshape] * 3 + [jax.ShapeDtypeStruct((S // tm, 8, LANES), F32)] + [
        jax.ShapeDtypeStruct((S, 512), BF16)] * 3 + [jax.ShapeDtypeStruct((S, D_MODEL), BF16)] * 2 + [
        jax.ShapeDtypeStruct((8, LANES), F32)]
    out_specs = [hspec] * 3 + [pl.BlockSpec((1, 8, LANES), lambda i: (i, 0, 0))] + [row(512)] * 3 + [
        row(D_MODEL)] * 2 + [pl.BlockSpec((8, LANES), lambda i: (0, 0))]
    if do_ln:
        out_shape.append(jax.ShapeDtypeStruct((S, D_MODEL), F32))
        out_specs.append(row(D_MODEL))
    return pl.pallas_call(
        functools.partial(_inproj_kernel, do_ln=do_ln),
        grid=(S // tm,),
        in_specs=[row(D_MODEL), full(g), full(b), full(w), full(bf)],
        out_specs=out_specs,
        out_shape=out_shape,
        scratch_shapes=[pltpu.VMEM((8, LANES), F32)],
        compiler_params=_cparams(("arbitrary",)),
        name="inproj",
    )(x, g, b, w, bf)


FIXED_REF_MAX_SHIFT = 64.0
SKIP_LOG2 = 150.0


def _plan_kernel(st_ref, start_ref, kmax2_ref, *, nt, nsub):
    qn = jnp.sqrt(st_ref[pl.ds(0, nt, stride=8), :])
    kn = jnp.sqrt(st_ref[pl.ds(1, nt, stride=8), :])
    cum_end = st_ref[pl.ds(2, nt, stride=8), :]
    tile = lax.broadcasted_iota(jnp.int32, (nt, LANES), 0)
    rows = []
    for i in range(nt // nsub):
        own = jnp.logical_and(tile >= i * nsub, tile < (i + 1) * nsub)
        q_max = jnp.max(jnp.where(own, qn, 0.0), axis=0, keepdims=True)
        k_own = jnp.max(jnp.where(own, kn, 0.0), axis=0, keepdims=True)
        cum_q = cum_end[i * nsub - 1:i * nsub, :] if i > 0 else jnp.zeros((1, LANES), F32)
        upper = q_max * kn + (cum_q - cum_end) + q_max * k_own
        slack = SKIP_LOG2 + 1e-4 * jnp.abs(cum_end)
        needed = jnp.logical_and(upper >= -slack, tile < i * nsub)
        rows.append(jnp.min(jnp.where(needed, tile, i * nsub), axis=0, keepdims=True))
    rows += [jnp.zeros((1, LANES), jnp.int32)] * (8 - len(rows))
    start_ref[...] = jnp.concatenate(rows, axis=0)
    kmax2_ref[...] = jnp.broadcast_to(jnp.max(st_ref[pl.ds(1, nt, stride=8), :], axis=0, keepdims=True),
                                      kmax2_ref.shape)


def _plan(stats, *, nsub):
    nt = stats.shape[0]
    assert nt // nsub <= 8
    return pl.pallas_call(
        functools.partial(_plan_kernel, nt=nt, nsub=nsub),
        out_shape=[jax.ShapeDtypeStruct((8, LANES), jnp.int32), jax.ShapeDtypeStruct((8, LANES), F32)],
        compiler_params=pltpu.CompilerParams(vmem_limit_bytes=VMEM_LIMIT),
        name="fox_plan",
    )(stats.reshape(nt * 8, LANES))


def _nt_dot(a, b):
    return lax.dot_general(a, b, (((1,), (1,)), ((), ())), preferred_element_type=F32)


def _fox_kernel(start_ref, q_ref, k_ref, v_ref, kmax2_ref, o_ref, m_sc, acc_sc, *, tq, tk):
    i = pl.program_id(1)
    pair = pl.program_id(0)
    acc_sc[...] = jnp.zeros_like(acc_sc)
    nsub = tq // tk
    first = jnp.minimum(start_ref[i * FOX_HEADS + 2 * pair], start_ref[i * FOX_HEADS + 2 * pair + 1])

    def run(fixed):
        def tile(j, r0, masked):
            off = pl.multiple_of(j * tk, tk)
            rows = tq - r0
            for hh in range(2):
                s = _nt_dot(q_ref[hh, r0:, :], k_ref[hh, pl.ds(off, tk), :])
                if masked:
                    r = lax.broadcasted_iota(jnp.int32, (rows, tk), 0)
                    c = lax.broadcasted_iota(jnp.int32, (rows, tk), 1)
                    s = jnp.where(c <= r, s, NEG)
                m_prev = m_sc[hh, r0:, :]
                v = v_ref[hh, pl.ds(off, tk), :]
                if fixed:
                    p = jnp.exp2(s - jnp.concatenate([m_prev] * (tk // LANES), axis=1)).astype(BF16)
                    acc_sc[hh, r0:, :] += jnp.dot(p, v, preferred_element_type=F32)
                else:
                    m_next = jnp.maximum(m_prev, jnp.max(s, axis=1, keepdims=True))
                    alpha = jnp.exp2(m_prev - m_next)
                    p = jnp.exp2(s - jnp.concatenate([m_next] * (tk // LANES), axis=1)).astype(BF16)
                    acc_sc[hh, r0:, :] = alpha * acc_sc[hh, r0:, :] + jnp.dot(p, v, preferred_element_type=F32)
                    m_sc[hh, r0:, :] = m_next

        def body(j, carry):
            tile(j, 0, False)
            return carry

        lax.fori_loop(first, i * nsub, body, 0)
        for u in range(nsub):
            tile(i * nsub + u, u * tk, True)

    lane = lax.broadcasted_iota(jnp.int32, (tq, LANES), 1)
    lane1 = lax.broadcasted_iota(jnp.int32, (1, LANES), 1)
    bounds = []
    for hh in range(2):
        qf = q_ref[hh].astype(F32)
        qn = jnp.sqrt(jnp.sum(jnp.where(lane < 64, qf * qf, 0.0), axis=1, keepdims=True))
        kmax = jnp.sqrt(jnp.sum(jnp.where(lane1 == 2 * pair + hh, kmax2_ref[0:1, :], 0.0), axis=1, keepdims=True))
        bounds.append(qn * kmax + 1.0)
    fits = 2.0 * jnp.max(jnp.maximum(bounds[0], bounds[1])) <= FIXED_REF_MAX_SHIFT

    @pl.when(fits)
    def _():
        for hh in range(2):
            m_sc[hh] = jnp.broadcast_to(bounds[hh], (tq, LANES))
        run(True)

    @pl.when(jnp.logical_not(fits))
    def _():
        m_sc[...] = jnp.full_like(m_sc, NEG)
        run(False)

    a0 = acc_sc[0]
    a1 = acc_sc[1]
    o_ref[...] = jnp.where(lane < 64, a0 / a0[:, 64:65], a1 / a1[:, 0:1]).astype(o_ref.dtype)


def _fox_attention(start, qa, ka, va, kmax2, *, tq, tk):
    S = qa.shape[1]
    grid_spec = pltpu.PrefetchScalarGridSpec(
        num_scalar_prefetch=1,
        grid=(FOX_HEADS // 2, S // tq),
        in_specs=[pl.BlockSpec((2, tq, LANES), lambda p, i, st: (p, i, 0)),
                  pl.BlockSpec((2, S, LANES), lambda p, i, st: (p, 0, 0)),
                  pl.BlockSpec((2, S, LANES), lambda p, i, st: (p, 0, 0)),
                  pl.BlockSpec((8, LANES), lambda p, i, st: (0, 0))],
        out_specs=pl.BlockSpec((tq, LANES), lambda p, i, st: (i, p)),
        scratch_shapes=[pltpu.VMEM((2, tq, LANES), F32), pltpu.VMEM((2, tq, LANES), F32)],
    )
    return pl.pallas_call(
        functools.partial(_fox_kernel, tq=tq, tk=tk),
        grid_spec=grid_spec,
        out_shape=jax.ShapeDtypeStruct((S, FOX_WIDTH), BF16),
        compiler_params=_cparams(("arbitrary", "arbitrary")),
        name="fox_attn",
    )(start, qa, ka, va, kmax2)


BIAS_KINDS = 2


def _bias_tile_kernel(table_ref, o_ref, *, t):
    h = pl.program_id(0)
    kind = pl.program_id(1)
    r = lax.broadcasted_iota(jnp.int32, (t, t), 0)
    c = lax.broadcasted_iota(jnp.int32, (t, t), 1)
    dist = r - c + kind * t
    n = jnp.maximum(dist, 0)
    max_exact = REL_BUCKETS // 2
    nf = jnp.maximum(n, 1).astype(F32)
    log_part = jnp.log(nf / max_exact) / math.log(REL_MAX_DISTANCE / max_exact) * (REL_BUCKETS - max_exact)
    large = jnp.minimum(max_exact + log_part.astype(jnp.int32), REL_BUCKETS - 1)
    bucket = jnp.where(n < max_exact, n, large)
    val = jnp.zeros((t, t), F32)
    for b in range(REL_BUCKETS):
        val = jnp.where(bucket == b, table_ref[h * REL_BUCKETS + b], val)
    far = table_ref[h * REL_BUCKETS + REL_BUCKETS - 1]
    o_ref[0, 0] = jnp.where(dist >= 0, (val - far) * LOG2E, NEG)


def _bias_tiles(table_flat, *, t):
    return pl.pallas_call(
        functools.partial(_bias_tile_kernel, t=t),
        grid=(DIFF_HEADS, BIAS_KINDS),
        in_specs=[pl.BlockSpec(memory_space=pltpu.SMEM)],
        out_specs=pl.BlockSpec((1, 1, t, t), lambda h, k: (h, k, 0, 0)),
        out_shape=jax.ShapeDtypeStruct((DIFF_HEADS, BIAS_KINDS, t, t), F32),
        compiler_params=_cparams(("arbitrary", "arbitrary")),
        name="bias_tiles",
    )(table_flat)


def _diff_kernel(tab_ref, q_ref, k_ref, v_ref, bias_ref, kn_ref, lam_ref, g_ref, o_ref, m_sc, l_sc, acc_sc,
                 *, tq, tk, lam_init):
    h = pl.program_id(0)
    i = pl.program_id(1)
    l_sc[...] = jnp.zeros_like(l_sc)
    acc_sc[...] = jnp.zeros_like(acc_sc)
    nsub = tq // tk

    lane = lax.broadcasted_iota(jnp.int32, (tq, LANES), 1)
    q = q_ref[...]
    zero = jnp.zeros_like(q)
    qs = (jnp.where(lane < 64, q, zero), jnp.where(lane >= 64, q, zero))

    def run(fixed):
        def tile(j, r0, kinds):
            off = pl.multiple_of(j * tk, tk)
            k = k_ref[pl.ds(off, tk), :]
            v = v_ref[pl.ds(off, tk), :]
            for c in range(2):
                s = _nt_dot(qs[c][r0:], k)
                if kinds is not None:
                    s = jnp.concatenate([s[a * tk:(a + 1) * tk] + bias_ref[0, kd] if kd < 2
                                         else s[a * tk:(a + 1) * tk] for a, kd in enumerate(kinds)], axis=0)
                m_prev = m_sc[c, r0:, :]
                if fixed:
                    p = jnp.exp2(s - jnp.concatenate([m_prev] * (tk // LANES), axis=1))
                    l_sc[c, r0:, :] += sum(p[:, a * LANES:(a + 1) * LANES] for a in range(tk // LANES))
                    acc_sc[c, r0:, :] += jnp.dot(p.astype(BF16), v, preferred_element_type=F32)
                else:
                    m_next = jnp.maximum(m_prev, jnp.max(s, axis=1, keepdims=True))
                    p = jnp.exp2(s - jnp.concatenate([m_next] * (tk // LANES), axis=1))
                    alpha = jnp.exp2(m_prev - m_next)
                    l_sc[c, r0:, :] = alpha * l_sc[c, r0:, :] + sum(p[:, a * LANES:(a + 1) * LANES]
                                                                    for a in range(tk // LANES))
                    pv = jnp.dot(p.astype(BF16), v, preferred_element_type=F32)
                    acc_sc[c, r0:, :] = alpha * acc_sc[c, r0:, :] + pv
                    m_sc[c, r0:, :] = m_next

        def body(j, carry):
            tile(j, 0, None)
            return carry

        lax.fori_loop(0, i * nsub - 1, body, 0)

        @pl.when(i > 0)
        def _():
            tile(i * nsub - 1, 0, (1,) + (2,) * (nsub - 1))

        for u in range(nsub):
            tile(i * nsub + u, u * tk, ((0, 1) + (2,) * nsub)[:nsub - u])

    lane1 = lax.broadcasted_iota(jnp.int32, (1, LANES), 1)
    far = tab_ref[h * REL_BUCKETS + REL_BUCKETS - 1]
    b0 = (tab_ref[h * REL_BUCKETS] - far) * LOG2E
    bmax = b0
    for b in range(1, REL_BUCKETS):
        bmax = jnp.maximum(bmax, (tab_ref[h * REL_BUCKETS + b] - far) * LOG2E)
    bounds = []
    for c in range(2):
        qf = qs[c].astype(F32)
        qn = jnp.sqrt(jnp.sum(qf * qf, axis=1, keepdims=True))
        kmax = jnp.sqrt(jnp.sum(jnp.where(lane1 == 2 * h + c, kn_ref[0:1, :], 0.0), axis=1, keepdims=True))
        bounds.append(qn * kmax + 1.0)
    gap = 2.0 * jnp.max(jnp.maximum(bounds[0], bounds[1])) + (bmax - b0)
    fits = gap <= FIXED_REF_MAX_SHIFT

    @pl.when(fits)
    def _():
        for c in range(2):
            m_sc[c] = jnp.broadcast_to(bounds[c] + bmax, (tq, LANES))
        run(True)

    @pl.when(jnp.logical_not(fits))
    def _():
        m_sc[...] = jnp.full_like(m_sc, NEG)
        run(False)

    lv = lam_ref[...]
    lam = (jnp.exp(jnp.sum(lv[0:1] * lv[1:2], axis=1, keepdims=True))
           - jnp.exp(jnp.sum(lv[2:3] * lv[3:4], axis=1, keepdims=True)) + lam_init)
    l0 = jnp.sum(l_sc[0], axis=1, keepdims=True)
    l1 = jnp.sum(l_sc[1], axis=1, keepdims=True)
    o = acc_sc[0] / l0 - lam * (acc_sc[1] / l1)
    o = o * lax.rsqrt(jnp.mean(o * o, axis=1, keepdims=True) + SUBLN_EPS) * g_ref[...] * (1.0 - lam_init)
    o_ref[...] = o.astype(o_ref.dtype)


def _diff_attention(table_flat, dq, dk, dv, bias, dkn, lamvec, g, *, tq, tk, lam_init):
    S = dq.shape[0]
    grid_spec = pltpu.PrefetchScalarGridSpec(
        num_scalar_prefetch=1,
        grid=(DIFF_HEADS, S // tq),
        in_specs=[pl.BlockSpec((tq, LANES), lambda h, i, tab: (i, h)),
                  pl.BlockSpec((S, LANES), lambda h, i, tab: (0, h)),
                  pl.BlockSpec((S, LANES), lambda h, i, tab: (0, h)),
                  pl.BlockSpec((1, BIAS_KINDS, tk, tk), lambda h, i, tab: (h, 0, 0, 0)),
                  pl.BlockSpec((8, LANES), lambda h, i, tab: (0, 0)),
                  pl.BlockSpec((8, LANES), lambda h, i, tab: (0, 0)),
                  pl.BlockSpec((1, LANES), lambda h, i, tab: (0, 0))],
        out_specs=pl.BlockSpec((tq, LANES), lambda h, i, tab: (i, h)),
        scratch_shapes=[pltpu.VMEM((2, tq, LANES), F32), pltpu.VMEM((2, tq, LANES), F32),
                        pltpu.VMEM((2, tq, LANES), F32)],
    )
    return pl.pallas_call(
        functools.partial(_diff_kernel, tq=tq, tk=tk, lam_init=lam_init),
        grid_spec=grid_spec,
        out_shape=jax.ShapeDtypeStruct((S, DIFF_V_WIDTH), BF16),
        compiler_params=_cparams(("arbitrary", "arbitrary")),
        name="diff_attn",
    )(table_flat, dq, dk, dv, bias, dkn, lamvec, g)


def _merge_kernel(yf_ref, yd_ref, sa_ref, sb_ref, h_ref, wf_ref, wd_ref, wo_ref, g_ref, b_ref, o_ref, *, alpha):
    bf = jnp.dot(yf_ref[...], wf_ref[...], preferred_element_type=F32)
    bd = jnp.dot(yd_ref[...], wd_ref[...], preferred_element_type=F32)
    merged = sa_ref[...].astype(F32) * bf + sb_ref[...].astype(F32) * bd
    mix = jnp.dot(merged.astype(BF16), wo_ref[...], preferred_element_type=F32)
    o_ref[...] = _layer_norm(alpha * h_ref[...] + mix, g_ref[...], b_ref[...])


def _merge(yf, yd, sa, sb, h, wf, wd, wo, g, b, *, alpha, tm):
    S = h.shape[0]
    row = lambda n: pl.BlockSpec((tm, n), lambda i: (i, 0))
    full = lambda a: pl.BlockSpec(a.shape, lambda i: (0,) * a.ndim)
    return pl.pallas_call(
        functools.partial(_merge_kernel, alpha=alpha),
        grid=(S // tm,),
        in_specs=[row(512), row(512), row(D_MODEL), row(D_MODEL), row(D_MODEL),
                  full(wf), full(wd), full(wo), full(g), full(b)],
        out_specs=row(D_MODEL),
        out_shape=jax.ShapeDtypeStruct((S, D_MODEL), F32),
        compiler_params=_cparams(("arbitrary",)),
        name="merge_out",
    )(yf, yd, sa, sb, h, wf, wd, wo, g, b)


MOE_ROWS = 272
SCATTER_ROWS = 256
RANK_CHUNK = 256


def _router_kernel(h_ref, whi_ref, wlo_ref, cw_ref, rank_ref, rank_t_ref, cnt_ref):
    tm = h_ref.shape[0]
    h = h_ref[...]
    hhi = h.astype(BF16)
    hlo = (h - hhi.astype(F32)).astype(BF16)
    whi = whi_ref[...]
    dot = lambda a, b: jnp.dot(a, b, preferred_element_type=F32)
    logits = dot(hhi, whi) + dot(hhi, wlo_ref[...]) + dot(hlo, whi)
    lane = lax.broadcasted_iota(jnp.int32, logits.shape, 1).astype(F32)
    lg = jnp.where(lane < N_EXPERTS, logits, NEG)
    m1 = jnp.max(lg, axis=1, keepdims=True)
    i1 = jnp.min(jnp.where(lg == m1, lane, float(LANES)), axis=1, keepdims=True)
    lg2 = jnp.where(lane == i1, NEG, lg)
    m2 = jnp.max(lg2, axis=1, keepdims=True)
    i2 = jnp.min(jnp.where(lg2 == m2, lane, float(LANES)), axis=1, keepdims=True)
    e = jnp.exp(m2 - m1)
    g1 = 1.0 / (1.0 + e)
    chosen = jnp.logical_or(lane == i1, lane == i2)
    cw_ref[...] = jnp.where(lane == i1, g1, 0.0) + jnp.where(lane == i2, e * g1, 0.0)

    a = chosen.astype(BF16)
    r = lax.broadcasted_iota(jnp.int32, (RANK_CHUNK, RANK_CHUNK), 0)
    c = lax.broadcasted_iota(jnp.int32, (RANK_CHUNK, RANK_CHUNK), 1)
    strict_lower = (c < r).astype(BF16)
    ones_row = jnp.ones((8, RANK_CHUNK), BF16)
    carry = jnp.zeros((1, LANES), F32)
    for q in range(tm // RANK_CHUNK):
        rows = slice(q * RANK_CHUNK, (q + 1) * RANK_CHUNK)
        rank = jnp.where(chosen[rows], dot(strict_lower, a[rows]) + carry, -1.0)
        rank_ref[rows, :] = rank
        rank_t_ref[0, :, rows] = jnp.transpose(rank)[0:N_EXPERTS, :]
        carry = carry + dot(ones_row, a[rows])[0:1, :]
    cnt_ref[0] = jnp.broadcast_to(carry, (8, LANES)).astype(jnp.int32)


def _router(h, whi, wlo, *, tm):
    S = h.shape[0]
    nb = S // tm
    return pl.pallas_call(
        _router_kernel,
        grid=(nb,),
        in_specs=[pl.BlockSpec((tm, D_MODEL), lambda i: (i, 0)),
                  pl.BlockSpec(whi.shape, lambda i: (0, 0)), pl.BlockSpec(wlo.shape, lambda i: (0, 0))],
        out_specs=[pl.BlockSpec((tm, LANES), lambda i: (i, 0)),
                   pl.BlockSpec((tm, LANES), lambda i: (i, 0)),
                   pl.BlockSpec((1, N_EXPERTS, tm), lambda i: (i, 0, 0)),
                   pl.BlockSpec((1, 8, LANES), lambda i: (i, 0, 0))],
        out_shape=[jax.ShapeDtypeStruct((S, LANES), F32),
                   jax.ShapeDtypeStruct((S, LANES), F32),
                   jax.ShapeDtypeStruct((nb, N_EXPERTS, tm), F32),
                   jax.ShapeDtypeStruct((nb, 8, LANES), jnp.int32)],
        compiler_params=_cparams(("arbitrary",)),
        name="router",
    )(h, whi, wlo)


def _moe_kernel(cnt_ref, h_ref, cw_ref, rank_ref, rank_t_ref, wg_ref, wu_ref, wd_ref, g_ref, b_ref, o_ref,
                hb_sc, xs_sc, acc_sc, *, alpha):
    tm = h_ref.shape[0]
    i = pl.program_id(0)
    e = pl.program_id(1)
    c = pl.program_id(2)
    last_c = c == pl.num_programs(2) - 1
    n_rows = cnt_ref[i * N_EXPERTS + e]
    n_steps = (n_rows + MOE_ROWS - 1) // MOE_ROWS
    n_scatter = (n_rows + SCATTER_ROWS - 1) // SCATTER_ROWS

    @pl.when(jnp.logical_and(e == 0, c == 0))
    def _():
        hb_sc[...] = h_ref[...].astype(BF16)
        o_ref[...] = jnp.zeros_like(o_ref)

    @pl.when(c == 0)
    def _():
        rank_row = rank_t_ref[0, pl.ds(e, 1), :]

        def gather(t, carry):
            r0 = pl.multiple_of(t * MOE_ROWS, 16)
            slot = (lax.broadcasted_iota(jnp.int32, (MOE_ROWS, tm), 0) + r0).astype(F32)
            onehot = (rank_row == slot).astype(BF16)
            xs_sc[pl.ds(r0, MOE_ROWS), :] = jnp.dot(onehot, hb_sc[...], preferred_element_type=F32).astype(BF16)
            acc_sc[pl.ds(r0, MOE_ROWS), :] = jnp.zeros((MOE_ROWS, D_MODEL), F32)
            return carry

        lax.fori_loop(0, n_steps, gather, 0)

        @pl.when(n_steps * MOE_ROWS < n_scatter * SCATTER_ROWS)
        def _():
            acc_sc[pl.ds(pl.multiple_of(n_steps * MOE_ROWS, 16), SCATTER_ROWS), :] = jnp.zeros(
                (SCATTER_ROWS, D_MODEL), F32)

    def swiglu(t, carry):
        r0 = pl.multiple_of(t * MOE_ROWS, 16)
        x = xs_sc[pl.ds(r0, MOE_ROWS), :]
        gate = jnp.dot(x, wg_ref[0], preferred_element_type=F32)
        up = jnp.dot(x, wu_ref[0], preferred_element_type=F32)
        a = (gate * _sigmoid(gate) * up).astype(BF16)
        acc_sc[pl.ds(r0, MOE_ROWS), :] += jnp.dot(a, wd_ref[0], preferred_element_type=F32)
        return carry

    lax.fori_loop(0, n_steps, swiglu, 0)

    @pl.when(last_c)
    def _():
        lane = lax.broadcasted_iota(jnp.int32, (tm, LANES), 1)
        pick = lambda ref: jnp.broadcast_to(
            jnp.sum(jnp.where(lane == e, ref[...], 0.0), axis=1, keepdims=True), (tm, LANES))
        rank_col = pick(rank_ref)
        gate_col = jnp.concatenate([pick(cw_ref)] * (D_MODEL // LANES), axis=1)

        def scatter(t, carry):
            r0 = pl.multiple_of(t * SCATTER_ROWS, SCATTER_ROWS)
            slot = (lane + r0).astype(F32)
            onehot = jnp.concatenate([(rank_col == slot + float(a * LANES)) for a in range(SCATTER_ROWS // LANES)],
                                     axis=1).astype(BF16)
            y = acc_sc[pl.ds(r0, SCATTER_ROWS), :].astype(BF16)
            o_ref[...] += gate_col * jnp.dot(onehot, y, preferred_element_type=F32)
            return carry

        lax.fori_loop(0, n_scatter, scatter, 0)

    @pl.when(jnp.logical_and(e == pl.num_programs(1) - 1, last_c))
    def _():
        o_ref[...] = _layer_norm(alpha * h_ref[...] + o_ref[...], g_ref[...], b_ref[...])


def _moe(h, cw, rank, rank_t, counts, w_gu, w_dn, g, b, *, alpha, tm, fc):
    S = h.shape[0]
    E, _, F2 = w_gu.shape
    nc = F2 // 2 // fc
    once = pl.Buffered(1)
    rows_cap = pl.cdiv(tm, MOE_ROWS) * MOE_ROWS
    grid_spec = pltpu.PrefetchScalarGridSpec(
        num_scalar_prefetch=1,
        grid=(S // tm, E, nc),
        in_specs=[pl.BlockSpec((tm, D_MODEL), lambda i, e, c, cnt: (i, 0), pipeline_mode=once),
                  pl.BlockSpec((tm, LANES), lambda i, e, c, cnt: (i, 0), pipeline_mode=once),
                  pl.BlockSpec((tm, LANES), lambda i, e, c, cnt: (i, 0), pipeline_mode=once),
                  pl.BlockSpec((1, N_EXPERTS, tm), lambda i, e, c, cnt: (i, 0, 0), pipeline_mode=once),
                  pl.BlockSpec((1, D_MODEL, fc), lambda i, e, c, cnt: (e, 0, c)),
                  pl.BlockSpec((1, D_MODEL, fc), lambda i, e, c, cnt: (e, 0, nc + c)),
                  pl.BlockSpec((1, fc, D_MODEL), lambda i, e, c, cnt: (e, c, 0)),
                  pl.BlockSpec(g.shape, lambda i, e, c, cnt: (0, 0)),
                  pl.BlockSpec(b.shape, lambda i, e, c, cnt: (0, 0))],
        out_specs=pl.BlockSpec((tm, D_MODEL), lambda i, e, c, cnt: (i, 0), pipeline_mode=once),
        scratch_shapes=[pltpu.VMEM((tm, D_MODEL), BF16), pltpu.VMEM((rows_cap, D_MODEL), BF16),
                        pltpu.VMEM((rows_cap, D_MODEL), F32)],
    )
    return pl.pallas_call(
        functools.partial(_moe_kernel, alpha=alpha),
        grid_spec=grid_spec,
        out_shape=jax.ShapeDtypeStruct((S, D_MODEL), F32),
        compiler_params=_cparams(("arbitrary", "arbitrary", "arbitrary")),
        name="moe_ffn",
    )(counts, h, cw, rank, rank_t, w_gu, w_gu, w_dn, g, b)


def _ffn_kernel(h_ref, wgu_ref, wd_ref, g_ref, b_ref, o_ref, *, alpha, fc):
    h = h_ref[...]
    hb = h.astype(BF16)
    d_ff = wd_ref.shape[0]
    acc = None
    for c in range(d_ff // fc):
        gate = jnp.dot(hb, wgu_ref[:, c * fc:(c + 1) * fc], preferred_element_type=F32)
        up = jnp.dot(hb, wgu_ref[:, d_ff + c * fc:d_ff + (c + 1) * fc], preferred_element_type=F32)
        a = (gate * _sigmoid(gate) * up).astype(BF16)
        y = jnp.dot(a, wd_ref[c * fc:(c + 1) * fc, :], preferred_element_type=F32)
        acc = y if acc is None else acc + y
    o_ref[...] = _layer_norm(alpha * h + acc, g_ref[...], b_ref[...])


def _ffn(h, w_gu, w_dn, g, b, *, alpha, tm, fc):
    S = h.shape[0]
    once = pl.Buffered(1)
    return pl.pallas_call(
        functools.partial(_ffn_kernel, alpha=alpha, fc=fc),
        grid=(S // tm,),
        in_specs=[pl.BlockSpec((tm, D_MODEL), lambda i: (i, 0)),
                  pl.BlockSpec(w_gu.shape, lambda i: (0, 0), pipeline_mode=once),
                  pl.BlockSpec(w_dn.shape, lambda i: (0, 0), pipeline_mode=once),
                  pl.BlockSpec(g.shape, lambda i: (0, 0)), pl.BlockSpec(b.shape, lambda i: (0, 0))],
        out_specs=pl.BlockSpec((tm, D_MODEL), lambda i: (i, 0)),
        out_shape=jax.ShapeDtypeStruct((S, D_MODEL), F32),
        compiler_params=_cparams(("arbitrary",)),
        name="dense_ffn",
    )(h, w_gu, w_dn, g, b)


def _pad_lanes(a, n=LANES):
    return jnp.pad(a, [(0, 0)] * (a.ndim - 1) + [(0, n - a.shape[-1])])


def kernel(x, ln_in_g, ln_in_b, w_in, b_fgate, lam_q1, lam_k1, lam_q2, lam_k2, subln_g, w_branch_fox, w_branch_diff, w_out, ln_mix_g, ln_mix_b, rel_bias, w_ffn_gate_up, w_ffn_down, w_router, w_expert_gate_up, w_expert_down, ln_ffn_g, ln_ffn_b):
    B, S, _ = x.shape
    assert B == 1
    depth = w_in.shape[0]
    alpha = (2.0 * depth) ** 0.25
    tq = min(ATTN_TQ, S)
    tk = min(ATTN_TK, tq)
    tm = min(512, S)
    tf = min(1024, S)
    te = min(MOE_TM, S)
    assert tk >= REL_MAX_DISTANCE and tq % tk == 0 and S % tq == 0 and S % tf == 0 and S % tm == 0 and S % te == 0
    assert te % SCATTER_ROWS == 0 and te % RANK_CHUNK == 0 and MOE_ROWS % 16 == 0
    row = lambda v: v.reshape(1, -1).astype(F32)

    table_flat = rel_bias.astype(F32).T.reshape(-1)
    bias = _bias_tiles(table_flat, t=tk)

    h = x[0]
    for l in range(depth):
        lam_init = 0.8 - 0.6 * math.exp(-0.3 * l)
        fq, fk, fv, fg, dq, dk, dv, ga, gb = jnp.split(w_in[l], list(_cumsum(IN_SPLIT_SIZES))[:-1], axis=-1)
        w_cat = jnp.concatenate([fq, fk, fv, dq, dk, dv, ga, gb, _pad_lanes(fg)], axis=-1).astype(BF16)
        outs = _inproj(h, row(ln_in_g), row(ln_in_b), w_cat, _pad_lanes(row(b_fgate[l])), do_ln=(l == 0), tm=tk)
        qa, ka, va, stats, dq, dk, dv, sa, sb, dkn = outs[:10]
        if l == 0:
            h = outs[10]
        start, kmax2 = _plan(stats, nsub=tq // tk)
        y_fox = _fox_attention(start[:S // tq, :FOX_HEADS].reshape(-1), qa, ka, va, kmax2, tq=tq, tk=tk)
        lamvec = jnp.zeros((8, LANES), F32).at[0:4, 0:HEAD_DIM].set(
            jnp.stack([lam_q1[l], lam_k1[l], lam_q2[l], lam_k2[l]]).astype(F32))
        y_diff = _diff_attention(table_flat, dq, dk, dv, bias, dkn, lamvec, row(subln_g[l]), tq=tq, tk=tk,
                                 lam_init=lam_init)
        h = _merge(y_fox, y_diff, sa, sb, h, w_branch_fox[l].astype(BF16), w_branch_diff[l].astype(BF16),
                   w_out[l].astype(BF16), row(ln_mix_g[l]), row(ln_mix_b[l]), alpha=alpha, tm=tm)
        g, b = row(ln_ffn_g[l]), row(ln_ffn_b[l])
        if l % 2 == 0:
            h = _ffn(h, w_ffn_gate_up[l // 2].astype(BF16), w_ffn_down[l // 2].astype(BF16), g, b,
                     alpha=alpha, tm=tf, fc=1408)
        else:
            wr = _pad_lanes(w_router[l // 2].astype(F32))
            whi = wr.astype(BF16)
            wlo = (wr - whi.astype(F32)).astype(BF16)
            cw, rank, rank_t, counts = _router(h, whi, wlo, tm=te)
            h = _moe(h, cw, rank, rank_t, counts[:, 0, :N_EXPERTS].reshape(-1),
                     w_expert_gate_up[l // 2].astype(BF16), w_expert_down[l // 2].astype(BF16),
                     g, b, alpha=alpha, tm=te, fc=896)
    return h[None]


def _cumsum(sizes):
    tot = 0
    for s in sizes:
        tot += s
        yield tot
```

```python
import functools
import math

import jax
import jax.numpy as jnp
from jax import lax
from jax.experimental import pallas as pl
from jax.experimental.pallas import tpu as pltpu

F32 = jnp.float32
BF16 = jnp.bfloat16

D_MODEL = 1024
HEAD_DIM = 64
FOX_HEADS = 8
FOX_WIDTH = FOX_HEADS * HEAD_DIM
DIFF_HEADS = 4
DIFF_QK_WIDTH = DIFF_HEADS * 2 * HEAD_DIM
DIFF_V_DIM = 2 * HEAD_DIM
DIFF_V_WIDTH = DIFF_HEADS * DIFF_V_DIM
IN_SPLIT_SIZES = (FOX_WIDTH, FOX_WIDTH, FOX_WIDTH, FOX_HEADS, DIFF_QK_WIDTH, DIFF_QK_WIDTH, DIFF_V_WIDTH,
                  D_MODEL, D_MODEL)
REL_BUCKETS = 32
REL_MAX_DISTANCE = 128
N_EXPERTS = 8
LN_EPS = 1e-5
SUBLN_EPS = 1e-5

LANES = 128
LOG2E = 1.4426950408889634
Q_SCALE = HEAD_DIM ** -0.5 * LOG2E
NEG = -1e30
VMEM_LIMIT = 56 * 1024 * 1024
ATTN_TQ = 2048
ATTN_TK = 512
MOE_TM = 2048


def _cparams(sem):
    return pltpu.CompilerParams(dimension_semantics=sem, vmem_limit_bytes=VMEM_LIMIT)


def _layer_norm(x, g, b):
    mu = jnp.mean(x, axis=-1, keepdims=True)
    xc = x - mu
    var = jnp.mean(xc * xc, axis=-1, keepdims=True)
    return xc * lax.rsqrt(var + LN_EPS) * g + b


def _sigmoid(x):
    return 1.0 / (1.0 + jnp.exp(-x))


def _split3(x):
    hi = x.astype(BF16)
    r1 = x - hi.astype(F32)
    mid = r1.astype(BF16)
    lo = (r1 - mid.astype(F32)).astype(BF16)
    return hi, mid, lo


_C_FQ, _C_FK, _C_FV, _C_DQ, _C_DK, _C_DV, _C_GA, _C_GB, _C_FG, _C_END = (
    0, 512, 1024, 1536, 2048, 2560, 3072, 4096, 5120, 5248)


def _inproj_kernel(x_ref, g_ref, b_ref, w_ref, bf_ref, qa_ref, ka_ref, va_ref, st_ref, dq, dk, dv, sa, sb, dkn,
                   *rest, do_ln):
    carry_ref = rest[-1]
    tb = x_ref.shape[0]

    @pl.when(pl.program_id(0) == 0)
    def _():
        carry_ref[...] = jnp.zeros_like(carry_ref)

    x = x_ref[...]
    if do_ln:
        x = _layer_norm(x, g_ref[...], b_ref[...])
        rest[0][...] = x
    xb = x.astype(BF16)

    def mm(a, b):
        return jnp.dot(xb, w_ref[:, a:b], preferred_element_type=F32)

    fq = (mm(_C_FQ, _C_FK) * Q_SCALE).astype(BF16)
    fk = mm(_C_FK, _C_FV).astype(BF16)
    fv = mm(_C_FV, _C_DQ).astype(BF16)

    z = mm(_C_FG, _C_END) + bf_ref[...]
    logf = jnp.minimum(z, 0.0) - jnp.log1p(jnp.exp(-jnp.abs(z)))
    r = lax.broadcasted_iota(jnp.int32, (tb, tb), 0)
    c = lax.broadcasted_iota(jnp.int32, (tb, tb), 1)
    tri = (c <= r).astype(BF16)
    hi, mid, lo = _split3(logf)
    dot = lambda a: jnp.dot(tri, a, preferred_element_type=F32)
    cum = (dot(hi) + dot(mid)) + dot(lo) + carry_ref[0:1, :]
    carry_ref[0:1, :] = cum[tb - 1:tb, :]

    c_hi, c_mid, c_lo = (t.astype(F32) for t in _split3(cum * LOG2E))
    lane = lax.broadcasted_iota(jnp.int32, (tb, LANES), 1)
    one = jnp.ones((tb, LANES), F32)
    zero = jnp.zeros((tb, LANES), F32)
    lane1 = lax.broadcasted_iota(jnp.int32, (1, LANES), 1)
    q_norm2 = jnp.zeros((1, LANES), F32)
    k_norm2 = jnp.zeros((1, LANES), F32)

    def max_norm2(slab):
        n2 = jnp.sum(jnp.where(lane < 64, slab * slab, 0.0), axis=1, keepdims=True)
        return jnp.max(n2, axis=0, keepdims=True)

    for h in range(FOX_HEADS):
        p, odd = divmod(h, 2)
        col = lambda t: jnp.broadcast_to(t[:, h:h + 1], (tb, LANES))
        chi, cmid, clo = col(c_hi), col(c_mid), col(c_lo)

        def head_slab(a):
            slab = a[:, p * LANES:(p + 1) * LANES].astype(F32)
            return pltpu.roll(slab, HEAD_DIM, 1) if odd else slab

        qslab = head_slab(fq)
        kslab = head_slab(fk)
        q_norm2 = jnp.where(lane1 == h, max_norm2(qslab), q_norm2)
        k_norm2 = jnp.where(lane1 == h, max_norm2(kslab), k_norm2)
        qa = jnp.where(lane < 64, qslab,
                       jnp.where(lane == 64, chi, jnp.where(lane == 65, cmid, jnp.where(lane == 66, clo,
                                 jnp.where(lane < 70, one, zero)))))
        ka = jnp.where(lane < 64, kslab,
                       jnp.where(lane < 67, one, jnp.where(lane == 67, -chi, jnp.where(lane == 68, -cmid,
                                 jnp.where(lane == 69, -clo, zero)))))
        qa_ref[h] = qa.astype(BF16)
        ka_ref[h] = ka.astype(BF16)
        vslab = fv[:, p * LANES:(p + 1) * LANES].astype(F32)
        if odd:
            va = jnp.where(lane >= 64, vslab, jnp.where(lane == 0, one, zero))
        else:
            va = jnp.where(lane < 64, vslab, jnp.where(lane == 64, one, zero))
        va_ref[h] = va.astype(BF16)

    dq[...] = (mm(_C_DQ, _C_DK) * Q_SCALE).astype(BF16)
    dk16 = mm(_C_DK, _C_DV).astype(BF16)
    dk[...] = dk16
    dv[...] = mm(_C_DV, _C_GA).astype(BF16)
    sa[...] = _sigmoid(mm(_C_GA, _C_GB)).astype(BF16)
    sb[...] = _sigmoid(mm(_C_GB, _C_FG)).astype(BF16)

    dk_norm2 = carry_ref[1:2, :]
    for h in range(DIFF_HEADS):
        slab = dk16[:, h * LANES:(h + 1) * LANES].astype(F32)
        sq = slab * slab
        for c in range(2):
            n2 = jnp.sum(jnp.where((lane < 64) == (c == 0), sq, 0.0), axis=1, keepdims=True)
            dk_norm2 = jnp.where(lane1 == 2 * h + c, jnp.maximum(dk_norm2, jnp.max(n2, axis=0, keepdims=True)),
                                 dk_norm2)
    carry_ref[1:2, :] = dk_norm2
    dkn[...] = jnp.broadcast_to(dk_norm2, dkn.shape)

    st_ref[0] = jnp.concatenate([q_norm2, k_norm2, cum[tb - 1:tb, :] * LOG2E, jnp.zeros((5, LANES), F32)], axis=0)


def _inproj(x, g, b, w, bf, *, do_ln, tm):
    S = x.shape[0]
    row = lambda n: pl.BlockSpec((tm, n), lambda i: (i, 0))
    full = lambda a: pl.BlockSpec(a.shape, lambda i: (0,) * a.ndim)
    hspec = pl.BlockSpec((FOX_HEADS, tm, LANES), lambda i: (0, i, 0))
    hshape = jax.ShapeDtypeStruct((FOX_HEADS, S, LANES), BF16)
    out_shape = [hshape] * 3 + [jax.ShapeDtypeStruct((S // tm, 8, LANES), F32)] + [
        jax.ShapeDtypeStruct((S, 512), BF16)] * 3 + [jax.ShapeDtypeStruct((S, D_MODEL), BF16)] * 2 + [
        jax.ShapeDtypeStruct((8, LANES), F32)]
    out_specs = [hspec] * 3 + [pl.BlockSpec((1, 8, LANES), lambda i: (i, 0, 0))] + [row(512)] * 3 + [
        row(D_MODEL)] * 2 + [pl.BlockSpec((8, LANES), lambda i: (0, 0))]
    if do_ln:
        out_shape.append(jax.ShapeDtypeStruct((S, D_MODEL), F32))
        out_specs.append(row(D_MODEL))
    return pl.pallas_call(
        functools.partial(_inproj_kernel, do_ln=do_ln),
        grid=(S // tm,),
        in_specs=[row(D_MODEL), full(g), full(b), full(w), full(bf)],
        out_specs=out_specs,
        out_shape=out_shape,
        scratch_shapes=[pltpu.VMEM((8, LANES), F32)],
        compiler_params=_cparams(("arbitrary",)),
        name="inproj",
    )(x, g, b, w, bf)


SKIP_LOG2 = 130.0


def _plan_kernel(st_ref, start_ref, *, nt, nsub):
    qn = jnp.sqrt(st_ref[pl.ds(0, nt, stride=8), :])
    kn = jnp.sqrt(st_ref[pl.ds(1, nt, stride=8), :])
    cum_end = st_ref[pl.ds(2, nt, stride=8), :]
    tile = lax.broadcasted_iota(jnp.int32, (nt, LANES), 0)
    rows = []
    for i in range(nt // nsub):
        own = jnp.logical_and(tile >= i * nsub, tile < (i + 1) * nsub)
        q_max = jnp.max(jnp.where(own, qn, 0.0), axis=0, keepdims=True)
        k_own = jnp.max(jnp.where(own, kn, 0.0), axis=0, keepdims=True)
        cum_q = cum_end[i * nsub - 1:i * nsub, :] if i > 0 else jnp.zeros((1, LANES), F32)
        upper = q_max * kn + (cum_q - cum_end) + q_max * k_own
        slack = SKIP_LOG2 + 1e-4 * jnp.abs(cum_end)
        needed = jnp.logical_and(upper >= -slack, tile < i * nsub)
        rows.append(jnp.min(jnp.where(needed, tile, i * nsub), axis=0, keepdims=True))
    rows += [jnp.zeros((1, LANES), jnp.int32)] * (8 - len(rows))
    start_ref[...] = jnp.concatenate(rows, axis=0)


def _plan(stats, *, nsub):
    nt = stats.shape[0]
    assert nt // nsub <= 8
    return pl.pallas_call(
        functools.partial(_plan_kernel, nt=nt, nsub=nsub),
        out_shape=jax.ShapeDtypeStruct((8, LANES), jnp.int32),
        compiler_params=pltpu.CompilerParams(vmem_limit_bytes=VMEM_LIMIT),
        name="fox_plan",
    )(stats.reshape(nt * 8, LANES))


def _nt_dot(a, b):
    return lax.dot_general(a, b, (((1,), (1,)), ((), ())), preferred_element_type=F32)


def _fox_kernel(start_ref, q_ref, k_ref, v_ref, o_ref, m_sc, acc_sc, *, tq, tk):
    i = pl.program_id(1)
    pair = pl.program_id(0)
    m_sc[...] = jnp.full_like(m_sc, NEG)
    acc_sc[...] = jnp.zeros_like(acc_sc)
    nsub = tq // tk

    def tile(j, r0, masked):
        off = pl.multiple_of(j * tk, tk)
        rows = tq - r0
        for hh in range(2):
            s = _nt_dot(q_ref[hh, r0:, :], k_ref[hh, pl.ds(off, tk), :])
            if masked:
                r = lax.broadcasted_iota(jnp.int32, (rows, tk), 0)
                c = lax.broadcasted_iota(jnp.int32, (rows, tk), 1)
                s = jnp.where(c <= r, s, NEG)
            m_prev = m_sc[hh, r0:, :]
            m_next = jnp.maximum(m_prev, jnp.max(s, axis=1, keepdims=True))
            alpha = jnp.exp2(m_prev - m_next)
            p = jnp.exp2(s - jnp.concatenate([m_next] * (tk // LANES), axis=1)).astype(BF16)
            pv = jnp.dot(p, v_ref[hh, pl.ds(off, tk), :], preferred_element_type=F32)
            acc_sc[hh, r0:, :] = alpha * acc_sc[hh, r0:, :] + pv
            m_sc[hh, r0:, :] = m_next

    def body(j, carry):
        tile(j, 0, False)
        return carry

    first = jnp.minimum(start_ref[i * FOX_HEADS + 2 * pair], start_ref[i * FOX_HEADS + 2 * pair + 1])
    lax.fori_loop(first, i * nsub, body, 0)
    for u in range(nsub):
        tile(i * nsub + u, u * tk, True)

    a0 = acc_sc[0]
    a1 = acc_sc[1]
    lane = lax.broadcasted_iota(jnp.int32, (tq, LANES), 1)
    o_ref[...] = jnp.where(lane < 64, a0 / a0[:, 64:65], a1 / a1[:, 0:1]).astype(o_ref.dtype)


def _fox_attention(start, qa, ka, va, *, tq, tk):
    S = qa.shape[1]
    grid_spec = pltpu.PrefetchScalarGridSpec(
        num_scalar_prefetch=1,
        grid=(FOX_HEADS // 2, S // tq),
        in_specs=[pl.BlockSpec((2, tq, LANES), lambda p, i, st: (p, i, 0)),
                  pl.BlockSpec((2, S, LANES), lambda p, i, st: (p, 0, 0)),
                  pl.BlockSpec((2, S, LANES), lambda p, i, st: (p, 0, 0))],
        out_specs=pl.BlockSpec((tq, LANES), lambda p, i, st: (i, p)),
        scratch_shapes=[pltpu.VMEM((2, tq, LANES), F32), pltpu.VMEM((2, tq, LANES), F32)],
    )
    return pl.pallas_call(
        functools.partial(_fox_kernel, tq=tq, tk=tk),
        grid_spec=grid_spec,
        out_shape=jax.ShapeDtypeStruct((S, FOX_WIDTH), BF16),
        compiler_params=_cparams(("arbitrary", "arbitrary")),
        name="fox_attn",
    )(start, qa, ka, va)


BIAS_KINDS = 2


def _bias_tile_kernel(table_ref, o_ref, *, t):
    h = pl.program_id(0)
    kind = pl.program_id(1)
    r = lax.broadcasted_iota(jnp.int32, (t, t), 0)
    c = lax.broadcasted_iota(jnp.int32, (t, t), 1)
    dist = r - c + kind * t
    n = jnp.maximum(dist, 0)
    max_exact = REL_BUCKETS // 2
    nf = jnp.maximum(n, 1).astype(F32)
    log_part = jnp.log(nf / max_exact) / math.log(REL_MAX_DISTANCE / max_exact) * (REL_BUCKETS - max_exact)
    large = jnp.minimum(max_exact + log_part.astype(jnp.int32), REL_BUCKETS - 1)
    bucket = jnp.where(n < max_exact, n, large)
    val = jnp.zeros((t, t), F32)
    for b in range(REL_BUCKETS):
        val = jnp.where(bucket == b, table_ref[h * REL_BUCKETS + b], val)
    far = table_ref[h * REL_BUCKETS + REL_BUCKETS - 1]
    o_ref[0, 0] = jnp.where(dist >= 0, (val - far) * LOG2E, NEG)


def _bias_tiles(table_flat, *, t):
    return pl.pallas_call(
        functools.partial(_bias_tile_kernel, t=t),
        grid=(DIFF_HEADS, BIAS_KINDS),
        in_specs=[pl.BlockSpec(memory_space=pltpu.SMEM)],
        out_specs=pl.BlockSpec((1, 1, t, t), lambda h, k: (h, k, 0, 0)),
        out_shape=jax.ShapeDtypeStruct((DIFF_HEADS, BIAS_KINDS, t, t), F32),
        compiler_params=_cparams(("arbitrary", "arbitrary")),
        name="bias_tiles",
    )(table_flat)


FIXED_REF_MAX_SHIFT = 64.0


def _diff_kernel(tab_ref, q_ref, k_ref, v_ref, bias_ref, kn_ref, lam_ref, g_ref, o_ref, m_sc, l_sc, acc_sc,
                 *, tq, tk, lam_init):
    h = pl.program_id(0)
    i = pl.program_id(1)
    l_sc[...] = jnp.zeros_like(l_sc)
    acc_sc[...] = jnp.zeros_like(acc_sc)
    nsub = tq // tk

    lane = lax.broadcasted_iota(jnp.int32, (tq, LANES), 1)
    q = q_ref[...]
    zero = jnp.zeros_like(q)
    qs = (jnp.where(lane < 64, q, zero), jnp.where(lane >= 64, q, zero))

    def run(fixed):
        def tile(j, r0, kinds):
            off = pl.multiple_of(j * tk, tk)
            k = k_ref[pl.ds(off, tk), :]
            v = v_ref[pl.ds(off, tk), :]
            for c in range(2):
                s = _nt_dot(qs[c][r0:], k)
                if kinds is not None:
                    s = jnp.concatenate([s[a * tk:(a + 1) * tk] + bias_ref[0, kd] if kd < 2
                                         else s[a * tk:(a + 1) * tk] for a, kd in enumerate(kinds)], axis=0)
                m_prev = m_sc[c, r0:, :]
                if fixed:
                    p = jnp.exp2(s - jnp.concatenate([m_prev] * (tk // LANES), axis=1))
                    l_sc[c, r0:, :] += sum(p[:, a * LANES:(a + 1) * LANES] for a in range(tk // LANES))
                    acc_sc[c, r0:, :] += jnp.dot(p.astype(BF16), v, preferred_element_type=F32)
                else:
                    m_next = jnp.maximum(m_prev, jnp.max(s, axis=1, keepdims=True))
                    p = jnp.exp2(s - jnp.concatenate([m_next] * (tk // LANES), axis=1))
                    alpha = jnp.exp2(m_prev - m_next)
                    l_sc[c, r0:, :] = alpha * l_sc[c, r0:, :] + sum(p[:, a * LANES:(a + 1) * LANES]
                                                                    for a in range(tk // LANES))
                    pv = jnp.dot(p.astype(BF16), v, preferred_element_type=F32)
                    acc_sc[c, r0:, :] = alpha * acc_sc[c, r0:, :] + pv
                    m_sc[c, r0:, :] = m_next

        def body(j, carry):
            tile(j, 0, None)
            return carry

        lax.fori_loop(0, i * nsub - 1, body, 0)

        @pl.when(i > 0)
        def _():
            tile(i * nsub - 1, 0, (1,) + (2,) * (nsub - 1))

        for u in range(nsub):
            tile(i * nsub + u, u * tk, ((0, 1) + (2,) * nsub)[:nsub - u])

    lane1 = lax.broadcasted_iota(jnp.int32, (1, LANES), 1)
    far = tab_ref[h * REL_BUCKETS + REL_BUCKETS - 1]
    b0 = (tab_ref[h * REL_BUCKETS] - far) * LOG2E
    bmax = b0
    for b in range(1, REL_BUCKETS):
        bmax = jnp.maximum(bmax, (tab_ref[h * REL_BUCKETS + b] - far) * LOG2E)
    bounds = []
    for c in range(2):
        qf = qs[c].astype(F32)
        qn = jnp.sqrt(jnp.sum(qf * qf, axis=1, keepdims=True))
        kmax = jnp.sqrt(jnp.sum(jnp.where(lane1 == 2 * h + c, kn_ref[0:1, :], 0.0), axis=1, keepdims=True))
        bounds.append(qn * kmax + 1.0)
    gap = 2.0 * jnp.max(jnp.maximum(bounds[0], bounds[1])) + (bmax - b0)
    fits = gap <= FIXED_REF_MAX_SHIFT

    @pl.when(fits)
    def _():
        for c in range(2):
            m_sc[c] = jnp.broadcast_to(bounds[c] + bmax, (tq, LANES))
        run(True)

    @pl.when(jnp.logical_not(fits))
    def _():
        m_sc[...] = jnp.full_like(m_sc, NEG)
        run(False)

    lv = lam_ref[...]
    lam = (jnp.exp(jnp.sum(lv[0:1] * lv[1:2], axis=1, keepdims=True))
           - jnp.exp(jnp.sum(lv[2:3] * lv[3:4], axis=1, keepdims=True)) + lam_init)
    l0 = jnp.sum(l_sc[0], axis=1, keepdims=True)
    l1 = jnp.sum(l_sc[1], axis=1, keepdims=True)
    o = acc_sc[0] / l0 - lam * (acc_sc[1] / l1)
    o = o * lax.rsqrt(jnp.mean(o * o, axis=1, keepdims=True) + SUBLN_EPS) * g_ref[...] * (1.0 - lam_init)
    o_ref[...] = o.astype(o_ref.dtype)


def _diff_attention(table_flat, dq, dk, dv, bias, dkn, lamvec, g, *, tq, tk, lam_init):
    S = dq.shape[0]
    grid_spec = pltpu.PrefetchScalarGridSpec(
        num_scalar_prefetch=1,
        grid=(DIFF_HEADS, S // tq),
        in_specs=[pl.BlockSpec((tq, LANES), lambda h, i, tab: (i, h)),
                  pl.BlockSpec((S, LANES), lambda h, i, tab: (0, h)),
                  pl.BlockSpec((S, LANES), lambda h, i, tab: (0, h)),
                  pl.BlockSpec((1, BIAS_KINDS, tk, tk), lambda h, i, tab: (h, 0, 0, 0)),
                  pl.BlockSpec((8, LANES), lambda h, i, tab: (0, 0)),
                  pl.BlockSpec((8, LANES), lambda h, i, tab: (0, 0)),
                  pl.BlockSpec((1, LANES), lambda h, i, tab: (0, 0))],
        out_specs=pl.BlockSpec((tq, LANES), lambda h, i, tab: (i, h)),
        scratch_shapes=[pltpu.VMEM((2, tq, LANES), F32), pltpu.VMEM((2, tq, LANES), F32),
                        pltpu.VMEM((2, tq, LANES), F32)],
    )
    return pl.pallas_call(
        functools.partial(_diff_kernel, tq=tq, tk=tk, lam_init=lam_init),
        grid_spec=grid_spec,
        out_shape=jax.ShapeDtypeStruct((S, DIFF_V_WIDTH), BF16),
        compiler_params=_cparams(("arbitrary", "arbitrary")),
        name="diff_attn",
    )(table_flat, dq, dk, dv, bias, dkn, lamvec, g)


def _merge_kernel(yf_ref, yd_ref, sa_ref, sb_ref, h_ref, wf_ref, wd_ref, wo_ref, g_ref, b_ref, o_ref, *, alpha):
    bf = jnp.dot(yf_ref[...], wf_ref[...], preferred_element_type=F32)
    bd = jnp.dot(yd_ref[...], wd_ref[...], preferred_element_type=F32)
    merged = sa_ref[...].astype(F32) * bf + sb_ref[...].astype(F32) * bd
    mix = jnp.dot(merged.astype(BF16), wo_ref[...], preferred_element_type=F32)
    o_ref[...] = _layer_norm(alpha * h_ref[...] + mix, g_ref[...], b_ref[...])


def _merge(yf, yd, sa, sb, h, wf, wd, wo, g, b, *, alpha, tm):
    S = h.shape[0]
    row = lambda n: pl.BlockSpec((tm, n), lambda i: (i, 0))
    full = lambda a: pl.BlockSpec(a.shape, lambda i: (0,) * a.ndim)
    return pl.pallas_call(
        functools.partial(_merge_kernel, alpha=alpha),
        grid=(S // tm,),
        in_specs=[row(512), row(512), row(D_MODEL), row(D_MODEL), row(D_MODEL),
                  full(wf), full(wd), full(wo), full(g), full(b)],
        out_specs=row(D_MODEL),
        out_shape=jax.ShapeDtypeStruct((S, D_MODEL), F32),
        compiler_params=_cparams(("arbitrary",)),
        name="merge_out",
    )(yf, yd, sa, sb, h, wf, wd, wo, g, b)


MOE_ROWS = 272
SCATTER_ROWS = 256
RANK_CHUNK = 256


def _router_kernel(h_ref, whi_ref, wlo_ref, cw_ref, rank_ref, rank_t_ref, cnt_ref):
    tm = h_ref.shape[0]
    h = h_ref[...]
    hhi = h.astype(BF16)
    hlo = (h - hhi.astype(F32)).astype(BF16)
    whi = whi_ref[...]
    dot = lambda a, b: jnp.dot(a, b, preferred_element_type=F32)
    logits = dot(hhi, whi) + dot(hhi, wlo_ref[...]) + dot(hlo, whi)
    lane = lax.broadcasted_iota(jnp.int32, logits.shape, 1).astype(F32)
    lg = jnp.where(lane < N_EXPERTS, logits, NEG)
    m1 = jnp.max(lg, axis=1, keepdims=True)
    i1 = jnp.min(jnp.where(lg == m1, lane, float(LANES)), axis=1, keepdims=True)
    lg2 = jnp.where(lane == i1, NEG, lg)
    m2 = jnp.max(lg2, axis=1, keepdims=True)
    i2 = jnp.min(jnp.where(lg2 == m2, lane, float(LANES)), axis=1, keepdims=True)
    e = jnp.exp(m2 - m1)
    g1 = 1.0 / (1.0 + e)
    chosen = jnp.logical_or(lane == i1, lane == i2)
    cw_ref[...] = jnp.where(lane == i1, g1, 0.0) + jnp.where(lane == i2, e * g1, 0.0)

    a = chosen.astype(BF16)
    r = lax.broadcasted_iota(jnp.int32, (RANK_CHUNK, RANK_CHUNK), 0)
    c = lax.broadcasted_iota(jnp.int32, (RANK_CHUNK, RANK_CHUNK), 1)
    strict_lower = (c < r).astype(BF16)
    ones_row = jnp.ones((8, RANK_CHUNK), BF16)
    carry = jnp.zeros((1, LANES), F32)
    for q in range(tm // RANK_CHUNK):
        rows = slice(q * RANK_CHUNK, (q + 1) * RANK_CHUNK)
        rank = jnp.where(chosen[rows], dot(strict_lower, a[rows]) + carry, -1.0)
        rank_ref[rows, :] = rank
        rank_t_ref[0, :, rows] = jnp.transpose(rank)[0:N_EXPERTS, :]
        carry = carry + dot(ones_row, a[rows])[0:1, :]
    cnt_ref[0] = jnp.broadcast_to(carry, (8, LANES)).astype(jnp.int32)


def _router(h, whi, wlo, *, tm):
    S = h.shape[0]
    nb = S // tm
    return pl.pallas_call(
        _router_kernel,
        grid=(nb,),
        in_specs=[pl.BlockSpec((tm, D_MODEL), lambda i: (i, 0)),
                  pl.BlockSpec(whi.shape, lambda i: (0, 0)), pl.BlockSpec(wlo.shape, lambda i: (0, 0))],
        out_specs=[pl.BlockSpec((tm, LANES), lambda i: (i, 0)),
                   pl.BlockSpec((tm, LANES), lambda i: (i, 0)),
                   pl.BlockSpec((1, N_EXPERTS, tm), lambda i: (i, 0, 0)),
                   pl.BlockSpec((1, 8, LANES), lambda i: (i, 0, 0))],
        out_shape=[jax.ShapeDtypeStruct((S, LANES), F32),
                   jax.ShapeDtypeStruct((S, LANES), F32),
                   jax.ShapeDtypeStruct((nb, N_EXPERTS, tm), F32),
                   jax.ShapeDtypeStruct((nb, 8, LANES), jnp.int32)],
        compiler_params=_cparams(("arbitrary",)),
        name="router",
    )(h, whi, wlo)


def _moe_kernel(cnt_ref, h_ref, cw_ref, rank_ref, rank_t_ref, wg_ref, wu_ref, wd_ref, g_ref, b_ref, o_ref,
                hb_sc, xs_sc, acc_sc, *, alpha):
    tm = h_ref.shape[0]
    i = pl.program_id(0)
    e = pl.program_id(1)
    c = pl.program_id(2)
    last_c = c == pl.num_programs(2) - 1
    n_rows = cnt_ref[i * N_EXPERTS + e]
    n_steps = (n_rows + MOE_ROWS - 1) // MOE_ROWS
    n_scatter = (n_rows + SCATTER_ROWS - 1) // SCATTER_ROWS

    @pl.when(jnp.logical_and(e == 0, c == 0))
    def _():
        hb_sc[...] = h_ref[...].astype(BF16)
        o_ref[...] = jnp.zeros_like(o_ref)

    @pl.when(c == 0)
    def _():
        rank_row = rank_t_ref[0, pl.ds(e, 1), :]

        def gather(t, carry):
            r0 = pl.multiple_of(t * MOE_ROWS, 16)
            slot = (lax.broadcasted_iota(jnp.int32, (MOE_ROWS, tm), 0) + r0).astype(F32)
            onehot = (rank_row == slot).astype(BF16)
            xs_sc[pl.ds(r0, MOE_ROWS), :] = jnp.dot(onehot, hb_sc[...], preferred_element_type=F32).astype(BF16)
            acc_sc[pl.ds(r0, MOE_ROWS), :] = jnp.zeros((MOE_ROWS, D_MODEL), F32)
            return carry

        lax.fori_loop(0, n_steps, gather, 0)

        @pl.when(n_steps * MOE_ROWS < n_scatter * SCATTER_ROWS)
        def _():
            acc_sc[pl.ds(pl.multiple_of(n_steps * MOE_ROWS, 16), SCATTER_ROWS), :] = jnp.zeros(
                (SCATTER_ROWS, D_MODEL), F32)

    def swiglu(t, carry):
        r0 = pl.multiple_of(t * MOE_ROWS, 16)
        x = xs_sc[pl.ds(r0, MOE_ROWS), :]
        gate = jnp.dot(x, wg_ref[0], preferred_element_type=F32)
        up = jnp.dot(x, wu_ref[0], preferred_element_type=F32)
        a = (gate * _sigmoid(gate) * up).astype(BF16)
        acc_sc[pl.ds(r0, MOE_ROWS), :] += jnp.dot(a, wd_ref[0], preferred_element_type=F32)
        return carry

    lax.fori_loop(0, n_steps, swiglu, 0)

    @pl.when(last_c)
    def _():
        lane = lax.broadcasted_iota(jnp.int32, (tm, LANES), 1)
        pick = lambda ref: jnp.broadcast_to(
            jnp.sum(jnp.where(lane == e, ref[...], 0.0), axis=1, keepdims=True), (tm, LANES))
        rank_col = pick(rank_ref)
        gate_col = jnp.concatenate([pick(cw_ref)] * (D_MODEL // LANES), axis=1)

        def scatter(t, carry):
            r0 = pl.multiple_of(t * SCATTER_ROWS, SCATTER_ROWS)
            slot = (lane + r0).astype(F32)
            onehot = jnp.concatenate([(rank_col == slot + float(a * LANES)) for a in range(SCATTER_ROWS // LANES)],
                                     axis=1).astype(BF16)
            y = acc_sc[pl.ds(r0, SCATTER_ROWS), :].astype(BF16)
            o_ref[...] += gate_col * jnp.dot(onehot, y, preferred_element_type=F32)
            return carry

        lax.fori_loop(0, n_scatter, scatter, 0)

    @pl.when(jnp.logical_and(e == pl.num_programs(1) - 1, last_c))
    def _():
        o_ref[...] = _layer_norm(alpha * h_ref[...] + o_ref[...], g_ref[...], b_ref[...])


def _moe(h, cw, rank, rank_t, counts, w_gu, w_dn, g, b, *, alpha, tm, fc):
    S = h.shape[0]
    E, _, F2 = w_gu.shape
    nc = F2 // 2 // fc
    once = pl.Buffered(1)
    rows_cap = pl.cdiv(tm, MOE_ROWS) * MOE_ROWS
    grid_spec = pltpu.PrefetchScalarGridSpec(
        num_scalar_prefetch=1,
        grid=(S // tm, E, nc),
        in_specs=[pl.BlockSpec((tm, D_MODEL), lambda i, e, c, cnt: (i, 0), pipeline_mode=once),
                  pl.BlockSpec((tm, LANES), lambda i, e, c, cnt: (i, 0), pipeline_mode=once),
                  pl.BlockSpec((tm, LANES), lambda i, e, c, cnt: (i, 0), pipeline_mode=once),
                  pl.BlockSpec((1, N_EXPERTS, tm), lambda i, e, c, cnt: (i, 0, 0), pipeline_mode=once),
                  pl.BlockSpec((1, D_MODEL, fc), lambda i, e, c, cnt: (e, 0, c)),
                  pl.BlockSpec((1, D_MODEL, fc), lambda i, e, c, cnt: (e, 0, nc + c)),
                  pl.BlockSpec((1, fc, D_MODEL), lambda i, e, c, cnt: (e, c, 0)),
                  pl.BlockSpec(g.shape, lambda i, e, c, cnt: (0, 0)),
                  pl.BlockSpec(b.shape, lambda i, e, c, cnt: (0, 0))],
        out_specs=pl.BlockSpec((tm, D_MODEL), lambda i, e, c, cnt: (i, 0), pipeline_mode=once),
        scratch_shapes=[pltpu.VMEM((tm, D_MODEL), BF16), pltpu.VMEM((rows_cap, D_MODEL), BF16),
                        pltpu.VMEM((rows_cap, D_MODEL), F32)],
    )
    return pl.pallas_call(
        functools.partial(_moe_kernel, alpha=alpha),
        grid_spec=grid_spec,
        out_shape=jax.ShapeDtypeStruct((S, D_MODEL), F32),
        compiler_params=_cparams(("arbitrary", "arbitrary", "arbitrary")),
        name="moe_ffn",
    )(counts, h, cw, rank, rank_t, w_gu, w_gu, w_dn, g, b)


def _ffn_kernel(h_ref, wgu_ref, wd_ref, g_ref, b_ref, o_ref, *, alpha, fc):
    h = h_ref[...]
    hb = h.astype(BF16)
    d_ff = wd_ref.shape[0]
    acc = None
    for c in range(d_ff // fc):
        gate = jnp.dot(hb, wgu_ref[:, c * fc:(c + 1) * fc], preferred_element_type=F32)
        up = jnp.dot(hb, wgu_ref[:, d_ff + c * fc:d_ff + (c + 1) * fc], preferred_element_type=F32)
        a = (gate * _sigmoid(gate) * up).astype(BF16)
        y = jnp.dot(a, wd_ref[c * fc:(c + 1) * fc, :], preferred_element_type=F32)
        acc = y if acc is None else acc + y
    o_ref[...] = _layer_norm(alpha * h + acc, g_ref[...], b_ref[...])


def _ffn(h, w_gu, w_dn, g, b, *, alpha, tm, fc):
    S = h.shape[0]
    once = pl.Buffered(1)
    return pl.pallas_call(
        functools.partial(_ffn_kernel, alpha=alpha, fc=fc),
        grid=(S // tm,),
        in_specs=[pl.BlockSpec((tm, D_MODEL), lambda i: (i, 0)),
                  pl.BlockSpec(w_gu.shape, lambda i: (0, 0), pipeline_mode=once),
                  pl.BlockSpec(w_dn.shape, lambda i: (0, 0), pipeline_mode=once),
                  pl.BlockSpec(g.shape, lambda i: (0, 0)), pl.BlockSpec(b.shape, lambda i: (0, 0))],
        out_specs=pl.BlockSpec((tm, D_MODEL), lambda i: (i, 0)),
        out_shape=jax.ShapeDtypeStruct((S, D_MODEL), F32),
        compiler_params=_cparams(("arbitrary",)),
        name="dense_ffn",
    )(h, w_gu, w_dn, g, b)


def _pad_lanes(a, n=LANES):
    return jnp.pad(a, [(0, 0)] * (a.ndim - 1) + [(0, n - a.shape[-1])])


def kernel(x, ln_in_g, ln_in_b, w_in, b_fgate, lam_q1, lam_k1, lam_q2, lam_k2, subln_g, w_branch_fox, w_branch_diff, w_out, ln_mix_g, ln_mix_b, rel_bias, w_ffn_gate_up, w_ffn_down, w_router, w_expert_gate_up, w_expert_down, ln_ffn_g, ln_ffn_b):
    B, S, _ = x.shape
    assert B == 1
    depth = w_in.shape[0]
    alpha = (2.0 * depth) ** 0.25
    tq = min(ATTN_TQ, S)
    tk = min(ATTN_TK, tq)
    tm = min(512, S)
    tf = min(1024, S)
    te = min(MOE_TM, S)
    assert tk >= REL_MAX_DISTANCE and tq % tk == 0 and S % tq == 0 and S % tf == 0 and S % tm == 0 and S % te == 0
    assert te % SCATTER_ROWS == 0 and te % RANK_CHUNK == 0 and MOE_ROWS % 16 == 0
    row = lambda v: v.reshape(1, -1).astype(F32)

    table_flat = rel_bias.astype(F32).T.reshape(-1)
    bias = _bias_tiles(table_flat, t=tk)

    h = x[0]
    for l in range(depth):
        lam_init = 0.8 - 0.6 * math.exp(-0.3 * l)
        fq, fk, fv, fg, dq, dk, dv, ga, gb = jnp.split(w_in[l], list(_cumsum(IN_SPLIT_SIZES))[:-1], axis=-1)
        w_cat = jnp.concatenate([fq, fk, fv, dq, dk, dv, ga, gb, _pad_lanes(fg)], axis=-1).astype(BF16)
        outs = _inproj(h, row(ln_in_g), row(ln_in_b), w_cat, _pad_lanes(row(b_fgate[l])), do_ln=(l == 0), tm=tk)
        qa, ka, va, stats, dq, dk, dv, sa, sb, dkn = outs[:10]
        if l == 0:
            h = outs[10]
        start = _plan(stats, nsub=tq // tk)[:S // tq, :FOX_HEADS].reshape(-1)
        y_fox = _fox_attention(start, qa, ka, va, tq=tq, tk=tk)
        lamvec = jnp.zeros((8, LANES), F32).at[0:4, 0:HEAD_DIM].set(
            jnp.stack([lam_q1[l], lam_k1[l], lam_q2[l], lam_k2[l]]).astype(F32))
        y_diff = _diff_attention(table_flat, dq, dk, dv, bias, dkn, lamvec, row(subln_g[l]), tq=tq, tk=tk,
                                 lam_init=lam_init)
        h = _merge(y_fox, y_diff, sa, sb, h, w_branch_fox[l].astype(BF16), w_branch_diff[l].astype(BF16),
                   w_out[l].astype(BF16), row(ln_mix_g[l]), row(ln_mix_b[l]), alpha=alpha, tm=tm)
        g, b = row(ln_ffn_g[l]), row(ln_ffn_b[l])
        if l % 2 == 0:
            h = _ffn(h, w_ffn_gate_up[l // 2].astype(BF16), w_ffn_down[l // 2].astype(BF16), g, b,
                     alpha=alpha, tm=tf, fc=1408)
        else:
            wr = _pad_lanes(w_router[l // 2].astype(F32))
            whi = wr.astype(BF16)
            wlo = (wr - whi.astype(F32)).astype(BF16)
            cw, rank, rank_t, counts = _router(h, whi, wlo, tm=te)
            h = _moe(h, cw, rank, rank_t, counts[:, 0, :N_EXPERTS].reshape(-1),
                     w_expert_gate_up[l // 2].astype(BF16), w_expert_down[l // 2].astype(BF16),
                     g, b, alpha=alpha, tm=te, fc=896)
    return h[None]


def _cumsum(sizes):
    tot = 0
    for s in sizes:
        tot += s
        yield tot
```

```python
import functools
import math

import jax
import jax.numpy as jnp
from jax import lax
from jax.experimental import pallas as pl
from jax.experimental.pallas import tpu as pltpu

F32 = jnp.float32
BF16 = jnp.bfloat16

D_MODEL = 1024
HEAD_DIM = 64
FOX_HEADS = 8
FOX_WIDTH = FOX_HEADS * HEAD_DIM
DIFF_HEADS = 4
DIFF_QK_WIDTH = DIFF_HEADS * 2 * HEAD_DIM
DIFF_V_DIM = 2 * HEAD_DIM
DIFF_V_WIDTH = DIFF_HEADS * DIFF_V_DIM
IN_SPLIT_SIZES = (FOX_WIDTH, FOX_WIDTH, FOX_WIDTH, FOX_HEADS, DIFF_QK_WIDTH, DIFF_QK_WIDTH, DIFF_V_WIDTH,
                  D_MODEL, D_MODEL)
REL_BUCKETS = 32
REL_MAX_DISTANCE = 128
N_EXPERTS = 8
LN_EPS = 1e-5
SUBLN_EPS = 1e-5

LANES = 128
LOG2E = 1.4426950408889634
Q_SCALE = HEAD_DIM ** -0.5 * LOG2E
NEG = -1e30
VMEM_LIMIT = 56 * 1024 * 1024
ATTN_TQ = 2048
ATTN_TK = 512
MOE_TM = 2048


def _cparams(sem):
    return pltpu.CompilerParams(dimension_semantics=sem, vmem_limit_bytes=VMEM_LIMIT)


def _layer_norm(x, g, b):
    mu = jnp.mean(x, axis=-1, keepdims=True)
    xc = x - mu
    var = jnp.mean(xc * xc, axis=-1, keepdims=True)
    return xc * lax.rsqrt(var + LN_EPS) * g + b


def _sigmoid(x):
    return 1.0 / (1.0 + jnp.exp(-x))


def _split3(x):
    hi = x.astype(BF16)
    r1 = x - hi.astype(F32)
    mid = r1.astype(BF16)
    lo = (r1 - mid.astype(F32)).astype(BF16)
    return hi, mid, lo


_C_FQ, _C_FK, _C_FV, _C_DQ, _C_DK, _C_DV, _C_GA, _C_GB, _C_FG, _C_END = (
    0, 512, 1024, 1536, 2048, 2560, 3072, 4096, 5120, 5248)


def _inproj_kernel(x_ref, g_ref, b_ref, w_ref, bf_ref, qa_ref, ka_ref, va_ref, st_ref, dq, dk, dv, sa, sb, dkn,
                   *rest, do_ln):
    carry_ref = rest[-1]
    tb = x_ref.shape[0]

    @pl.when(pl.program_id(0) == 0)
    def _():
        carry_ref[...] = jnp.zeros_like(carry_ref)

    x = x_ref[...]
    if do_ln:
        x = _layer_norm(x, g_ref[...], b_ref[...])
        rest[0][...] = x
    xb = x.astype(BF16)

    def mm(a, b):
        return jnp.dot(xb, w_ref[:, a:b], preferred_element_type=F32)

    fq = (mm(_C_FQ, _C_FK) * Q_SCALE).astype(BF16)
    fk = mm(_C_FK, _C_FV).astype(BF16)
    fv = mm(_C_FV, _C_DQ).astype(BF16)

    z = mm(_C_FG, _C_END) + bf_ref[...]
    logf = jnp.minimum(z, 0.0) - jnp.log1p(jnp.exp(-jnp.abs(z)))
    r = lax.broadcasted_iota(jnp.int32, (tb, tb), 0)
    c = lax.broadcasted_iota(jnp.int32, (tb, tb), 1)
    tri = (c <= r).astype(BF16)
    hi, mid, lo = _split3(logf)
    dot = lambda a: jnp.dot(tri, a, preferred_element_type=F32)
    cum = (dot(hi) + dot(mid)) + dot(lo) + carry_ref[0:1, :]
    carry_ref[0:1, :] = cum[tb - 1:tb, :]

    c_hi, c_mid, c_lo = (t.astype(F32) for t in _split3(cum * LOG2E))
    lane = lax.broadcasted_iota(jnp.int32, (tb, LANES), 1)
    one = jnp.ones((tb, LANES), F32)
    zero = jnp.zeros((tb, LANES), F32)
    lane1 = lax.broadcasted_iota(jnp.int32, (1, LANES), 1)
    q_norm2 = jnp.zeros((1, LANES), F32)
    k_norm2 = jnp.zeros((1, LANES), F32)
    diag_min = jnp.zeros((1, LANES), F32)

    def max_norm2(slab):
        n2 = jnp.sum(jnp.where(lane < 64, slab * slab, 0.0), axis=1, keepdims=True)
        return jnp.max(n2, axis=0, keepdims=True)

    for h in range(FOX_HEADS):
        p, odd = divmod(h, 2)
        col = lambda t: jnp.broadcast_to(t[:, h:h + 1], (tb, LANES))
        chi, cmid, clo = col(c_hi), col(c_mid), col(c_lo)

        def head_slab(a):
            slab = a[:, p * LANES:(p + 1) * LANES].astype(F32)
            return pltpu.roll(slab, HEAD_DIM, 1) if odd else slab

        qslab = head_slab(fq)
        kslab = head_slab(fk)
        q_norm2 = jnp.where(lane1 == h, max_norm2(qslab), q_norm2)
        k_norm2 = jnp.where(lane1 == h, max_norm2(kslab), k_norm2)
        own = jnp.sum(jnp.where(lane < 64, qslab * kslab, 0.0), axis=1, keepdims=True)
        diag_min = jnp.where(lane1 == h, jnp.min(own, axis=0, keepdims=True), diag_min)
        qa = jnp.where(lane < 64, qslab,
                       jnp.where(lane == 64, chi, jnp.where(lane == 65, cmid, jnp.where(lane == 66, clo,
                                 jnp.where(lane < 70, one, zero)))))
        ka = jnp.where(lane < 64, kslab,
                       jnp.where(lane < 67, one, jnp.where(lane == 67, -chi, jnp.where(lane == 68, -cmid,
                                 jnp.where(lane == 69, -clo, zero)))))
        qa_ref[h] = qa.astype(BF16)
        ka_ref[h] = ka.astype(BF16)
        vslab = fv[:, p * LANES:(p + 1) * LANES].astype(F32)
        if odd:
            va = jnp.where(lane >= 64, vslab, jnp.where(lane == 0, one, zero))
        else:
            va = jnp.where(lane < 64, vslab, jnp.where(lane == 64, one, zero))
        va_ref[h] = va.astype(BF16)

    dq[...] = (mm(_C_DQ, _C_DK) * Q_SCALE).astype(BF16)
    dk16 = mm(_C_DK, _C_DV).astype(BF16)
    dk[...] = dk16
    dv[...] = mm(_C_DV, _C_GA).astype(BF16)
    sa[...] = _sigmoid(mm(_C_GA, _C_GB)).astype(BF16)
    sb[...] = _sigmoid(mm(_C_GB, _C_FG)).astype(BF16)

    dk_norm2 = carry_ref[1:2, :]
    for h in range(DIFF_HEADS):
        slab = dk16[:, h * LANES:(h + 1) * LANES].astype(F32)
        sq = slab * slab
        for c in range(2):
            n2 = jnp.sum(jnp.where((lane < 64) == (c == 0), sq, 0.0), axis=1, keepdims=True)
            dk_norm2 = jnp.where(lane1 == 2 * h + c, jnp.maximum(dk_norm2, jnp.max(n2, axis=0, keepdims=True)),
                                 dk_norm2)
    carry_ref[1:2, :] = dk_norm2
    dkn[...] = jnp.broadcast_to(dk_norm2, dkn.shape)

    st_ref[0] = jnp.concatenate([q_norm2, k_norm2, cum[tb - 1:tb, :] * LOG2E, diag_min,
                                 jnp.zeros((4, LANES), F32)], axis=0)


def _inproj(x, g, b, w, bf, *, do_ln, tm):
    S = x.shape[0]
    row = lambda n: pl.BlockSpec((tm, n), lambda i: (i, 0))
    full = lambda a: pl.BlockSpec(a.shape, lambda i: (0,) * a.ndim)
    hspec = pl.BlockSpec((FOX_HEADS, tm, LANES), lambda i: (0, i, 0))
    hshape = jax.ShapeDtypeStruct((FOX_HEADS, S, LANES), BF16)
    out_shape = [hshape] * 3 + [jax.ShapeDtypeStruct((S // tm, 8, LANES), F32)] + [
        jax.ShapeDtypeStruct((S, 512), BF16)] * 3 + [jax.ShapeDtypeStruct((S, D_MODEL), BF16)] * 2 + [
        jax.ShapeDtypeStruct((8, LANES), F32)]
    out_specs = [hspec] * 3 + [pl.BlockSpec((1, 8, LANES), lambda i: (i, 0, 0))] + [row(512)] * 3 + [
        row(D_MODEL)] * 2 + [pl.BlockSpec((8, LANES), lambda i: (0, 0))]
    if do_ln:
        out_shape.append(jax.ShapeDtypeStruct((S, D_MODEL), F32))
        out_specs.append(row(D_MODEL))
    return pl.pallas_call(
        functools.partial(_inproj_kernel, do_ln=do_ln),
        grid=(S // tm,),
        in_specs=[row(D_MODEL), full(g), full(b), full(w), full(bf)],
        out_specs=out_specs,
        out_shape=out_shape,
        scratch_shapes=[pltpu.VMEM((8, LANES), F32)],
        compiler_params=_cparams(("arbitrary",)),
        name="inproj",
    )(x, g, b, w, bf)


SKIP_LOG2 = 130.0


def _plan_kernel(st_ref, start_ref, *, nt, nsub):
    qn = jnp.sqrt(st_ref[pl.ds(0, nt, stride=8), :])
    kn = jnp.sqrt(st_ref[pl.ds(1, nt, stride=8), :])
    cum_end = st_ref[pl.ds(2, nt, stride=8), :]
    diag_min = st_ref[pl.ds(3, nt, stride=8), :]
    tile = lax.broadcasted_iota(jnp.int32, (nt, LANES), 0)
    rows = []
    for i in range(nt // nsub):
        own = jnp.logical_and(tile >= i * nsub, tile < (i + 1) * nsub)
        q_max = jnp.max(jnp.where(own, qn, 0.0), axis=0, keepdims=True)
        m_low = jnp.min(jnp.where(own, diag_min, jnp.inf), axis=0, keepdims=True) - 0.5
        cum_q = cum_end[i * nsub - 1:i * nsub, :] if i > 0 else jnp.zeros((1, LANES), F32)
        upper = q_max * kn + (cum_q - cum_end) - m_low
        slack = SKIP_LOG2 + 1e-4 * jnp.abs(cum_end)
        needed = jnp.logical_and(upper >= -slack, tile < i * nsub)
        rows.append(jnp.min(jnp.where(needed, tile, i * nsub), axis=0, keepdims=True))
    rows += [jnp.zeros((1, LANES), jnp.int32)] * (8 - len(rows))
    start_ref[...] = jnp.concatenate(rows, axis=0)


def _plan(stats, *, nsub):
    nt = stats.shape[0]
    assert nt // nsub <= 8
    return pl.pallas_call(
        functools.partial(_plan_kernel, nt=nt, nsub=nsub),
        out_shape=jax.ShapeDtypeStruct((8, LANES), jnp.int32),
        compiler_params=pltpu.CompilerParams(vmem_limit_bytes=VMEM_LIMIT),
        name="fox_plan",
    )(stats.reshape(nt * 8, LANES))


def _nt_dot(a, b):
    return lax.dot_general(a, b, (((1,), (1,)), ((), ())), preferred_element_type=F32)


def _fox_kernel(start_ref, q_ref, k_ref, v_ref, o_ref, m_sc, acc_sc, *, tq, tk):
    i = pl.program_id(1)
    pair = pl.program_id(0)
    m_sc[...] = jnp.full_like(m_sc, NEG)
    acc_sc[...] = jnp.zeros_like(acc_sc)
    nsub = tq // tk

    def tile(j, r0, masked):
        off = pl.multiple_of(j * tk, tk)
        rows = tq - r0
        for hh in range(2):
            s = _nt_dot(q_ref[hh, r0:, :], k_ref[hh, pl.ds(off, tk), :])
            if masked:
                r = lax.broadcasted_iota(jnp.int32, (rows, tk), 0)
                c = lax.broadcasted_iota(jnp.int32, (rows, tk), 1)
                s = jnp.where(c <= r, s, NEG)
            m_prev = m_sc[hh, r0:, :]
            m_next = jnp.maximum(m_prev, jnp.max(s, axis=1, keepdims=True))
            alpha = jnp.exp2(m_prev - m_next)
            p = jnp.exp2(s - jnp.concatenate([m_next] * (tk // LANES), axis=1)).astype(BF16)
            pv = jnp.dot(p, v_ref[hh, pl.ds(off, tk), :], preferred_element_type=F32)
            acc_sc[hh, r0:, :] = alpha * acc_sc[hh, r0:, :] + pv
            m_sc[hh, r0:, :] = m_next

    def body(j, carry):
        tile(j, 0, False)
        return carry

    first = jnp.minimum(start_ref[i * FOX_HEADS + 2 * pair], start_ref[i * FOX_HEADS + 2 * pair + 1])
    lax.fori_loop(first, i * nsub, body, 0)
    for u in range(nsub):
        tile(i * nsub + u, u * tk, True)

    a0 = acc_sc[0]
    a1 = acc_sc[1]
    lane = lax.broadcasted_iota(jnp.int32, (tq, LANES), 1)
    o_ref[...] = jnp.where(lane < 64, a0 / a0[:, 64:65], a1 / a1[:, 0:1]).astype(o_ref.dtype)


def _fox_attention(start, qa, ka, va, *, tq, tk):
    S = qa.shape[1]
    grid_spec = pltpu.PrefetchScalarGridSpec(
        num_scalar_prefetch=1,
        grid=(FOX_HEADS // 2, S // tq),
        in_specs=[pl.BlockSpec((2, tq, LANES), lambda p, i, st: (p, i, 0)),
                  pl.BlockSpec((2, S, LANES), lambda p, i, st: (p, 0, 0)),
                  pl.BlockSpec((2, S, LANES), lambda p, i, st: (p, 0, 0))],
        out_specs=pl.BlockSpec((tq, LANES), lambda p, i, st: (i, p)),
        scratch_shapes=[pltpu.VMEM((2, tq, LANES), F32), pltpu.VMEM((2, tq, LANES), F32)],
    )
    return pl.pallas_call(
        functools.partial(_fox_kernel, tq=tq, tk=tk),
        grid_spec=grid_spec,
        out_shape=jax.ShapeDtypeStruct((S, FOX_WIDTH), BF16),
        compiler_params=_cparams(("arbitrary", "arbitrary")),
        name="fox_attn",
    )(start, qa, ka, va)


BIAS_KINDS = 2


def _bias_tile_kernel(table_ref, o_ref, *, t):
    h = pl.program_id(0)
    kind = pl.program_id(1)
    r = lax.broadcasted_iota(jnp.int32, (t, t), 0)
    c = lax.broadcasted_iota(jnp.int32, (t, t), 1)
    dist = r - c + kind * t
    n = jnp.maximum(dist, 0)
    max_exact = REL_BUCKETS // 2
    nf = jnp.maximum(n, 1).astype(F32)
    log_part = jnp.log(nf / max_exact) / math.log(REL_MAX_DISTANCE / max_exact) * (REL_BUCKETS - max_exact)
    large = jnp.minimum(max_exact + log_part.astype(jnp.int32), REL_BUCKETS - 1)
    bucket = jnp.where(n < max_exact, n, large)
    val = jnp.zeros((t, t), F32)
    for b in range(REL_BUCKETS):
        val = jnp.where(bucket == b, table_ref[h * REL_BUCKETS + b], val)
    far = table_ref[h * REL_BUCKETS + REL_BUCKETS - 1]
    o_ref[0, 0] = jnp.where(dist >= 0, (val - far) * LOG2E, NEG)


def _bias_tiles(table_flat, *, t):
    return pl.pallas_call(
        functools.partial(_bias_tile_kernel, t=t),
        grid=(DIFF_HEADS, BIAS_KINDS),
        in_specs=[pl.BlockSpec(memory_space=pltpu.SMEM)],
        out_specs=pl.BlockSpec((1, 1, t, t), lambda h, k: (h, k, 0, 0)),
        out_shape=jax.ShapeDtypeStruct((DIFF_HEADS, BIAS_KINDS, t, t), F32),
        compiler_params=_cparams(("arbitrary", "arbitrary")),
        name="bias_tiles",
    )(table_flat)


FIXED_REF_MAX_SHIFT = 64.0


def _diff_kernel(tab_ref, q_ref, k_ref, v_ref, bias_ref, kn_ref, lam_ref, g_ref, o_ref, m_sc, l_sc, acc_sc,
                 *, tq, tk, lam_init):
    h = pl.program_id(0)
    i = pl.program_id(1)
    l_sc[...] = jnp.zeros_like(l_sc)
    acc_sc[...] = jnp.zeros_like(acc_sc)
    nsub = tq // tk

    lane = lax.broadcasted_iota(jnp.int32, (tq, LANES), 1)
    q = q_ref[...]
    zero = jnp.zeros_like(q)
    qs = (jnp.where(lane < 64, q, zero), jnp.where(lane >= 64, q, zero))

    def run(fixed):
        def tile(j, r0, kinds):
            off = pl.multiple_of(j * tk, tk)
            k = k_ref[pl.ds(off, tk), :]
            v = v_ref[pl.ds(off, tk), :]
            for c in range(2):
                s = _nt_dot(qs[c][r0:], k)
                if kinds is not None:
                    s = jnp.concatenate([s[a * tk:(a + 1) * tk] + bias_ref[0, kd] if kd < 2
                                         else s[a * tk:(a + 1) * tk] for a, kd in enumerate(kinds)], axis=0)
                m_prev = m_sc[c, r0:, :]
                if fixed:
                    p = jnp.exp2(s - jnp.concatenate([m_prev] * (tk // LANES), axis=1))
                    l_sc[c, r0:, :] += sum(p[:, a * LANES:(a + 1) * LANES] for a in range(tk // LANES))
                    acc_sc[c, r0:, :] += jnp.dot(p.astype(BF16), v, preferred_element_type=F32)
                else:
                    m_next = jnp.maximum(m_prev, jnp.max(s, axis=1, keepdims=True))
                    p = jnp.exp2(s - jnp.concatenate([m_next] * (tk // LANES), axis=1))
                    alpha = jnp.exp2(m_prev - m_next)
                    l_sc[c, r0:, :] = alpha * l_sc[c, r0:, :] + sum(p[:, a * LANES:(a + 1) * LANES]
                                                                    for a in range(tk // LANES))
                    pv = jnp.dot(p.astype(BF16), v, preferred_element_type=F32)
                    acc_sc[c, r0:, :] = alpha * acc_sc[c, r0:, :] + pv
                    m_sc[c, r0:, :] = m_next

        def body(j, carry):
            tile(j, 0, None)
            return carry

        lax.fori_loop(0, i * nsub - 1, body, 0)

        @pl.when(i > 0)
        def _():
            tile(i * nsub - 1, 0, (1,) + (2,) * (nsub - 1))

        for u in range(nsub):
            tile(i * nsub + u, u * tk, ((0, 1) + (2,) * nsub)[:nsub - u])

    lane1 = lax.broadcasted_iota(jnp.int32, (1, LANES), 1)
    far = tab_ref[h * REL_BUCKETS + REL_BUCKETS - 1]
    b0 = (tab_ref[h * REL_BUCKETS] - far) * LOG2E
    bmax = b0
    for b in range(1, REL_BUCKETS):
        bmax = jnp.maximum(bmax, (tab_ref[h * REL_BUCKETS + b] - far) * LOG2E)
    bounds = []
    for c in range(2):
        qf = qs[c].astype(F32)
        qn = jnp.sqrt(jnp.sum(qf * qf, axis=1, keepdims=True))
        kmax = jnp.sqrt(jnp.sum(jnp.where(lane1 == 2 * h + c, kn_ref[0:1, :], 0.0), axis=1, keepdims=True))
        bounds.append(qn * kmax + 1.0)
    gap = 2.0 * jnp.max(jnp.maximum(bounds[0], bounds[1])) + (bmax - b0)
    fits = gap <= FIXED_REF_MAX_SHIFT

    @pl.when(fits)
    def _():
        for c in range(2):
            m_sc[c] = jnp.broadcast_to(bounds[c] + bmax, (tq, LANES))
        run(True)

    @pl.when(jnp.logical_not(fits))
    def _():
        m_sc[...] = jnp.full_like(m_sc, NEG)
        run(False)

    lv = lam_ref[...]
    lam = (jnp.exp(jnp.sum(lv[0:1] * lv[1:2], axis=1, keepdims=True))
           - jnp.exp(jnp.sum(lv[2:3] * lv[3:4], axis=1, keepdims=True)) + lam_init)
    l0 = jnp.sum(l_sc[0], axis=1, keepdims=True)
    l1 = jnp.sum(l_sc[1], axis=1, keepdims=True)
    o = acc_sc[0] / l0 - lam * (acc_sc[1] / l1)
    o = o * lax.rsqrt(jnp.mean(o * o, axis=1, keepdims=True) + SUBLN_EPS) * g_ref[...] * (1.0 - lam_init)
    o_ref[...] = o.astype(o_ref.dtype)


def _diff_attention(table_flat, dq, dk, dv, bias, dkn, lamvec, g, *, tq, tk, lam_init):
    S = dq.shape[0]
    grid_spec = pltpu.PrefetchScalarGridSpec(
        num_scalar_prefetch=1,
        grid=(DIFF_HEADS, S // tq),
        in_specs=[pl.BlockSpec((tq, LANES), lambda h, i, tab: (i, h)),
                  pl.BlockSpec((S, LANES), lambda h, i, tab: (0, h)),
                  pl.BlockSpec((S, LANES), lambda h, i, tab: (0, h)),
                  pl.BlockSpec((1, BIAS_KINDS, tk, tk), lambda h, i, tab: (h, 0, 0, 0)),
                  pl.BlockSpec((8, LANES), lambda h, i, tab: (0, 0)),
                  pl.BlockSpec((8, LANES), lambda h, i, tab: (0, 0)),
                  pl.BlockSpec((1, LANES), lambda h, i, tab: (0, 0))],
        out_specs=pl.BlockSpec((tq, LANES), lambda h, i, tab: (i, h)),
        scratch_shapes=[pltpu.VMEM((2, tq, LANES), F32), pltpu.VMEM((2, tq, LANES), F32),
                        pltpu.VMEM((2, tq, LANES), F32)],
    )
    return pl.pallas_call(
        functools.partial(_diff_kernel, tq=tq, tk=tk, lam_init=lam_init),
        grid_spec=grid_spec,
        out_shape=jax.ShapeDtypeStruct((S, DIFF_V_WIDTH), BF16),
        compiler_params=_cparams(("arbitrary", "arbitrary")),
        name="diff_attn",
    )(table_flat, dq, dk, dv, bias, dkn, lamvec, g)


def _merge_kernel(yf_ref, yd_ref, sa_ref, sb_ref, h_ref, wf_ref, wd_ref, wo_ref, g_ref, b_ref, o_ref, *, alpha):
    bf = jnp.dot(yf_ref[...], wf_ref[...], preferred_element_type=F32)
    bd = jnp.dot(yd_ref[...], wd_ref[...], preferred_element_type=F32)
    merged = sa_ref[...].astype(F32) * bf + sb_ref[...].astype(F32) * bd
    mix = jnp.dot(merged.astype(BF16), wo_ref[...], preferred_element_type=F32)
    o_ref[...] = _layer_norm(alpha * h_ref[...] + mix, g_ref[...], b_ref[...])


def _merge(yf, yd, sa, sb, h, wf, wd, wo, g, b, *, alpha, tm):
    S = h.shape[0]
    row = lambda n: pl.BlockSpec((tm, n), lambda i: (i, 0))
    full = lambda a: pl.BlockSpec(a.shape, lambda i: (0,) * a.ndim)
    return pl.pallas_call(
        functools.partial(_merge_kernel, alpha=alpha),
        grid=(S // tm,),
        in_specs=[row(512), row(512), row(D_MODEL), row(D_MODEL), row(D_MODEL),
                  full(wf), full(wd), full(wo), full(g), full(b)],
        out_specs=row(D_MODEL),
        out_shape=jax.ShapeDtypeStruct((S, D_MODEL), F32),
        compiler_params=_cparams(("arbitrary",)),
        name="merge_out",
    )(yf, yd, sa, sb, h, wf, wd, wo, g, b)


MOE_ROWS = 272
SCATTER_ROWS = 256
RANK_CHUNK = 256


def _router_kernel(h_ref, whi_ref, wlo_ref, cw_ref, rank_ref, rank_t_ref, cnt_ref):
    tm = h_ref.shape[0]
    h = h_ref[...]
    hhi = h.astype(BF16)
    hlo = (h - hhi.astype(F32)).astype(BF16)
    whi = whi_ref[...]
    dot = lambda a, b: jnp.dot(a, b, preferred_element_type=F32)
    logits = dot(hhi, whi) + dot(hhi, wlo_ref[...]) + dot(hlo, whi)
    lane = lax.broadcasted_iota(jnp.int32, logits.shape, 1).astype(F32)
    lg = jnp.where(lane < N_EXPERTS, logits, NEG)
    m1 = jnp.max(lg, axis=1, keepdims=True)
    i1 = jnp.min(jnp.where(lg == m1, lane, float(LANES)), axis=1, keepdims=True)
    lg2 = jnp.where(lane == i1, NEG, lg)
    m2 = jnp.max(lg2, axis=1, keepdims=True)
    i2 = jnp.min(jnp.where(lg2 == m2, lane, float(LANES)), axis=1, keepdims=True)
    e = jnp.exp(m2 - m1)
    g1 = 1.0 / (1.0 + e)
    chosen = jnp.logical_or(lane == i1, lane == i2)
    cw_ref[...] = jnp.where(lane == i1, g1, 0.0) + jnp.where(lane == i2, e * g1, 0.0)

    a = chosen.astype(BF16)
    r = lax.broadcasted_iota(jnp.int32, (RANK_CHUNK, RANK_CHUNK), 0)
    c = lax.broadcasted_iota(jnp.int32, (RANK_CHUNK, RANK_CHUNK), 1)
    strict_lower = (c < r).astype(BF16)
    ones_row = jnp.ones((8, RANK_CHUNK), BF16)
    carry = jnp.zeros((1, LANES), F32)
    for q in range(tm // RANK_CHUNK):
        rows = slice(q * RANK_CHUNK, (q + 1) * RANK_CHUNK)
        rank = jnp.where(chosen[rows], dot(strict_lower, a[rows]) + carry, -1.0)
        rank_ref[rows, :] = rank
        rank_t_ref[0, :, rows] = jnp.transpose(rank)[0:N_EXPERTS, :]
        carry = carry + dot(ones_row, a[rows])[0:1, :]
    cnt_ref[0] = jnp.broadcast_to(carry, (8, LANES)).astype(jnp.int32)


def _router(h, whi, wlo, *, tm):
    S = h.shape[0]
    nb = S // tm
    return pl.pallas_call(
        _router_kernel,
        grid=(nb,),
        in_specs=[pl.BlockSpec((tm, D_MODEL), lambda i: (i, 0)),
                  pl.BlockSpec(whi.shape, lambda i: (0, 0)), pl.BlockSpec(wlo.shape, lambda i: (0, 0))],
        out_specs=[pl.BlockSpec((tm, LANES), lambda i: (i, 0)),
                   pl.BlockSpec((tm, LANES), lambda i: (i, 0)),
                   pl.BlockSpec((1, N_EXPERTS, tm), lambda i: (i, 0, 0)),
                   pl.BlockSpec((1, 8, LANES), lambda i: (i, 0, 0))],
        out_shape=[jax.ShapeDtypeStruct((S, LANES), F32),
                   jax.ShapeDtypeStruct((S, LANES), F32),
                   jax.ShapeDtypeStruct((nb, N_EXPERTS, tm), F32),
                   jax.ShapeDtypeStruct((nb, 8, LANES), jnp.int32)],
        compiler_params=_cparams(("arbitrary",)),
        name="router",
    )(h, whi, wlo)


def _moe_kernel(cnt_ref, h_ref, cw_ref, rank_ref, rank_t_ref, wg_ref, wu_ref, wd_ref, g_ref, b_ref, o_ref,
                hb_sc, xs_sc, acc_sc, *, alpha):
    tm = h_ref.shape[0]
    i = pl.program_id(0)
    e = pl.program_id(1)
    c = pl.program_id(2)
    last_c = c == pl.num_programs(2) - 1
    n_rows = cnt_ref[i * N_EXPERTS + e]
    n_steps = (n_rows + MOE_ROWS - 1) // MOE_ROWS
    n_scatter = (n_rows + SCATTER_ROWS - 1) // SCATTER_ROWS

    @pl.when(jnp.logical_and(e == 0, c == 0))
    def _():
        hb_sc[...] = h_ref[...].astype(BF16)
        o_ref[...] = jnp.zeros_like(o_ref)

    @pl.when(c == 0)
    def _():
        rank_row = rank_t_ref[0, pl.ds(e, 1), :]

        def gather(t, carry):
            r0 = pl.multiple_of(t * MOE_ROWS, 16)
            slot = (lax.broadcasted_iota(jnp.int32, (MOE_ROWS, tm), 0) + r0).astype(F32)
            onehot = (rank_row == slot).astype(BF16)
            xs_sc[pl.ds(r0, MOE_ROWS), :] = jnp.dot(onehot, hb_sc[...], preferred_element_type=F32).astype(BF16)
            acc_sc[pl.ds(r0, MOE_ROWS), :] = jnp.zeros((MOE_ROWS, D_MODEL), F32)
            return carry

        lax.fori_loop(0, n_steps, gather, 0)

        @pl.when(n_steps * MOE_ROWS < n_scatter * SCATTER_ROWS)
        def _():
            acc_sc[pl.ds(pl.multiple_of(n_steps * MOE_ROWS, 16), SCATTER_ROWS), :] = jnp.zeros(
                (SCATTER_ROWS, D_MODEL), F32)

    def swiglu(t, carry):
        r0 = pl.multiple_of(t * MOE_ROWS, 16)
        x = xs_sc[pl.ds(r0, MOE_ROWS), :]
        gate = jnp.dot(x, wg_ref[0], preferred_element_type=F32)
        up = jnp.dot(x, wu_ref[0], preferred_element_type=F32)
        a = (gate * _sigmoid(gate) * up).astype(BF16)
        acc_sc[pl.ds(r0, MOE_ROWS), :] += jnp.dot(a, wd_ref[0], preferred_element_type=F32)
        return carry

    lax.fori_loop(0, n_steps, swiglu, 0)

    @pl.when(last_c)
    def _():
        lane = lax.broadcasted_iota(jnp.int32, (tm, LANES), 1)
        pick = lambda ref: jnp.broadcast_to(
            jnp.sum(jnp.where(lane == e, ref[...], 0.0), axis=1, keepdims=True), (tm, LANES))
        rank_col = pick(rank_ref)
        gate_col = jnp.concatenate([pick(cw_ref)] * (D_MODEL // LANES), axis=1)

        def scatter(t, carry):
            r0 = pl.multiple_of(t * SCATTER_ROWS, SCATTER_ROWS)
            slot = (lane + r0).astype(F32)
            onehot = jnp.concatenate([(rank_col == slot + float(a * LANES)) for a in range(SCATTER_ROWS // LANES)],
                                     axis=1).astype(BF16)
            y = acc_sc[pl.ds(r0, SCATTER_ROWS), :].astype(BF16)
            o_ref[...] += gate_col * jnp.dot(onehot, y, preferred_element_type=F32)
            return carry

        lax.fori_loop(0, n_scatter, scatter, 0)

    @pl.when(jnp.logical_and(e == pl.num_programs(1) - 1, last_c))
    def _():
        o_ref[...] = _layer_norm(alpha * h_ref[...] + o_ref[...], g_ref[...], b_ref[...])


def _moe(h, cw, rank, rank_t, counts, w_gu, w_dn, g, b, *, alpha, tm, fc):
    S = h.shape[0]
    E, _, F2 = w_gu.shape
    nc = F2 // 2 // fc
    once = pl.Buffered(1)
    rows_cap = pl.cdiv(tm, MOE_ROWS) * MOE_ROWS
    grid_spec = pltpu.PrefetchScalarGridSpec(
        num_scalar_prefetch=1,
        grid=(S // tm, E, nc),
        in_specs=[pl.BlockSpec((tm, D_MODEL), lambda i, e, c, cnt: (i, 0), pipeline_mode=once),
                  pl.BlockSpec((tm, LANES), lambda i, e, c, cnt: (i, 0), pipeline_mode=once),
                  pl.BlockSpec((tm, LANES), lambda i, e, c, cnt: (i, 0), pipeline_mode=once),
                  pl.BlockSpec((1, N_EXPERTS, tm), lambda i, e, c, cnt: (i, 0, 0), pipeline_mode=once),
                  pl.BlockSpec((1, D_MODEL, fc), lambda i, e, c, cnt: (e, 0, c)),
                  pl.BlockSpec((1, D_MODEL, fc), lambda i, e, c, cnt: (e, 0, nc + c)),
                  pl.BlockSpec((1, fc, D_MODEL), lambda i, e, c, cnt: (e, c, 0)),
                  pl.BlockSpec(g.shape, lambda i, e, c, cnt: (0, 0)),
                  pl.BlockSpec(b.shape, lambda i, e, c, cnt: (0, 0))],
        out_specs=pl.BlockSpec((tm, D_MODEL), lambda i, e, c, cnt: (i, 0), pipeline_mode=once),
        scratch_shapes=[pltpu.VMEM((tm, D_MODEL), BF16), pltpu.VMEM((rows_cap, D_MODEL), BF16),
                        pltpu.VMEM((rows_cap, D_MODEL), F32)],
    )
    return pl.pallas_call(
        functools.partial(_moe_kernel, alpha=alpha),
        grid_spec=grid_spec,
        out_shape=jax.ShapeDtypeStruct((S, D_MODEL), F32),
        compiler_params=_cparams(("arbitrary", "arbitrary", "arbitrary")),
        name="moe_ffn",
    )(counts, h, cw, rank, rank_t, w_gu, w_gu, w_dn, g, b)


def _ffn_kernel(h_ref, wgu_ref, wd_ref, g_ref, b_ref, o_ref, *, alpha, fc):
    h = h_ref[...]
    hb = h.astype(BF16)
    d_ff = wd_ref.shape[0]
    acc = None
    for c in range(d_ff // fc):
        gate = jnp.dot(hb, wgu_ref[:, c * fc:(c + 1) * fc], preferred_element_type=F32)
        up = jnp.dot(hb, wgu_ref[:, d_ff + c * fc:d_ff + (c + 1) * fc], preferred_element_type=F32)
        a = (gate * _sigmoid(gate) * up).astype(BF16)
        y = jnp.dot(a, wd_ref[c * fc:(c + 1) * fc, :], preferred_element_type=F32)
        acc = y if acc is None else acc + y
    o_ref[...] = _layer_norm(alpha * h + acc, g_ref[...], b_ref[...])


def _ffn(h, w_gu, w_dn, g, b, *, alpha, tm, fc):
    S = h.shape[0]
    once = pl.Buffered(1)
    return pl.pallas_call(
        functools.partial(_ffn_kernel, alpha=alpha, fc=fc),
        grid=(S // tm,),
        in_specs=[pl.BlockSpec((tm, D_MODEL), lambda i: (i, 0)),
                  pl.BlockSpec(w_gu.shape, lambda i: (0, 0), pipeline_mode=once),
                  pl.BlockSpec(w_dn.shape, lambda i: (0, 0), pipeline_mode=once),
                  pl.BlockSpec(g.shape, lambda i: (0, 0)), pl.BlockSpec(b.shape, lambda i: (0, 0))],
        out_specs=pl.BlockSpec((tm, D_MODEL), lambda i: (i, 0)),
        out_shape=jax.ShapeDtypeStruct((S, D_MODEL), F32),
        compiler_params=_cparams(("arbitrary",)),
        name="dense_ffn",
    )(h, w_gu, w_dn, g, b)


def _pad_lanes(a, n=LANES):
    return jnp.pad(a, [(0, 0)] * (a.ndim - 1) + [(0, n - a.shape[-1])])


def kernel(x, ln_in_g, ln_in_b, w_in, b_fgate, lam_q1, lam_k1, lam_q2, lam_k2, subln_g, w_branch_fox, w_branch_diff, w_out, ln_mix_g, ln_mix_b, rel_bias, w_ffn_gate_up, w_ffn_down, w_router, w_expert_gate_up, w_expert_down, ln_ffn_g, ln_ffn_b):
    B, S, _ = x.shape
    assert B == 1
    depth = w_in.shape[0]
    alpha = (2.0 * depth) ** 0.25
    tq = min(ATTN_TQ, S)
    tk = min(ATTN_TK, tq)
    tm = min(512, S)
    tf = min(1024, S)
    te = min(MOE_TM, S)
    assert tk >= REL_MAX_DISTANCE and tq % tk == 0 and S % tq == 0 and S % tf == 0 and S % tm == 0 and S % te == 0
    assert te % SCATTER_ROWS == 0 and te % RANK_CHUNK == 0 and MOE_ROWS % 16 == 0
    row = lambda v: v.reshape(1, -1).astype(F32)

    table_flat = rel_bias.astype(F32).T.reshape(-1)
    bias = _bias_tiles(table_flat, t=tk)

    h = x[0]
    for l in range(depth):
        lam_init = 0.8 - 0.6 * math.exp(-0.3 * l)
        fq, fk, fv, fg, dq, dk, dv, ga, gb = jnp.split(w_in[l], list(_cumsum(IN_SPLIT_SIZES))[:-1], axis=-1)
        w_cat = jnp.concatenate([fq, fk, fv, dq, dk, dv, ga, gb, _pad_lanes(fg)], axis=-1).astype(BF16)
        outs = _inproj(h, row(ln_in_g), row(ln_in_b), w_cat, _pad_lanes(row(b_fgate[l])), do_ln=(l == 0), tm=tk)
        qa, ka, va, stats, dq, dk, dv, sa, sb, dkn = outs[:10]
        if l == 0:
            h = outs[10]
        start = _plan(stats, nsub=tq // tk)[:S // tq, :FOX_HEADS].reshape(-1)
        y_fox = _fox_attention(start, qa, ka, va, tq=tq, tk=tk)
        lamvec = jnp.zeros((8, LANES), F32).at[0:4, 0:HEAD_DIM].set(
            jnp.stack([lam_q1[l], lam_k1[l], lam_q2[l], lam_k2[l]]).astype(F32))
        y_diff = _diff_attention(table_flat, dq, dk, dv, bias, dkn, lamvec, row(subln_g[l]), tq=tq, tk=tk,
                                 lam_init=lam_init)
        h = _merge(y_fox, y_diff, sa, sb, h, w_branch_fox[l].astype(BF16), w_branch_diff[l].astype(BF16),
                   w_out[l].astype(BF16), row(ln_mix_g[l]), row(ln_mix_b[l]), alpha=alpha, tm=tm)
        g, b = row(ln_ffn_g[l]), row(ln_ffn_b[l])
        if l % 2 == 0:
            h = _ffn(h, w_ffn_gate_up[l // 2].astype(BF16), w_ffn_down[l // 2].astype(BF16), g, b,
                     alpha=alpha, tm=tf, fc=1408)
        else:
            wr = _pad_lanes(w_router[l // 2].astype(F32))
            whi = wr.astype(BF16)
            wlo = (wr - whi.astype(F32)).astype(BF16)
            cw, rank, rank_t, counts = _router(h, whi, wlo, tm=te)
            h = _moe(h, cw, rank, rank_t, counts[:, 0, :N_EXPERTS].reshape(-1),
                     w_expert_gate_up[l // 2].astype(BF16), w_expert_down[l // 2].astype(BF16),
                     g, b, alpha=alpha, tm=te, fc=896)
    return h[None]


def _cumsum(sizes):
    tot = 0
    for s in sizes:
        tot += s
        yield tot
```

```python
import functools
import math

import jax
import jax.numpy as jnp
from jax import lax
from jax.experimental import pallas as pl
from jax.experimental.pallas import tpu as pltpu

F32 = jnp.float32
BF16 = jnp.bfloat16

D_MODEL = 1024
HEAD_DIM = 64
FOX_HEADS = 8
FOX_WIDTH = FOX_HEADS * HEAD_DIM
DIFF_HEADS = 4
DIFF_QK_WIDTH = DIFF_HEADS * 2 * HEAD_DIM
DIFF_V_DIM = 2 * HEAD_DIM
DIFF_V_WIDTH = DIFF_HEADS * DIFF_V_DIM
IN_SPLIT_SIZES = (FOX_WIDTH, FOX_WIDTH, FOX_WIDTH, FOX_HEADS, DIFF_QK_WIDTH, DIFF_QK_WIDTH, DIFF_V_WIDTH,
                  D_MODEL, D_MODEL)
REL_BUCKETS = 32
REL_MAX_DISTANCE = 128
N_EXPERTS = 8
LN_EPS = 1e-5
SUBLN_EPS = 1e-5

LANES = 128
LOG2E = 1.4426950408889634
Q_SCALE = HEAD_DIM ** -0.5 * LOG2E
NEG = -1e30
VMEM_LIMIT = 56 * 1024 * 1024
ATTN_TQ = 2048
ATTN_TK = 512
MOE_TM = 2048


def _cparams(sem):
    return pltpu.CompilerParams(dimension_semantics=sem, vmem_limit_bytes=VMEM_LIMIT)


def _layer_norm(x, g, b):
    mu = jnp.mean(x, axis=-1, keepdims=True)
    xc = x - mu
    var = jnp.mean(xc * xc, axis=-1, keepdims=True)
    return xc * lax.rsqrt(var + LN_EPS) * g + b


def _sigmoid(x):
    return 1.0 / (1.0 + jnp.exp(-x))


def _split3(x):
    hi = x.astype(BF16)
    r1 = x - hi.astype(F32)
    mid = r1.astype(BF16)
    lo = (r1 - mid.astype(F32)).astype(BF16)
    return hi, mid, lo


_C_FQ, _C_FK, _C_FV, _C_DQ, _C_DK, _C_DV, _C_GA, _C_GB, _C_FG, _C_END = (
    0, 512, 1024, 1536, 2048, 2560, 3072, 4096, 5120, 5248)


def _inproj_kernel(x_ref, g_ref, b_ref, w_ref, bf_ref, qa_ref, ka_ref, va_ref, st_ref, dq, dk, dv, sa, sb, dkn,
                   *rest, do_ln):
    carry_ref = rest[-1]
    tb = x_ref.shape[0]

    @pl.when(pl.program_id(0) == 0)
    def _():
        carry_ref[...] = jnp.zeros_like(carry_ref)

    x = x_ref[...]
    if do_ln:
        x = _layer_norm(x, g_ref[...], b_ref[...])
        rest[0][...] = x
    xb = x.astype(BF16)

    def mm(a, b):
        return jnp.dot(xb, w_ref[:, a:b], preferred_element_type=F32)

    fq = (mm(_C_FQ, _C_FK) * Q_SCALE).astype(BF16)
    fk = mm(_C_FK, _C_FV).astype(BF16)
    fv = mm(_C_FV, _C_DQ).astype(BF16)

    z = mm(_C_FG, _C_END) + bf_ref[...]
    logf = jnp.minimum(z, 0.0) - jnp.log1p(jnp.exp(-jnp.abs(z)))
    r = lax.broadcasted_iota(jnp.int32, (tb, tb), 0)
    c = lax.broadcasted_iota(jnp.int32, (tb, tb), 1)
    tri = (c <= r).astype(BF16)
    hi, mid, lo = _split3(logf)
    dot = lambda a: jnp.dot(tri, a, preferred_element_type=F32)
    cum = (dot(hi) + dot(mid)) + dot(lo) + carry_ref[0:1, :]
    carry_ref[0:1, :] = cum[tb - 1:tb, :]

    c_hi, c_mid, c_lo = (t.astype(F32) for t in _split3(cum * LOG2E))
    lane = lax.broadcasted_iota(jnp.int32, (tb, LANES), 1)
    one = jnp.ones((tb, LANES), F32)
    zero = jnp.zeros((tb, LANES), F32)
    lane1 = lax.broadcasted_iota(jnp.int32, (1, LANES), 1)
    q_norm2 = jnp.zeros((1, LANES), F32)
    k_norm2 = jnp.zeros((1, LANES), F32)
    diag_min = jnp.zeros((1, LANES), F32)

    def max_norm2(slab):
        n2 = jnp.sum(jnp.where(lane < 64, slab * slab, 0.0), axis=1, keepdims=True)
        return jnp.max(n2, axis=0, keepdims=True)

    for h in range(FOX_HEADS):
        p, odd = divmod(h, 2)
        col = lambda t: jnp.broadcast_to(t[:, h:h + 1], (tb, LANES))
        chi, cmid, clo = col(c_hi), col(c_mid), col(c_lo)

        def head_slab(a):
            slab = a[:, p * LANES:(p + 1) * LANES].astype(F32)
            return pltpu.roll(slab, HEAD_DIM, 1) if odd else slab

        qslab = head_slab(fq)
        kslab = head_slab(fk)
        q_norm2 = jnp.where(lane1 == h, max_norm2(qslab), q_norm2)
        k_norm2 = jnp.where(lane1 == h, max_norm2(kslab), k_norm2)
        own = jnp.sum(jnp.where(lane < 64, qslab * kslab, 0.0), axis=1, keepdims=True)
        diag_min = jnp.where(lane1 == h, jnp.min(own, axis=0, keepdims=True), diag_min)
        qa = jnp.where(lane < 64, qslab,
                       jnp.where(lane == 64, chi, jnp.where(lane == 65, cmid, jnp.where(lane == 66, clo,
                                 jnp.where(lane < 70, one, zero)))))
        ka = jnp.where(lane < 64, kslab,
                       jnp.where(lane < 67, one, jnp.where(lane == 67, -chi, jnp.where(lane == 68, -cmid,
                                 jnp.where(lane == 69, -clo, zero)))))
        qa_ref[h] = qa.astype(BF16)
        ka_ref[h] = ka.astype(BF16)
        vslab = fv[:, p * LANES:(p + 1) * LANES].astype(F32)
        if odd:
            va = jnp.where(lane >= 64, vslab, jnp.where(lane == 0, one, zero))
        else:
            va = jnp.where(lane < 64, vslab, jnp.where(lane == 64, one, zero))
        va_ref[h] = va.astype(BF16)

    dq[...] = (mm(_C_DQ, _C_DK) * Q_SCALE).astype(BF16)
    dk16 = mm(_C_DK, _C_DV).astype(BF16)
    dk[...] = dk16
    dv[...] = mm(_C_DV, _C_GA).astype(BF16)
    sa[...] = _sigmoid(mm(_C_GA, _C_GB)).astype(BF16)
    sb[...] = _sigmoid(mm(_C_GB, _C_FG)).astype(BF16)

    dk_norm2 = carry_ref[1:2, :]
    for h in range(DIFF_HEADS):
        slab = dk16[:, h * LANES:(h + 1) * LANES].astype(F32)
        sq = slab * slab
        for c in range(2):
            n2 = jnp.sum(jnp.where((lane < 64) == (c == 0), sq, 0.0), axis=1, keepdims=True)
            dk_norm2 = jnp.where(lane1 == 2 * h + c, jnp.maximum(dk_norm2, jnp.max(n2, axis=0, keepdims=True)),
                                 dk_norm2)
    carry_ref[1:2, :] = dk_norm2
    dkn[...] = jnp.broadcast_to(dk_norm2, dkn.shape)

    st_ref[0] = jnp.concatenate([q_norm2, k_norm2, cum[tb - 1:tb, :] * LOG2E, diag_min,
                                 jnp.zeros((4, LANES), F32)], axis=0)


def _inproj(x, g, b, w, bf, *, do_ln, tm):
    S = x.shape[0]
    row = lambda n: pl.BlockSpec((tm, n), lambda i: (i, 0))
    full = lambda a: pl.BlockSpec(a.shape, lambda i: (0,) * a.ndim)
    hspec = pl.BlockSpec((FOX_HEADS, tm, LANES), lambda i: (0, i, 0))
    hshape = jax.ShapeDtypeStruct((FOX_HEADS, S, LANES), BF16)
    out_shape = [hshape] * 3 + [jax.ShapeDtypeStruct((S // tm, 8, LANES), F32)] + [
        jax.ShapeDtypeStruct((S, 512), BF16)] * 3 + [jax.ShapeDtypeStruct((S, D_MODEL), BF16)] * 2 + [
        jax.ShapeDtypeStruct((8, LANES), F32)]
    out_specs = [hspec] * 3 + [pl.BlockSpec((1, 8, LANES), lambda i: (i, 0, 0))] + [row(512)] * 3 + [
        row(D_MODEL)] * 2 + [pl.BlockSpec((8, LANES), lambda i: (0, 0))]
    if do_ln:
        out_shape.append(jax.ShapeDtypeStruct((S, D_MODEL), F32))
        out_specs.append(row(D_MODEL))
    return pl.pallas_call(
        functools.partial(_inproj_kernel, do_ln=do_ln),
        grid=(S // tm,),
        in_specs=[row(D_MODEL), full(g), full(b), full(w), full(bf)],
        out_specs=out_specs,
        out_shape=out_shape,
        scratch_shapes=[pltpu.VMEM((8, LANES), F32)],
        compiler_params=_cparams(("arbitrary",)),
        name="inproj",
    )(x, g, b, w, bf)


SKIP_LOG2 = 130.0


def _plan_kernel(st_ref, start_ref, *, nt, nsub):
    qn = jnp.sqrt(st_ref[pl.ds(0, nt, stride=8), :])
    kn = jnp.sqrt(st_ref[pl.ds(1, nt, stride=8), :])
    cum_end = st_ref[pl.ds(2, nt, stride=8), :]
    diag_min = st_ref[pl.ds(3, nt, stride=8), :]
    tile = lax.broadcasted_iota(jnp.int32, (nt, LANES), 0)
    slack = SKIP_LOG2 + 1e-4 * jnp.abs(cum_end)
    rows = []
    for t in range(nt):
        block_start = (t // nsub) * nsub
        m_low = diag_min[t:t + 1, :] - 0.5
        cum_q = cum_end[t - 1:t, :] if t > 0 else jnp.zeros((1, LANES), F32)
        upper = qn[t:t + 1, :] * kn + (cum_q - cum_end) - m_low
        needed = jnp.logical_and(upper >= -slack, tile < block_start)
        rows.append(jnp.min(jnp.where(needed, tile, block_start), axis=0, keepdims=True))
    start_ref[...] = jnp.concatenate(rows, axis=0)


def _plan(stats, *, nsub):
    nt = stats.shape[0]
    assert nt % 8 == 0 or nt < 8
    return pl.pallas_call(
        functools.partial(_plan_kernel, nt=nt, nsub=nsub),
        out_shape=jax.ShapeDtypeStruct((nt, LANES), jnp.int32),
        compiler_params=pltpu.CompilerParams(vmem_limit_bytes=VMEM_LIMIT),
        name="fox_plan",
    )(stats.reshape(nt * 8, LANES))


def _nt_dot(a, b):
    return lax.dot_general(a, b, (((1,), (1,)), ((), ())), preferred_element_type=F32)


def _fox_kernel(start_ref, q_ref, k_ref, v_ref, o_ref, m_sc, acc_sc, *, tq, tk):
    i = pl.program_id(1)
    pair = pl.program_id(0)
    m_sc[...] = jnp.full_like(m_sc, NEG)
    acc_sc[...] = jnp.zeros_like(acc_sc)
    nsub = tq // tk

    def tile(j, r0, r1, masked):
        off = pl.multiple_of(j * tk, tk)
        rows = r1 - r0
        for hh in range(2):
            s = _nt_dot(q_ref[hh, r0:r1, :], k_ref[hh, pl.ds(off, tk), :])
            if masked:
                r = lax.broadcasted_iota(jnp.int32, (rows, tk), 0)
                c = lax.broadcasted_iota(jnp.int32, (rows, tk), 1)
                s = jnp.where(c <= r, s, NEG)
            m_prev = m_sc[hh, r0:r1, :]
            m_next = jnp.maximum(m_prev, jnp.max(s, axis=1, keepdims=True))
            alpha = jnp.exp2(m_prev - m_next)
            p = jnp.exp2(s - jnp.concatenate([m_next] * (tk // LANES), axis=1)).astype(BF16)
            pv = jnp.dot(p, v_ref[hh, pl.ds(off, tk), :], preferred_element_type=F32)
            acc_sc[hh, r0:r1, :] = alpha * acc_sc[hh, r0:r1, :] + pv
            m_sc[hh, r0:r1, :] = m_next

    first = [jnp.minimum(start_ref[(i * nsub + u) * FOX_HEADS + 2 * pair],
                         start_ref[(i * nsub + u) * FOX_HEADS + 2 * pair + 1]) for u in range(nsub)]
    for u in range(nsub - 2, -1, -1):
        first[u] = jnp.minimum(first[u], first[u + 1])
    first.append(i * nsub)
    for u in range(nsub):
        def body(j, carry, r1=(u + 1) * tk):
            tile(j, 0, r1, False)
            return carry

        lax.fori_loop(first[u], first[u + 1], body, 0)
    for u in range(nsub):
        tile(i * nsub + u, u * tk, tq, True)

    a0 = acc_sc[0]
    a1 = acc_sc[1]
    lane = lax.broadcasted_iota(jnp.int32, (tq, LANES), 1)
    o_ref[...] = jnp.where(lane < 64, a0 / a0[:, 64:65], a1 / a1[:, 0:1]).astype(o_ref.dtype)


def _fox_attention(start, qa, ka, va, *, tq, tk):
    S = qa.shape[1]
    grid_spec = pltpu.PrefetchScalarGridSpec(
        num_scalar_prefetch=1,
        grid=(FOX_HEADS // 2, S // tq),
        in_specs=[pl.BlockSpec((2, tq, LANES), lambda p, i, st: (p, i, 0)),
                  pl.BlockSpec((2, S, LANES), lambda p, i, st: (p, 0, 0)),
                  pl.BlockSpec((2, S, LANES), lambda p, i, st: (p, 0, 0))],
        out_specs=pl.BlockSpec((tq, LANES), lambda p, i, st: (i, p)),
        scratch_shapes=[pltpu.VMEM((2, tq, LANES), F32), pltpu.VMEM((2, tq, LANES), F32)],
    )
    return pl.pallas_call(
        functools.partial(_fox_kernel, tq=tq, tk=tk),
        grid_spec=grid_spec,
        out_shape=jax.ShapeDtypeStruct((S, FOX_WIDTH), BF16),
        compiler_params=_cparams(("arbitrary", "arbitrary")),
        name="fox_attn",
    )(start, qa, ka, va)


BIAS_KINDS = 2


def _bias_tile_kernel(table_ref, o_ref, *, t):
    h = pl.program_id(0)
    kind = pl.program_id(1)
    r = lax.broadcasted_iota(jnp.int32, (t, t), 0)
    c = lax.broadcasted_iota(jnp.int32, (t, t), 1)
    dist = r - c + kind * t
    n = jnp.maximum(dist, 0)
    max_exact = REL_BUCKETS // 2
    nf = jnp.maximum(n, 1).astype(F32)
    log_part = jnp.log(nf / max_exact) / math.log(REL_MAX_DISTANCE / max_exact) * (REL_BUCKETS - max_exact)
    large = jnp.minimum(max_exact + log_part.astype(jnp.int32), REL_BUCKETS - 1)
    bucket = jnp.where(n < max_exact, n, large)
    val = jnp.zeros((t, t), F32)
    for b in range(REL_BUCKETS):
        val = jnp.where(bucket == b, table_ref[h * REL_BUCKETS + b], val)
    far = table_ref[h * REL_BUCKETS + REL_BUCKETS - 1]
    o_ref[0, 0] = jnp.where(dist >= 0, (val - far) * LOG2E, NEG)


def _bias_tiles(table_flat, *, t):
    return pl.pallas_call(
        functools.partial(_bias_tile_kernel, t=t),
        grid=(DIFF_HEADS, BIAS_KINDS),
        in_specs=[pl.BlockSpec(memory_space=pltpu.SMEM)],
        out_specs=pl.BlockSpec((1, 1, t, t), lambda h, k: (h, k, 0, 0)),
        out_shape=jax.ShapeDtypeStruct((DIFF_HEADS, BIAS_KINDS, t, t), F32),
        compiler_params=_cparams(("arbitrary", "arbitrary")),
        name="bias_tiles",
    )(table_flat)


FIXED_REF_MAX_SHIFT = 64.0


def _diff_kernel(tab_ref, q_ref, k_ref, v_ref, bias_ref, kn_ref, lam_ref, g_ref, o_ref, m_sc, l_sc, acc_sc,
                 *, tq, tk, lam_init):
    h = pl.program_id(0)
    i = pl.program_id(1)
    l_sc[...] = jnp.zeros_like(l_sc)
    acc_sc[...] = jnp.zeros_like(acc_sc)
    nsub = tq // tk

    lane = lax.broadcasted_iota(jnp.int32, (tq, LANES), 1)
    q = q_ref[...]
    zero = jnp.zeros_like(q)
    qs = (jnp.where(lane < 64, q, zero), jnp.where(lane >= 64, q, zero))

    def run(fixed):
        def tile(j, r0, kinds):
            off = pl.multiple_of(j * tk, tk)
            k = k_ref[pl.ds(off, tk), :]
            v = v_ref[pl.ds(off, tk), :]
            for c in range(2):
                s = _nt_dot(qs[c][r0:], k)
                if kinds is not None:
                    s = jnp.concatenate([s[a * tk:(a + 1) * tk] + bias_ref[0, kd] if kd < 2
                                         else s[a * tk:(a + 1) * tk] for a, kd in enumerate(kinds)], axis=0)
                m_prev = m_sc[c, r0:, :]
                if fixed:
                    p = jnp.exp2(s - jnp.concatenate([m_prev] * (tk // LANES), axis=1))
                    l_sc[c, r0:, :] += sum(p[:, a * LANES:(a + 1) * LANES] for a in range(tk // LANES))
                    acc_sc[c, r0:, :] += jnp.dot(p.astype(BF16), v, preferred_element_type=F32)
                else:
                    m_next = jnp.maximum(m_prev, jnp.max(s, axis=1, keepdims=True))
                    p = jnp.exp2(s - jnp.concatenate([m_next] * (tk // LANES), axis=1))
                    alpha = jnp.exp2(m_prev - m_next)
                    l_sc[c, r0:, :] = alpha * l_sc[c, r0:, :] + sum(p[:, a * LANES:(a + 1) * LANES]
                                                                    for a in range(tk // LANES))
                    pv = jnp.dot(p.astype(BF16), v, preferred_element_type=F32)
                    acc_sc[c, r0:, :] = alpha * acc_sc[c, r0:, :] + pv
                    m_sc[c, r0:, :] = m_next

        def body(j, carry):
            tile(j, 0, None)
            return carry

        lax.fori_loop(0, i * nsub - 1, body, 0)

        @pl.when(i > 0)
        def _():
            tile(i * nsub - 1, 0, (1,) + (2,) * (nsub - 1))

        for u in range(nsub):
            tile(i * nsub + u, u * tk, ((0, 1) + (2,) * nsub)[:nsub - u])

    lane1 = lax.broadcasted_iota(jnp.int32, (1, LANES), 1)
    far = tab_ref[h * REL_BUCKETS + REL_BUCKETS - 1]
    b0 = (tab_ref[h * REL_BUCKETS] - far) * LOG2E
    bmax = b0
    for b in range(1, REL_BUCKETS):
        bmax = jnp.maximum(bmax, (tab_ref[h * REL_BUCKETS + b] - far) * LOG2E)
    bounds = []
    for c in range(2):
        qf = qs[c].astype(F32)
        qn = jnp.sqrt(jnp.sum(qf * qf, axis=1, keepdims=True))
        kmax = jnp.sqrt(jnp.sum(jnp.where(lane1 == 2 * h + c, kn_ref[0:1, :], 0.0), axis=1, keepdims=True))
        bounds.append(qn * kmax + 1.0)
    gap = 2.0 * jnp.max(jnp.maximum(bounds[0], bounds[1])) + (bmax - b0)
    fits = gap <= FIXED_REF_MAX_SHIFT

    @pl.when(fits)
    def _():
        for c in range(2):
            m_sc[c] = jnp.broadcast_to(bounds[c] + bmax, (tq, LANES))
        run(True)

    @pl.when(jnp.logical_not(fits))
    def _():
        m_sc[...] = jnp.full_like(m_sc, NEG)
        run(False)

    lv = lam_ref[...]
    lam = (jnp.exp(jnp.sum(lv[0:1] * lv[1:2], axis=1, keepdims=True))
           - jnp.exp(jnp.sum(lv[2:3] * lv[3:4], axis=1, keepdims=True)) + lam_init)
    l0 = jnp.sum(l_sc[0], axis=1, keepdims=True)
    l1 = jnp.sum(l_sc[1], axis=1, keepdims=True)
    o = acc_sc[0] / l0 - lam * (acc_sc[1] / l1)
    o = o * lax.rsqrt(jnp.mean(o * o, axis=1, keepdims=True) + SUBLN_EPS) * g_ref[...] * (1.0 - lam_init)
    o_ref[...] = o.astype(o_ref.dtype)


def _diff_attention(table_flat, dq, dk, dv, bias, dkn, lamvec, g, *, tq, tk, lam_init):
    S = dq.shape[0]
    grid_spec = pltpu.PrefetchScalarGridSpec(
        num_scalar_prefetch=1,
        grid=(DIFF_HEADS, S // tq),
        in_specs=[pl.BlockSpec((tq, LANES), lambda h, i, tab: (i, h)),
                  pl.BlockSpec((S, LANES), lambda h, i, tab: (0, h)),
                  pl.BlockSpec((S, LANES), lambda h, i, tab: (0, h)),
                  pl.BlockSpec((1, BIAS_KINDS, tk, tk), lambda h, i, tab: (h, 0, 0, 0)),
                  pl.BlockSpec((8, LANES), lambda h, i, tab: (0, 0)),
                  pl.BlockSpec((8, LANES), lambda h, i, tab: (0, 0)),
                  pl.BlockSpec((1, LANES), lambda h, i, tab: (0, 0))],
        out_specs=pl.BlockSpec((tq, LANES), lambda h, i, tab: (i, h)),
        scratch_shapes=[pltpu.VMEM((2, tq, LANES), F32), pltpu.VMEM((2, tq, LANES), F32),
                        pltpu.VMEM((2, tq, LANES), F32)],
    )
    return pl.pallas_call(
        functools.partial(_diff_kernel, tq=tq, tk=tk, lam_init=lam_init),
        grid_spec=grid_spec,
        out_shape=jax.ShapeDtypeStruct((S, DIFF_V_WIDTH), BF16),
        compiler_params=_cparams(("arbitrary", "arbitrary")),
        name="diff_attn",
    )(table_flat, dq, dk, dv, bias, dkn, lamvec, g)


def _merge_kernel(yf_ref, yd_ref, sa_ref, sb_ref, h_ref, wf_ref, wd_ref, wo_ref, g_ref, b_ref, o_ref, *, alpha):
    bf = jnp.dot(yf_ref[...], wf_ref[...], preferred_element_type=F32)
    bd = jnp.dot(yd_ref[...], wd_ref[...], preferred_element_type=F32)
    merged = sa_ref[...].astype(F32) * bf + sb_ref[...].astype(F32) * bd
    mix = jnp.dot(merged.astype(BF16), wo_ref[...], preferred_element_type=F32)
    o_ref[...] = _layer_norm(alpha * h_ref[...] + mix, g_ref[...], b_ref[...])


def _merge(yf, yd, sa, sb, h, wf, wd, wo, g, b, *, alpha, tm):
    S = h.shape[0]
    row = lambda n: pl.BlockSpec((tm, n), lambda i: (i, 0))
    full = lambda a: pl.BlockSpec(a.shape, lambda i: (0,) * a.ndim)
    return pl.pallas_call(
        functools.partial(_merge_kernel, alpha=alpha),
        grid=(S // tm,),
        in_specs=[row(512), row(512), row(D_MODEL), row(D_MODEL), row(D_MODEL),
                  full(wf), full(wd), full(wo), full(g), full(b)],
        out_specs=row(D_MODEL),
        out_shape=jax.ShapeDtypeStruct((S, D_MODEL), F32),
        compiler_params=_cparams(("arbitrary",)),
        name="merge_out",
    )(yf, yd, sa, sb, h, wf, wd, wo, g, b)


MOE_ROWS = 272
SCATTER_ROWS = 256
RANK_CHUNK = 256


def _router_kernel(h_ref, whi_ref, wlo_ref, cw_ref, rank_ref, rank_t_ref, cnt_ref):
    tm = h_ref.shape[0]
    h = h_ref[...]
    hhi = h.astype(BF16)
    hlo = (h - hhi.astype(F32)).astype(BF16)
    whi = whi_ref[...]
    dot = lambda a, b: jnp.dot(a, b, preferred_element_type=F32)
    logits = dot(hhi, whi) + dot(hhi, wlo_ref[...]) + dot(hlo, whi)
    lane = lax.broadcasted_iota(jnp.int32, logits.shape, 1).astype(F32)
    lg = jnp.where(lane < N_EXPERTS, logits, NEG)
    m1 = jnp.max(lg, axis=1, keepdims=True)
    i1 = jnp.min(jnp.where(lg == m1, lane, float(LANES)), axis=1, keepdims=True)
    lg2 = jnp.where(lane == i1, NEG, lg)
    m2 = jnp.max(lg2, axis=1, keepdims=True)
    i2 = jnp.min(jnp.where(lg2 == m2, lane, float(LANES)), axis=1, keepdims=True)
    e = jnp.exp(m2 - m1)
    g1 = 1.0 / (1.0 + e)
    chosen = jnp.logical_or(lane == i1, lane == i2)
    cw_ref[...] = jnp.where(lane == i1, g1, 0.0) + jnp.where(lane == i2, e * g1, 0.0)

    a = chosen.astype(BF16)
    r = lax.broadcasted_iota(jnp.int32, (RANK_CHUNK, RANK_CHUNK), 0)
    c = lax.broadcasted_iota(jnp.int32, (RANK_CHUNK, RANK_CHUNK), 1)
    strict_lower = (c < r).astype(BF16)
    ones_row = jnp.ones((8, RANK_CHUNK), BF16)
    carry = jnp.zeros((1, LANES), F32)
    for q in range(tm // RANK_CHUNK):
        rows = slice(q * RANK_CHUNK, (q + 1) * RANK_CHUNK)
        rank = jnp.where(chosen[rows], dot(strict_lower, a[rows]) + carry, -1.0)
        rank_ref[rows, :] = rank
        rank_t_ref[0, :, rows] = jnp.transpose(rank)[0:N_EXPERTS, :]
        carry = carry + dot(ones_row, a[rows])[0:1, :]
    cnt_ref[0] = jnp.broadcast_to(carry, (8, LANES)).astype(jnp.int32)


def _router(h, whi, wlo, *, tm):
    S = h.shape[0]
    nb = S // tm
    return pl.pallas_call(
        _router_kernel,
        grid=(nb,),
        in_specs=[pl.BlockSpec((tm, D_MODEL), lambda i: (i, 0)),
                  pl.BlockSpec(whi.shape, lambda i: (0, 0)), pl.BlockSpec(wlo.shape, lambda i: (0, 0))],
        out_specs=[pl.BlockSpec((tm, LANES), lambda i: (i, 0)),
                   pl.BlockSpec((tm, LANES), lambda i: (i, 0)),
                   pl.BlockSpec((1, N_EXPERTS, tm), lambda i: (i, 0, 0)),
                   pl.BlockSpec((1, 8, LANES), lambda i: (i, 0, 0))],
        out_shape=[jax.ShapeDtypeStruct((S, LANES), F32),
                   jax.ShapeDtypeStruct((S, LANES), F32),
                   jax.ShapeDtypeStruct((nb, N_EXPERTS, tm), F32),
                   jax.ShapeDtypeStruct((nb, 8, LANES), jnp.int32)],
        compiler_params=_cparams(("arbitrary",)),
        name="router",
    )(h, whi, wlo)


def _moe_kernel(cnt_ref, h_ref, cw_ref, rank_ref, rank_t_ref, wg_ref, wu_ref, wd_ref, g_ref, b_ref, o_ref,
                hb_sc, xs_sc, acc_sc, *, alpha):
    tm = h_ref.shape[0]
    i = pl.program_id(0)
    e = pl.program_id(1)
    c = pl.program_id(2)
    last_c = c == pl.num_programs(2) - 1
    n_rows = cnt_ref[i * N_EXPERTS + e]
    n_steps = (n_rows + MOE_ROWS - 1) // MOE_ROWS
    n_scatter = (n_rows + SCATTER_ROWS - 1) // SCATTER_ROWS

    @pl.when(jnp.logical_and(e == 0, c == 0))
    def _():
        hb_sc[...] = h_ref[...].astype(BF16)
        o_ref[...] = jnp.zeros_like(o_ref)

    @pl.when(c == 0)
    def _():
        rank_row = rank_t_ref[0, pl.ds(e, 1), :]

        def gather(t, carry):
            r0 = pl.multiple_of(t * MOE_ROWS, 16)
            slot = (lax.broadcasted_iota(jnp.int32, (MOE_ROWS, tm), 0) + r0).astype(F32)
            onehot = (rank_row == slot).astype(BF16)
            xs_sc[pl.ds(r0, MOE_ROWS), :] = jnp.dot(onehot, hb_sc[...], preferred_element_type=F32).astype(BF16)
            acc_sc[pl.ds(r0, MOE_ROWS), :] = jnp.zeros((MOE_ROWS, D_MODEL), F32)
            return carry

        lax.fori_loop(0, n_steps, gather, 0)

        @pl.when(n_steps * MOE_ROWS < n_scatter * SCATTER_ROWS)
        def _():
            acc_sc[pl.ds(pl.multiple_of(n_steps * MOE_ROWS, 16), SCATTER_ROWS), :] = jnp.zeros(
                (SCATTER_ROWS, D_MODEL), F32)

    def swiglu(t, carry):
        r0 = pl.multiple_of(t * MOE_ROWS, 16)
        x = xs_sc[pl.ds(r0, MOE_ROWS), :]
        gate = jnp.dot(x, wg_ref[0], preferred_element_type=F32)
        up = jnp.dot(x, wu_ref[0], preferred_element_type=F32)
        a = (gate * _sigmoid(gate) * up).astype(BF16)
        acc_sc[pl.ds(r0, MOE_ROWS), :] += jnp.dot(a, wd_ref[0], preferred_element_type=F32)
        return carry

    lax.fori_loop(0, n_steps, swiglu, 0)

    @pl.when(last_c)
    def _():
        lane = lax.broadcasted_iota(jnp.int32, (tm, LANES), 1)
        pick = lambda ref: jnp.broadcast_to(
            jnp.sum(jnp.where(lane == e, ref[...], 0.0), axis=1, keepdims=True), (tm, LANES))
        rank_col = pick(rank_ref)
        gate_col = jnp.concatenate([pick(cw_ref)] * (D_MODEL // LANES), axis=1)

        def scatter(t, carry):
            r0 = pl.multiple_of(t * SCATTER_ROWS, SCATTER_ROWS)
            slot = (lane + r0).astype(F32)
            onehot = jnp.concatenate([(rank_col == slot + float(a * LANES)) for a in range(SCATTER_ROWS // LANES)],
                                     axis=1).astype(BF16)
            y = acc_sc[pl.ds(r0, SCATTER_ROWS), :].astype(BF16)
            o_ref[...] += gate_col * jnp.dot(onehot, y, preferred_element_type=F32)
            return carry

        lax.fori_loop(0, n_scatter, scatter, 0)

    @pl.when(jnp.logical_and(e == pl.num_programs(1) - 1, last_c))
    def _():
        o_ref[...] = _layer_norm(alpha * h_ref[...] + o_ref[...], g_ref[...], b_ref[...])


def _moe(h, cw, rank, rank_t, counts, w_gu, w_dn, g, b, *, alpha, tm, fc):
    S = h.shape[0]
    E, _, F2 = w_gu.shape
    nc = F2 // 2 // fc
    once = pl.Buffered(1)
    rows_cap = pl.cdiv(tm, MOE_ROWS) * MOE_ROWS
    grid_spec = pltpu.PrefetchScalarGridSpec(
        num_scalar_prefetch=1,
        grid=(S // tm, E, nc),
        in_specs=[pl.BlockSpec((tm, D_MODEL), lambda i, e, c, cnt: (i, 0), pipeline_mode=once),
                  pl.BlockSpec((tm, LANES), lambda i, e, c, cnt: (i, 0), pipeline_mode=once),
                  pl.BlockSpec((tm, LANES), lambda i, e, c, cnt: (i, 0), pipeline_mode=once),
                  pl.BlockSpec((1, N_EXPERTS, tm), lambda i, e, c, cnt: (i, 0, 0), pipeline_mode=once),
                  pl.BlockSpec((1, D_MODEL, fc), lambda i, e, c, cnt: (e, 0, c)),
                  pl.BlockSpec((1, D_MODEL, fc), lambda i, e, c, cnt: (e, 0, nc + c)),
                  pl.BlockSpec((1, fc, D_MODEL), lambda i, e, c, cnt: (e, c, 0)),
                  pl.BlockSpec(g.shape, lambda i, e, c, cnt: (0, 0)),
                  pl.BlockSpec(b.shape, lambda i, e, c, cnt: (0, 0))],
        out_specs=pl.BlockSpec((tm, D_MODEL), lambda i, e, c, cnt: (i, 0), pipeline_mode=once),
        scratch_shapes=[pltpu.VMEM((tm, D_MODEL), BF16), pltpu.VMEM((rows_cap, D_MODEL), BF16),
                        pltpu.VMEM((rows_cap, D_MODEL), F32)],
    )
    return pl.pallas_call(
        functools.partial(_moe_kernel, alpha=alpha),
        grid_spec=grid_spec,
        out_shape=jax.ShapeDtypeStruct((S, D_MODEL), F32),
        compiler_params=_cparams(("arbitrary", "arbitrary", "arbitrary")),
        name="moe_ffn",
    )(counts, h, cw, rank, rank_t, w_gu, w_gu, w_dn, g, b)


def _ffn_kernel(h_ref, wgu_ref, wd_ref, g_ref, b_ref, o_ref, *, alpha, fc):
    h = h_ref[...]
    hb = h.astype(BF16)
    d_ff = wd_ref.shape[0]
    acc = None
    for c in range(d_ff // fc):
        gate = jnp.dot(hb, wgu_ref[:, c * fc:(c + 1) * fc], preferred_element_type=F32)
        up = jnp.dot(hb, wgu_ref[:, d_ff + c * fc:d_ff + (c + 1) * fc], preferred_element_type=F32)
        a = (gate * _sigmoid(gate) * up).astype(BF16)
        y = jnp.dot(a, wd_ref[c * fc:(c + 1) * fc, :], preferred_element_type=F32)
        acc = y if acc is None else acc + y
    o_ref[...] = _layer_norm(alpha * h + acc, g_ref[...], b_ref[...])


def _ffn(h, w_gu, w_dn, g, b, *, alpha, tm, fc):
    S = h.shape[0]
    once = pl.Buffered(1)
    return pl.pallas_call(
        functools.partial(_ffn_kernel, alpha=alpha, fc=fc),
        grid=(S // tm,),
        in_specs=[pl.BlockSpec((tm, D_MODEL), lambda i: (i, 0)),
                  pl.BlockSpec(w_gu.shape, lambda i: (0, 0), pipeline_mode=once),
                  pl.BlockSpec(w_dn.shape, lambda i: (0, 0), pipeline_mode=once),
                  pl.BlockSpec(g.shape, lambda i: (0, 0)), pl.BlockSpec(b.shape, lambda i: (0, 0))],
        out_specs=pl.BlockSpec((tm, D_MODEL), lambda i: (i, 0)),
        out_shape=jax.ShapeDtypeStruct((S, D_MODEL), F32),
        compiler_params=_cparams(("arbitrary",)),
        name="dense_ffn",
    )(h, w_gu, w_dn, g, b)


def _pad_lanes(a, n=LANES):
    return jnp.pad(a, [(0, 0)] * (a.ndim - 1) + [(0, n - a.shape[-1])])


def kernel(x, ln_in_g, ln_in_b, w_in, b_fgate, lam_q1, lam_k1, lam_q2, lam_k2, subln_g, w_branch_fox, w_branch_diff, w_out, ln_mix_g, ln_mix_b, rel_bias, w_ffn_gate_up, w_ffn_down, w_router, w_expert_gate_up, w_expert_down, ln_ffn_g, ln_ffn_b):
    B, S, _ = x.shape
    assert B == 1
    depth = w_in.shape[0]
    alpha = (2.0 * depth) ** 0.25
    tq = min(ATTN_TQ, S)
    tk = min(ATTN_TK, tq)
    tm = min(512, S)
    tf = min(1024, S)
    te = min(MOE_TM, S)
    assert tk >= REL_MAX_DISTANCE and tq % tk == 0 and S % tq == 0 and S % tf == 0 and S % tm == 0 and S % te == 0
    assert te % SCATTER_ROWS == 0 and te % RANK_CHUNK == 0 and MOE_ROWS % 16 == 0
    row = lambda v: v.reshape(1, -1).astype(F32)

    table_flat = rel_bias.astype(F32).T.reshape(-1)
    bias = _bias_tiles(table_flat, t=tk)

    h = x[0]
    for l in range(depth):
        lam_init = 0.8 - 0.6 * math.exp(-0.3 * l)
        fq, fk, fv, fg, dq, dk, dv, ga, gb = jnp.split(w_in[l], list(_cumsum(IN_SPLIT_SIZES))[:-1], axis=-1)
        w_cat = jnp.concatenate([fq, fk, fv, dq, dk, dv, ga, gb, _pad_lanes(fg)], axis=-1).astype(BF16)
        outs = _inproj(h, row(ln_in_g), row(ln_in_b), w_cat, _pad_lanes(row(b_fgate[l])), do_ln=(l == 0), tm=tk)
        qa, ka, va, stats, dq, dk, dv, sa, sb, dkn = outs[:10]
        if l == 0:
            h = outs[10]
        start = _plan(stats, nsub=tq // tk)[:, :FOX_HEADS].reshape(-1)
        y_fox = _fox_attention(start, qa, ka, va, tq=tq, tk=tk)
        lamvec = jnp.zeros((8, LANES), F32).at[0:4, 0:HEAD_DIM].set(
            jnp.stack([lam_q1[l], lam_k1[l], lam_q2[l], lam_k2[l]]).astype(F32))
        y_diff = _diff_attention(table_flat, dq, dk, dv, bias, dkn, lamvec, row(subln_g[l]), tq=tq, tk=tk,
                                 lam_init=lam_init)
        h = _merge(y_fox, y_diff, sa, sb, h, w_branch_fox[l].astype(BF16), w_branch_diff[l].astype(BF16),
                   w_out[l].astype(BF16), row(ln_mix_g[l]), row(ln_mix_b[l]), alpha=alpha, tm=tm)
        g, b = row(ln_ffn_g[l]), row(ln_ffn_b[l])
        if l % 2 == 0:
            h = _ffn(h, w_ffn_gate_up[l // 2].astype(BF16), w_ffn_down[l // 2].astype(BF16), g, b,
                     alpha=alpha, tm=tf, fc=1408)
        else:
            wr = _pad_lanes(w_router[l // 2].astype(F32))
            whi = wr.astype(BF16)
            wlo = (wr - whi.astype(F32)).astype(BF16)
            cw, rank, rank_t, counts = _router(h, whi, wlo, tm=te)
            h = _moe(h, cw, rank, rank_t, counts[:, 0, :N_EXPERTS].reshape(-1),
                     w_expert_gate_up[l // 2].astype(BF16), w_expert_down[l // 2].astype(BF16),
                     g, b, alpha=alpha, tm=te, fc=896)
    return h[None]


def _cumsum(sizes):
    tot = 0
    for s in sizes:
        tot += s
        yield tot
```

```python
import functools
import math

import jax
import jax.numpy as jnp
from jax import lax
from jax.experimental import pallas as pl
from jax.experimental.pallas import tpu as pltpu

F32 = jnp.float32
BF16 = jnp.bfloat16

D_MODEL = 1024
HEAD_DIM = 64
FOX_HEADS = 8
FOX_WIDTH = FOX_HEADS * HEAD_DIM
DIFF_HEADS = 4
DIFF_QK_WIDTH = DIFF_HEADS * 2 * HEAD_DIM
DIFF_V_DIM = 2 * HEAD_DIM
DIFF_V_WIDTH = DIFF_HEADS * DIFF_V_DIM
IN_SPLIT_SIZES = (FOX_WIDTH, FOX_WIDTH, FOX_WIDTH, FOX_HEADS, DIFF_QK_WIDTH, DIFF_QK_WIDTH, DIFF_V_WIDTH,
                  D_MODEL, D_MODEL)
REL_BUCKETS = 32
REL_MAX_DISTANCE = 128
N_EXPERTS = 8
LN_EPS = 1e-5
SUBLN_EPS = 1e-5

LANES = 128
LOG2E = 1.4426950408889634
Q_SCALE = HEAD_DIM ** -0.5 * LOG2E
NEG = -1e30
VMEM_LIMIT = 56 * 1024 * 1024
ATTN_TQ = 2048
ATTN_TK = 512
MOE_TM = 2048


def _cparams(sem):
    return pltpu.CompilerParams(dimension_semantics=sem, vmem_limit_bytes=VMEM_LIMIT)


def _layer_norm(x, g, b):
    mu = jnp.mean(x, axis=-1, keepdims=True)
    xc = x - mu
    var = jnp.mean(xc * xc, axis=-1, keepdims=True)
    return xc * lax.rsqrt(var + LN_EPS) * g + b


def _sigmoid(x):
    return 1.0 / (1.0 + jnp.exp(-x))


def _split3(x):
    hi = x.astype(BF16)
    r1 = x - hi.astype(F32)
    mid = r1.astype(BF16)
    lo = (r1 - mid.astype(F32)).astype(BF16)
    return hi, mid, lo


_C_FQ, _C_FK, _C_FV, _C_DQ, _C_DK, _C_DV, _C_GA, _C_GB, _C_FG, _C_END = (
    0, 512, 1024, 1536, 2048, 2560, 3072, 4096, 5120, 5248)


def _inproj_kernel(x_ref, g_ref, b_ref, w_ref, bf_ref, qa_ref, ka_ref, va_ref, st_ref, dq, dk, dv, sa, sb, dkn,
                   *rest, do_ln):
    carry_ref = rest[-1]
    tb = x_ref.shape[0]

    @pl.when(pl.program_id(0) == 0)
    def _():
        carry_ref[...] = jnp.zeros_like(carry_ref)

    x = x_ref[...]
    if do_ln:
        x = _layer_norm(x, g_ref[...], b_ref[...])
        rest[0][...] = x
    xb = x.astype(BF16)

    def mm(a, b):
        return jnp.dot(xb, w_ref[:, a:b], preferred_element_type=F32)

    fq = (mm(_C_FQ, _C_FK) * Q_SCALE).astype(BF16)
    fk = mm(_C_FK, _C_FV).astype(BF16)
    fv = mm(_C_FV, _C_DQ).astype(BF16)

    z = mm(_C_FG, _C_END) + bf_ref[...]
    logf = jnp.minimum(z, 0.0) - jnp.log1p(jnp.exp(-jnp.abs(z)))
    r = lax.broadcasted_iota(jnp.int32, (tb, tb), 0)
    c = lax.broadcasted_iota(jnp.int32, (tb, tb), 1)
    tri = (c <= r).astype(BF16)
    hi, mid, lo = _split3(logf)
    dot = lambda a: jnp.dot(tri, a, preferred_element_type=F32)
    cum = (dot(hi) + dot(mid)) + dot(lo) + carry_ref[0:1, :]
    carry_ref[0:1, :] = cum[tb - 1:tb, :]

    c_hi, c_mid, c_lo = (t.astype(F32) for t in _split3(cum * LOG2E))
    lane = lax.broadcasted_iota(jnp.int32, (tb, LANES), 1)
    one = jnp.ones((tb, LANES), F32)
    zero = jnp.zeros((tb, LANES), F32)
    lane1 = lax.broadcasted_iota(jnp.int32, (1, LANES), 1)
    q_norm2 = jnp.zeros((1, LANES), F32)
    k_norm2 = jnp.zeros((1, LANES), F32)
    diag_min = jnp.zeros((1, LANES), F32)

    def max_norm2(slab):
        n2 = jnp.sum(jnp.where(lane < 64, slab * slab, 0.0), axis=1, keepdims=True)
        return jnp.max(n2, axis=0, keepdims=True)

    for h in range(FOX_HEADS):
        p, odd = divmod(h, 2)
        col = lambda t: jnp.broadcast_to(t[:, h:h + 1], (tb, LANES))
        chi, cmid, clo = col(c_hi), col(c_mid), col(c_lo)

        def head_slab(a):
            slab = a[:, p * LANES:(p + 1) * LANES].astype(F32)
            return pltpu.roll(slab, HEAD_DIM, 1) if odd else slab

        qslab = head_slab(fq)
        kslab = head_slab(fk)
        q_norm2 = jnp.where(lane1 == h, max_norm2(qslab), q_norm2)
        k_norm2 = jnp.where(lane1 == h, max_norm2(kslab), k_norm2)
        own = jnp.sum(jnp.where(lane < 64, qslab * kslab, 0.0), axis=1, keepdims=True)
        diag_min = jnp.where(lane1 == h, jnp.min(own, axis=0, keepdims=True), diag_min)
        qa = jnp.where(lane < 64, qslab,
                       jnp.where(lane == 64, chi, jnp.where(lane == 65, cmid, jnp.where(lane == 66, clo,
                                 jnp.where(lane < 70, one, zero)))))
        ka = jnp.where(lane < 64, kslab,
                       jnp.where(lane < 67, one, jnp.where(lane == 67, -chi, jnp.where(lane == 68, -cmid,
                                 jnp.where(lane == 69, -clo, zero)))))
        qa_ref[h] = qa.astype(BF16)
        ka_ref[h] = ka.astype(BF16)
        vslab = fv[:, p * LANES:(p + 1) * LANES].astype(F32)
        if odd:
            va = jnp.where(lane >= 64, vslab, jnp.where(lane == 0, one, zero))
        else:
            va = jnp.where(lane < 64, vslab, jnp.where(lane == 64, one, zero))
        va_ref[h] = va.astype(BF16)

    dq[...] = (mm(_C_DQ, _C_DK) * Q_SCALE).astype(BF16)
    dk16 = mm(_C_DK, _C_DV).astype(BF16)
    dk[...] = dk16
    dv[...] = mm(_C_DV, _C_GA).astype(BF16)
    sa[...] = _sigmoid(mm(_C_GA, _C_GB)).astype(BF16)
    sb[...] = _sigmoid(mm(_C_GB, _C_FG)).astype(BF16)

    dk_norm2 = carry_ref[1:2, :]
    for h in range(DIFF_HEADS):
        slab = dk16[:, h * LANES:(h + 1) * LANES].astype(F32)
        sq = slab * slab
        for c in range(2):
            n2 = jnp.sum(jnp.where((lane < 64) == (c == 0), sq, 0.0), axis=1, keepdims=True)
            dk_norm2 = jnp.where(lane1 == 2 * h + c, jnp.maximum(dk_norm2, jnp.max(n2, axis=0, keepdims=True)),
                                 dk_norm2)
    carry_ref[1:2, :] = dk_norm2
    dkn[...] = jnp.broadcast_to(dk_norm2, dkn.shape)

    st_ref[0] = jnp.concatenate([q_norm2, k_norm2, cum[tb - 1:tb, :] * LOG2E, diag_min,
                                 jnp.zeros((4, LANES), F32)], axis=0)


def _inproj(x, g, b, w, bf, *, do_ln, tm):
    S = x.shape[0]
    row = lambda n: pl.BlockSpec((tm, n), lambda i: (i, 0))
    full = lambda a: pl.BlockSpec(a.shape, lambda i: (0,) * a.ndim)
    hspec = pl.BlockSpec((FOX_HEADS, tm, LANES), lambda i: (0, i, 0))
    hshape = jax.ShapeDtypeStruct((FOX_HEADS, S, LANES), BF16)
    out_shape = [hshape] * 3 + [jax.ShapeDtypeStruct((S // tm, 8, LANES), F32)] + [
        jax.ShapeDtypeStruct((S, 512), BF16)] * 3 + [jax.ShapeDtypeStruct((S, D_MODEL), BF16)] * 2 + [
        jax.ShapeDtypeStruct((8, LANES), F32)]
    out_specs = [hspec] * 3 + [pl.BlockSpec((1, 8, LANES), lambda i: (i, 0, 0))] + [row(512)] * 3 + [
        row(D_MODEL)] * 2 + [pl.BlockSpec((8, LANES), lambda i: (0, 0))]
    if do_ln:
        out_shape.append(jax.ShapeDtypeStruct((S, D_MODEL), F32))
        out_specs.append(row(D_MODEL))
    return pl.pallas_call(
        functools.partial(_inproj_kernel, do_ln=do_ln),
        grid=(S // tm,),
        in_specs=[row(D_MODEL), full(g), full(b), full(w), full(bf)],
        out_specs=out_specs,
        out_shape=out_shape,
        scratch_shapes=[pltpu.VMEM((8, LANES), F32)],
        compiler_params=_cparams(("arbitrary",)),
        name="inproj",
    )(x, g, b, w, bf)


SKIP_LOG2 = 130.0


def _plan_kernel(st_ref, start_ref, *, nt, nsub):
    qn = jnp.sqrt(st_ref[pl.ds(0, nt, stride=8), :])
    kn = jnp.sqrt(st_ref[pl.ds(1, nt, stride=8), :])
    cum_end = st_ref[pl.ds(2, nt, stride=8), :]
    diag_min = st_ref[pl.ds(3, nt, stride=8), :]
    tile = lax.broadcasted_iota(jnp.int32, (nt, LANES), 0)
    slack = SKIP_LOG2 + 1e-4 * jnp.abs(cum_end)
    rows = []
    for t in range(nt):
        block_start = (t // nsub) * nsub
        m_low = diag_min[t:t + 1, :] - 0.5
        cum_q = cum_end[t - 1:t, :] if t > 0 else jnp.zeros((1, LANES), F32)
        upper = qn[t:t + 1, :] * kn + (cum_q - cum_end) - m_low
        needed = jnp.logical_and(upper >= -slack, tile < block_start)
        rows.append(jnp.min(jnp.where(needed, tile, block_start), axis=0, keepdims=True))
    start_ref[...] = jnp.concatenate(rows, axis=0)


def _plan(stats, *, nsub):
    nt = stats.shape[0]
    assert nt % 8 == 0 or nt < 8
    return pl.pallas_call(
        functools.partial(_plan_kernel, nt=nt, nsub=nsub),
        out_shape=jax.ShapeDtypeStruct((nt, LANES), jnp.int32),
        compiler_params=pltpu.CompilerParams(vmem_limit_bytes=VMEM_LIMIT),
        name="fox_plan",
    )(stats.reshape(nt * 8, LANES))


def _nt_dot(a, b):
    return lax.dot_general(a, b, (((1,), (1,)), ((), ())), preferred_element_type=F32)


def _fox_kernel(start_ref, q_ref, k_ref, v_ref, o_ref, m_sc, acc_sc, *, tq, tk):
    i = pl.program_id(1)
    pair = pl.program_id(0)
    m_sc[...] = jnp.full_like(m_sc, NEG)
    acc_sc[...] = jnp.zeros_like(acc_sc)
    nsub = tq // tk

    def tile(j, r0, r1, masked):
        off = pl.multiple_of(j * tk, tk)
        rows = r1 - r0
        for hh in range(2):
            s = _nt_dot(q_ref[hh, r0:r1, :], k_ref[hh, pl.ds(off, tk), :])
            if masked:
                r = lax.broadcasted_iota(jnp.int32, (rows, tk), 0)
                c = lax.broadcasted_iota(jnp.int32, (rows, tk), 1)
                s = jnp.where(c <= r, s, NEG)
            m_prev = m_sc[hh, r0:r1, :]
            m_next = jnp.maximum(m_prev, jnp.max(s, axis=1, keepdims=True))
            alpha = jnp.exp2(m_prev - m_next)
            p = jnp.exp2(s - jnp.concatenate([m_next] * (tk // LANES), axis=1)).astype(BF16)
            pv = jnp.dot(p, v_ref[hh, pl.ds(off, tk), :], preferred_element_type=F32)
            acc_sc[hh, r0:r1, :] = alpha * acc_sc[hh, r0:r1, :] + pv
            m_sc[hh, r0:r1, :] = m_next

    first = [jnp.minimum(start_ref[(i * nsub + u) * FOX_HEADS + 2 * pair],
                         start_ref[(i * nsub + u) * FOX_HEADS + 2 * pair + 1]) for u in range(nsub)]
    for u in range(nsub - 2, -1, -1):
        first[u] = jnp.minimum(first[u], first[u + 1])
    first.append(i * nsub)
    for u in range(nsub):
        def body(j, carry, r1=(u + 1) * tk):
            tile(j, 0, r1, False)
            return carry

        lax.fori_loop(first[u], first[u + 1], body, 0)
    for u in range(nsub):
        tile(i * nsub + u, u * tk, tq, True)

    a0 = acc_sc[0]
    a1 = acc_sc[1]
    lane = lax.broadcasted_iota(jnp.int32, (tq, LANES), 1)
    o_ref[...] = jnp.where(lane < 64, a0 / a0[:, 64:65], a1 / a1[:, 0:1]).astype(o_ref.dtype)


def _fox_attention(start, qa, ka, va, *, tq, tk):
    S = qa.shape[1]
    grid_spec = pltpu.PrefetchScalarGridSpec(
        num_scalar_prefetch=1,
        grid=(FOX_HEADS // 2, S // tq),
        in_specs=[pl.BlockSpec((2, tq, LANES), lambda p, i, st: (p, i, 0)),
                  pl.BlockSpec((2, S, LANES), lambda p, i, st: (p, 0, 0)),
                  pl.BlockSpec((2, S, LANES), lambda p, i, st: (p, 0, 0))],
        out_specs=pl.BlockSpec((tq, LANES), lambda p, i, st: (i, p)),
        scratch_shapes=[pltpu.VMEM((2, tq, LANES), F32), pltpu.VMEM((2, tq, LANES), F32)],
    )
    return pl.pallas_call(
        functools.partial(_fox_kernel, tq=tq, tk=tk),
        grid_spec=grid_spec,
        out_shape=jax.ShapeDtypeStruct((S, FOX_WIDTH), BF16),
        compiler_params=_cparams(("arbitrary", "arbitrary")),
        name="fox_attn",
    )(start, qa, ka, va)


BIAS_KINDS = 2


def _bias_tile_kernel(table_ref, o_ref, *, t):
    h = pl.program_id(0)
    kind = pl.program_id(1)
    r = lax.broadcasted_iota(jnp.int32, (t, t), 0)
    c = lax.broadcasted_iota(jnp.int32, (t, t), 1)
    dist = r - c + kind * t
    n = jnp.maximum(dist, 0)
    max_exact = REL_BUCKETS // 2
    nf = jnp.maximum(n, 1).astype(F32)
    log_part = jnp.log(nf / max_exact) / math.log(REL_MAX_DISTANCE / max_exact) * (REL_BUCKETS - max_exact)
    large = jnp.minimum(max_exact + log_part.astype(jnp.int32), REL_BUCKETS - 1)
    bucket = jnp.where(n < max_exact, n, large)
    val = jnp.zeros((t, t), F32)
    for b in range(REL_BUCKETS):
        val = jnp.where(bucket == b, table_ref[h * REL_BUCKETS + b], val)
    far = table_ref[h * REL_BUCKETS + REL_BUCKETS - 1]
    o_ref[0, 0] = jnp.where(dist >= 0, (val - far) * LOG2E, NEG)


def _bias_tiles(table_flat, *, t):
    return pl.pallas_call(
        functools.partial(_bias_tile_kernel, t=t),
        grid=(DIFF_HEADS, BIAS_KINDS),
        in_specs=[pl.BlockSpec(memory_space=pltpu.SMEM)],
        out_specs=pl.BlockSpec((1, 1, t, t), lambda h, k: (h, k, 0, 0)),
        out_shape=jax.ShapeDtypeStruct((DIFF_HEADS, BIAS_KINDS, t, t), F32),
        compiler_params=_cparams(("arbitrary", "arbitrary")),
        name="bias_tiles",
    )(table_flat)


FIXED_REF_MAX_SHIFT = 64.0


def _diff_kernel(tab_ref, q_ref, k_ref, v_ref, bias_ref, kn_ref, lam_ref, g_ref, o_ref, m_sc, l_sc, acc_sc,
                 *, tq, tk, lam_init):
    h = pl.program_id(0)
    i = pl.program_id(1)
    l_sc[...] = jnp.zeros_like(l_sc)
    acc_sc[...] = jnp.zeros_like(acc_sc)
    nsub = tq // tk

    lane = lax.broadcasted_iota(jnp.int32, (tq, LANES), 1)
    q = q_ref[...]
    zero = jnp.zeros_like(q)
    qs = (jnp.where(lane < 64, q, zero), jnp.where(lane >= 64, q, zero))

    def run(fixed):
        def tile(j, r0, kinds):
            off = pl.multiple_of(j * tk, tk)
            k = k_ref[pl.ds(off, tk), :]
            v = v_ref[pl.ds(off, tk), :]
            for c in range(2):
                s = _nt_dot(qs[c][r0:], k)
                if kinds is not None:
                    s = jnp.concatenate([s[a * tk:(a + 1) * tk] + bias_ref[0, kd] if kd < 2
                                         else s[a * tk:(a + 1) * tk] for a, kd in enumerate(kinds)], axis=0)
                m_prev = m_sc[c, r0:, :]
                if fixed:
                    p = jnp.exp2(s - jnp.concatenate([m_prev] * (tk // LANES), axis=1))
                    l_sc[c, r0:, :] += sum(p[:, a * LANES:(a + 1) * LANES] for a in range(tk // LANES))
                    acc_sc[c, r0:, :] += jnp.dot(p.astype(BF16), v, preferred_element_type=F32)
                else:
                    m_next = jnp.maximum(m_prev, jnp.max(s, axis=1, keepdims=True))
                    p = jnp.exp2(s - jnp.concatenate([m_next] * (tk // LANES), axis=1))
                    alpha = jnp.exp2(m_prev - m_next)
                    l_sc[c, r0:, :] = alpha * l_sc[c, r0:, :] + sum(p[:, a * LANES:(a + 1) * LANES]
                                                                    for a in range(tk // LANES))
                    pv = jnp.dot(p.astype(BF16), v, preferred_element_type=F32)
                    acc_sc[c, r0:, :] = alpha * acc_sc[c, r0:, :] + pv
                    m_sc[c, r0:, :] = m_next

        def body(j, carry):
            tile(j, 0, None)
            return carry

        lax.fori_loop(0, i * nsub - 1, body, 0)

        @pl.when(i > 0)
        def _():
            tile(i * nsub - 1, 0, (1,) + (2,) * (nsub - 1))

        for u in range(nsub):
            tile(i * nsub + u, u * tk, ((0, 1) + (2,) * nsub)[:nsub - u])

    lane1 = lax.broadcasted_iota(jnp.int32, (1, LANES), 1)
    far = tab_ref[h * REL_BUCKETS + REL_BUCKETS - 1]
    b0 = (tab_ref[h * REL_BUCKETS] - far) * LOG2E
    bmax = b0
    for b in range(1, REL_BUCKETS):
        bmax = jnp.maximum(bmax, (tab_ref[h * REL_BUCKETS + b] - far) * LOG2E)
    bounds = []
    for c in range(2):
        qf = qs[c].astype(F32)
        qn = jnp.sqrt(jnp.sum(qf * qf, axis=1, keepdims=True))
        kmax = jnp.sqrt(jnp.sum(jnp.where(lane1 == 2 * h + c, kn_ref[0:1, :], 0.0), axis=1, keepdims=True))
        bounds.append(qn * kmax + 1.0)
    gap = 2.0 * jnp.max(jnp.maximum(bounds[0], bounds[1])) + (bmax - b0)
    fits = gap <= FIXED_REF_MAX_SHIFT

    @pl.when(fits)
    def _():
        for c in range(2):
            m_sc[c] = jnp.broadcast_to(bounds[c] + bmax, (tq, LANES))
        run(True)

    @pl.when(jnp.logical_not(fits))
    def _():
        m_sc[...] = jnp.full_like(m_sc, NEG)
        run(False)

    lv = lam_ref[...]
    lam = (jnp.exp(jnp.sum(lv[0:1] * lv[1:2], axis=1, keepdims=True))
           - jnp.exp(jnp.sum(lv[2:3] * lv[3:4], axis=1, keepdims=True)) + lam_init)
    l0 = jnp.sum(l_sc[0], axis=1, keepdims=True)
    l1 = jnp.sum(l_sc[1], axis=1, keepdims=True)
    o = acc_sc[0] / l0 - lam * (acc_sc[1] / l1)
    o = o * lax.rsqrt(jnp.mean(o * o, axis=1, keepdims=True) + SUBLN_EPS) * g_ref[...] * (1.0 - lam_init)
    o_ref[...] = o.astype(o_ref.dtype)


def _diff_attention(table_flat, dq, dk, dv, bias, dkn, lamvec, g, *, tq, tk, lam_init):
    S = dq.shape[0]
    grid_spec = pltpu.PrefetchScalarGridSpec(
        num_scalar_prefetch=1,
        grid=(DIFF_HEADS, S // tq),
        in_specs=[pl.BlockSpec((tq, LANES), lambda h, i, tab: (i, h)),
                  pl.BlockSpec((S, LANES), lambda h, i, tab: (0, h)),
                  pl.BlockSpec((S, LANES), lambda h, i, tab: (0, h)),
                  pl.BlockSpec((1, BIAS_KINDS, tk, tk), lambda h, i, tab: (h, 0, 0, 0)),
                  pl.BlockSpec((8, LANES), lambda h, i, tab: (0, 0)),
                  pl.BlockSpec((8, LANES), lambda h, i, tab: (0, 0)),
                  pl.BlockSpec((1, LANES), lambda h, i, tab: (0, 0))],
        out_specs=pl.BlockSpec((tq, LANES), lambda h, i, tab: (i, h)),
        scratch_shapes=[pltpu.VMEM((2, tq, LANES), F32), pltpu.VMEM((2, tq, LANES), F32),
                        pltpu.VMEM((2, tq, LANES), F32)],
    )
    return pl.pallas_call(
        functools.partial(_diff_kernel, tq=tq, tk=tk, lam_init=lam_init),
        grid_spec=grid_spec,
        out_shape=jax.ShapeDtypeStruct((S, DIFF_V_WIDTH), BF16),
        compiler_params=_cparams(("arbitrary", "arbitrary")),
        name="diff_attn",
    )(table_flat, dq, dk, dv, bias, dkn, lamvec, g)


def _merge_kernel(yf_ref, yd_ref, sa_ref, sb_ref, h_ref, wf_ref, wd_ref, wo_ref, g_ref, b_ref, o_ref, *, alpha):
    tm = h_ref.shape[0]
    halves = (slice(0, tm // 2), slice(tm // 2, tm))
    merged = []
    for r in halves:
        bf = jnp.dot(yf_ref[r, :], wf_ref[...], preferred_element_type=F32)
        bd = jnp.dot(yd_ref[r, :], wd_ref[...], preferred_element_type=F32)
        merged.append((sa_ref[r, :].astype(F32) * bf + sb_ref[r, :].astype(F32) * bd).astype(BF16))
    for r, m in zip(halves, merged):
        mix = jnp.dot(m, wo_ref[...], preferred_element_type=F32)
        o_ref[r, :] = _layer_norm(alpha * h_ref[r, :] + mix, g_ref[...], b_ref[...])


def _merge(yf, yd, sa, sb, h, wf, wd, wo, g, b, *, alpha, tm):
    S = h.shape[0]
    row = lambda n: pl.BlockSpec((tm, n), lambda i: (i, 0))
    full = lambda a: pl.BlockSpec(a.shape, lambda i: (0,) * a.ndim)
    return pl.pallas_call(
        functools.partial(_merge_kernel, alpha=alpha),
        grid=(S // tm,),
        in_specs=[row(512), row(512), row(D_MODEL), row(D_MODEL), row(D_MODEL),
                  full(wf), full(wd), full(wo), full(g), full(b)],
        out_specs=row(D_MODEL),
        out_shape=jax.ShapeDtypeStruct((S, D_MODEL), F32),
        compiler_params=_cparams(("arbitrary",)),
        name="merge_out",
    )(yf, yd, sa, sb, h, wf, wd, wo, g, b)


MOE_ROWS = 272
SCATTER_ROWS = 256
RANK_CHUNK = 256


def _router_kernel(h_ref, whi_ref, wlo_ref, cw_ref, rank_ref, rank_t_ref, cnt_ref):
    tm = h_ref.shape[0]
    h = h_ref[...]
    hhi = h.astype(BF16)
    hlo = (h - hhi.astype(F32)).astype(BF16)
    whi = whi_ref[...]
    dot = lambda a, b: jnp.dot(a, b, preferred_element_type=F32)
    logits = dot(hhi, whi) + dot(hhi, wlo_ref[...]) + dot(hlo, whi)
    lane = lax.broadcasted_iota(jnp.int32, logits.shape, 1).astype(F32)
    lg = jnp.where(lane < N_EXPERTS, logits, NEG)
    m1 = jnp.max(lg, axis=1, keepdims=True)
    i1 = jnp.min(jnp.where(lg == m1, lane, float(LANES)), axis=1, keepdims=True)
    lg2 = jnp.where(lane == i1, NEG, lg)
    m2 = jnp.max(lg2, axis=1, keepdims=True)
    i2 = jnp.min(jnp.where(lg2 == m2, lane, float(LANES)), axis=1, keepdims=True)
    e = jnp.exp(m2 - m1)
    g1 = 1.0 / (1.0 + e)
    chosen = jnp.logical_or(lane == i1, lane == i2)
    cw_ref[...] = jnp.where(lane == i1, g1, 0.0) + jnp.where(lane == i2, e * g1, 0.0)

    a = chosen.astype(BF16)
    r = lax.broadcasted_iota(jnp.int32, (RANK_CHUNK, RANK_CHUNK), 0)
    c = lax.broadcasted_iota(jnp.int32, (RANK_CHUNK, RANK_CHUNK), 1)
    strict_lower = (c < r).astype(BF16)
    ones_row = jnp.ones((8, RANK_CHUNK), BF16)
    carry = jnp.zeros((1, LANES), F32)
    for q in range(tm // RANK_CHUNK):
        rows = slice(q * RANK_CHUNK, (q + 1) * RANK_CHUNK)
        rank = jnp.where(chosen[rows], dot(strict_lower, a[rows]) + carry, -1.0)
        rank_ref[rows, :] = rank
        rank_t_ref[0, :, rows] = jnp.transpose(rank)[0:N_EXPERTS, :]
        carry = carry + dot(ones_row, a[rows])[0:1, :]
    cnt_ref[0] = jnp.broadcast_to(carry, (8, LANES)).astype(jnp.int32)


def _router(h, whi, wlo, *, tm):
    S = h.shape[0]
    nb = S // tm
    return pl.pallas_call(
        _router_kernel,
        grid=(nb,),
        in_specs=[pl.BlockSpec((tm, D_MODEL), lambda i: (i, 0)),
                  pl.BlockSpec(whi.shape, lambda i: (0, 0)), pl.BlockSpec(wlo.shape, lambda i: (0, 0))],
        out_specs=[pl.BlockSpec((tm, LANES), lambda i: (i, 0)),
                   pl.BlockSpec((tm, LANES), lambda i: (i, 0)),
                   pl.BlockSpec((1, N_EXPERTS, tm), lambda i: (i, 0, 0)),
                   pl.BlockSpec((1, 8, LANES), lambda i: (i, 0, 0))],
        out_shape=[jax.ShapeDtypeStruct((S, LANES), F32),
                   jax.ShapeDtypeStruct((S, LANES), F32),
                   jax.ShapeDtypeStruct((nb, N_EXPERTS, tm), F32),
                   jax.ShapeDtypeStruct((nb, 8, LANES), jnp.int32)],
        compiler_params=_cparams(("arbitrary",)),
        name="router",
    )(h, whi, wlo)


def _moe_kernel(cnt_ref, h_ref, cw_ref, rank_ref, rank_t_ref, wg_ref, wu_ref, wd_ref, g_ref, b_ref, o_ref,
                hb_sc, xs_sc, acc_sc, *, alpha):
    tm = h_ref.shape[0]
    i = pl.program_id(0)
    e = pl.program_id(1)
    c = pl.program_id(2)
    last_c = c == pl.num_programs(2) - 1
    n_rows = cnt_ref[i * N_EXPERTS + e]
    n_steps = (n_rows + MOE_ROWS - 1) // MOE_ROWS
    n_scatter = (n_rows + SCATTER_ROWS - 1) // SCATTER_ROWS

    @pl.when(jnp.logical_and(e == 0, c == 0))
    def _():
        hb_sc[...] = h_ref[...].astype(BF16)
        o_ref[...] = jnp.zeros_like(o_ref)

    @pl.when(c == 0)
    def _():
        rank_row = rank_t_ref[0, pl.ds(e, 1), :]

        def gather(t, carry):
            r0 = pl.multiple_of(t * MOE_ROWS, 16)
            slot = (lax.broadcasted_iota(jnp.int32, (MOE_ROWS, tm), 0) + r0).astype(F32)
            onehot = (rank_row == slot).astype(BF16)
            xs_sc[pl.ds(r0, MOE_ROWS), :] = jnp.dot(onehot, hb_sc[...], preferred_element_type=F32).astype(BF16)
            acc_sc[pl.ds(r0, MOE_ROWS), :] = jnp.zeros((MOE_ROWS, D_MODEL), F32)
            return carry

        lax.fori_loop(0, n_steps, gather, 0)

        @pl.when(n_steps * MOE_ROWS < n_scatter * SCATTER_ROWS)
        def _():
            acc_sc[pl.ds(pl.multiple_of(n_steps * MOE_ROWS, 16), SCATTER_ROWS), :] = jnp.zeros(
                (SCATTER_ROWS, D_MODEL), F32)

    def swiglu(t, carry):
        r0 = pl.multiple_of(t * MOE_ROWS, 16)
        x = xs_sc[pl.ds(r0, MOE_ROWS), :]
        gate = jnp.dot(x, wg_ref[0], preferred_element_type=F32)
        up = jnp.dot(x, wu_ref[0], preferred_element_type=F32)
        a = (gate * _sigmoid(gate) * up).astype(BF16)
        acc_sc[pl.ds(r0, MOE_ROWS), :] += jnp.dot(a, wd_ref[0], preferred_element_type=F32)
        return carry

    lax.fori_loop(0, n_steps, swiglu, 0)

    @pl.when(last_c)
    def _():
        lane = lax.broadcasted_iota(jnp.int32, (tm, LANES), 1)
        pick = lambda ref: jnp.broadcast_to(
            jnp.sum(jnp.where(lane == e, ref[...], 0.0), axis=1, keepdims=True), (tm, LANES))
        rank_col = pick(rank_ref)
        gate_col = jnp.concatenate([pick(cw_ref)] * (D_MODEL // LANES), axis=1)

        def scatter(t, carry):
            r0 = pl.multiple_of(t * SCATTER_ROWS, SCATTER_ROWS)
            slot = (lane + r0).astype(F32)
            onehot = jnp.concatenate([(rank_col == slot + float(a * LANES)) for a in range(SCATTER_ROWS // LANES)],
                                     axis=1).astype(BF16)
            y = acc_sc[pl.ds(r0, SCATTER_ROWS), :].astype(BF16)
            o_ref[...] += gate_col * jnp.dot(onehot, y, preferred_element_type=F32)
            return carry

        lax.fori_loop(0, n_scatter, scatter, 0)

    @pl.when(jnp.logical_and(e == pl.num_programs(1) - 1, last_c))
    def _():
        o_ref[...] = _layer_norm(alpha * h_ref[...] + o_ref[...], g_ref[...], b_ref[...])


def _moe(h, cw, rank, rank_t, counts, w_gu, w_dn, g, b, *, alpha, tm, fc):
    S = h.shape[0]
    E, _, F2 = w_gu.shape
    nc = F2 // 2 // fc
    once = pl.Buffered(1)
    rows_cap = pl.cdiv(tm, MOE_ROWS) * MOE_ROWS
    grid_spec = pltpu.PrefetchScalarGridSpec(
        num_scalar_prefetch=1,
        grid=(S // tm, E, nc),
        in_specs=[pl.BlockSpec((tm, D_MODEL), lambda i, e, c, cnt: (i, 0), pipeline_mode=once),
                  pl.BlockSpec((tm, LANES), lambda i, e, c, cnt: (i, 0), pipeline_mode=once),
                  pl.BlockSpec((tm, LANES), lambda i, e, c, cnt: (i, 0), pipeline_mode=once),
                  pl.BlockSpec((1, N_EXPERTS, tm), lambda i, e, c, cnt: (i, 0, 0), pipeline_mode=once),
                  pl.BlockSpec((1, D_MODEL, fc), lambda i, e, c, cnt: (e, 0, c)),
                  pl.BlockSpec((1, D_MODEL, fc), lambda i, e, c, cnt: (e, 0, nc + c)),
                  pl.BlockSpec((1, fc, D_MODEL), lambda i, e, c, cnt: (e, c, 0)),
                  pl.BlockSpec(g.shape, lambda i, e, c, cnt: (0, 0)),
                  pl.BlockSpec(b.shape, lambda i, e, c, cnt: (0, 0))],
        out_specs=pl.BlockSpec((tm, D_MODEL), lambda i, e, c, cnt: (i, 0), pipeline_mode=once),
        scratch_shapes=[pltpu.VMEM((tm, D_MODEL), BF16), pltpu.VMEM((rows_cap, D_MODEL), BF16),
                        pltpu.VMEM((rows_cap, D_MODEL), F32)],
    )
    return pl.pallas_call(
        functools.partial(_moe_kernel, alpha=alpha),
        grid_spec=grid_spec,
        out_shape=jax.ShapeDtypeStruct((S, D_MODEL), F32),
        compiler_params=_cparams(("arbitrary", "arbitrary", "arbitrary")),
        name="moe_ffn",
    )(counts, h, cw, rank, rank_t, w_gu, w_gu, w_dn, g, b)


def _ffn_kernel(h_ref, wgu_ref, wd_ref, g_ref, b_ref, o_ref, *, alpha, fc):
    h = h_ref[...]
    hb = h.astype(BF16)
    d_ff = wd_ref.shape[0]
    acc = None
    for c in range(d_ff // fc):
        gate = jnp.dot(hb, wgu_ref[:, c * fc:(c + 1) * fc], preferred_element_type=F32)
        up = jnp.dot(hb, wgu_ref[:, d_ff + c * fc:d_ff + (c + 1) * fc], preferred_element_type=F32)
        a = (gate * _sigmoid(gate) * up).astype(BF16)
        y = jnp.dot(a, wd_ref[c * fc:(c + 1) * fc, :], preferred_element_type=F32)
        acc = y if acc is None else acc + y
    o_ref[...] = _layer_norm(alpha * h + acc, g_ref[...], b_ref[...])


def _ffn(h, w_gu, w_dn, g, b, *, alpha, tm, fc):
    S = h.shape[0]
    once = pl.Buffered(1)
    return pl.pallas_call(
        functools.partial(_ffn_kernel, alpha=alpha, fc=fc),
        grid=(S // tm,),
        in_specs=[pl.BlockSpec((tm, D_MODEL), lambda i: (i, 0)),
                  pl.BlockSpec(w_gu.shape, lambda i: (0, 0), pipeline_mode=once),
                  pl.BlockSpec(w_dn.shape, lambda i: (0, 0), pipeline_mode=once),
                  pl.BlockSpec(g.shape, lambda i: (0, 0)), pl.BlockSpec(b.shape, lambda i: (0, 0))],
        out_specs=pl.BlockSpec((tm, D_MODEL), lambda i: (i, 0)),
        out_shape=jax.ShapeDtypeStruct((S, D_MODEL), F32),
        compiler_params=_cparams(("arbitrary",)),
        name="dense_ffn",
    )(h, w_gu, w_dn, g, b)


def _pad_lanes(a, n=LANES):
    return jnp.pad(a, [(0, 0)] * (a.ndim - 1) + [(0, n - a.shape[-1])])


def kernel(x, ln_in_g, ln_in_b, w_in, b_fgate, lam_q1, lam_k1, lam_q2, lam_k2, subln_g, w_branch_fox, w_branch_diff, w_out, ln_mix_g, ln_mix_b, rel_bias, w_ffn_gate_up, w_ffn_down, w_router, w_expert_gate_up, w_expert_down, ln_ffn_g, ln_ffn_b):
    B, S, _ = x.shape
    assert B == 1
    depth = w_in.shape[0]
    alpha = (2.0 * depth) ** 0.25
    tq = min(ATTN_TQ, S)
    tk = min(ATTN_TK, tq)
    tm = min(512, S)
    tf = min(1024, S)
    te = min(MOE_TM, S)
    assert tk >= REL_MAX_DISTANCE and tq % tk == 0 and S % tq == 0 and S % tf == 0 and S % tm == 0 and S % te == 0
    assert te % SCATTER_ROWS == 0 and te % RANK_CHUNK == 0 and MOE_ROWS % 16 == 0
    row = lambda v: v.reshape(1, -1).astype(F32)

    table_flat = rel_bias.astype(F32).T.reshape(-1)
    bias = _bias_tiles(table_flat, t=tk)

    h = x[0]
    for l in range(depth):
        lam_init = 0.8 - 0.6 * math.exp(-0.3 * l)
        fq, fk, fv, fg, dq, dk, dv, ga, gb = jnp.split(w_in[l], list(_cumsum(IN_SPLIT_SIZES))[:-1], axis=-1)
        w_cat = jnp.concatenate([fq, fk, fv, dq, dk, dv, ga, gb, _pad_lanes(fg)], axis=-1).astype(BF16)
        outs = _inproj(h, row(ln_in_g), row(ln_in_b), w_cat, _pad_lanes(row(b_fgate[l])), do_ln=(l == 0), tm=tk)
        qa, ka, va, stats, dq, dk, dv, sa, sb, dkn = outs[:10]
        if l == 0:
            h = outs[10]
        start = _plan(stats, nsub=tq // tk)[:, :FOX_HEADS].reshape(-1)
        y_fox = _fox_attention(start, qa, ka, va, tq=tq, tk=tk)
        lamvec = jnp.zeros((8, LANES), F32).at[0:4, 0:HEAD_DIM].set(
            jnp.stack([lam_q1[l], lam_k1[l], lam_q2[l], lam_k2[l]]).astype(F32))
        y_diff = _diff_attention(table_flat, dq, dk, dv, bias, dkn, lamvec, row(subln_g[l]), tq=tq, tk=tk,
                                 lam_init=lam_init)
        h = _merge(y_fox, y_diff, sa, sb, h, w_branch_fox[l].astype(BF16), w_branch_diff[l].astype(BF16),
                   w_out[l].astype(BF16), row(ln_mix_g[l]), row(ln_mix_b[l]), alpha=alpha, tm=tm)
        g, b = row(ln_ffn_g[l]), row(ln_ffn_b[l])
        if l % 2 == 0:
            h = _ffn(h, w_ffn_gate_up[l // 2].astype(BF16), w_ffn_down[l // 2].astype(BF16), g, b,
                     alpha=alpha, tm=tf, fc=1408)
        else:
            wr = _pad_lanes(w_router[l // 2].astype(F32))
            whi = wr.astype(BF16)
            wlo = (wr - whi.astype(F32)).astype(BF16)
            cw, rank, rank_t, counts = _router(h, whi, wlo, tm=te)
            h = _moe(h, cw, rank, rank_t, counts[:, 0, :N_EXPERTS].reshape(-1),
                     w_expert_gate_up[l // 2].astype(BF16), w_expert_down[l // 2].astype(BF16),
                     g, b, alpha=alpha, tm=te, fc=896)
    return h[None]


def _cumsum(sizes):
    tot = 0
    for s in sizes:
        tot += s
        yield tot
```
